```python
import math
import jax, jax.numpy as jnp
from jax import lax
import numpy as np

D_MODEL = 1024
BATCH = 8
SEQ = 2048
DEPTH = 2

HEAD_DIM = 64
ROPE_THETA = 10000.0
RNN_WIDTH = 1024
RNN_BLOCKS = 16
RNN_BLOCK_W = RNN_WIDTH // RNN_BLOCKS
CONV_W = 4
LRU_C = 8.0
DIFF_HEADS = 4
DIFF_QK = DIFF_HEADS * 2 * HEAD_DIM
DIFF_V = DIFF_HEADS * 2 * HEAD_DIM
MOBA_HEADS = 8
MOBA_W = MOBA_HEADS * HEAD_DIM
MOBA_BLOCK = 256
MOBA_TOPK = 3
MOBA_Q_CHUNK = 16
ATTN_Q_BLOCK = 128
N_BRANCH = 3
IN_SPLITS = (RNN_WIDTH, RNN_WIDTH, DIFF_QK, DIFF_QK, DIFF_V, MOBA_W, MOBA_W, MOBA_W, N_BRANCH * D_MODEL)
IN_COLS = sum(IN_SPLITS)
D_FF = 2816
N_EXPERTS = 8
TOP_K = 2
N_DENSE = (DEPTH + 1) // 2
N_MOE = DEPTH // 2
EPS = 1e-6
NEG = -1e30

kernel_name = "hybrid_gated_rglru_diffattn_moba_moe_block"


def rms_norm(x, g):
    xf = x.astype(jnp.float32)
    y = xf * lax.rsqrt(jnp.mean(xf * xf, axis=-1, keepdims=True) + EPS)
    return (y * g.astype(jnp.float32)).astype(x.dtype)


def rope_tables(positions):
    inv = 1.0 / (ROPE_THETA ** (jnp.arange(0, HEAD_DIM, 2, dtype=jnp.float32) / HEAD_DIM))
    ang = positions.astype(jnp.float32)[..., None] * inv
    return jnp.cos(ang), jnp.sin(ang)


def apply_rope(x, cos, sin):
    xf = x.astype(jnp.float32)
    x1, x2 = jnp.split(xf, 2, axis=-1)
    c = cos[:, :, None, :]
    s = sin[:, :, None, :]
    return jnp.concatenate([x1 * c - x2 * s, x2 * c + x1 * s], axis=-1).astype(x.dtype)


def causal_depthwise_conv(x, w, b):
    C = x.shape[-1]
    y = lax.conv_general_dilated(
        x, w[:, None, :].astype(x.dtype), window_strides=(1,), padding=[(CONV_W - 1, 0)],
        dimension_numbers=("NWC", "WIO", "NWC"), feature_group_count=C)
    return y + b


def rg_lru(x, w_a, b_a, w_x, b_x, lam):
    B, S, _ = x.shape
    xb = x.reshape(B, S, RNN_BLOCKS, RNN_BLOCK_W)
    r = jax.nn.sigmoid(jnp.einsum("bsnd,nde->bsne", xb, w_a).reshape(B, S, RNN_WIDTH) + b_a).astype(jnp.float32)
    i = jax.nn.sigmoid(jnp.einsum("bsnd,nde->bsne", xb, w_x).reshape(B, S, RNN_WIDTH) + b_x).astype(jnp.float32)
    log_a = -LRU_C * r * jax.nn.softplus(-lam.astype(jnp.float32))
    a = jnp.exp(log_a)
    mult = jnp.sqrt(-jnp.expm1(2.0 * log_a))
    u = mult * i * x.astype(jnp.float32)

    def combine(left, right):
        a1, u1 = left
        a2, u2 = right
        return a1 * a2, a2 * u1 + u2

    _, h = lax.associative_scan(combine, (a, u), axis=1)
    return h.astype(x.dtype)


def diff_attention(q, k, v, lam, lam_init, subln_g):
    B, H, _, S, Dh = q.shape
    nq = S // ATTN_Q_BLOCK
    scale = 1.0 / math.sqrt(Dh)
    kf = k.astype(jnp.float32)
    vf = v.astype(jnp.float32)
    qb = jnp.moveaxis(q.reshape(B, H, 2, nq, ATTN_Q_BLOCK, Dh), 3, 0)
    k_pos = jnp.arange(S)

    def block(args):
        qi, i = args
        s = jnp.einsum("bhjqd,bhjkd->bhjqk", qi.astype(jnp.float32), kf) * scale
        q_pos = i * ATTN_Q_BLOCK + jnp.arange(ATTN_Q_BLOCK)
        s = jnp.where(k_pos[None, :] <= q_pos[:, None], s, NEG)
        p = jax.nn.softmax(s, axis=-1)
        a = p[:, :, 0] - lam * p[:, :, 1]
        return jnp.einsum("bhqk,bhkd->bhqd", a, vf)

    o = lax.map(block, (qb, jnp.arange(nq)))
    o = jnp.moveaxis(o, 0, 2).reshape(B, H, S, 2 * Dh)
    o = rms_norm(o, subln_g) * (1.0 - lam_init)
    return o.transpose(0, 2, 1, 3).reshape(B, S, H * 2 * Dh).astype(v.dtype)


def moba_attention(q, k, v):
    B, H, S, Dh = q.shape
    scale = 1.0 / math.sqrt(Dh)
    nb = -(-S // MOBA_BLOCK)
    pad = nb * MOBA_BLOCK - S
    kf = jnp.pad(k.astype(jnp.float32), ((0, 0), (0, 0), (0, pad), (0, 0)))
    vf = jnp.pad(v.astype(jnp.float32), ((0, 0), (0, 0), (0, pad), (0, 0)))
    k_blk = kf.reshape(B, H, nb, MOBA_BLOCK, Dh)
    v_blk = vf.reshape(B, H, nb, MOBA_BLOCK, Dh)
    k_mean = jnp.mean(k_blk, axis=3)
    n_sel = min(MOBA_TOPK, nb - 1)
    nq = S // MOBA_Q_CHUNK
    qc = jnp.moveaxis(q.reshape(B, H, nq, MOBA_Q_CHUNK, Dh), 2, 0)
    bi = jnp.arange(B)[:, None, None, None]
    hi = jnp.arange(H)[None, :, None, None]
    blk_ids = jnp.arange(nb)

    def chunk(args):
        qi, i = args
        qi = qi.astype(jnp.float32)
        q_pos = i * MOBA_Q_CHUNK + jnp.arange(MOBA_Q_CHUNK)
        cur = (i * MOBA_Q_CHUNK) // MOBA_BLOCK
        k_own = lax.dynamic_index_in_dim(k_blk, cur, axis=2, keepdims=False)
        v_own = lax.dynamic_index_in_dim(v_blk, cur, axis=2, keepdims=False)
        own_pos = cur * MOBA_BLOCK + jnp.arange(MOBA_BLOCK)
        s_own = jnp.einsum("bhqd,bhkd->bhqk", qi, k_own) * scale
        s_own = jnp.where(own_pos[None, :] <= q_pos[:, None], s_own, NEG)
        if n_sel == 0:
            p_own = jax.nn.softmax(s_own, axis=-1)
            return jnp.einsum("bhqk,bhkd->bhqd", p_own, v_own)
        gate = jnp.einsum("bhqd,bhnd->bhqn", qi, k_mean)
        gate = jnp.where(blk_ids < cur, gate, NEG)
        _, idx = lax.top_k(gate, n_sel)
        valid = idx < cur
        kg = k_blk[bi, hi, idx]
        vg = v_blk[bi, hi, idx]
        s_sel = jnp.einsum("bhqd,bhqnkd->bhqnk", qi, kg) * scale
        s_sel = jnp.where(valid[..., None], s_sel, NEG).reshape(B, H, MOBA_Q_CHUNK, n_sel * MOBA_BLOCK)
        p = jax.nn.softmax(jnp.concatenate([s_sel, s_own], axis=-1), axis=-1)
        p_sel = p[..., : n_sel * MOBA_BLOCK].reshape(B, H, MOBA_Q_CHUNK, n_sel, MOBA_BLOCK)
        p_own = p[..., n_sel * MOBA_BLOCK:]
        return (jnp.einsum("bhqnk,bhqnkd->bhqd", p_sel, vg)
                + jnp.einsum("bhqk,bhkd->bhqd", p_own, v_own))

    o = lax.map(chunk, (qc, jnp.arange(nq)))
    o = jnp.moveaxis(o, 0, 2).reshape(B, H, S, Dh)
    return o.transpose(0, 2, 1, 3).reshape(B, S, H * Dh).astype(v.dtype)


def token_mixers(h, cos, sin, lam_init, w_in, gate_b, conv_w, conv_b, lru_wa, lru_ba, lru_wx, lru_bx,
                 lru_lambda, diff_qn, diff_kn, diff_lq1, diff_lk1, diff_lq2, diff_lk2, diff_subln,
                 moba_qn, moba_kn, w_br_a, w_br_b, w_br_c, w_out):
    B, S, _ = h.shape
    offsets = [sum(IN_SPLITS[:n]) for n in range(1, len(IN_SPLITS))]
    proj = h @ w_in
    x_rnn, g_rnn, dq, dk, dv, mq, mk, mv, g_br = jnp.split(proj, offsets, axis=-1)

    xc = causal_depthwise_conv(x_rnn, conv_w, conv_b)
    y_a = jax.nn.gelu(g_rnn) * rg_lru(xc, lru_wa, lru_ba, lru_wx, lru_bx, lru_lambda)

    dq = apply_rope(rms_norm(dq.reshape(B, S, 2 * DIFF_HEADS, HEAD_DIM), diff_qn), cos, sin)
    dk = apply_rope(rms_norm(dk.reshape(B, S, 2 * DIFF_HEADS, HEAD_DIM), diff_kn), cos, sin)
    dq = dq.reshape(B, S, DIFF_HEADS, 2, HEAD_DIM).transpose(0, 2, 3, 1, 4)
    dk = dk.reshape(B, S, DIFF_HEADS, 2, HEAD_DIM).transpose(0, 2, 3, 1, 4)
    dv = dv.reshape(B, S, DIFF_HEADS, 2 * HEAD_DIM).transpose(0, 2, 1, 3)
    lam = (jnp.exp(jnp.sum(diff_lq1.astype(jnp.float32) * diff_lk1.astype(jnp.float32)))
           - jnp.exp(jnp.sum(diff_lq2.astype(jnp.float32) * diff_lk2.astype(jnp.float32))) + lam_init)
    y_b = diff_attention(dq, dk, dv, lam, lam_init, diff_subln)

    mq = apply_rope(rms_norm(mq.reshape(B, S, MOBA_HEADS, HEAD_DIM), moba_qn), cos, sin).transpose(0, 2, 1, 3)
    mk = apply_rope(rms_norm(mk.reshape(B, S, MOBA_HEADS, HEAD_DIM), moba_kn), cos, sin).transpose(0, 2, 1, 3)
    mv = mv.reshape(B, S, MOBA_HEADS, HEAD_DIM).transpose(0, 2, 1, 3)
    y_c = moba_attention(mq, mk, mv)

    gates = jax.nn.sigmoid(g_br.reshape(B, S, N_BRANCH, D_MODEL) + gate_b)
    merged = (gates[:, :, 0] * (y_a @ w_br_a) + gates[:, :, 1] * (y_b @ w_br_b)
              + gates[:, :, 2] * (y_c @ w_br_c))
    return merged @ w_out


def swiglu(h, w1, w3, w2):
    return (jax.nn.silu(h @ w1) * (h @ w3)) @ w2


def moe_ffn(h, router_w, router_b, w1, w3, w2):
    logits = (h @ router_w).astype(jnp.float32) + router_b.astype(jnp.float32)
    top_v, top_i = lax.top_k(logits, TOP_K)
    top_w = jax.nn.softmax(top_v, axis=-1)
    combine = jnp.sum(jax.nn.one_hot(top_i, N_EXPERTS, dtype=jnp.float32) * top_w[..., None], axis=-2)
    combine = combine.astype(h.dtype)
    y = jnp.zeros_like(h)
    for e in range(N_EXPERTS):
        y = y + combine[..., e:e + 1] * swiglu(h, w1[e], w3[e], w2[e])
    return y


def setup_inputs(seed: int = 0) -> dict:
    key = jax.random.key(seed)
    counter = [0]

    def nrm(shape, scale):
        counter[0] += 1
        return scale * jax.random.normal(jax.random.fold_in(key, counter[0]), shape, jnp.float32)

    L, D = DEPTH, D_MODEL
    x = nrm((BATCH, SEQ, D), 1.0)
    c = nrm((BATCH, D), 1.0)
    offs = jax.random.randint(jax.random.fold_in(key, 1000), (BATCH, 1), 0, 4096, dtype=jnp.int32)
    positions = offs + jnp.arange(SEQ, dtype=jnp.int32)[None, :]
    u = jax.random.uniform(jax.random.fold_in(key, 1001), (L, RNN_WIDTH), jnp.float32, 0.9, 0.999)
    a0 = u ** (1.0 / LRU_C)
    lru_lambda = jnp.log(a0) - jnp.log1p(-a0)
    return {
        "x": x,
        "c": c,
        "positions": positions,
        "ada_w": nrm((L, D, 6 * D), 0.2 * D ** -0.5),
        "ada_b": nrm((L, 6 * D), 0.02),
        "norm1_g": 1.0 + nrm((L, D), 0.02),
        "norm2_g": 1.0 + nrm((L, D), 0.02),
        "w_in": nrm((L, D, IN_COLS), D ** -0.5),
        "gate_b": nrm((L, N_BRANCH, D), 0.1),
        "conv_w": nrm((L, CONV_W, RNN_WIDTH), CONV_W ** -0.5),
        "conv_b": nrm((L, RNN_WIDTH), 0.02),
        "lru_wa": nrm((L, RNN_BLOCKS, RNN_BLOCK_W, RNN_BLOCK_W), RNN_BLOCK_W ** -0.5),
        "lru_ba": nrm((L, RNN_WIDTH), 0.1),
        "lru_wx": nrm((L, RNN_BLOCKS, RNN_BLOCK_W, RNN_BLOCK_W), RNN_BLOCK_W ** -0.5),
        "lru_bx": nrm((L, RNN_WIDTH), 0.1),
        "lru_lambda": lru_lambda,
        "diff_qn": 1.0 + nrm((L, HEAD_DIM), 0.02),
        "diff_kn": 1.0 + nrm((L, HEAD_DIM), 0.02),
        "diff_lq1": nrm((L, HEAD_DIM), 0.1),
        "diff_lk1": nrm((L, HEAD_DIM), 0.1),
        "diff_lq2": nrm((L, HEAD_DIM), 0.1),
        "diff_lk2": nrm((L, HEAD_DIM), 0.1),
        "diff_subln": 1.0 + nrm((L, 2 * HEAD_DIM), 0.02),
        "moba_qn": 1.0 + nrm((L, HEAD_DIM), 0.02),
        "moba_kn": 1.0 + nrm((L, HEAD_DIM), 0.02),
        "w_br_a": nrm((L, RNN_WIDTH, D), RNN_WIDTH ** -0.5),
        "w_br_b": nrm((L, DIFF_V, D), DIFF_V ** -0.5),
        "w_br_c": nrm((L, MOBA_W, D), MOBA_W ** -0.5),
        "w_out": nrm((L, D, D), D ** -0.5),
        "ffn_w1": nrm((N_DENSE, D, D_FF), D ** -0.5),
        "ffn_w3": nrm((N_DENSE, D, D_FF), D ** -0.5),
        "ffn_w2": nrm((N_DENSE, D_FF, D), D_FF ** -0.5),
        "router_w": nrm((N_MOE, D, N_EXPERTS), D ** -0.5),
        "router_b": nrm((N_MOE, N_EXPERTS), 0.01),
        "moe_w1": nrm((N_MOE, N_EXPERTS, D, D_FF), D ** -0.5),
        "moe_w3": nrm((N_MOE, N_EXPERTS, D, D_FF), D ** -0.5),
        "moe_w2": nrm((N_MOE, N_EXPERTS, D_FF, D), D_FF ** -0.5),
    }


def reference(x, c, positions, ada_w, ada_b, norm1_g, norm2_g, w_in, gate_b, conv_w, conv_b,
              lru_wa, lru_ba, lru_wx, lru_bx, lru_lambda, diff_qn, diff_kn, diff_lq1, diff_lk1,
              diff_lq2, diff_lk2, diff_subln, moba_qn, moba_kn, w_br_a, w_br_b, w_br_c, w_out,
              ffn_w1, ffn_w3, ffn_w2, router_w, router_b, moe_w1, moe_w3, moe_w2):
    cos, sin = rope_tables(positions)
    for l in range(DEPTH):
        mod = (c @ ada_w[l] + ada_b[l])[:, None, :]
        shift1, scale1, gate1, shift2, scale2, gate2 = jnp.split(mod, 6, axis=-1)
        lam_init = 0.8 - 0.6 * math.exp(-0.3 * l)
        h = rms_norm(x, norm1_g[l]) * (1.0 + scale1) + shift1
        mix = token_mixers(h, cos, sin, lam_init, w_in[l], gate_b[l], conv_w[l], conv_b[l],
                           lru_wa[l], lru_ba[l], lru_wx[l], lru_bx[l], lru_lambda[l],
                           diff_qn[l], diff_kn[l], diff_lq1[l], diff_lk1[l], diff_lq2[l], diff_lk2[l],
                           diff_subln[l], moba_qn[l], moba_kn[l], w_br_a[l], w_br_b[l], w_br_c[l], w_out[l])
        x = x + gate1 * mix
        h = rms_norm(x, norm2_g[l]) * (1.0 + scale2) + shift2
        if l % 2 == 0:
            y = swiglu(h, ffn_w1[l // 2], ffn_w3[l // 2], ffn_w2[l // 2])
        else:
            y = moe_ffn(h, router_w[l // 2], router_b[l // 2], moe_w1[l // 2], moe_w3[l // 2], moe_w2[l // 2])
        x = x + gate2 * y
    return x
```

```python
import functools
import math

import jax
import jax.numpy as jnp
from jax import lax
from jax.experimental import pallas as pl
from jax.experimental.pallas import tpu as pltpu

F32 = jnp.float32
BF16 = jnp.bfloat16
HIGHEST = lax.Precision.HIGHEST

HEAD_DIM = 64
ROPE_THETA = 10000.0
RNN_BLOCKS = 16
CONV_W = 4
LRU_C = 8.0
DIFF_HEADS = 4
MOBA_HEADS = 8
MOBA_BLOCK = 256
MOBA_TOPK = 3
N_BRANCH = 3
N_EXPERTS = 8
TOP_K = 2
EPS = 1e-6
NEG = -1e30

LANES = 128
VMEM_LIMIT = 56 * 1024 * 1024

ROW_TILE = 512


def _cparams(*sem):
    return pltpu.CompilerParams(dimension_semantics=sem, vmem_limit_bytes=VMEM_LIMIT)


def _modulated_norm(x, g, mod, base):
    ms = jnp.mean(x * x, axis=-1, keepdims=True)
    y = x * lax.rsqrt(ms + EPS) * g
    return y * (1.0 + mod[base + 1:base + 2]) + mod[base:base + 1]


def _dot_nt(a, b, **kw):
    return lax.dot_general(a, b, (((1,), (1,)), ((), ())), preferred_element_type=F32, **kw)


def _ada_kernel(c_ref, w_ref, b_ref, o_ref):
    o_ref[0] = jnp.dot(c_ref[...], w_ref[0], preferred_element_type=F32, precision=HIGHEST) + b_ref[0]


def _ada_mod(c, ada_w, ada_b):
    L, D, N = ada_w.shape
    B = c.shape[0]
    tn = 1536
    return pl.pallas_call(
        _ada_kernel,
        grid=(L, N // tn),
        in_specs=[pl.BlockSpec((B, D), lambda l, j: (0, 0)),
                  pl.BlockSpec((1, D, tn), lambda l, j: (l, 0, j)),
                  pl.BlockSpec((1, 1, tn), lambda l, j: (l, 0, j))],
        out_specs=pl.BlockSpec((1, B, tn), lambda l, j: (l, 0, j)),
        out_shape=jax.ShapeDtypeStruct((L, B, N), F32),
        compiler_params=_cparams("parallel", "parallel"),
        name="ada_mod",
    )(c, ada_w, ada_b.reshape(L, 1, N))


def _inproj_kernel(x_ref, mod_ref, g_ref, w_ref, o_ref, h_ref):
    @pl.when(pl.program_id(1) == 0)
    def _():
        h_ref[...] = _modulated_norm(x_ref[...], g_ref[...], mod_ref[0], 0).astype(BF16)

    o_ref[...] = jnp.dot(h_ref[...], w_ref[...], preferred_element_type=F32).astype(BF16)


def _in_proj(x2, mod, g, w_bf16, seq):
    T, D = x2.shape
    N = w_bf16.shape[1]
    tm, tn = 1024, 1024
    per_b = seq // tm
    return pl.pallas_call(
        _inproj_kernel,
        grid=(T // tm, N // tn),
        in_specs=[pl.BlockSpec((tm, D), lambda i, j: (i, 0)),
                  pl.BlockSpec((1, 6, D), lambda i, j: (i // per_b, 0, 0)),
                  pl.BlockSpec((1, D), lambda i, j: (0, 0)),
                  pl.BlockSpec((D, tn), lambda i, j: (0, j))],
        out_specs=pl.BlockSpec((tm, tn), lambda i, j: (i, j)),
        out_shape=jax.ShapeDtypeStruct((T, N), BF16),
        scratch_shapes=[pltpu.VMEM((tm, D), BF16)],
        compiler_params=_cparams("parallel", "arbitrary"),
        name="in_proj",
    )(x2, mod, g.reshape(1, D), w_bf16)


def _rope_kernel(pos_ref, inv_ref, sign_ref, cos_ref, sin_ref):
    ang = pos_ref[...] * inv_ref[...]
    cos_ref[...] = jnp.cos(ang)
    sin_ref[...] = jnp.sin(ang) * sign_ref[...]


def _rope_tables(positions):
    T = positions.size
    pos = positions.reshape(T, 1).astype(F32)
    inv = 1.0 / (ROPE_THETA ** (jnp.arange(0, HEAD_DIM, 2, dtype=F32) / HEAD_DIM))
    half = HEAD_DIM // 2
    inv128 = jnp.tile(inv, LANES // half).reshape(1, LANES)
    sign = jnp.tile(jnp.concatenate([-jnp.ones((half,), F32), jnp.ones((half,), F32)]),
                    LANES // HEAD_DIM).reshape(1, LANES)
    tm = 1024
    return pl.pallas_call(
        _rope_kernel,
        grid=(T // tm,),
        in_specs=[pl.BlockSpec((tm, 1), lambda i: (i, 0)),
                  pl.BlockSpec((1, LANES), lambda i: (0, 0)),
                  pl.BlockSpec((1, LANES), lambda i: (0, 0))],
        out_specs=[pl.BlockSpec((tm, LANES), lambda i: (i, 0))] * 2,
        out_shape=[jax.ShapeDtypeStruct((T, LANES), F32)] * 2,
        compiler_params=_cparams("parallel"),
        name="rope_tables",
    )(pos, inv128, sign)


def _qkprep_kernel(dq_ref, dk_ref, mq_ref, mk_ref, gain_ref, cos_ref, sin_ref, seg_ref, o_ref):
    cos = cos_ref[...]
    sin = sin_ref[...]
    seg = seg_ref[...]
    lane = lax.broadcasted_iota(jnp.int32, cos.shape, 1)
    first_half = (lane % HEAD_DIM) < (HEAD_DIM // 2)
    width = dq_ref.shape[1]
    for gi, ref in enumerate((dq_ref, dk_ref, mq_ref, mk_ref)):
        gain = gain_ref[gi:gi + 1, :]
        for cb in range(width // LANES):
            x = ref[:, cb * LANES:(cb + 1) * LANES].astype(F32)
            ms = jnp.dot(x * x, seg, preferred_element_type=F32, precision=HIGHEST)
            y = x * lax.rsqrt(ms + EPS) * gain
            swapped = jnp.where(first_half, pltpu.roll(y, LANES - HEAD_DIM // 2, 1),
                                pltpu.roll(y, HEAD_DIM // 2, 1))
            r = y * cos + swapped * sin
            if gi % 2 == 0:
                r = r * (1.0 / math.sqrt(HEAD_DIM))
            col = gi * width + cb * LANES
            o_ref[:, col:col + LANES] = r.astype(BF16)


def _qk_prep(proj, gains, cos, sin):
    T = proj.shape[0]
    width = DIFF_HEADS * 2 * HEAD_DIM
    seg = jnp.kron(jnp.eye(LANES // HEAD_DIM, dtype=F32),
                   jnp.full((HEAD_DIM, HEAD_DIM), 1.0 / HEAD_DIM, F32))
    tm = ROW_TILE
    col_blocks = (4, 5, 7, 8)
    in_specs = [pl.BlockSpec((tm, width), functools.partial(lambda i, c: (i, c), c=c)) for c in col_blocks]
    in_specs += [pl.BlockSpec((4, LANES), lambda i: (0, 0)),
                 pl.BlockSpec((tm, LANES), lambda i: (i, 0)),
                 pl.BlockSpec((tm, LANES), lambda i: (i, 0)),
                 pl.BlockSpec((LANES, LANES), lambda i: (0, 0))]
    return pl.pallas_call(
        _qkprep_kernel,
        grid=(T // tm,),
        in_specs=in_specs,
        out_specs=pl.BlockSpec((tm, 4 * width), lambda i: (i, 0)),
        out_shape=jax.ShapeDtypeStruct((T, 4 * width), BF16),
        compiler_params=_cparams("parallel"),
        name="qk_prep",
    )(proj, proj, proj, proj, gains, cos, sin, seg)


def _gelu_tanh(x):
    return 0.5 * x * (1.0 + jnp.tanh(math.sqrt(2.0 / math.pi) * (x + 0.044715 * x * x * x)))


def _rglru_kernel(x_ref, g_ref, cw_ref, cb_ref, wa_ref, ba_ref, wx_ref, bx_ref, lam_ref, o_ref,
                  xbuf, abuf, ubuf, hcar):
    tc, C = x_ref.shape
    HALO = 8

    @pl.when(pl.program_id(1) == 0)
    def _():
        xbuf[0:HALO, :] = jnp.zeros((HALO, C), F32)
        hcar[...] = jnp.zeros_like(hcar)

    xbuf[HALO:HALO + tc, :] = x_ref[...].astype(F32)
    xc = cb_ref[...] + cw_ref[CONV_W - 1:CONV_W, :] * xbuf[HALO:HALO + tc, :]
    for j in range(1, CONV_W):
        xc = xc + cw_ref[CONV_W - 1 - j:CONV_W - j, :] * xbuf[HALO - j:HALO - j + tc, :]
    xbuf[0:HALO, :] = xbuf[tc:tc + HALO, :]

    xcb = xc.astype(BF16)
    nblk = C // LANES
    ra = jnp.concatenate([jnp.dot(xcb[:, n * LANES:(n + 1) * LANES], wa_ref[n], preferred_element_type=F32)
                          for n in range(nblk)], axis=1)
    rx = jnp.concatenate([jnp.dot(xcb[:, n * LANES:(n + 1) * LANES], wx_ref[n], preferred_element_type=F32)
                          for n in range(nblk)], axis=1)
    r = jax.nn.sigmoid(ra + ba_ref[...])
    gi = jax.nn.sigmoid(rx + bx_ref[...])
    neg_lam = -lam_ref[...]
    softplus = jnp.maximum(neg_lam, 0.0) + jnp.log1p(jnp.exp(-jnp.abs(neg_lam)))
    log_a = (-LRU_C) * r * softplus
    a = jnp.exp(log_a)
    mult = jnp.sqrt(1.0 - a * a)
    abuf[...] = a
    ubuf[...] = mult * gi * xc

    def step(t8, h):
        base = pl.multiple_of(t8 * 8, 8)
        a8 = abuf[pl.ds(base, 8), :]
        u8 = ubuf[pl.ds(base, 8), :]
        rows = []
        for rr in range(8):
            h = a8[rr:rr + 1, :] * h + u8[rr:rr + 1, :]
            rows.append(h)
        ubuf[pl.ds(base, 8), :] = jnp.concatenate(rows, axis=0)
        return h

    h_last = lax.fori_loop(0, tc // 8, step, hcar[0:1, :])
    hcar[0:1, :] = h_last
    o_ref[...] = (_gelu_tanh(g_ref[...].astype(F32)) * ubuf[...]).astype(BF16)


def _rg_lru_branch(proj, seq, conv_w, conv_b, wa2, ba, wx2, bx, lam):
    T = proj.shape[0]
    C = conv_w.shape[1]
    B = T // seq
    tc = ROW_TILE
    per_b = seq // tc
    row = lambda v: v.reshape(1, C)
    full2 = lambda shape: pl.BlockSpec(shape, lambda b, t: (0,) * len(shape))
    return pl.pallas_call(
        _rglru_kernel,
        grid=(B, per_b),
        in_specs=[pl.BlockSpec((tc, C), lambda b, t: (b * per_b + t, 0)),
                  pl.BlockSpec((tc, C), lambda b, t: (b * per_b + t, 1)),
                  full2((CONV_W, C)), full2((1, C)),
                  full2(wa2.shape), full2((1, C)), full2(wx2.shape), full2((1, C)), full2((1, C))],
        out_specs=pl.BlockSpec((tc, C), lambda b, t: (b * per_b + t, 0)),
        out_shape=jax.ShapeDtypeStruct((T, C), BF16),
        scratch_shapes=[pltpu.VMEM((tc + 8, C), F32), pltpu.VMEM((tc, C), F32),
                        pltpu.VMEM((tc, C), F32), pltpu.VMEM((8, C), F32)],
        compiler_params=_cparams("parallel", "arbitrary"),
        name="rg_lru",
    )(proj, proj, conv_w, row(conv_b), wa2, row(ba), wx2, row(bx), row(lam))


def _pair_block_diag(w):
    n, d, _ = w.shape
    z = jnp.zeros((n // 2, 2, d, 2, d), w.dtype)
    z = z.at[:, 0, :, 0, :].set(w[0::2]).at[:, 1, :, 1, :].set(w[1::2])
    return z.reshape(n // 2, 2 * d, 2 * d).astype(BF16)


def _softmax_step(s, m_ref, acc_ref, v_aug):
    m_old = m_ref[...]
    m_new = jnp.maximum(m_old, jnp.max(s, axis=-1, keepdims=True))
    p = jnp.exp(s - m_new).astype(BF16)
    acc_ref[...] = jnp.exp(m_old - m_new) * acc_ref[...] + jnp.dot(p, v_aug, preferred_element_type=F32)
    m_ref[...] = m_new


def _with_ones(v):
    return jnp.concatenate([v, jnp.ones_like(v)], axis=1)


def _diffattn_kernel(q_ref, k_ref, v_ref, lam_ref, g_ref, o_ref, m1, acc1, m2, acc2, *, lam_init):
    i = pl.program_id(2)
    tq = q_ref.shape[0]
    q = q_ref[...]
    lane = lax.broadcasted_iota(jnp.int32, q.shape, 1)
    zero = jnp.zeros_like(q)
    q1 = jnp.where(lane < HEAD_DIM, q, zero)
    q2 = jnp.where(lane < HEAD_DIM, zero, q)
    for m_ref, acc_ref in ((m1, acc1), (m2, acc2)):
        m_ref[...] = jnp.full(m_ref.shape, NEG, F32)
        acc_ref[...] = jnp.zeros(acc_ref.shape, F32)

    def body(j, carry):
        start = pl.multiple_of(j * tq, tq)
        k = k_ref[pl.ds(start, tq), :]
        v_aug = _with_ones(v_ref[pl.ds(start, tq), :])
        _softmax_step(_dot_nt(q1, k), m1, acc1, v_aug)
        _softmax_step(_dot_nt(q2, k), m2, acc2, v_aug)
        return carry

    lax.fori_loop(0, i, body, 0)

    start = pl.multiple_of(i * tq, tq)
    k = k_ref[pl.ds(start, tq), :]
    v_aug = _with_ones(v_ref[pl.ds(start, tq), :])
    causal = (lax.broadcasted_iota(jnp.int32, (tq, tq), 1) <= lax.broadcasted_iota(jnp.int32, (tq, tq), 0))
    _softmax_step(jnp.where(causal, _dot_nt(q1, k), NEG), m1, acc1, v_aug)
    _softmax_step(jnp.where(causal, _dot_nt(q2, k), NEG), m2, acc2, v_aug)

    lp = lam_ref[...]
    lam = (jnp.exp(jnp.sum(lp[0:1] * lp[1:2], axis=-1, keepdims=True))
           - jnp.exp(jnp.sum(lp[2:3] * lp[3:4], axis=-1, keepdims=True)) + lam_init)
    o = acc1[:, :LANES] / acc1[:, LANES:] - lam * (acc2[:, :LANES] / acc2[:, LANES:])
    ms = jnp.mean(o * o, axis=-1, keepdims=True)
    o_ref[...] = (o * lax.rsqrt(ms + EPS) * g_ref[...] * (1.0 - lam_init)).astype(BF16)


def _diff_attention(qk, proj, seq, lam_params, subln_g, lam_init):
    T = qk.shape[0]
    B = T // seq
    tq = MOBA_BLOCK
    nq = seq // tq
    H = DIFF_HEADS
    return pl.pallas_call(
        functools.partial(_diffattn_kernel, lam_init=lam_init),
        grid=(B, H, nq),
        in_specs=[pl.BlockSpec((tq, LANES), lambda b, h, i: (b * nq + i, h)),
                  pl.BlockSpec((seq, LANES), lambda b, h, i: (b, H + h)),
                  pl.BlockSpec((seq, LANES), lambda b, h, i: (b, 24 + h)),
                  pl.BlockSpec((4, HEAD_DIM), lambda b, h, i: (0, 0)),
                  pl.BlockSpec((1, LANES), lambda b, h, i: (0, 0))],
        out_specs=pl.BlockSpec((tq, LANES), lambda b, h, i: (b * nq + i, h)),
        out_shape=jax.ShapeDtypeStruct((T, H * LANES), BF16),
        scratch_shapes=[pltpu.VMEM((tq, 1), F32), pltpu.VMEM((tq, 2 * LANES), F32),
                        pltpu.VMEM((tq, 1), F32), pltpu.VMEM((tq, 2 * LANES), F32)],
        compiler_params=_cparams("parallel", "parallel", "arbitrary"),
        name="diff_attn",
    )(qk, qk, proj, lam_params, subln_g.reshape(1, LANES))


def _moba_kernel(q_ref, k_ref, v_ref, o_ref, kmean, m_a, acc_a, m_b, acc_b):
    i = pl.program_id(2)
    tq = q_ref.shape[0]
    nb = k_ref.shape[0] // MOBA_BLOCK
    lane_row = lax.broadcasted_iota(jnp.int32, (1, LANES), 1)

    @pl.when(i == 0)
    def _():
        kmean[...] = jnp.zeros_like(kmean)
        for blk in range(nb):
            mean = jnp.mean(k_ref[blk * MOBA_BLOCK:(blk + 1) * MOBA_BLOCK, :].astype(F32), axis=0, keepdims=True)
            kmean[blk:blk + 1, :] = jnp.where(lane_row < HEAD_DIM, mean, 0.0)
            kmean[HEAD_DIM + blk:HEAD_DIM + blk + 1, :] = jnp.where(lane_row < HEAD_DIM, 0.0, mean)

    q = q_ref[...]
    lane = lax.broadcasted_iota(jnp.int32, q.shape, 1)
    zero = jnp.zeros_like(q)
    q_a = jnp.where(lane < HEAD_DIM, q, zero)
    q_b = jnp.where(lane < HEAD_DIM, zero, q)

    gate = _dot_nt(q.astype(F32), kmean[...], precision=HIGHEST)
    valid = (lane % HEAD_DIM) < i
    gate = jnp.where(valid, gate, NEG)
    rank = jnp.zeros(gate.shape, jnp.int32)
    for d in range(1, nb):
        lower = pltpu.roll(gate, d, 1)
        upper = pltpu.roll(gate, LANES - d, 1)
        rank = rank + (lower >= gate).astype(jnp.int32) + (upper > gate).astype(jnp.int32)
    bias_a = jnp.where(valid & (rank < MOBA_TOPK), 0.0, NEG)
    bias_b = pltpu.roll(bias_a, HEAD_DIM, 1)
    qa_aug = jnp.concatenate([q_a, bias_a.astype(BF16)], axis=1)
    qb_aug = jnp.concatenate([q_b, bias_b.astype(BF16)], axis=1)

    for m_ref, acc_ref in ((m_a, acc_a), (m_b, acc_b)):
        m_ref[...] = jnp.full(m_ref.shape, NEG, F32)
        acc_ref[...] = jnp.zeros(acc_ref.shape, F32)

    start = pl.multiple_of(i * tq, tq)
    k = k_ref[pl.ds(start, tq), :]
    v_aug = _with_ones(v_ref[pl.ds(start, tq), :])
    causal = (lax.broadcasted_iota(jnp.int32, (tq, tq), 1) <= lax.broadcasted_iota(jnp.int32, (tq, tq), 0))
    _softmax_step(jnp.where(causal, _dot_nt(q_a, k), NEG), m_a, acc_a, v_aug)
    _softmax_step(jnp.where(causal, _dot_nt(q_b, k), NEG), m_b, acc_b, v_aug)

    def body(j, carry):
        start = pl.multiple_of(j * tq, tq)
        k_aug = jnp.concatenate([k_ref[pl.ds(start, tq), :], (lane == j).astype(BF16)], axis=1)
        v_aug = _with_ones(v_ref[pl.ds(start, tq), :])
        _softmax_step(_dot_nt(qa_aug, k_aug), m_a, acc_a, v_aug)
        _softmax_step(_dot_nt(qb_aug, k_aug), m_b, acc_b, v_aug)
        return carry

    lax.fori_loop(0, i, body, 0)

    o_a = acc_a[:, :LANES] / acc_a[:, LANES:]
    o_b = acc_b[:, :LANES] / acc_b[:, LANES:]
    o_ref[...] = jnp.where(lane < HEAD_DIM, o_a, o_b).astype(BF16)


def _moba_attention(qk, proj, seq):
    T = qk.shape[0]
    B = T // seq
    tq = MOBA_BLOCK
    nq = seq // tq
    HP = MOBA_HEADS // 2
    return pl.pallas_call(
        _moba_kernel,
        grid=(B, HP, nq),
        in_specs=[pl.BlockSpec((tq, LANES), lambda b, h, i: (b * nq + i, 8 + h)),
                  pl.BlockSpec((seq, LANES), lambda b, h, i: (b, 12 + h)),
                  pl.BlockSpec((seq, LANES), lambda b, h, i: (b, 36 + h))],
        out_specs=pl.BlockSpec((tq, LANES), lambda b, h, i: (b * nq + i, h)),
        out_shape=jax.ShapeDtypeStruct((T, HP * LANES), BF16),
        scratch_shapes=[pltpu.VMEM((LANES, LANES), F32),
                        pltpu.VMEM((tq, 1), F32), pltpu.VMEM((tq, 2 * LANES), F32),
                        pltpu.VMEM((tq, 1), F32), pltpu.VMEM((tq, 2 * LANES), F32)],
        compiler_params=_cparams("parallel", "parallel", "arbitrary"),
        name="moba_attn",
    )(qk, qk, proj)


def _merge_kernel(x_ref, ya_ref, yb_ref, yc_ref, ga_ref, gbb_ref, gc_ref, gb_ref, mod_ref,
                  wa_ref, wb_ref, wc_ref, wo_ref, o_ref):
    merged = None
    branches = ((ya_ref, wa_ref, ga_ref), (yb_ref, wb_ref, gbb_ref), (yc_ref, wc_ref, gc_ref))
    for n, (y_ref, w_ref, gl_ref) in enumerate(branches):
        gate = jax.nn.sigmoid(gl_ref[...].astype(F32) + gb_ref[n:n + 1, :])
        term = gate * jnp.dot(y_ref[...], w_ref[...], preferred_element_type=F32)
        merged = term if merged is None else merged + term
    mix = jnp.dot(merged.astype(BF16), wo_ref[...], preferred_element_type=F32)
    o_ref[...] = x_ref[...] + mod_ref[0][2:3] * mix


def _merge_out(x2, ya, yb, yc, proj, gate_b, mod, wa, wb, wc, wo, seq):
    T, D = x2.shape
    tm = ROW_TILE
    per_b = seq // tm
    gbr_block = 5120 // D
    assert gbr_block * D == 5120
    wspec = lambda w: pl.BlockSpec(w.shape, lambda i: (0, 0))
    return pl.pallas_call(
        _merge_kernel,
        grid=(T // tm,),
        in_specs=[pl.BlockSpec((tm, D), lambda i: (i, 0)),
                  pl.BlockSpec((tm, ya.shape[1]), lambda i: (i, 0)),
                  pl.BlockSpec((tm, yb.shape[1]), lambda i: (i, 0)),
                  pl.BlockSpec((tm, yc.shape[1]), lambda i: (i, 0)),
                  pl.BlockSpec((tm, D), lambda i: (i, gbr_block)),
                  pl.BlockSpec((tm, D), lambda i: (i, gbr_block + 1)),
                  pl.BlockSpec((tm, D), lambda i: (i, gbr_block + 2)),
                  pl.BlockSpec((N_BRANCH, D), lambda i: (0, 0)),
                  pl.BlockSpec((1, 6, D), lambda i: (i // per_b, 0, 0)),
                  wspec(wa), wspec(wb), wspec(wc), wspec(wo)],
        out_specs=pl.BlockSpec((tm, D), lambda i: (i, 0)),
        out_shape=jax.ShapeDtypeStruct((T, D), F32),
        compiler_params=_cparams("parallel"),
        name="merge_out",
    )(x2, ya, yb, yc, proj, proj, proj, gate_b, mod, wa, wb, wc, wo)


def _swiglu_partial(h, w1, w3, w2):
    g = jnp.dot(h, w1, preferred_element_type=F32)
    u = jnp.dot(h, w3, preferred_element_type=F32)
    act = (g * jax.nn.sigmoid(g) * u).astype(BF16)
    return jnp.dot(act, w2, preferred_element_type=F32)


def _ffn_kernel(x_ref, mod_ref, g_ref, w1_ref, w3_ref, w2_ref, o_ref, h_ref, acc_ref):
    f = pl.program_id(1)

    @pl.when(f == 0)
    def _():
        h_ref[...] = _modulated_norm(x_ref[...], g_ref[...], mod_ref[0], 3).astype(BF16)
        acc_ref[...] = jnp.zeros_like(acc_ref)

    acc_ref[...] += _swiglu_partial(h_ref[...], w1_ref[...], w3_ref[...], w2_ref[...])

    @pl.when(f == pl.num_programs(1) - 1)
    def _():
        o_ref[...] = x_ref[...] + mod_ref[0][5:6] * acc_ref[...]


def _dense_ffn(x2, mod, g, w1, w3, w2, seq):
    T, D = x2.shape
    F = w1.shape[1]
    tm = ROW_TILE
    nf = 2
    tf = F // nf
    per_b = seq // tm
    return pl.pallas_call(
        _ffn_kernel,
        grid=(T // tm, nf),
        in_specs=[pl.BlockSpec((tm, D), lambda i, f: (i, 0)),
                  pl.BlockSpec((1, 6, D), lambda i, f: (i // per_b, 0, 0)),
                  pl.BlockSpec((1, D), lambda i, f: (0, 0)),
                  pl.BlockSpec((D, tf), lambda i, f: (0, f)),
                  pl.BlockSpec((D, tf), lambda i, f: (0, f)),
                  pl.BlockSpec((tf, D), lambda i, f: (f, 0))],
        out_specs=pl.BlockSpec((tm, D), lambda i, f: (i, 0)),
        out_shape=jax.ShapeDtypeStruct((T, D), F32),
        scratch_shapes=[pltpu.VMEM((tm, D), BF16), pltpu.VMEM((tm, D), F32)],
        compiler_params=_cparams("parallel", "arbitrary"),
        name="dense_ffn",
    )(x2, mod, g.reshape(1, D), w1, w3, w2)


def _router_kernel(x_ref, mod_ref, g_ref, rw_ref, rb_ref, h_ref, comb_ref, sel_ref):
    h = _modulated_norm(x_ref[...], g_ref[...], mod_ref[0], 3)
    h_ref[...] = h.astype(BF16)
    logits = jnp.dot(h, rw_ref[...], preferred_element_type=F32, precision=HIGHEST) + rb_ref[...]
    E = logits.shape[1]
    lane = lax.broadcasted_iota(jnp.int32, logits.shape, 1)
    v1 = jnp.max(logits, axis=-1, keepdims=True)
    i1 = jnp.min(jnp.where(logits == v1, lane, E), axis=-1, keepdims=True)
    rest = jnp.where(lane == i1, -jnp.inf, logits)
    v2 = jnp.max(rest, axis=-1, keepdims=True)
    i2 = jnp.min(jnp.where(rest == v2, lane, E), axis=-1, keepdims=True)
    e2 = jnp.exp(v2 - v1)
    w1 = 1.0 / (1.0 + e2)
    w2 = e2 / (1.0 + e2)
    comb_ref[...] = jnp.where(lane == i1, w1, 0.0) + jnp.where(lane == i2, w2, 0.0)
    sel_ref[...] = ((lane == i1) | (lane == i2)).astype(jnp.int32)


def _router(x2, mod, g, router_w, router_b, seq):
    T, D = x2.shape
    E = router_w.shape[1]
    tm = ROW_TILE
    per_b = seq // tm
    return pl.pallas_call(
        _router_kernel,
        grid=(T // tm,),
        in_specs=[pl.BlockSpec((tm, D), lambda i: (i, 0)),
                  pl.BlockSpec((1, 6, D), lambda i: (i // per_b, 0, 0)),
                  pl.BlockSpec((1, D), lambda i: (0, 0)),
                  pl.BlockSpec((D, E), lambda i: (0, 0)),
                  pl.BlockSpec((1, E), lambda i: (0, 0))],
        out_specs=[pl.BlockSpec((tm, D), lambda i: (i, 0)),
                   pl.BlockSpec((tm, E), lambda i: (i, 0)),
                   pl.BlockSpec((tm, E), lambda i: (i, 0))],
        out_shape=[jax.ShapeDtypeStruct((T, D), BF16),
                   jax.ShapeDtypeStruct((T, E), F32),
                   jax.ShapeDtypeStruct((T, E), jnp.int32)],
        compiler_params=_cparams("parallel"),
        name="moe_router",
    )(x2, mod, g.reshape(1, D), router_w, router_b.reshape(1, E))


def _moe_kernel(te_ref, nu_ref, xs_ref, w1_ref, w3_ref, w2_ref, o_ref, acc_ref):
    i = pl.program_id(0)
    f = pl.program_id(1)
    used = i < nu_ref[0]

    @pl.when(f == 0)
    def _():
        acc_ref[...] = jnp.zeros_like(acc_ref)

    @pl.when(used)
    def _():
        acc_ref[...] += _swiglu_partial(xs_ref[...], w1_ref[0], w3_ref[0], w2_ref[0])

    @pl.when(f == pl.num_programs(1) - 1)
    def _():
        o_ref[...] = acc_ref[...]


def _moe_grouped(xs, tile_expert, n_used, w1, w3, w2, tm):
    N, D = xs.shape
    F = w1.shape[2]
    nf = 2
    tf = F // nf
    n_tiles = N // tm
    row = lambda i, te, nu: jnp.minimum(i, nu[0] - 1)
    fcol = lambda i, f, nu: jnp.where(i < nu[0], f, nf - 1)
    grid_spec = pltpu.PrefetchScalarGridSpec(
        num_scalar_prefetch=2,
        grid=(n_tiles, nf),
        in_specs=[pl.BlockSpec((tm, D), lambda i, f, te, nu: (row(i, te, nu), 0)),
                  pl.BlockSpec((1, D, tf), lambda i, f, te, nu: (te[i], 0, fcol(i, f, nu))),
                  pl.BlockSpec((1, D, tf), lambda i, f, te, nu: (te[i], 0, fcol(i, f, nu))),
                  pl.BlockSpec((1, tf, D), lambda i, f, te, nu: (te[i], fcol(i, f, nu), 0))],
        out_specs=pl.BlockSpec((tm, D), lambda i, f, te, nu: (i, 0)),
        scratch_shapes=[pltpu.VMEM((tm, D), F32)],
    )
    return pl.pallas_call(
        _moe_kernel,
        grid_spec=grid_spec,
        out_shape=jax.ShapeDtypeStruct((N, D), F32),
        compiler_params=_cparams("arbitrary", "arbitrary"),
        name="moe_experts",
    )(tile_expert, n_used, xs, w1, w3, w2)


def _combine_kernel(x_ref, y1_ref, y2_ref, w_ref, mod_ref, o_ref):
    w = w_ref[...]
    y = w[:, 0:1] * y1_ref[...] + w[:, 1:2] * y2_ref[...]
    o_ref[...] = x_ref[...] + mod_ref[0][5:6] * y


def _moe_combine(x2, y1, y2, w12, mod, seq):
    T, D = x2.shape
    tm = ROW_TILE
    per_b = seq // tm
    return pl.pallas_call(
        _combine_kernel,
        grid=(T // tm,),
        in_specs=[pl.BlockSpec((tm, D), lambda i: (i, 0)),
                  pl.BlockSpec((tm, D), lambda i: (i, 0)),
                  pl.BlockSpec((tm, D), lambda i: (i, 0)),
                  pl.BlockSpec((tm, 2), lambda i: (i, 0)),
                  pl.BlockSpec((1, 6, D), lambda i: (i // per_b, 0, 0))],
        out_specs=pl.BlockSpec((tm, D), lambda i: (i, 0)),
        out_shape=jax.ShapeDtypeStruct((T, D), F32),
        compiler_params=_cparams("parallel"),
        name="moe_combine",
    )(x2, y1, y2, w12, mod)


def _moe_ffn(x2, mod, g, router_w, router_b, w1, w3, w2, seq):
    T, D = x2.shape
    E = router_w.shape[1]
    tm = ROW_TILE
    h, comb, sel = _router(x2, mod, g, router_w, router_b, seq)

    counts = jnp.sum(sel, axis=0)
    padded = ((counts + tm - 1) // tm) * tm
    group_end = jnp.cumsum(padded)
    group_start = group_end - padded
    rank = jnp.cumsum(sel, axis=0) - sel
    dest = group_start[None, :] + rank
    n_rows = TOP_K * T + E * tm
    tok = jnp.broadcast_to(jnp.arange(T, dtype=jnp.int32)[:, None], (T, E))
    src = jnp.zeros((n_rows,), jnp.int32).at[jnp.where(sel > 0, dest, n_rows)].set(tok, mode="drop")
    n_tiles = n_rows // tm
    n_used = (group_end[-1] // tm).astype(jnp.int32).reshape(1)
    tile_expert = jnp.searchsorted(group_end, jnp.arange(n_tiles, dtype=jnp.int32) * tm, side="right")
    tile_expert = jnp.minimum(tile_expert, jnp.max(jnp.where(counts > 0, jnp.arange(E), 0))).astype(jnp.int32)

    xs = jnp.take(h, src, axis=0)
    ys = _moe_grouped(xs, tile_expert, n_used, w1, w3, w2, tm)

    lane = jnp.arange(E, dtype=jnp.int32)[None, :]
    e_lo = jnp.min(jnp.where(sel > 0, lane, E), axis=1)
    e_hi = jnp.max(jnp.where(sel > 0, lane, -1), axis=1)
    pick = lambda a, e: jnp.take_along_axis(a, e[:, None], axis=1)[:, 0]
    w12 = jnp.stack([pick(comb, e_lo), pick(comb, e_hi)], axis=1)
    y1 = jnp.take(ys, pick(dest, e_lo), axis=0)
    y2 = jnp.take(ys, pick(dest, e_hi), axis=0)
    return _moe_combine(x2, y1, y2, w12, mod, seq)


def kernel(x, c, positions, ada_w, ada_b, norm1_g, norm2_g, w_in, gate_b, conv_w, conv_b, lru_wa, lru_ba,
           lru_wx, lru_bx, lru_lambda, diff_qn, diff_kn, diff_lq1, diff_lk1, diff_lq2, diff_lk2, diff_subln,
           moba_qn, moba_kn, w_br_a, w_br_b, w_br_c, w_out, ffn_w1, ffn_w3, ffn_w2, router_w, router_b,
           moe_w1, moe_w3, moe_w2):
    B, S, D = x.shape
    L = ada_w.shape[0]
    T = B * S
    assert S % MOBA_BLOCK == 0 and S // MOBA_BLOCK <= 56 and S % 1024 == 0
    x2 = x.reshape(T, D)
    mod_all = _ada_mod(c, ada_w, ada_b).reshape(L, B, 6, D)
    cos, sin = _rope_tables(positions)
    bf = lambda w: w.astype(BF16)
    tile2 = lambda v: jnp.tile(v, LANES // HEAD_DIM)

    for l in range(L):
        mod = mod_all[l]
        lam_init = 0.8 - 0.6 * math.exp(-0.3 * l)
        proj = _in_proj(x2, mod, norm1_g[l], bf(w_in[l]), S)
        gains = jnp.stack([tile2(diff_qn[l]), tile2(diff_kn[l]), tile2(moba_qn[l]), tile2(moba_kn[l])])
        qk = _qk_prep(proj, gains, cos, sin)
        y_a = _rg_lru_branch(proj, S, conv_w[l], conv_b[l], _pair_block_diag(lru_wa[l]), lru_ba[l],
                             _pair_block_diag(lru_wx[l]), lru_bx[l], lru_lambda[l])
        lam_params = jnp.stack([diff_lq1[l], diff_lk1[l], diff_lq2[l], diff_lk2[l]])
        y_b = _diff_attention(qk, proj, S, lam_params, diff_subln[l], lam_init)
        y_c = _moba_attention(qk, proj, S)
        x2 = _merge_out(x2, y_a, y_b, y_c, proj, gate_b[l], mod, bf(w_br_a[l]), bf(w_br_b[l]),
                        bf(w_br_c[l]), bf(w_out[l]), S)
        if l % 2 == 0:
            x2 = _dense_ffn(x2, mod, norm2_g[l], bf(ffn_w1[l // 2]), bf(ffn_w3[l // 2]), bf(ffn_w2[l // 2]), S)
        else:
            x2 = _moe_ffn(x2, mod, norm2_g[l], router_w[l // 2], router_b[l // 2], bf(moe_w1[l // 2]),
                          bf(moe_w3[l // 2]), bf(moe_w2[l // 2]), S)
    return x2.reshape(B, S, D)
```

```python
import functools
import math

import jax
import jax.numpy as jnp
from jax import lax
from jax.experimental import pallas as pl
from jax.experimental.pallas import tpu as pltpu

F32 = jnp.float32
BF16 = jnp.bfloat16
HIGHEST = lax.Precision.HIGHEST

HEAD_DIM = 64
ROPE_THETA = 10000.0
RNN_BLOCKS = 16
CONV_W = 4
LRU_C = 8.0
DIFF_HEADS = 4
MOBA_HEADS = 8
MOBA_BLOCK = 256
MOBA_TOPK = 3
N_BRANCH = 3
N_EXPERTS = 8
TOP_K = 2
EPS = 1e-6
NEG = -1e30

LANES = 128
VMEM_LIMIT = 56 * 1024 * 1024

ROW_TILE = 512


def _cparams(*sem):
    return pltpu.CompilerParams(dimension_semantics=sem, vmem_limit_bytes=VMEM_LIMIT)


def _modulated_norm(x, g, mod, base):
    ms = jnp.mean(x * x, axis=-1, keepdims=True)
    y = x * lax.rsqrt(ms + EPS) * g
    return y * (1.0 + mod[base + 1:base + 2]) + mod[base:base + 1]


def _dot_nt(a, b, **kw):
    return lax.dot_general(a, b, (((1,), (1,)), ((), ())), preferred_element_type=F32, **kw)


def _ada_kernel(c_ref, w_ref, b_ref, o_ref):
    o_ref[0] = jnp.dot(c_ref[...], w_ref[0], preferred_element_type=F32, precision=HIGHEST) + b_ref[0]


def _ada_mod(c, ada_w, ada_b):
    L, D, N = ada_w.shape
    B = c.shape[0]
    tn = 1536
    return pl.pallas_call(
        _ada_kernel,
        grid=(L, N // tn),
        in_specs=[pl.BlockSpec((B, D), lambda l, j: (0, 0)),
                  pl.BlockSpec((1, D, tn), lambda l, j: (l, 0, j)),
                  pl.BlockSpec((1, 1, tn), lambda l, j: (l, 0, j))],
        out_specs=pl.BlockSpec((1, B, tn), lambda l, j: (l, 0, j)),
        out_shape=jax.ShapeDtypeStruct((L, B, N), F32),
        compiler_params=_cparams("parallel", "parallel"),
        name="ada_mod",
    )(c, ada_w, ada_b.reshape(L, 1, N))


def _inproj_kernel(x_ref, mod_ref, g_ref, w_ref, o_ref, h_ref):
    @pl.when(pl.program_id(1) == 0)
    def _():
        h_ref[...] = _modulated_norm(x_ref[...], g_ref[...], mod_ref[0], 0).astype(BF16)

    o_ref[...] = jnp.dot(h_ref[...], w_ref[...], preferred_element_type=F32).astype(BF16)


def _in_proj(x2, mod, g, w_bf16, seq):
    T, D = x2.shape
    N = w_bf16.shape[1]
    tm, tn = 1024, 1024
    per_b = seq // tm
    return pl.pallas_call(
        _inproj_kernel,
        grid=(T // tm, N // tn),
        in_specs=[pl.BlockSpec((tm, D), lambda i, j: (i, 0)),
                  pl.BlockSpec((1, 6, D), lambda i, j: (i // per_b, 0, 0)),
                  pl.BlockSpec((1, D), lambda i, j: (0, 0)),
                  pl.BlockSpec((D, tn), lambda i, j: (0, j))],
        out_specs=pl.BlockSpec((tm, tn), lambda i, j: (i, j)),
        out_shape=jax.ShapeDtypeStruct((T, N), BF16),
        scratch_shapes=[pltpu.VMEM((tm, D), BF16)],
        compiler_params=_cparams("parallel", "arbitrary"),
        name="in_proj",
    )(x2, mod, g.reshape(1, D), w_bf16)


def _rope_kernel(pos_ref, inv_ref, sign_ref, cos_ref, sin_ref):
    ang = pos_ref[...] * inv_ref[...]
    cos_ref[...] = jnp.cos(ang)
    sin_ref[...] = jnp.sin(ang) * sign_ref[...]


def _rope_tables(positions):
    T = positions.size
    pos = positions.reshape(T, 1).astype(F32)
    inv = 1.0 / (ROPE_THETA ** (jnp.arange(0, HEAD_DIM, 2, dtype=F32) / HEAD_DIM))
    half = HEAD_DIM // 2
    inv128 = jnp.tile(inv, LANES // half).reshape(1, LANES)
    sign = jnp.tile(jnp.concatenate([-jnp.ones((half,), F32), jnp.ones((half,), F32)]),
                    LANES // HEAD_DIM).reshape(1, LANES)
    tm = 1024
    return pl.pallas_call(
        _rope_kernel,
        grid=(T // tm,),
        in_specs=[pl.BlockSpec((tm, 1), lambda i: (i, 0)),
                  pl.BlockSpec((1, LANES), lambda i: (0, 0)),
                  pl.BlockSpec((1, LANES), lambda i: (0, 0))],
        out_specs=[pl.BlockSpec((tm, LANES), lambda i: (i, 0))] * 2,
        out_shape=[jax.ShapeDtypeStruct((T, LANES), F32)] * 2,
        compiler_params=_cparams("parallel"),
        name="rope_tables",
    )(pos, inv128, sign)


def _qkprep_kernel(dq_ref, dk_ref, mq_ref, mk_ref, gain_ref, cos_ref, sin_ref, seg_ref, o_ref):
    cos = cos_ref[...]
    sin = sin_ref[...]
    seg = seg_ref[...]
    lane = lax.broadcasted_iota(jnp.int32, cos.shape, 1)
    first_half = (lane % HEAD_DIM) < (HEAD_DIM // 2)
    width = dq_ref.shape[1]
    for gi, ref in enumerate((dq_ref, dk_ref, mq_ref, mk_ref)):
        gain = gain_ref[gi:gi + 1, :]
        for cb in range(width // LANES):
            x = ref[:, cb * LANES:(cb + 1) * LANES].astype(F32)
            ms = jnp.dot(x * x, seg, preferred_element_type=F32, precision=HIGHEST)
            y = x * lax.rsqrt(ms + EPS) * gain
            swapped = jnp.where(first_half, pltpu.roll(y, LANES - HEAD_DIM // 2, 1),
                                pltpu.roll(y, HEAD_DIM // 2, 1))
            r = y * cos + swapped * sin
            if gi % 2 == 0:
                r = r * (1.0 / math.sqrt(HEAD_DIM))
            col = gi * width + cb * LANES
            o_ref[:, col:col + LANES] = r.astype(BF16)


def _qk_prep(proj, gains, cos, sin):
    T = proj.shape[0]
    width = DIFF_HEADS * 2 * HEAD_DIM
    seg = jnp.kron(jnp.eye(LANES // HEAD_DIM, dtype=F32),
                   jnp.full((HEAD_DIM, HEAD_DIM), 1.0 / HEAD_DIM, F32))
    tm = ROW_TILE
    col_blocks = (4, 5, 7, 8)
    in_specs = [pl.BlockSpec((tm, width), functools.partial(lambda i, c: (i, c), c=c)) for c in col_blocks]
    in_specs += [pl.BlockSpec((4, LANES), lambda i: (0, 0)),
                 pl.BlockSpec((tm, LANES), lambda i: (i, 0)),
                 pl.BlockSpec((tm, LANES), lambda i: (i, 0)),
                 pl.BlockSpec((LANES, LANES), lambda i: (0, 0))]
    return pl.pallas_call(
        _qkprep_kernel,
        grid=(T // tm,),
        in_specs=in_specs,
        out_specs=pl.BlockSpec((tm, 4 * width), lambda i: (i, 0)),
        out_shape=jax.ShapeDtypeStruct((T, 4 * width), BF16),
        compiler_params=_cparams("parallel"),
        name="qk_prep",
    )(proj, proj, proj, proj, gains, cos, sin, seg)


def _gelu_tanh(x):
    return 0.5 * x * (1.0 + jnp.tanh(math.sqrt(2.0 / math.pi) * (x + 0.044715 * x * x * x)))


def _rglru_kernel(x_ref, g_ref, cw_ref, cb_ref, wa_ref, ba_ref, wx_ref, bx_ref, lam_ref, o_ref,
                  xbuf, abuf, ubuf, hcar):
    tc, C = x_ref.shape
    HALO = 8

    @pl.when(pl.program_id(1) == 0)
    def _():
        xbuf[0:HALO, :] = jnp.zeros((HALO, C), F32)
        hcar[...] = jnp.zeros_like(hcar)

    xbuf[HALO:HALO + tc, :] = x_ref[...].astype(F32)
    xc = cb_ref[...] + cw_ref[CONV_W - 1:CONV_W, :] * xbuf[HALO:HALO + tc, :]
    for j in range(1, CONV_W):
        xc = xc + cw_ref[CONV_W - 1 - j:CONV_W - j, :] * xbuf[HALO - j:HALO - j + tc, :]
    xbuf[0:HALO, :] = xbuf[tc:tc + HALO, :]

    xcb = xc.astype(BF16)
    nblk = C // LANES
    ra = jnp.concatenate([jnp.dot(xcb[:, n * LANES:(n + 1) * LANES], wa_ref[n], preferred_element_type=F32)
                          for n in range(nblk)], axis=1)
    rx = jnp.concatenate([jnp.dot(xcb[:, n * LANES:(n + 1) * LANES], wx_ref[n], preferred_element_type=F32)
                          for n in range(nblk)], axis=1)
    r = jax.nn.sigmoid(ra + ba_ref[...])
    gi = jax.nn.sigmoid(rx + bx_ref[...])
    neg_lam = -lam_ref[...]
    softplus = jnp.maximum(neg_lam, 0.0) + jnp.log1p(jnp.exp(-jnp.abs(neg_lam)))
    log_a = (-LRU_C) * r * softplus
    a = jnp.exp(log_a)
    gap = 1.0 - a * a
    mult = jnp.where(gap > 0.0, gap * lax.rsqrt(gap), 0.0)
    abuf[...] = a
    ubuf[...] = mult * gi * xc

    def step(t8, h):
        base = pl.multiple_of(t8 * 8, 8)
        a8 = abuf[pl.ds(base, 8), :]
        u8 = ubuf[pl.ds(base, 8), :]
        rows = []
        for rr in range(8):
            h = a8[rr:rr + 1, :] * h + u8[rr:rr + 1, :]
            rows.append(h)
        ubuf[pl.ds(base, 8), :] = jnp.concatenate(rows, axis=0)
        return h

    h_last = lax.fori_loop(0, tc // 8, step, hcar[0:1, :])
    hcar[0:1, :] = h_last
    o_ref[...] = (_gelu_tanh(g_ref[...].astype(F32)) * ubuf[...]).astype(BF16)


def _rg_lru_branch(proj, seq, conv_w, conv_b, wa2, ba, wx2, bx, lam):
    T = proj.shape[0]
    C = conv_w.shape[1]
    B = T // seq
    tc = ROW_TILE
    per_b = seq // tc
    row = lambda v: v.reshape(1, C)
    full2 = lambda shape: pl.BlockSpec(shape, lambda b, t: (0,) * len(shape))
    return pl.pallas_call(
        _rglru_kernel,
        grid=(B, per_b),
        in_specs=[pl.BlockSpec((tc, C), lambda b, t: (b * per_b + t, 0)),
                  pl.BlockSpec((tc, C), lambda b, t: (b * per_b + t, 1)),
                  full2((CONV_W, C)), full2((1, C)),
                  full2(wa2.shape), full2((1, C)), full2(wx2.shape), full2((1, C)), full2((1, C))],
        out_specs=pl.BlockSpec((tc, C), lambda b, t: (b * per_b + t, 0)),
        out_shape=jax.ShapeDtypeStruct((T, C), BF16),
        scratch_shapes=[pltpu.VMEM((tc + 8, C), F32), pltpu.VMEM((tc, C), F32),
                        pltpu.VMEM((tc, C), F32), pltpu.VMEM((8, C), F32)],
        compiler_params=_cparams("parallel", "arbitrary"),
        name="rg_lru",
    )(proj, proj, conv_w, row(conv_b), wa2, row(ba), wx2, row(bx), row(lam))


def _pair_block_diag(w):
    n, d, _ = w.shape
    z = jnp.zeros((n // 2, 2, d, 2, d), w.dtype)
    z = z.at[:, 0, :, 0, :].set(w[0::2]).at[:, 1, :, 1, :].set(w[1::2])
    return z.reshape(n // 2, 2 * d, 2 * d).astype(BF16)


ATTN_TILE = 512


def _softmax_step(s, m_ref, acc_ref, v_aug):
    m_old = m_ref[...]
    m_new = jnp.maximum(m_old, jnp.max(s, axis=-1, keepdims=True))
    p = jnp.exp(s - pltpu.repeat(m_new, s.shape[1] // LANES, 1)).astype(BF16)
    alpha = pltpu.repeat(jnp.exp(m_old - m_new), acc_ref.shape[1] // LANES, 1)
    acc_ref[...] = alpha * acc_ref[...] + jnp.dot(p, v_aug, preferred_element_type=F32)
    m_ref[...] = m_new


def _causal_mask(t):
    return lax.broadcasted_iota(jnp.int32, (t, t), 1) <= lax.broadcasted_iota(jnp.int32, (t, t), 0)


def _with_ones(v):
    return jnp.concatenate([v, jnp.ones_like(v)], axis=1)


def _diffattn_kernel(q_ref, k_ref, v_ref, lam_ref, g_ref, o_ref, m1, acc1, m2, acc2, *, lam_init):
    i = pl.program_id(2)
    tq = q_ref.shape[0]
    q = q_ref[...]
    lane = lax.broadcasted_iota(jnp.int32, q.shape, 1)
    zero = jnp.zeros_like(q)
    q1 = jnp.where(lane < HEAD_DIM, q, zero)
    q2 = jnp.where(lane < HEAD_DIM, zero, q)
    for m_ref, acc_ref in ((m1, acc1), (m2, acc2)):
        m_ref[...] = jnp.full(m_ref.shape, NEG, F32)
        acc_ref[...] = jnp.zeros(acc_ref.shape, F32)

    def body(j, carry):
        start = pl.multiple_of(j * tq, tq)
        k = k_ref[pl.ds(start, tq), :]
        v_aug = _with_ones(v_ref[pl.ds(start, tq), :])
        _softmax_step(_dot_nt(q1, k), m1, acc1, v_aug)
        _softmax_step(_dot_nt(q2, k), m2, acc2, v_aug)
        return carry

    lax.fori_loop(0, i, body, 0)

    start = pl.multiple_of(i * tq, tq)
    k = k_ref[pl.ds(start, tq), :]
    v_aug = _with_ones(v_ref[pl.ds(start, tq), :])
    causal = _causal_mask(tq)
    _softmax_step(jnp.where(causal, _dot_nt(q1, k), NEG), m1, acc1, v_aug)
    _softmax_step(jnp.where(causal, _dot_nt(q2, k), NEG), m2, acc2, v_aug)

    lp = lam_ref[...]
    lam = (jnp.exp(jnp.sum(lp[0:1] * lp[1:2], axis=-1, keepdims=True))
           - jnp.exp(jnp.sum(lp[2:3] * lp[3:4], axis=-1, keepdims=True)) + lam_init)
    o = acc1[:, :LANES] / acc1[:, LANES:] - lam * (acc2[:, :LANES] / acc2[:, LANES:])
    ms = jnp.mean(o * o, axis=-1, keepdims=True)
    o_ref[...] = (o * lax.rsqrt(ms + EPS) * g_ref[...] * (1.0 - lam_init)).astype(BF16)


def _diff_attention(qk, proj, seq, lam_params, subln_g, lam_init):
    T = qk.shape[0]
    B = T // seq
    tq = ATTN_TILE
    nq = seq // tq
    H = DIFF_HEADS
    return pl.pallas_call(
        functools.partial(_diffattn_kernel, lam_init=lam_init),
        grid=(B, H, nq),
        in_specs=[pl.BlockSpec((tq, LANES), lambda b, h, i: (b * nq + i, h)),
                  pl.BlockSpec((seq, LANES), lambda b, h, i: (b, H + h)),
                  pl.BlockSpec((seq, LANES), lambda b, h, i: (b, 24 + h)),
                  pl.BlockSpec((4, HEAD_DIM), lambda b, h, i: (0, 0)),
                  pl.BlockSpec((1, LANES), lambda b, h, i: (0, 0))],
        out_specs=pl.BlockSpec((tq, LANES), lambda b, h, i: (b * nq + i, h)),
        out_shape=jax.ShapeDtypeStruct((T, H * LANES), BF16),
        scratch_shapes=[pltpu.VMEM((tq, LANES), F32), pltpu.VMEM((tq, 2 * LANES), F32),
                        pltpu.VMEM((tq, LANES), F32), pltpu.VMEM((tq, 2 * LANES), F32)],
        compiler_params=_cparams("parallel", "parallel", "arbitrary"),
        name="diff_attn",
    )(qk, qk, proj, lam_params, subln_g.reshape(1, LANES))


def _moba_kernel(q_ref, k_ref, v_ref, o_ref, kmean, m_a, acc_a, m_b, acc_b):
    i = pl.program_id(2)
    tq = q_ref.shape[0]
    nb = k_ref.shape[0] // MOBA_BLOCK
    lane_row = lax.broadcasted_iota(jnp.int32, (1, LANES), 1)

    @pl.when(i == 0)
    def _():
        kmean[...] = jnp.zeros_like(kmean)
        for blk in range(nb):
            mean = jnp.mean(k_ref[blk * MOBA_BLOCK:(blk + 1) * MOBA_BLOCK, :].astype(F32), axis=0, keepdims=True)
            kmean[blk:blk + 1, :] = jnp.where(lane_row < HEAD_DIM, mean, 0.0)
            kmean[HEAD_DIM + blk:HEAD_DIM + blk + 1, :] = jnp.where(lane_row < HEAD_DIM, 0.0, mean)

    q = q_ref[...]
    lane = lax.broadcasted_iota(jnp.int32, q.shape, 1)
    zero = jnp.zeros_like(q)
    q_a = jnp.where(lane < HEAD_DIM, q, zero)
    q_b = jnp.where(lane < HEAD_DIM, zero, q)

    row = lax.broadcasted_iota(jnp.int32, q.shape, 0)
    own = (i * tq + row) // MOBA_BLOCK
    blk = lane % HEAD_DIM
    gate = _dot_nt(q.astype(F32), kmean[...], precision=HIGHEST)
    past = blk < own
    gate = jnp.where(past, gate, NEG)
    rank = jnp.zeros(gate.shape, jnp.int32)
    for d in range(1, nb):
        lower = pltpu.roll(gate, d, 1)
        upper = pltpu.roll(gate, LANES - d, 1)
        rank = rank + (lower >= gate).astype(jnp.int32) + (upper > gate).astype(jnp.int32)
    allowed = (past & (rank < MOBA_TOPK)) | (blk == own)
    bias_a = jnp.where(allowed, 0.0, NEG)
    bias_b = pltpu.roll(bias_a, HEAD_DIM, 1)
    qa_aug = jnp.concatenate([q_a, bias_a.astype(BF16)], axis=1)
    qb_aug = jnp.concatenate([q_b, bias_b.astype(BF16)], axis=1)

    for m_ref, acc_ref in ((m_a, acc_a), (m_b, acc_b)):
        m_ref[...] = jnp.full(m_ref.shape, NEG, F32)
        acc_ref[...] = jnp.zeros(acc_ref.shape, F32)

    def keys(j):
        start = pl.multiple_of(j * tq, tq)
        key_blk = (j * tq + row) // MOBA_BLOCK
        k_aug = jnp.concatenate([k_ref[pl.ds(start, tq), :], (lane == key_blk).astype(BF16)], axis=1)
        return k_aug, _with_ones(v_ref[pl.ds(start, tq), :])

    k_aug, v_aug = keys(i)
    causal = _causal_mask(tq)
    _softmax_step(jnp.where(causal, _dot_nt(qa_aug, k_aug), NEG), m_a, acc_a, v_aug)
    _softmax_step(jnp.where(causal, _dot_nt(qb_aug, k_aug), NEG), m_b, acc_b, v_aug)

    def body(j, carry):
        k_aug, v_aug = keys(j)
        _softmax_step(_dot_nt(qa_aug, k_aug), m_a, acc_a, v_aug)
        _softmax_step(_dot_nt(qb_aug, k_aug), m_b, acc_b, v_aug)
        return carry

    lax.fori_loop(0, i, body, 0)

    o_a = acc_a[:, :LANES] / acc_a[:, LANES:]
    o_b = acc_b[:, :LANES] / acc_b[:, LANES:]
    o_ref[...] = jnp.where(lane < HEAD_DIM, o_a, o_b).astype(BF16)


def _moba_attention(qk, proj, seq):
    T = qk.shape[0]
    B = T // seq
    tq = ATTN_TILE
    nq = seq // tq
    HP = MOBA_HEADS // 2
    return pl.pallas_call(
        _moba_kernel,
        grid=(B, HP, nq),
        in_specs=[pl.BlockSpec((tq, LANES), lambda b, h, i: (b * nq + i, 8 + h)),
                  pl.BlockSpec((seq, LANES), lambda b, h, i: (b, 12 + h)),
                  pl.BlockSpec((seq, LANES), lambda b, h, i: (b, 36 + h))],
        out_specs=pl.BlockSpec((tq, LANES), lambda b, h, i: (b * nq + i, h)),
        out_shape=jax.ShapeDtypeStruct((T, HP * LANES), BF16),
        scratch_shapes=[pltpu.VMEM((LANES, LANES), F32),
                        pltpu.VMEM((tq, LANES), F32), pltpu.VMEM((tq, 2 * LANES), F32),
                        pltpu.VMEM((tq, LANES), F32), pltpu.VMEM((tq, 2 * LANES), F32)],
        compiler_params=_cparams("parallel", "parallel", "arbitrary"),
        name="moba_attn",
    )(qk, qk, proj)


def _merge_kernel(x_ref, ya_ref, yb_ref, yc_ref, ga_ref, gbb_ref, gc_ref, gb_ref, mod_ref,
                  wa_ref, wb_ref, wc_ref, wo_ref, o_ref):
    merged = None
    branches = ((ya_ref, wa_ref, ga_ref), (yb_ref, wb_ref, gbb_ref), (yc_ref, wc_ref, gc_ref))
    for n, (y_ref, w_ref, gl_ref) in enumerate(branches):
        gate = jax.nn.sigmoid(gl_ref[...].astype(F32) + gb_ref[n:n + 1, :])
        term = gate * jnp.dot(y_ref[...], w_ref[...], preferred_element_type=F32)
        merged = term if merged is None else merged + term
    mix = jnp.dot(merged.astype(BF16), wo_ref[...], preferred_element_type=F32)
    o_ref[...] = x_ref[...] + mod_ref[0][2:3] * mix


def _merge_out(x2, ya, yb, yc, proj, gate_b, mod, wa, wb, wc, wo, seq):
    T, D = x2.shape
    tm = ROW_TILE
    per_b = seq // tm
    gbr_block = 5120 // D
    assert gbr_block * D == 5120
    wspec = lambda w: pl.BlockSpec(w.shape, lambda i: (0, 0))
    return pl.pallas_call(
        _merge_kernel,
        grid=(T // tm,),
        in_specs=[pl.BlockSpec((tm, D), lambda i: (i, 0)),
                  pl.BlockSpec((tm, ya.shape[1]), lambda i: (i, 0)),
                  pl.BlockSpec((tm, yb.shape[1]), lambda i: (i, 0)),
                  pl.BlockSpec((tm, yc.shape[1]), lambda i: (i, 0)),
                  pl.BlockSpec((tm, D), lambda i: (i, gbr_block)),
                  pl.BlockSpec((tm, D), lambda i: (i, gbr_block + 1)),
                  pl.BlockSpec((tm, D), lambda i: (i, gbr_block + 2)),
                  pl.BlockSpec((N_BRANCH, D), lambda i: (0, 0)),
                  pl.BlockSpec((1, 6, D), lambda i: (i // per_b, 0, 0)),
                  wspec(wa), wspec(wb), wspec(wc), wspec(wo)],
        out_specs=pl.BlockSpec((tm, D), lambda i: (i, 0)),
        out_shape=jax.ShapeDtypeStruct((T, D), F32),
        compiler_params=_cparams("parallel"),
        name="merge_out",
    )(x2, ya, yb, yc, proj, proj, proj, gate_b, mod, wa, wb, wc, wo)


def _swiglu_partial(h, w1, w3, w2):
    g = jnp.dot(h, w1, preferred_element_type=F32)
    u = jnp.dot(h, w3, preferred_element_type=F32)
    act = (g * jax.nn.sigmoid(g) * u).astype(BF16)
    return jnp.dot(act, w2, preferred_element_type=F32)


def _ffn_kernel(x_ref, mod_ref, g_ref, w1_ref, w3_ref, w2_ref, o_ref, h_ref, acc_ref):
    f = pl.program_id(1)

    @pl.when(f == 0)
    def _():
        h_ref[...] = _modulated_norm(x_ref[...], g_ref[...], mod_ref[0], 3).astype(BF16)
        acc_ref[...] = jnp.zeros_like(acc_ref)

    acc_ref[...] += _swiglu_partial(h_ref[...], w1_ref[...], w3_ref[...], w2_ref[...])

    @pl.when(f == pl.num_programs(1) - 1)
    def _():
        o_ref[...] = x_ref[...] + mod_ref[0][5:6] * acc_ref[...]


def _dense_ffn(x2, mod, g, w1, w3, w2, seq):
    T, D = x2.shape
    F = w1.shape[1]
    tm = ROW_TILE
    nf = 2
    tf = F // nf
    per_b = seq // tm
    return pl.pallas_call(
        _ffn_kernel,
        grid=(T // tm, nf),
        in_specs=[pl.BlockSpec((tm, D), lambda i, f: (i, 0)),
                  pl.BlockSpec((1, 6, D), lambda i, f: (i // per_b, 0, 0)),
                  pl.BlockSpec((1, D), lambda i, f: (0, 0)),
                  pl.BlockSpec((D, tf), lambda i, f: (0, f)),
                  pl.BlockSpec((D, tf), lambda i, f: (0, f)),
                  pl.BlockSpec((tf, D), lambda i, f: (f, 0))],
        out_specs=pl.BlockSpec((tm, D), lambda i, f: (i, 0)),
        out_shape=jax.ShapeDtypeStruct((T, D), F32),
        scratch_shapes=[pltpu.VMEM((tm, D), BF16), pltpu.VMEM((tm, D), F32)],
        compiler_params=_cparams("parallel", "arbitrary"),
        name="dense_ffn",
    )(x2, mod, g.reshape(1, D), w1, w3, w2)


def _router_kernel(x_ref, mod_ref, g_ref, rw_ref, rb_ref, h_ref, comb_ref, sel_ref):
    h = _modulated_norm(x_ref[...], g_ref[...], mod_ref[0], 3)
    h_ref[...] = h
    logits = jnp.dot(h, rw_ref[...], preferred_element_type=F32, precision=HIGHEST) + rb_ref[...]
    E = logits.shape[1]
    lane = lax.broadcasted_iota(jnp.int32, logits.shape, 1)
    v1 = jnp.max(logits, axis=-1, keepdims=True)
    i1 = jnp.min(jnp.where(logits == v1, lane, E), axis=-1, keepdims=True)
    rest = jnp.where(lane == i1, -jnp.inf, logits)
    v2 = jnp.max(rest, axis=-1, keepdims=True)
    i2 = jnp.min(jnp.where(rest == v2, lane, E), axis=-1, keepdims=True)
    e2 = jnp.exp(v2 - v1)
    w1 = 1.0 / (1.0 + e2)
    w2 = e2 / (1.0 + e2)
    comb_ref[...] = jnp.where(lane == i1, w1, 0.0) + jnp.where(lane == i2, w2, 0.0)
    sel_ref[...] = ((lane == i1) | (lane == i2)).astype(jnp.int32)


def _router(x2, mod, g, router_w, router_b, seq):
    T, D = x2.shape
    E = router_w.shape[1]
    tm = ROW_TILE
    per_b = seq // tm
    return pl.pallas_call(
        _router_kernel,
        grid=(T // tm,),
        in_specs=[pl.BlockSpec((tm, D), lambda i: (i, 0)),
                  pl.BlockSpec((1, 6, D), lambda i: (i // per_b, 0, 0)),
                  pl.BlockSpec((1, D), lambda i: (0, 0)),
                  pl.BlockSpec((D, E), lambda i: (0, 0)),
                  pl.BlockSpec((1, E), lambda i: (0, 0))],
        out_specs=[pl.BlockSpec((tm, D), lambda i: (i, 0)),
                   pl.BlockSpec((tm, E), lambda i: (i, 0)),
                   pl.BlockSpec((tm, E), lambda i: (i, 0))],
        out_shape=[jax.ShapeDtypeStruct((T, D), F32),
                   jax.ShapeDtypeStruct((T, E), F32),
                   jax.ShapeDtypeStruct((T, E), jnp.int32)],
        compiler_params=_cparams("parallel"),
        name="moe_router",
    )(x2, mod, g.reshape(1, D), router_w, router_b.reshape(1, E))


def _moe_kernel(te_ref, nu_ref, xs_ref, w1_ref, w3_ref, w2_ref, o_ref, acc_ref):
    i = pl.program_id(0)
    f = pl.program_id(1)
    used = i < nu_ref[0]

    @pl.when(f == 0)
    def _():
        acc_ref[...] = jnp.zeros_like(acc_ref)

    @pl.when(used)
    def _():
        acc_ref[...] += _swiglu_partial(xs_ref[...].astype(BF16), w1_ref[0], w3_ref[0], w2_ref[0])

    @pl.when(f == pl.num_programs(1) - 1)
    def _():
        o_ref[...] = acc_ref[...]


def _moe_grouped(xs, tile_expert, n_used, w1, w3, w2, tm):
    N, D = xs.shape
    F = w1.shape[2]
    nf = 2
    tf = F // nf
    n_tiles = N // tm
    row = lambda i, te, nu: jnp.minimum(i, nu[0] - 1)
    fcol = lambda i, f, nu: jnp.where(i < nu[0], f, nf - 1)
    grid_spec = pltpu.PrefetchScalarGridSpec(
        num_scalar_prefetch=2,
        grid=(n_tiles, nf),
        in_specs=[pl.BlockSpec((tm, D), lambda i, f, te, nu: (row(i, te, nu), 0)),
                  pl.BlockSpec((1, D, tf), lambda i, f, te, nu: (te[i], 0, fcol(i, f, nu))),
                  pl.BlockSpec((1, D, tf), lambda i, f, te, nu: (te[i], 0, fcol(i, f, nu))),
                  pl.BlockSpec((1, tf, D), lambda i, f, te, nu: (te[i], fcol(i, f, nu), 0))],
        out_specs=pl.BlockSpec((tm, D), lambda i, f, te, nu: (i, 0)),
        scratch_shapes=[pltpu.VMEM((tm, D), F32)],
    )
    return pl.pallas_call(
        _moe_kernel,
        grid_spec=grid_spec,
        out_shape=jax.ShapeDtypeStruct((N, D), F32),
        compiler_params=_cparams("arbitrary", "arbitrary"),
        name="moe_experts",
    )(tile_expert, n_used, xs, w1, w3, w2)


def _combine_kernel(x_ref, y1_ref, y2_ref, w_ref, mod_ref, o_ref):
    w = w_ref[...]
    y = w[:, 0:1] * y1_ref[...] + w[:, 1:2] * y2_ref[...]
    o_ref[...] = x_ref[...] + mod_ref[0][5:6] * y


def _moe_combine(x2, y1, y2, w12, mod, seq):
    T, D = x2.shape
    tm = ROW_TILE
    per_b = seq // tm
    return pl.pallas_call(
        _combine_kernel,
        grid=(T // tm,),
        in_specs=[pl.BlockSpec((tm, D), lambda i: (i, 0)),
                  pl.BlockSpec((tm, D), lambda i: (i, 0)),
                  pl.BlockSpec((tm, D), lambda i: (i, 0)),
                  pl.BlockSpec((tm, 2), lambda i: (i, 0)),
                  pl.BlockSpec((1, 6, D), lambda i: (i // per_b, 0, 0))],
        out_specs=pl.BlockSpec((tm, D), lambda i: (i, 0)),
        out_shape=jax.ShapeDtypeStruct((T, D), F32),
        compiler_params=_cparams("parallel"),
        name="moe_combine",
    )(x2, y1, y2, w12, mod)


def _moe_ffn(x2, mod, g, router_w, router_b, w1, w3, w2, seq):
    T, D = x2.shape
    E = router_w.shape[1]
    tm = ROW_TILE
    h, comb, sel = _router(x2, mod, g, router_w, router_b, seq)

    counts = jnp.sum(sel, axis=0)
    padded = ((counts + tm - 1) // tm) * tm
    group_end = jnp.cumsum(padded)
    group_start = group_end - padded
    rank = jnp.cumsum(sel, axis=0) - sel
    dest = group_start[None, :] + rank
    n_rows = TOP_K * T + E * tm
    n_tiles = n_rows // tm
    n_used = (group_end[-1] // tm).astype(jnp.int32).reshape(1)
    tile_start = jnp.arange(n_tiles, dtype=jnp.int32) * tm
    tile_expert = jnp.sum((group_end[None, :] <= tile_start[:, None]).astype(jnp.int32), axis=1)
    tile_expert = jnp.minimum(tile_expert, jnp.max(jnp.where(counts > 0, jnp.arange(E), 0))).astype(jnp.int32)

    lane = jnp.arange(E, dtype=jnp.int32)[None, :]
    e_lo = jnp.min(jnp.where(sel > 0, lane, E), axis=1)
    e_hi = jnp.max(jnp.where(sel > 0, lane, -1), axis=1)
    pick = lambda a, e: jnp.sum(jnp.where(lane == e[:, None], a, 0), axis=1)
    w12 = jnp.stack([pick(comb, e_lo), pick(comb, e_hi)], axis=1)

    tok = jnp.arange(T, dtype=jnp.int32)
    tok_sorted = jnp.sort(jnp.concatenate([e_lo * T + tok, e_hi * T + tok])) % T
    row = jnp.arange(n_rows, dtype=jnp.int32)
    row_expert = jnp.repeat(tile_expert, tm)
    first_sorted = (jnp.cumsum(counts) - counts)[row_expert]
    src = tok_sorted[jnp.clip(first_sorted + row - group_start[row_expert], 0, TOP_K * T - 1)]

    take_rows = lambda a, idx: a.at[idx].get(mode="promise_in_bounds")
    ys = _moe_grouped(take_rows(h, src), tile_expert, n_used, w1, w3, w2, tm)
    y1 = take_rows(ys, pick(dest, e_lo))
    y2 = take_rows(ys, pick(dest, e_hi))
    return _moe_combine(x2, y1, y2, w12, mod, seq)


def kernel(x, c, positions, ada_w, ada_b, norm1_g, norm2_g, w_in, gate_b, conv_w, conv_b, lru_wa, lru_ba,
           lru_wx, lru_bx, lru_lambda, diff_qn, diff_kn, diff_lq1, diff_lk1, diff_lq2, diff_lk2, diff_subln,
           moba_qn, moba_kn, w_br_a, w_br_b, w_br_c, w_out, ffn_w1, ffn_w3, ffn_w2, router_w, router_b,
           moe_w1, moe_w3, moe_w2):
    B, S, D = x.shape
    L = ada_w.shape[0]
    T = B * S
    assert S % MOBA_BLOCK == 0 and S // MOBA_BLOCK <= 56 and S % 1024 == 0
    x2 = x.reshape(T, D)
    mod_all = _ada_mod(c, ada_w, ada_b).reshape(L, B, 6, D)
    cos, sin = _rope_tables(positions)
    bf = lambda w: w.astype(BF16)
    tile2 = lambda v: jnp.tile(v, LANES // HEAD_DIM)

    for l in range(L):
        mod = mod_all[l]
        lam_init = 0.8 - 0.6 * math.exp(-0.3 * l)
        proj = _in_proj(x2, mod, norm1_g[l], bf(w_in[l]), S)
        gains = jnp.stack([tile2(diff_qn[l]), tile2(diff_kn[l]), tile2(moba_qn[l]), tile2(moba_kn[l])])
        qk = _qk_prep(proj, gains, cos, sin)
        y_a = _rg_lru_branch(proj, S, conv_w[l], conv_b[l], _pair_block_diag(lru_wa[l]), lru_ba[l],
                             _pair_block_diag(lru_wx[l]), lru_bx[l], lru_lambda[l])
        lam_params = jnp.stack([diff_lq1[l], diff_lk1[l], diff_lq2[l], diff_lk2[l]])
        y_b = _diff_attention(qk, proj, S, lam_params, diff_subln[l], lam_init)
        y_c = _moba_attention(qk, proj, S)
        x2 = _merge_out(x2, y_a, y_b, y_c, proj, gate_b[l], mod, bf(w_br_a[l]), bf(w_br_b[l]),
                        bf(w_br_c[l]), bf(w_out[l]), S)
        if l % 2 == 0:
            x2 = _dense_ffn(x2, mod, norm2_g[l], bf(ffn_w1[l // 2]), bf(ffn_w3[l // 2]), bf(ffn_w2[l // 2]), S)
        else:
            x2 = _moe_ffn(x2, mod, norm2_g[l], router_w[l // 2], router_b[l // 2], bf(moe_w1[l // 2]),
                          bf(moe_w3[l // 2]), bf(moe_w2[l // 2]), S)
    return x2.reshape(B, S, D)
```

```python
import functools
import math

import jax
import jax.numpy as jnp
from jax import lax
from jax.experimental import pallas as pl
from jax.experimental.pallas import tpu as pltpu

F32 = jnp.float32
BF16 = jnp.bfloat16
HIGHEST = lax.Precision.HIGHEST

HEAD_DIM = 64
ROPE_THETA = 10000.0
RNN_BLOCKS = 16
CONV_W = 4
LRU_C = 8.0
DIFF_HEADS = 4
MOBA_HEADS = 8
MOBA_BLOCK = 256
MOBA_TOPK = 3
N_BRANCH = 3
N_EXPERTS = 8
TOP_K = 2
EPS = 1e-6
NEG = -1e30

LANES = 128
VMEM_LIMIT = 56 * 1024 * 1024

ROW_TILE = 512


def _cparams(*sem):
    return pltpu.CompilerParams(dimension_semantics=sem, vmem_limit_bytes=VMEM_LIMIT)


def _modulated_norm(x, g, mod, base):
    ms = jnp.mean(x * x, axis=-1, keepdims=True)
    y = x * lax.rsqrt(ms + EPS) * g
    return y * (1.0 + mod[base + 1:base + 2]) + mod[base:base + 1]


def _dot_nt(a, b, **kw):
    return lax.dot_general(a, b, (((1,), (1,)), ((), ())), preferred_element_type=F32, **kw)


def _ada_kernel(c_ref, w_ref, b_ref, o_ref):
    o_ref[0] = jnp.dot(c_ref[...], w_ref[0], preferred_element_type=F32, precision=HIGHEST) + b_ref[0]


def _ada_mod(c, ada_w, ada_b):
    L, D, N = ada_w.shape
    B = c.shape[0]
    tn = 1536
    return pl.pallas_call(
        _ada_kernel,
        grid=(L, N // tn),
        in_specs=[pl.BlockSpec((B, D), lambda l, j: (0, 0)),
                  pl.BlockSpec((1, D, tn), lambda l, j: (l, 0, j)),
                  pl.BlockSpec((1, 1, tn), lambda l, j: (l, 0, j))],
        out_specs=pl.BlockSpec((1, B, tn), lambda l, j: (l, 0, j)),
        out_shape=jax.ShapeDtypeStruct((L, B, N), F32),
        compiler_params=_cparams("parallel", "parallel"),
        name="ada_mod",
    )(c, ada_w, ada_b.reshape(L, 1, N))


def _inproj_kernel(x_ref, mod_ref, g_ref, w_ref, o_ref, h_ref):
    @pl.when(pl.program_id(1) == 0)
    def _():
        h_ref[...] = _modulated_norm(x_ref[...], g_ref[...], mod_ref[0], 0).astype(BF16)

    o_ref[...] = jnp.dot(h_ref[...], w_ref[...], preferred_element_type=F32).astype(BF16)


def _in_proj(x2, mod, g, w_bf16, seq):
    T, D = x2.shape
    N = w_bf16.shape[1]
    tm, tn = min(seq, 2048), 1024
    per_b = seq // tm
    return pl.pallas_call(
        _inproj_kernel,
        grid=(T // tm, N // tn),
        in_specs=[pl.BlockSpec((tm, D), lambda i, j: (i, 0)),
                  pl.BlockSpec((1, 6, D), lambda i, j: (i // per_b, 0, 0)),
                  pl.BlockSpec((1, D), lambda i, j: (0, 0)),
                  pl.BlockSpec((D, tn), lambda i, j: (0, j))],
        out_specs=pl.BlockSpec((tm, tn), lambda i, j: (i, j)),
        out_shape=jax.ShapeDtypeStruct((T, N), BF16),
        scratch_shapes=[pltpu.VMEM((tm, D), BF16)],
        compiler_params=_cparams("parallel", "arbitrary"),
        name="in_proj",
    )(x2, mod, g.reshape(1, D), w_bf16)


def _rope_kernel(pos_ref, inv_ref, sign_ref, cos_ref, sin_ref):
    ang = pos_ref[...] * inv_ref[...]
    cos_ref[...] = jnp.cos(ang)
    sin_ref[...] = jnp.sin(ang) * sign_ref[...]


def _rope_tables(positions):
    T = positions.size
    pos = positions.reshape(T, 1).astype(F32)
    inv = 1.0 / (ROPE_THETA ** (jnp.arange(0, HEAD_DIM, 2, dtype=F32) / HEAD_DIM))
    half = HEAD_DIM // 2
    inv128 = jnp.tile(inv, LANES // half).reshape(1, LANES)
    sign = jnp.tile(jnp.concatenate([-jnp.ones((half,), F32), jnp.ones((half,), F32)]),
                    LANES // HEAD_DIM).reshape(1, LANES)
    tm = 1024
    return pl.pallas_call(
        _rope_kernel,
        grid=(T // tm,),
        in_specs=[pl.BlockSpec((tm, 1), lambda i: (i, 0)),
                  pl.BlockSpec((1, LANES), lambda i: (0, 0)),
                  pl.BlockSpec((1, LANES), lambda i: (0, 0))],
        out_specs=[pl.BlockSpec((tm, LANES), lambda i: (i, 0))] * 2,
        out_shape=[jax.ShapeDtypeStruct((T, LANES), F32)] * 2,
        compiler_params=_cparams("parallel"),
        name="rope_tables",
    )(pos, inv128, sign)


def _qkprep_kernel(dq_ref, dk_ref, mq_ref, mk_ref, gain_ref, cos_ref, sin_ref, seg_ref, o_ref):
    cos = cos_ref[...]
    sin = sin_ref[...]
    seg = seg_ref[...]
    lane = lax.broadcasted_iota(jnp.int32, cos.shape, 1)
    first_half = (lane % HEAD_DIM) < (HEAD_DIM // 2)
    width = dq_ref.shape[1]
    for gi, ref in enumerate((dq_ref, dk_ref, mq_ref, mk_ref)):
        gain = gain_ref[gi:gi + 1, :]
        for cb in range(width // LANES):
            x = ref[:, cb * LANES:(cb + 1) * LANES].astype(F32)
            sq = x * x
            sq_hi = sq.astype(BF16)
            sq_lo = (sq - sq_hi.astype(F32)).astype(BF16)
            ms = (jnp.dot(sq_hi, seg, preferred_element_type=F32)
                  + jnp.dot(sq_lo, seg, preferred_element_type=F32))
            y = x * lax.rsqrt(ms + EPS) * gain
            swapped = jnp.where(first_half, pltpu.roll(y, LANES - HEAD_DIM // 2, 1),
                                pltpu.roll(y, HEAD_DIM // 2, 1))
            r = y * cos + swapped * sin
            if gi % 2 == 0:
                r = r * (1.0 / math.sqrt(HEAD_DIM))
            col = gi * width + cb * LANES
            o_ref[:, col:col + LANES] = r.astype(BF16)


def _qk_prep(proj, gains, cos, sin):
    T = proj.shape[0]
    width = DIFF_HEADS * 2 * HEAD_DIM
    seg = jnp.kron(jnp.eye(LANES // HEAD_DIM, dtype=F32),
                   jnp.full((HEAD_DIM, HEAD_DIM), 1.0 / HEAD_DIM, F32)).astype(BF16)
    tm = ROW_TILE
    col_blocks = (4, 5, 7, 8)
    in_specs = [pl.BlockSpec((tm, width), functools.partial(lambda i, c: (i, c), c=c)) for c in col_blocks]
    in_specs += [pl.BlockSpec((4, LANES), lambda i: (0, 0)),
                 pl.BlockSpec((tm, LANES), lambda i: (i, 0)),
                 pl.BlockSpec((tm, LANES), lambda i: (i, 0)),
                 pl.BlockSpec((LANES, LANES), lambda i: (0, 0))]
    return pl.pallas_call(
        _qkprep_kernel,
        grid=(T // tm,),
        in_specs=in_specs,
        out_specs=pl.BlockSpec((tm, 4 * width), lambda i: (i, 0)),
        out_shape=jax.ShapeDtypeStruct((T, 4 * width), BF16),
        compiler_params=_cparams("parallel"),
        name="qk_prep",
    )(proj, proj, proj, proj, gains, cos, sin, seg)


def _gelu_tanh(x):
    return 0.5 * x * (1.0 + jnp.tanh(math.sqrt(2.0 / math.pi) * (x + 0.044715 * x * x * x)))


def _rglru_kernel(x_ref, g_ref, cw_ref, cb_ref, wa_ref, ba_ref, wx_ref, bx_ref, lam_ref, o_ref,
                  xbuf, abuf, ubuf, hcar):
    tc, C = x_ref.shape
    HALO = 8

    @pl.when(pl.program_id(1) == 0)
    def _():
        xbuf[0:HALO, :] = jnp.zeros((HALO, C), F32)
        hcar[...] = jnp.zeros_like(hcar)

    xbuf[HALO:HALO + tc, :] = x_ref[...].astype(F32)
    xc = cb_ref[...] + cw_ref[CONV_W - 1:CONV_W, :] * xbuf[HALO:HALO + tc, :]
    for j in range(1, CONV_W):
        xc = xc + cw_ref[CONV_W - 1 - j:CONV_W - j, :] * xbuf[HALO - j:HALO - j + tc, :]
    xbuf[0:HALO, :] = xbuf[tc:tc + HALO, :]

    xcb = xc.astype(BF16)
    nblk = C // LANES
    ra = jnp.concatenate([jnp.dot(xcb[:, n * LANES:(n + 1) * LANES], wa_ref[n], preferred_element_type=F32)
                          for n in range(nblk)], axis=1)
    rx = jnp.concatenate([jnp.dot(xcb[:, n * LANES:(n + 1) * LANES], wx_ref[n], preferred_element_type=F32)
                          for n in range(nblk)], axis=1)
    r = jax.nn.sigmoid(ra + ba_ref[...])
    gi = jax.nn.sigmoid(rx + bx_ref[...])
    neg_lam = -lam_ref[...]
    softplus = jnp.maximum(neg_lam, 0.0) + jnp.log1p(jnp.exp(-jnp.abs(neg_lam)))
    log_a = (-LRU_C) * r * softplus
    a = jnp.exp(log_a)
    gap = 1.0 - a * a
    mult = jnp.where(gap > 0.0, gap * lax.rsqrt(gap), 0.0)
    abuf[...] = a
    ubuf[...] = mult * gi * xc

    def step(t8, h):
        base = pl.multiple_of(t8 * 8, 8)
        a8 = abuf[pl.ds(base, 8), :]
        u8 = ubuf[pl.ds(base, 8), :]
        rows = []
        for rr in range(8):
            h = a8[rr:rr + 1, :] * h + u8[rr:rr + 1, :]
            rows.append(h)
        ubuf[pl.ds(base, 8), :] = jnp.concatenate(rows, axis=0)
        return h

    h_last = lax.fori_loop(0, tc // 8, step, hcar[0:1, :])
    hcar[0:1, :] = h_last
    o_ref[...] = (_gelu_tanh(g_ref[...].astype(F32)) * ubuf[...]).astype(BF16)


def _rg_lru_branch(proj, seq, conv_w, conv_b, wa2, ba, wx2, bx, lam):
    T = proj.shape[0]
    C = conv_w.shape[1]
    B = T // seq
    tc = ROW_TILE
    per_b = seq // tc
    row = lambda v: v.reshape(1, C)
    full2 = lambda shape: pl.BlockSpec(shape, lambda b, t: (0,) * len(shape))
    return pl.pallas_call(
        _rglru_kernel,
        grid=(B, per_b),
        in_specs=[pl.BlockSpec((tc, C), lambda b, t: (b * per_b + t, 0)),
                  pl.BlockSpec((tc, C), lambda b, t: (b * per_b + t, 1)),
                  full2((CONV_W, C)), full2((1, C)),
                  full2(wa2.shape), full2((1, C)), full2(wx2.shape), full2((1, C)), full2((1, C))],
        out_specs=pl.BlockSpec((tc, C), lambda b, t: (b * per_b + t, 0)),
        out_shape=jax.ShapeDtypeStruct((T, C), BF16),
        scratch_shapes=[pltpu.VMEM((tc + 8, C), F32), pltpu.VMEM((tc, C), F32),
                        pltpu.VMEM((tc, C), F32), pltpu.VMEM((8, C), F32)],
        compiler_params=_cparams("parallel", "arbitrary"),
        name="rg_lru",
    )(proj, proj, conv_w, row(conv_b), wa2, row(ba), wx2, row(bx), row(lam))


def _pair_block_diag(w):
    n, d, _ = w.shape
    z = jnp.zeros((n // 2, 2, d, 2, d), w.dtype)
    z = z.at[:, 0, :, 0, :].set(w[0::2]).at[:, 1, :, 1, :].set(w[1::2])
    return z.reshape(n // 2, 2 * d, 2 * d).astype(BF16)


ATTN_TILE = 512


def _lane_tile(x, n):
    return jnp.concatenate([x] * n, axis=1)


def _softmax_update(s, state, v_aug):
    row_max = jnp.max(s, axis=-1, keepdims=True)
    if state is None:
        m_new = jnp.broadcast_to(row_max, (s.shape[0], LANES))
        p = jnp.exp(s - _lane_tile(m_new, s.shape[1] // LANES)).astype(BF16)
        return m_new, jnp.dot(p, v_aug, preferred_element_type=F32)
    m_old, acc = state
    m_new = jnp.maximum(m_old, row_max)
    p = jnp.exp(s - _lane_tile(m_new, s.shape[1] // LANES)).astype(BF16)
    alpha = _lane_tile(jnp.exp(m_old - m_new), acc.shape[1] // LANES)
    return m_new, alpha * acc + jnp.dot(p, v_aug, preferred_element_type=F32)


def _softmax_finish(state):
    _, acc = state
    return acc[:, :LANES] / acc[:, LANES:]


def _causal_mask(t):
    return lax.broadcasted_iota(jnp.int32, (t, t), 1) <= lax.broadcasted_iota(jnp.int32, (t, t), 0)


def _with_ones(v):
    return jnp.concatenate([v, jnp.ones_like(v)], axis=1)


def _diffattn_kernel(q_ref, k_ref, v_ref, lam_ref, g_ref, o_ref, *, lam_init):
    t = ATTN_TILE
    nq = q_ref.shape[0] // t
    lane = lax.broadcasted_iota(jnp.int32, (t, LANES), 1)
    causal = _causal_mask(t)
    lp = lam_ref[...]
    lam = (jnp.exp(jnp.sum(lp[0:1] * lp[1:2], axis=-1, keepdims=True))
           - jnp.exp(jnp.sum(lp[2:3] * lp[3:4], axis=-1, keepdims=True)) + lam_init)
    for i in range(nq):
        q = q_ref[i * t:(i + 1) * t, :]
        zero = jnp.zeros_like(q)
        q1 = jnp.where(lane < HEAD_DIM, q, zero)
        q2 = jnp.where(lane < HEAD_DIM, zero, q)
        st1 = st2 = None
        for j in range(i + 1):
            k = k_ref[j * t:(j + 1) * t, :]
            v_aug = _with_ones(v_ref[j * t:(j + 1) * t, :])
            s1 = _dot_nt(q1, k)
            s2 = _dot_nt(q2, k)
            if j == i:
                s1 = jnp.where(causal, s1, NEG)
                s2 = jnp.where(causal, s2, NEG)
            st1 = _softmax_update(s1, st1, v_aug)
            st2 = _softmax_update(s2, st2, v_aug)
        o = _softmax_finish(st1) - lam * _softmax_finish(st2)
        ms = jnp.mean(o * o, axis=-1, keepdims=True)
        o_ref[i * t:(i + 1) * t, :] = (o * lax.rsqrt(ms + EPS) * g_ref[...] * (1.0 - lam_init)).astype(BF16)


def _diff_attention(qk, proj, seq, lam_params, subln_g, lam_init):
    T = qk.shape[0]
    B = T // seq
    H = DIFF_HEADS
    return pl.pallas_call(
        functools.partial(_diffattn_kernel, lam_init=lam_init),
        grid=(B, H),
        in_specs=[pl.BlockSpec((seq, LANES), lambda b, h: (b, h)),
                  pl.BlockSpec((seq, LANES), lambda b, h: (b, H + h)),
                  pl.BlockSpec((seq, LANES), lambda b, h: (b, 24 + h)),
                  pl.BlockSpec((4, HEAD_DIM), lambda b, h: (0, 0)),
                  pl.BlockSpec((1, LANES), lambda b, h: (0, 0))],
        out_specs=pl.BlockSpec((seq, LANES), lambda b, h: (b, h)),
        out_shape=jax.ShapeDtypeStruct((T, H * LANES), BF16),
        compiler_params=_cparams("parallel", "parallel"),
        name="diff_attn",
    )(qk, qk, proj, lam_params, subln_g.reshape(1, LANES))


def _moba_kernel(q_ref, k_ref, v_ref, o_ref, kmean):
    t = ATTN_TILE
    nq = q_ref.shape[0] // t
    nb = k_ref.shape[0] // MOBA_BLOCK
    nb8 = -(-nb // 8) * 8
    lane_row = lax.broadcasted_iota(jnp.int32, (1, LANES), 1)

    kmean[...] = jnp.zeros_like(kmean)
    for blk in range(nb):
        mean = jnp.mean(k_ref[blk * MOBA_BLOCK:(blk + 1) * MOBA_BLOCK, :].astype(F32), axis=0, keepdims=True)
        kmean[blk:blk + 1, :] = jnp.where(lane_row < HEAD_DIM, mean, 0.0)
        kmean[HEAD_DIM + blk:HEAD_DIM + blk + 1, :] = jnp.where(lane_row < HEAD_DIM, 0.0, mean)

    lane = lax.broadcasted_iota(jnp.int32, (t, LANES), 1)
    row = lax.broadcasted_iota(jnp.int32, (t, LANES), 0)
    causal = _causal_mask(t)
    gate_blk = lax.broadcasted_iota(jnp.int32, (nb8, t), 0)
    gate_col = lax.broadcasted_iota(jnp.int32, (nb8, t), 1)
    unused_rows = jnp.full((HEAD_DIM - nb8, t), NEG, F32)

    def keys(j):
        key_blk = (j * t + row) // MOBA_BLOCK
        k_aug = jnp.concatenate([k_ref[j * t:(j + 1) * t, :], (lane == key_blk).astype(BF16)], axis=1)
        return k_aug, _with_ones(v_ref[j * t:(j + 1) * t, :])

    for i in range(nq):
        q = q_ref[i * t:(i + 1) * t, :]
        zero = jnp.zeros_like(q)
        q_a = jnp.where(lane < HEAD_DIM, q, zero)
        q_b = jnp.where(lane < HEAD_DIM, zero, q)

        gate = _dot_nt(kmean[...], q.astype(F32), precision=HIGHEST)
        own = (i * t + gate_col) // MOBA_BLOCK
        past = gate_blk < own
        last_past = min(nb, (i * t + t - 1) // MOBA_BLOCK)
        bias_rows = []
        for head in range(2):
            g = jnp.where(past, gate[head * HEAD_DIM:head * HEAD_DIM + nb8, :], NEG)
            rank = jnp.zeros(g.shape, jnp.int32)
            for jb in range(last_past):
                other = g[jb:jb + 1, :]
                beats = (other > g) | ((other == g) & (jb < gate_blk))
                rank = rank + beats.astype(jnp.int32)
            allowed = (past & (rank < MOBA_TOPK)) | (gate_blk == own)
            bias_rows += [jnp.where(allowed, 0.0, NEG), unused_rows]
        bias_a = jnp.concatenate(bias_rows, axis=0).T
        bias_b = pltpu.roll(bias_a, HEAD_DIM, 1)
        qa_aug = jnp.concatenate([q_a, bias_a.astype(BF16)], axis=1)
        qb_aug = jnp.concatenate([q_b, bias_b.astype(BF16)], axis=1)

        st_a = st_b = None
        for j in [i] + list(range(i)):
            k_aug, v_aug = keys(j)
            s_a = _dot_nt(qa_aug, k_aug)
            s_b = _dot_nt(qb_aug, k_aug)
            if j == i:
                s_a = jnp.where(causal, s_a, NEG)
                s_b = jnp.where(causal, s_b, NEG)
            st_a = _softmax_update(s_a, st_a, v_aug)
            st_b = _softmax_update(s_b, st_b, v_aug)
        o = jnp.where(lane < HEAD_DIM, _softmax_finish(st_a), _softmax_finish(st_b))
        o_ref[i * t:(i + 1) * t, :] = o.astype(BF16)


def _moba_attention(qk, proj, seq):
    T = qk.shape[0]
    B = T // seq
    HP = MOBA_HEADS // 2
    return pl.pallas_call(
        _moba_kernel,
        grid=(B, HP),
        in_specs=[pl.BlockSpec((seq, LANES), lambda b, h: (b, 8 + h)),
                  pl.BlockSpec((seq, LANES), lambda b, h: (b, 12 + h)),
                  pl.BlockSpec((seq, LANES), lambda b, h: (b, 36 + h))],
        out_specs=pl.BlockSpec((seq, LANES), lambda b, h: (b, h)),
        out_shape=jax.ShapeDtypeStruct((T, HP * LANES), BF16),
        scratch_shapes=[pltpu.VMEM((LANES, LANES), F32)],
        compiler_params=_cparams("parallel", "parallel"),
        name="moba_attn",
    )(qk, qk, proj)


def _merge_kernel(x_ref, ya_ref, yb_ref, yc_ref, ga_ref, gbb_ref, gc_ref, gb_ref, mod_ref,
                  wa_ref, wb_ref, wc_ref, wo_ref, o_ref):
    merged = None
    branches = ((ya_ref, wa_ref, ga_ref), (yb_ref, wb_ref, gbb_ref), (yc_ref, wc_ref, gc_ref))
    for n, (y_ref, w_ref, gl_ref) in enumerate(branches):
        gate = jax.nn.sigmoid(gl_ref[...].astype(F32) + gb_ref[n:n + 1, :])
        term = gate * jnp.dot(y_ref[...], w_ref[...], preferred_element_type=F32)
        merged = term if merged is None else merged + term
    mix = jnp.dot(merged.astype(BF16), wo_ref[...], preferred_element_type=F32)
    o_ref[...] = x_ref[...] + mod_ref[0][2:3] * mix


def _merge_out(x2, ya, yb, yc, proj, gate_b, mod, wa, wb, wc, wo, seq):
    T, D = x2.shape
    tm = ROW_TILE
    per_b = seq // tm
    gbr_block = 5120 // D
    assert gbr_block * D == 5120
    wspec = lambda w: pl.BlockSpec(w.shape, lambda i: (0, 0))
    return pl.pallas_call(
        _merge_kernel,
        grid=(T // tm,),
        in_specs=[pl.BlockSpec((tm, D), lambda i: (i, 0)),
                  pl.BlockSpec((tm, ya.shape[1]), lambda i: (i, 0)),
                  pl.BlockSpec((tm, yb.shape[1]), lambda i: (i, 0)),
                  pl.BlockSpec((tm, yc.shape[1]), lambda i: (i, 0)),
                  pl.BlockSpec((tm, D), lambda i: (i, gbr_block)),
                  pl.BlockSpec((tm, D), lambda i: (i, gbr_block + 1)),
                  pl.BlockSpec((tm, D), lambda i: (i, gbr_block + 2)),
                  pl.BlockSpec((N_BRANCH, D), lambda i: (0, 0)),
                  pl.BlockSpec((1, 6, D), lambda i: (i // per_b, 0, 0)),
                  wspec(wa), wspec(wb), wspec(wc), wspec(wo)],
        out_specs=pl.BlockSpec((tm, D), lambda i: (i, 0)),
        out_shape=jax.ShapeDtypeStruct((T, D), F32),
        compiler_params=_cparams("parallel"),
        name="merge_out",
    )(x2, ya, yb, yc, proj, proj, proj, gate_b, mod, wa, wb, wc, wo)


def _swiglu_partial(h, w1, w3, w2):
    g = jnp.dot(h, w1, preferred_element_type=F32)
    u = jnp.dot(h, w3, preferred_element_type=F32)
    act = (g * jax.nn.sigmoid(g) * u).astype(BF16)
    return jnp.dot(act, w2, preferred_element_type=F32)


def _ffn_kernel(x_ref, mod_ref, g_ref, w1_ref, w3_ref, w2_ref, o_ref, h_ref, acc_ref):
    f = pl.program_id(1)

    @pl.when(f == 0)
    def _():
        h_ref[...] = _modulated_norm(x_ref[...], g_ref[...], mod_ref[0], 3).astype(BF16)
        acc_ref[...] = jnp.zeros_like(acc_ref)

    acc_ref[...] += _swiglu_partial(h_ref[...], w1_ref[...], w3_ref[...], w2_ref[...])

    @pl.when(f == pl.num_programs(1) - 1)
    def _():
        o_ref[...] = x_ref[...] + mod_ref[0][5:6] * acc_ref[...]


def _dense_ffn(x2, mod, g, w1, w3, w2, seq):
    T, D = x2.shape
    F = w1.shape[1]
    tm = ROW_TILE
    nf = 2
    tf = F // nf
    per_b = seq // tm
    return pl.pallas_call(
        _ffn_kernel,
        grid=(T // tm, nf),
        in_specs=[pl.BlockSpec((tm, D), lambda i, f: (i, 0)),
                  pl.BlockSpec((1, 6, D), lambda i, f: (i // per_b, 0, 0)),
                  pl.BlockSpec((1, D), lambda i, f: (0, 0)),
                  pl.BlockSpec((D, tf), lambda i, f: (0, f)),
                  pl.BlockSpec((D, tf), lambda i, f: (0, f)),
                  pl.BlockSpec((tf, D), lambda i, f: (f, 0))],
        out_specs=pl.BlockSpec((tm, D), lambda i, f: (i, 0)),
        out_shape=jax.ShapeDtypeStruct((T, D), F32),
        scratch_shapes=[pltpu.VMEM((tm, D), BF16), pltpu.VMEM((tm, D), F32)],
        compiler_params=_cparams("parallel", "arbitrary"),
        name="dense_ffn",
    )(x2, mod, g.reshape(1, D), w1, w3, w2)


def _router_kernel(x_ref, mod_ref, g_ref, rw_ref, rb_ref, h_ref, comb_ref, sel_ref):
    h = _modulated_norm(x_ref[...], g_ref[...], mod_ref[0], 3)
    h_ref[...] = h
    logits = jnp.dot(h, rw_ref[...], preferred_element_type=F32, precision=HIGHEST) + rb_ref[...]
    E = logits.shape[1]
    lane = lax.broadcasted_iota(jnp.int32, logits.shape, 1)
    v1 = jnp.max(logits, axis=-1, keepdims=True)
    i1 = jnp.min(jnp.where(logits == v1, lane, E), axis=-1, keepdims=True)
    rest = jnp.where(lane == i1, -jnp.inf, logits)
    v2 = jnp.max(rest, axis=-1, keepdims=True)
    i2 = jnp.min(jnp.where(rest == v2, lane, E), axis=-1, keepdims=True)
    e2 = jnp.exp(v2 - v1)
    w1 = 1.0 / (1.0 + e2)
    w2 = e2 / (1.0 + e2)
    comb_ref[...] = jnp.where(lane == i1, w1, 0.0) + jnp.where(lane == i2, w2, 0.0)
    sel_ref[...] = ((lane == i1) | (lane == i2)).astype(jnp.int32)


def _router(x2, mod, g, router_w, router_b, seq):
    T, D = x2.shape
    E = router_w.shape[1]
    tm = ROW_TILE
    per_b = seq // tm
    return pl.pallas_call(
        _router_kernel,
        grid=(T // tm,),
        in_specs=[pl.BlockSpec((tm, D), lambda i: (i, 0)),
                  pl.BlockSpec((1, 6, D), lambda i: (i // per_b, 0, 0)),
                  pl.BlockSpec((1, D), lambda i: (0, 0)),
                  pl.BlockSpec((D, E), lambda i: (0, 0)),
                  pl.BlockSpec((1, E), lambda i: (0, 0))],
        out_specs=[pl.BlockSpec((tm, D), lambda i: (i, 0)),
                   pl.BlockSpec((tm, E), lambda i: (i, 0)),
                   pl.BlockSpec((tm, E), lambda i: (i, 0))],
        out_shape=[jax.ShapeDtypeStruct((T, D), F32),
                   jax.ShapeDtypeStruct((T, E), F32),
                   jax.ShapeDtypeStruct((T, E), jnp.int32)],
        compiler_params=_cparams("parallel"),
        name="moe_router",
    )(x2, mod, g.reshape(1, D), router_w, router_b.reshape(1, E))


def _moe_kernel(te_ref, nu_ref, xs_ref, w1_ref, w3_ref, w2_ref, o_ref, acc_ref):
    i = pl.program_id(0)
    f = pl.program_id(1)
    used = i < nu_ref[0]

    @pl.when(f == 0)
    def _():
        acc_ref[...] = jnp.zeros_like(acc_ref)

    @pl.when(used)
    def _():
        acc_ref[...] += _swiglu_partial(xs_ref[...].astype(BF16), w1_ref[0], w3_ref[0], w2_ref[0])

    @pl.when(f == pl.num_programs(1) - 1)
    def _():
        o_ref[...] = acc_ref[...]


def _moe_grouped(xs, tile_expert, n_used, w1, w3, w2, tm):
    N, D = xs.shape
    F = w1.shape[2]
    nf = 2
    tf = F // nf
    n_tiles = N // tm
    row = lambda i, te, nu: jnp.minimum(i, nu[0] - 1)
    fcol = lambda i, f, nu: jnp.where(i < nu[0], f, nf - 1)
    grid_spec = pltpu.PrefetchScalarGridSpec(
        num_scalar_prefetch=2,
        grid=(n_tiles, nf),
        in_specs=[pl.BlockSpec((tm, D), lambda i, f, te, nu: (row(i, te, nu), 0)),
                  pl.BlockSpec((1, D, tf), lambda i, f, te, nu: (te[i], 0, fcol(i, f, nu))),
                  pl.BlockSpec((1, D, tf), lambda i, f, te, nu: (te[i], 0, fcol(i, f, nu))),
                  pl.BlockSpec((1, tf, D), lambda i, f, te, nu: (te[i], fcol(i, f, nu), 0))],
        out_specs=pl.BlockSpec((tm, D), lambda i, f, te, nu: (i, 0)),
        scratch_shapes=[pltpu.VMEM((tm, D), F32)],
    )
    return pl.pallas_call(
        _moe_kernel,
        grid_spec=grid_spec,
        out_shape=jax.ShapeDtypeStruct((N, D), F32),
        compiler_params=_cparams("arbitrary", "arbitrary"),
        name="moe_experts",
    )(tile_expert, n_used, xs, w1, w3, w2)


def _combine_kernel(x_ref, y1_ref, y2_ref, w_ref, mod_ref, o_ref):
    w = w_ref[...]
    y = w[:, 0:1] * y1_ref[...] + w[:, 1:2] * y2_ref[...]
    o_ref[...] = x_ref[...] + mod_ref[0][5:6] * y


def _moe_combine(x2, y1, y2, w12, mod, seq):
    T, D = x2.shape
    tm = ROW_TILE
    per_b = seq // tm
    return pl.pallas_call(
        _combine_kernel,
        grid=(T // tm,),
        in_specs=[pl.BlockSpec((tm, D), lambda i: (i, 0)),
                  pl.BlockSpec((tm, D), lambda i: (i, 0)),
                  pl.BlockSpec((tm, D), lambda i: (i, 0)),
                  pl.BlockSpec((tm, 2), lambda i: (i, 0)),
                  pl.BlockSpec((1, 6, D), lambda i: (i // per_b, 0, 0))],
        out_specs=pl.BlockSpec((tm, D), lambda i: (i, 0)),
        out_shape=jax.ShapeDtypeStruct((T, D), F32),
        compiler_params=_cparams("parallel"),
        name="moe_combine",
    )(x2, y1, y2, w12, mod)


def _moe_ffn(x2, mod, g, router_w, router_b, w1, w3, w2, seq):
    T, D = x2.shape
    E = router_w.shape[1]
    tm = ROW_TILE
    h, comb, sel = _router(x2, mod, g, router_w, router_b, seq)

    counts = jnp.sum(sel, axis=0)
    padded = ((counts + tm - 1) // tm) * tm
    group_end = jnp.cumsum(padded)
    group_start = group_end - padded
    rank = jnp.cumsum(sel, axis=0) - sel
    dest = group_start[None, :] + rank
    n_rows = TOP_K * T + E * tm
    n_tiles = n_rows // tm
    n_used = (group_end[-1] // tm).astype(jnp.int32).reshape(1)
    tile_start = jnp.arange(n_tiles, dtype=jnp.int32) * tm
    tile_expert = jnp.sum((group_end[None, :] <= tile_start[:, None]).astype(jnp.int32), axis=1)
    tile_expert = jnp.minimum(tile_expert, jnp.max(jnp.where(counts > 0, jnp.arange(E), 0))).astype(jnp.int32)

    lane = jnp.arange(E, dtype=jnp.int32)[None, :]
    e_lo = jnp.min(jnp.where(sel > 0, lane, E), axis=1)
    e_hi = jnp.max(jnp.where(sel > 0, lane, -1), axis=1)
    pick = lambda a, e: jnp.sum(jnp.where(lane == e[:, None], a, 0), axis=1)
    w12 = jnp.stack([pick(comb, e_lo), pick(comb, e_hi)], axis=1)

    tok = jnp.arange(T, dtype=jnp.int32)
    tok_sorted = jnp.sort(jnp.concatenate([e_lo * T + tok, e_hi * T + tok])) % T
    row = jnp.arange(n_rows, dtype=jnp.int32)
    row_expert = jnp.repeat(tile_expert, tm)
    first_sorted = (jnp.cumsum(counts) - counts)[row_expert]
    src = tok_sorted[jnp.clip(first_sorted + row - group_start[row_expert], 0, TOP_K * T - 1)]

    take_rows = lambda a, idx: a.at[idx].get(mode="promise_in_bounds")
    ys = _moe_grouped(take_rows(h, src), tile_expert, n_used, w1, w3, w2, tm)
    y1 = take_rows(ys, pick(dest, e_lo))
    y2 = take_rows(ys, pick(dest, e_hi))
    return _moe_combine(x2, y1, y2, w12, mod, seq)


def kernel(x, c, positions, ada_w, ada_b, norm1_g, norm2_g, w_in, gate_b, conv_w, conv_b, lru_wa, lru_ba,
           lru_wx, lru_bx, lru_lambda, diff_qn, diff_kn, diff_lq1, diff_lk1, diff_lq2, diff_lk2, diff_subln,
           moba_qn, moba_kn, w_br_a, w_br_b, w_br_c, w_out, ffn_w1, ffn_w3, ffn_w2, router_w, router_b,
           moe_w1, moe_w3, moe_w2):
    B, S, D = x.shape
    L = ada_w.shape[0]
    T = B * S
    assert S % MOBA_BLOCK == 0 and S // MOBA_BLOCK <= 56 and S % 1024 == 0
    x2 = x.reshape(T, D)
    mod_all = _ada_mod(c, ada_w, ada_b).reshape(L, B, 6, D)
    cos, sin = _rope_tables(positions)
    bf = lambda w: w.astype(BF16)
    tile2 = lambda v: jnp.tile(v, LANES // HEAD_DIM)

    for l in range(L):
        mod = mod_all[l]
        lam_init = 0.8 - 0.6 * math.exp(-0.3 * l)
        proj = _in_proj(x2, mod, norm1_g[l], bf(w_in[l]), S)
        gains = jnp.stack([tile2(diff_qn[l]), tile2(diff_kn[l]), tile2(moba_qn[l]), tile2(moba_kn[l])])
        qk = _qk_prep(proj, gains, cos, sin)
        y_a = _rg_lru_branch(proj, S, conv_w[l], conv_b[l], _pair_block_diag(lru_wa[l]), lru_ba[l],
                             _pair_block_diag(lru_wx[l]), lru_bx[l], lru_lambda[l])
        lam_params = jnp.stack([diff_lq1[l], diff_lk1[l], diff_lq2[l], diff_lk2[l]])
        y_b = _diff_attention(qk, proj, S, lam_params, diff_subln[l], lam_init)
        y_c = _moba_attention(qk, proj, S)
        x2 = _merge_out(x2, y_a, y_b, y_c, proj, gate_b[l], mod, bf(w_br_a[l]), bf(w_br_b[l]),
                        bf(w_br_c[l]), bf(w_out[l]), S)
        if l % 2 == 0:
            x2 = _dense_ffn(x2, mod, norm2_g[l], bf(ffn_w1[l // 2]), bf(ffn_w3[l // 2]), bf(ffn_w2[l // 2]), S)
        else:
            x2 = _moe_ffn(x2, mod, norm2_g[l], router_w[l // 2], router_b[l // 2], bf(moe_w1[l // 2]),
                          bf(moe_w3[l // 2]), bf(moe_w2[l // 2]), S)
    return x2.reshape(B, S, D)
```

```python
import functools
import math

import jax
import jax.numpy as jnp
from jax import lax
from jax.experimental import pallas as pl
from jax.experimental.pallas import tpu as pltpu

F32 = jnp.float32
BF16 = jnp.bfloat16
HIGHEST = lax.Precision.HIGHEST

HEAD_DIM = 64
ROPE_THETA = 10000.0
RNN_BLOCKS = 16
CONV_W = 4
LRU_C = 8.0
DIFF_HEADS = 4
MOBA_HEADS = 8
MOBA_BLOCK = 256
MOBA_TOPK = 3
N_BRANCH = 3
N_EXPERTS = 8
TOP_K = 2
EPS = 1e-6
NEG = -1e30

LANES = 128
VMEM_LIMIT = 56 * 1024 * 1024

ROW_TILE = 512


def _cparams(*sem):
    return pltpu.CompilerParams(dimension_semantics=sem, vmem_limit_bytes=VMEM_LIMIT)


def _modulated_norm(x, g, mod, base):
    ms = jnp.mean(x * x, axis=-1, keepdims=True)
    y = x * lax.rsqrt(ms + EPS) * g
    return y * (1.0 + mod[base + 1:base + 2]) + mod[base:base + 1]


def _dot_nt(a, b, **kw):
    return lax.dot_general(a, b, (((1,), (1,)), ((), ())), preferred_element_type=F32, **kw)


def _ada_kernel(c_ref, w_ref, b_ref, o_ref):
    o_ref[0] = jnp.dot(c_ref[...], w_ref[0], preferred_element_type=F32, precision=HIGHEST) + b_ref[0]


def _ada_mod(c, ada_w, ada_b):
    L, D, N = ada_w.shape
    B = c.shape[0]
    tn = 1536
    return pl.pallas_call(
        _ada_kernel,
        grid=(L, N // tn),
        in_specs=[pl.BlockSpec((B, D), lambda l, j: (0, 0)),
                  pl.BlockSpec((1, D, tn), lambda l, j: (l, 0, j)),
                  pl.BlockSpec((1, 1, tn), lambda l, j: (l, 0, j))],
        out_specs=pl.BlockSpec((1, B, tn), lambda l, j: (l, 0, j)),
        out_shape=jax.ShapeDtypeStruct((L, B, N), F32),
        compiler_params=_cparams("parallel", "parallel"),
        name="ada_mod",
    )(c, ada_w, ada_b.reshape(L, 1, N))


def _inproj_kernel(x_ref, mod_ref, g_ref, w_ref, o_ref, h_ref):
    @pl.when(pl.program_id(1) == 0)
    def _():
        h_ref[...] = _modulated_norm(x_ref[...], g_ref[...], mod_ref[0], 0).astype(BF16)

    o_ref[...] = jnp.dot(h_ref[...], w_ref[...], preferred_element_type=F32).astype(BF16)


def _in_proj(x2, mod, g, w_bf16, seq):
    T, D = x2.shape
    N = w_bf16.shape[1]
    tm, tn = min(seq, 2048), 1024
    per_b = seq // tm
    return pl.pallas_call(
        _inproj_kernel,
        grid=(T // tm, N // tn),
        in_specs=[pl.BlockSpec((tm, D), lambda i, j: (i, 0)),
                  pl.BlockSpec((1, 6, D), lambda i, j: (i // per_b, 0, 0)),
                  pl.BlockSpec((1, D), lambda i, j: (0, 0)),
                  pl.BlockSpec((D, tn), lambda i, j: (0, j))],
        out_specs=pl.BlockSpec((tm, tn), lambda i, j: (i, j)),
        out_shape=jax.ShapeDtypeStruct((T, N), BF16),
        scratch_shapes=[pltpu.VMEM((tm, D), BF16)],
        compiler_params=_cparams("parallel", "arbitrary"),
        name="in_proj",
    )(x2, mod, g.reshape(1, D), w_bf16)


def _rope_kernel(pos_ref, inv_ref, sign_ref, cos_ref, sin_ref):
    ang = pos_ref[...] * inv_ref[...]
    cos_ref[...] = jnp.cos(ang)
    sin_ref[...] = jnp.sin(ang) * sign_ref[...]


def _rope_tables(positions):
    T = positions.size
    pos = positions.reshape(T, 1).astype(F32)
    inv = 1.0 / (ROPE_THETA ** (jnp.arange(0, HEAD_DIM, 2, dtype=F32) / HEAD_DIM))
    half = HEAD_DIM // 2
    inv128 = jnp.tile(inv, LANES // half).reshape(1, LANES)
    sign = jnp.tile(jnp.concatenate([-jnp.ones((half,), F32), jnp.ones((half,), F32)]),
                    LANES // HEAD_DIM).reshape(1, LANES)
    tm = 1024
    return pl.pallas_call(
        _rope_kernel,
        grid=(T // tm,),
        in_specs=[pl.BlockSpec((tm, 1), lambda i: (i, 0)),
                  pl.BlockSpec((1, LANES), lambda i: (0, 0)),
                  pl.BlockSpec((1, LANES), lambda i: (0, 0))],
        out_specs=[pl.BlockSpec((tm, LANES), lambda i: (i, 0))] * 2,
        out_shape=[jax.ShapeDtypeStruct((T, LANES), F32)] * 2,
        compiler_params=_cparams("parallel"),
        name="rope_tables",
    )(pos, inv128, sign)


def _qkprep_kernel(dq_ref, dk_ref, mq_ref, mk_ref, gain_ref, cos_ref, sin_ref, seg_ref, o_ref):
    cos = cos_ref[...]
    sin = sin_ref[...]
    seg = seg_ref[...]
    lane = lax.broadcasted_iota(jnp.int32, cos.shape, 1)
    first_half = (lane % HEAD_DIM) < (HEAD_DIM // 2)
    width = dq_ref.shape[1]
    for gi, ref in enumerate((dq_ref, dk_ref, mq_ref, mk_ref)):
        gain = gain_ref[gi:gi + 1, :]
        for cb in range(width // LANES):
            x = ref[:, cb * LANES:(cb + 1) * LANES].astype(F32)
            sq = x * x
            sq_hi = sq.astype(BF16)
            sq_lo = (sq - sq_hi.astype(F32)).astype(BF16)
            ms = (jnp.dot(sq_hi, seg, preferred_element_type=F32)
                  + jnp.dot(sq_lo, seg, preferred_element_type=F32))
            y = x * lax.rsqrt(ms + EPS) * gain
            swapped = jnp.where(first_half, pltpu.roll(y, LANES - HEAD_DIM // 2, 1),
                                pltpu.roll(y, HEAD_DIM // 2, 1))
            r = y * cos + swapped * sin
            if gi % 2 == 0:
                r = r * (1.0 / math.sqrt(HEAD_DIM))
            col = gi * width + cb * LANES
            o_ref[:, col:col + LANES] = r.astype(BF16)


def _qk_prep(proj, gains, cos, sin):
    T = proj.shape[0]
    width = DIFF_HEADS * 2 * HEAD_DIM
    seg = jnp.kron(jnp.eye(LANES // HEAD_DIM, dtype=F32),
                   jnp.full((HEAD_DIM, HEAD_DIM), 1.0 / HEAD_DIM, F32)).astype(BF16)
    tm = ROW_TILE
    col_blocks = (4, 5, 7, 8)
    in_specs = [pl.BlockSpec((tm, width), functools.partial(lambda i, c: (i, c), c=c)) for c in col_blocks]
    in_specs += [pl.BlockSpec((4, LANES), lambda i: (0, 0)),
                 pl.BlockSpec((tm, LANES), lambda i: (i, 0)),
                 pl.BlockSpec((tm, LANES), lambda i: (i, 0)),
                 pl.BlockSpec((LANES, LANES), lambda i: (0, 0))]
    return pl.pallas_call(
        _qkprep_kernel,
        grid=(T // tm,),
        in_specs=in_specs,
        out_specs=pl.BlockSpec((tm, 4 * width), lambda i: (i, 0)),
        out_shape=jax.ShapeDtypeStruct((T, 4 * width), BF16),
        compiler_params=_cparams("parallel"),
        name="qk_prep",
    )(proj, proj, proj, proj, gains, cos, sin, seg)


def _gelu_tanh(x):
    return 0.5 * x * (1.0 + jnp.tanh(math.sqrt(2.0 / math.pi) * (x + 0.044715 * x * x * x)))


def _rglru_kernel(x_ref, g_ref, cw_ref, cb_ref, wa_ref, ba_ref, wx_ref, bx_ref, lam_ref, o_ref,
                  xbuf, abuf, ubuf, hcar):
    tc, C = x_ref.shape
    HALO = 8

    @pl.when(pl.program_id(1) == 0)
    def _():
        xbuf[0:HALO, :] = jnp.zeros((HALO, C), F32)
        hcar[...] = jnp.zeros_like(hcar)

    xbuf[HALO:HALO + tc, :] = x_ref[...].astype(F32)
    xc = cb_ref[...] + cw_ref[CONV_W - 1:CONV_W, :] * xbuf[HALO:HALO + tc, :]
    for j in range(1, CONV_W):
        xc = xc + cw_ref[CONV_W - 1 - j:CONV_W - j, :] * xbuf[HALO - j:HALO - j + tc, :]
    xbuf[0:HALO, :] = xbuf[tc:tc + HALO, :]

    xcb = xc.astype(BF16)
    nblk = C // LANES
    ra = jnp.concatenate([jnp.dot(xcb[:, n * LANES:(n + 1) * LANES], wa_ref[n], preferred_element_type=F32)
                          for n in range(nblk)], axis=1)
    rx = jnp.concatenate([jnp.dot(xcb[:, n * LANES:(n + 1) * LANES], wx_ref[n], preferred_element_type=F32)
                          for n in range(nblk)], axis=1)
    r = jax.nn.sigmoid(ra + ba_ref[...])
    gi = jax.nn.sigmoid(rx + bx_ref[...])
    neg_lam = -lam_ref[...]
    softplus = jnp.maximum(neg_lam, 0.0) + jnp.log1p(jnp.exp(-jnp.abs(neg_lam)))
    log_a = (-LRU_C) * r * softplus
    a = jnp.exp(log_a)
    gap = 1.0 - a * a
    mult = jnp.where(gap > 0.0, gap * lax.rsqrt(gap), 0.0)
    abuf[...] = a
    ubuf[...] = mult * gi * xc

    def step(t8, h):
        base = pl.multiple_of(t8 * 8, 8)
        a8 = abuf[pl.ds(base, 8), :]
        u8 = ubuf[pl.ds(base, 8), :]
        rows = []
        for rr in range(8):
            h = a8[rr:rr + 1, :] * h + u8[rr:rr + 1, :]
            rows.append(h)
        ubuf[pl.ds(base, 8), :] = jnp.concatenate(rows, axis=0)
        return h

    h_last = lax.fori_loop(0, tc // 8, step, hcar[0:1, :])
    hcar[0:1, :] = h_last
    o_ref[...] = (_gelu_tanh(g_ref[...].astype(F32)) * ubuf[...]).astype(BF16)


def _rg_lru_branch(proj, seq, conv_w, conv_b, wa2, ba, wx2, bx, lam):
    T = proj.shape[0]
    C = conv_w.shape[1]
    B = T // seq
    tc = ROW_TILE
    per_b = seq // tc
    row = lambda v: v.reshape(1, C)
    full2 = lambda shape: pl.BlockSpec(shape, lambda b, t: (0,) * len(shape))
    return pl.pallas_call(
        _rglru_kernel,
        grid=(B, per_b),
        in_specs=[pl.BlockSpec((tc, C), lambda b, t: (b * per_b + t, 0)),
                  pl.BlockSpec((tc, C), lambda b, t: (b * per_b + t, 1)),
                  full2((CONV_W, C)), full2((1, C)),
                  full2(wa2.shape), full2((1, C)), full2(wx2.shape), full2((1, C)), full2((1, C))],
        out_specs=pl.BlockSpec((tc, C), lambda b, t: (b * per_b + t, 0)),
        out_shape=jax.ShapeDtypeStruct((T, C), BF16),
        scratch_shapes=[pltpu.VMEM((tc + 8, C), F32), pltpu.VMEM((tc, C), F32),
                        pltpu.VMEM((tc, C), F32), pltpu.VMEM((8, C), F32)],
        compiler_params=_cparams("parallel", "arbitrary"),
        name="rg_lru",
    )(proj, proj, conv_w, row(conv_b), wa2, row(ba), wx2, row(bx), row(lam))


def _pair_block_diag(w):
    n, d, _ = w.shape
    z = jnp.zeros((n // 2, 2, d, 2, d), w.dtype)
    z = z.at[:, 0, :, 0, :].set(w[0::2]).at[:, 1, :, 1, :].set(w[1::2])
    return z.reshape(n // 2, 2 * d, 2 * d).astype(BF16)


ATTN_TILE = 512


def _lane_tile(x, n):
    return jnp.concatenate([x] * n, axis=1)


def _softmax_update(s, state, v_aug):
    row_max = jnp.max(s, axis=-1, keepdims=True)
    if state is None:
        m_new = jnp.broadcast_to(row_max, (s.shape[0], LANES))
        p = jnp.exp(s - _lane_tile(m_new, s.shape[1] // LANES)).astype(BF16)
        return m_new, jnp.dot(p, v_aug, preferred_element_type=F32)
    m_old, acc = state
    m_new = jnp.maximum(m_old, row_max)
    p = jnp.exp(s - _lane_tile(m_new, s.shape[1] // LANES)).astype(BF16)
    alpha = _lane_tile(jnp.exp(m_old - m_new), acc.shape[1] // LANES)
    return m_new, alpha * acc + jnp.dot(p, v_aug, preferred_element_type=F32)


def _softmax_finish(state):
    _, acc = state
    return acc[:, :LANES] / acc[:, LANES:]


def _causal_mask(t):
    return lax.broadcasted_iota(jnp.int32, (t, t), 1) <= lax.broadcasted_iota(jnp.int32, (t, t), 0)


def _with_ones(v):
    return jnp.concatenate([v, jnp.ones_like(v)], axis=1)


def _diffattn_kernel(q_ref, k_ref, v_ref, lam_ref, g_ref, o_ref, *, lam_init):
    t = ATTN_TILE
    nq = q_ref.shape[0] // t
    lane = lax.broadcasted_iota(jnp.int32, (t, LANES), 1)
    causal = _causal_mask(t)
    lp = lam_ref[...]
    lam = (jnp.exp(jnp.sum(lp[0:1] * lp[1:2], axis=-1, keepdims=True))
           - jnp.exp(jnp.sum(lp[2:3] * lp[3:4], axis=-1, keepdims=True)) + lam_init)
    for i in range(nq):
        q = q_ref[i * t:(i + 1) * t, :]
        zero = jnp.zeros_like(q)
        q1 = jnp.where(lane < HEAD_DIM, q, zero)
        q2 = jnp.where(lane < HEAD_DIM, zero, q)
        st1 = st2 = None
        for j in range(i + 1):
            k = k_ref[j * t:(j + 1) * t, :]
            v_aug = _with_ones(v_ref[j * t:(j + 1) * t, :])
            s1 = _dot_nt(q1, k)
            s2 = _dot_nt(q2, k)
            if j == i:
                s1 = jnp.where(causal, s1, NEG)
                s2 = jnp.where(causal, s2, NEG)
            st1 = _softmax_update(s1, st1, v_aug)
            st2 = _softmax_update(s2, st2, v_aug)
        o = _softmax_finish(st1) - lam * _softmax_finish(st2)
        ms = jnp.mean(o * o, axis=-1, keepdims=True)
        o_ref[i * t:(i + 1) * t, :] = (o * lax.rsqrt(ms + EPS) * g_ref[...] * (1.0 - lam_init)).astype(BF16)


def _diff_attention(qk, proj, seq, lam_params, subln_g, lam_init):
    T = qk.shape[0]
    B = T // seq
    H = DIFF_HEADS
    return pl.pallas_call(
        functools.partial(_diffattn_kernel, lam_init=lam_init),
        grid=(B, H),
        in_specs=[pl.BlockSpec((seq, LANES), lambda b, h: (b, h)),
                  pl.BlockSpec((seq, LANES), lambda b, h: (b, H + h)),
                  pl.BlockSpec((seq, LANES), lambda b, h: (b, 24 + h)),
                  pl.BlockSpec((4, HEAD_DIM), lambda b, h: (0, 0)),
                  pl.BlockSpec((1, LANES), lambda b, h: (0, 0))],
        out_specs=pl.BlockSpec((seq, LANES), lambda b, h: (b, h)),
        out_shape=jax.ShapeDtypeStruct((T, H * LANES), BF16),
        compiler_params=_cparams("parallel", "parallel"),
        name="diff_attn",
    )(qk, qk, proj, lam_params, subln_g.reshape(1, LANES))


def _moba_kernel(q_ref, k_ref, v_ref, o_ref, kmean):
    t = ATTN_TILE
    nq = q_ref.shape[0] // t
    nb = k_ref.shape[0] // MOBA_BLOCK
    nb8 = -(-nb // 8) * 8
    lane_row = lax.broadcasted_iota(jnp.int32, (1, LANES), 1)

    kmean[...] = jnp.zeros_like(kmean)
    for blk in range(nb):
        mean = jnp.mean(k_ref[blk * MOBA_BLOCK:(blk + 1) * MOBA_BLOCK, :].astype(F32), axis=0, keepdims=True)
        kmean[blk:blk + 1, :] = jnp.where(lane_row < HEAD_DIM, mean, 0.0)
        kmean[HEAD_DIM + blk:HEAD_DIM + blk + 1, :] = jnp.where(lane_row < HEAD_DIM, 0.0, mean)

    lane = lax.broadcasted_iota(jnp.int32, (t, LANES), 1)
    row = lax.broadcasted_iota(jnp.int32, (t, LANES), 0)
    causal = _causal_mask(t)
    gate_blk = lax.broadcasted_iota(jnp.int32, (nb8, t), 0)
    gate_col = lax.broadcasted_iota(jnp.int32, (nb8, t), 1)
    unused_rows = jnp.full((HEAD_DIM - nb8, t), NEG, F32)

    def keys(j):
        key_blk = (j * t + row) // MOBA_BLOCK
        k_aug = jnp.concatenate([k_ref[j * t:(j + 1) * t, :], (lane == key_blk).astype(BF16)], axis=1)
        return k_aug, _with_ones(v_ref[j * t:(j + 1) * t, :])

    for i in range(nq):
        q = q_ref[i * t:(i + 1) * t, :]
        zero = jnp.zeros_like(q)
        q_a = jnp.where(lane < HEAD_DIM, q, zero)
        q_b = jnp.where(lane < HEAD_DIM, zero, q)

        gate = _dot_nt(kmean[...], q.astype(F32), precision=HIGHEST)
        own = (i * t + gate_col) // MOBA_BLOCK
        past = gate_blk < own
        last_past = min(nb, (i * t + t - 1) // MOBA_BLOCK)
        bias_rows = []
        for head in range(2):
            g = jnp.where(past, gate[head * HEAD_DIM:head * HEAD_DIM + nb8, :], NEG)
            rank = jnp.zeros(g.shape, jnp.int32)
            for jb in range(last_past):
                other = g[jb:jb + 1, :]
                beats = (other > g) | ((other == g) & (jb < gate_blk))
                rank = rank + beats.astype(jnp.int32)
            allowed = (past & (rank < MOBA_TOPK)) | (gate_blk == own)
            bias_rows += [jnp.where(allowed, 0.0, NEG), unused_rows]
        bias_a = jnp.concatenate(bias_rows, axis=0).T
        bias_b = pltpu.roll(bias_a, HEAD_DIM, 1)
        qa_aug = jnp.concatenate([q_a, bias_a.astype(BF16)], axis=1)
        qb_aug = jnp.concatenate([q_b, bias_b.astype(BF16)], axis=1)

        st_a = st_b = None
        for j in [i] + list(range(i)):
            k_aug, v_aug = keys(j)
            s_a = _dot_nt(qa_aug, k_aug)
            s_b = _dot_nt(qb_aug, k_aug)
            if j == i:
                s_a = jnp.where(causal, s_a, NEG)
                s_b = jnp.where(causal, s_b, NEG)
            st_a = _softmax_update(s_a, st_a, v_aug)
            st_b = _softmax_update(s_b, st_b, v_aug)
        o = jnp.where(lane < HEAD_DIM, _softmax_finish(st_a), _softmax_finish(st_b))
        o_ref[i * t:(i + 1) * t, :] = o.astype(BF16)


def _moba_attention(qk, proj, seq):
    T = qk.shape[0]
    B = T // seq
    HP = MOBA_HEADS // 2
    return pl.pallas_call(
        _moba_kernel,
        grid=(B, HP),
        in_specs=[pl.BlockSpec((seq, LANES), lambda b, h: (b, 8 + h)),
                  pl.BlockSpec((seq, LANES), lambda b, h: (b, 12 + h)),
                  pl.BlockSpec((seq, LANES), lambda b, h: (b, 36 + h))],
        out_specs=pl.BlockSpec((seq, LANES), lambda b, h: (b, h)),
        out_shape=jax.ShapeDtypeStruct((T, HP * LANES), BF16),
        scratch_shapes=[pltpu.VMEM((LANES, LANES), F32)],
        compiler_params=_cparams("parallel", "parallel"),
        name="moba_attn",
    )(qk, qk, proj)


def _merge_kernel(x_ref, ya_ref, yb_ref, yc_ref, ga_ref, gbb_ref, gc_ref, gb_ref, mod_ref,
                  wa_ref, wb_ref, wc_ref, wo_ref, o_ref):
    merged = None
    branches = ((ya_ref, wa_ref, ga_ref), (yb_ref, wb_ref, gbb_ref), (yc_ref, wc_ref, gc_ref))
    for n, (y_ref, w_ref, gl_ref) in enumerate(branches):
        gate = jax.nn.sigmoid(gl_ref[...].astype(F32) + gb_ref[n:n + 1, :])
        term = gate * jnp.dot(y_ref[...], w_ref[...], preferred_element_type=F32)
        merged = term if merged is None else merged + term
    mix = jnp.dot(merged.astype(BF16), wo_ref[...], preferred_element_type=F32)
    o_ref[...] = x_ref[...] + mod_ref[0][2:3] * mix


def _merge_out(x2, ya, yb, yc, proj, gate_b, mod, wa, wb, wc, wo, seq):
    T, D = x2.shape
    tm = ROW_TILE
    per_b = seq // tm
    gbr_block = 5120 // D
    assert gbr_block * D == 5120
    wspec = lambda w: pl.BlockSpec(w.shape, lambda i: (0, 0))
    return pl.pallas_call(
        _merge_kernel,
        grid=(T // tm,),
        in_specs=[pl.BlockSpec((tm, D), lambda i: (i, 0)),
                  pl.BlockSpec((tm, ya.shape[1]), lambda i: (i, 0)),
                  pl.BlockSpec((tm, yb.shape[1]), lambda i: (i, 0)),
                  pl.BlockSpec((tm, yc.shape[1]), lambda i: (i, 0)),
                  pl.BlockSpec((tm, D), lambda i: (i, gbr_block)),
                  pl.BlockSpec((tm, D), lambda i: (i, gbr_block + 1)),
                  pl.BlockSpec((tm, D), lambda i: (i, gbr_block + 2)),
                  pl.BlockSpec((N_BRANCH, D), lambda i: (0, 0)),
                  pl.BlockSpec((1, 6, D), lambda i: (i // per_b, 0, 0)),
                  wspec(wa), wspec(wb), wspec(wc), wspec(wo)],
        out_specs=pl.BlockSpec((tm, D), lambda i: (i, 0)),
        out_shape=jax.ShapeDtypeStruct((T, D), F32),
        compiler_params=_cparams("parallel"),
        name="merge_out",
    )(x2, ya, yb, yc, proj, proj, proj, gate_b, mod, wa, wb, wc, wo)


FF_CHUNK = 256


def _stash_weight_chunk(c, w1_ref, w3_ref, w2_ref, wb1, wb3, wb2):
    wb1[c] = w1_ref[...].reshape(wb1.shape[1:]).astype(BF16)
    wb3[c] = w3_ref[...].reshape(wb3.shape[1:]).astype(BF16)
    wb2[c] = w2_ref[...].reshape(wb2.shape[1:]).astype(BF16)


def _swiglu_resident(h, wb1, wb3, wb2):
    acc = None
    for f in range(wb1.shape[0]):
        g = jnp.dot(h, wb1[f], preferred_element_type=F32)
        u = jnp.dot(h, wb3[f], preferred_element_type=F32)
        act = (g * jax.nn.sigmoid(g) * u).astype(BF16)
        part = jnp.dot(act, wb2[f], preferred_element_type=F32)
        acc = part if acc is None else acc + part
    return acc


def _ffn_kernel(x_ref, mod_ref, g_ref, w1_ref, w3_ref, w2_ref, o_ref, wb1, wb3, wb2):
    s = pl.program_id(0)
    nc = wb1.shape[0]

    @pl.when(s < nc)
    def _():
        _stash_weight_chunk(s, w1_ref, w3_ref, w2_ref, wb1, wb3, wb2)

    @pl.when(s >= nc)
    def _():
        h = _modulated_norm(x_ref[...], g_ref[...], mod_ref[0], 3).astype(BF16)
        o_ref[...] = x_ref[...] + mod_ref[0][5:6] * _swiglu_resident(h, wb1, wb3, wb2)


def _dense_ffn(x2, mod, g, w1, w3, w2, seq):
    T, D = x2.shape
    F = w1.shape[1]
    tm = ROW_TILE
    tf = FF_CHUNK
    nc = F // tf
    per_b = seq // tm
    chunk = lambda s: jnp.minimum(s, nc - 1)
    tile = lambda s: jnp.maximum(s - nc, 0)
    return pl.pallas_call(
        _ffn_kernel,
        grid=(nc + T // tm,),
        in_specs=[pl.BlockSpec((tm, D), lambda s: (tile(s), 0)),
                  pl.BlockSpec((1, 6, D), lambda s: (tile(s) // per_b, 0, 0)),
                  pl.BlockSpec((1, D), lambda s: (0, 0)),
                  pl.BlockSpec((D, tf), lambda s: (0, chunk(s))),
                  pl.BlockSpec((D, tf), lambda s: (0, chunk(s))),
                  pl.BlockSpec((tf, D), lambda s: (chunk(s), 0))],
        out_specs=pl.BlockSpec((tm, D), lambda s: (tile(s), 0)),
        out_shape=jax.ShapeDtypeStruct((T, D), F32),
        scratch_shapes=[pltpu.VMEM((nc, D, tf), BF16), pltpu.VMEM((nc, D, tf), BF16),
                        pltpu.VMEM((nc, tf, D), BF16)],
        compiler_params=_cparams("arbitrary"),
        name="dense_ffn",
    )(x2, mod, g.reshape(1, D), w1, w3, w2)


def _router_kernel(x_ref, mod_ref, g_ref, rw_ref, rb_ref, h_ref, comb_ref, sel_ref):
    h = _modulated_norm(x_ref[...], g_ref[...], mod_ref[0], 3)
    h_ref[...] = h
    logits = jnp.dot(h, rw_ref[...], preferred_element_type=F32, precision=HIGHEST) + rb_ref[...]
    E = logits.shape[1]
    lane = lax.broadcasted_iota(jnp.int32, logits.shape, 1)
    v1 = jnp.max(logits, axis=-1, keepdims=True)
    i1 = jnp.min(jnp.where(logits == v1, lane, E), axis=-1, keepdims=True)
    rest = jnp.where(lane == i1, -jnp.inf, logits)
    v2 = jnp.max(rest, axis=-1, keepdims=True)
    i2 = jnp.min(jnp.where(rest == v2, lane, E), axis=-1, keepdims=True)
    e2 = jnp.exp(v2 - v1)
    w1 = 1.0 / (1.0 + e2)
    w2 = e2 / (1.0 + e2)
    comb_ref[...] = jnp.where(lane == i1, w1, 0.0) + jnp.where(lane == i2, w2, 0.0)
    sel_ref[...] = ((lane == i1) | (lane == i2)).astype(jnp.int32)


def _router(x2, mod, g, router_w, router_b, seq):
    T, D = x2.shape
    E = router_w.shape[1]
    tm = ROW_TILE
    per_b = seq // tm
    return pl.pallas_call(
        _router_kernel,
        grid=(T // tm,),
        in_specs=[pl.BlockSpec((tm, D), lambda i: (i, 0)),
                  pl.BlockSpec((1, 6, D), lambda i: (i // per_b, 0, 0)),
                  pl.BlockSpec((1, D), lambda i: (0, 0)),
                  pl.BlockSpec((D, E), lambda i: (0, 0)),
                  pl.BlockSpec((1, E), lambda i: (0, 0))],
        out_specs=[pl.BlockSpec((tm, D), lambda i: (i, 0)),
                   pl.BlockSpec((tm, E), lambda i: (i, 0)),
                   pl.BlockSpec((tm, E), lambda i: (i, 0))],
        out_shape=[jax.ShapeDtypeStruct((T, D), F32),
                   jax.ShapeDtypeStruct((T, E), F32),
                   jax.ShapeDtypeStruct((T, E), jnp.int32)],
        compiler_params=_cparams("parallel"),
        name="moe_router",
    )(x2, mod, g.reshape(1, D), router_w, router_b.reshape(1, E))


STEP_LOAD, STEP_TILE, STEP_IDLE = 0, 1, 2


def _moe_kernel(kind_ref, e_ref, c_ref, t_ref, xs_ref, w1_ref, w3_ref, w2_ref, o_ref, wb1, wb3, wb2):
    s = pl.program_id(0)

    @pl.when(kind_ref[s] == STEP_LOAD)
    def _():
        _stash_weight_chunk(c_ref[s], w1_ref, w3_ref, w2_ref, wb1, wb3, wb2)

    @pl.when(kind_ref[s] == STEP_TILE)
    def _():
        o_ref[...] = _swiglu_resident(xs_ref[...].astype(BF16), wb1, wb3, wb2)


def _moe_schedule(tiles_per_expert, n_tiles, nc):
    E = tiles_per_expert.shape[0]
    first_tile = jnp.cumsum(tiles_per_expert) - tiles_per_expert
    first_step = nc * jnp.arange(E, dtype=jnp.int32) + first_tile
    step = jnp.arange(nc * E + n_tiles, dtype=jnp.int32)
    e = jnp.sum((first_step[None, :] <= step[:, None]).astype(jnp.int32), axis=1) - 1
    off = step - first_step[e]
    kind = jnp.where(off < nc, STEP_LOAD, jnp.where(off < nc + tiles_per_expert[e], STEP_TILE, STEP_IDLE))
    chunk = jnp.clip(off, 0, nc - 1)
    n_used = jnp.sum(tiles_per_expert)
    tile = jnp.clip(first_tile[e] + jnp.clip(off - nc, 0, tiles_per_expert[e] - 1), 0, n_used - 1)
    as_i32 = lambda a: a.astype(jnp.int32)
    return as_i32(kind), as_i32(e), as_i32(chunk), as_i32(tile)


def _moe_grouped(xs, tiles_per_expert, w1, w3, w2, tm):
    N, D = xs.shape
    E, _, F = w1.shape
    tf = FF_CHUNK
    nc = F // tf
    n_tiles = N // tm
    kind, e, chunk, tile = _moe_schedule(tiles_per_expert, n_tiles, nc)
    grid_spec = pltpu.PrefetchScalarGridSpec(
        num_scalar_prefetch=4,
        grid=(nc * E + n_tiles,),
        in_specs=[pl.BlockSpec((tm, D), lambda s, kind, e, c, t: (t[s], 0)),
                  pl.BlockSpec((1, D, tf), lambda s, kind, e, c, t: (e[s], 0, c[s])),
                  pl.BlockSpec((1, D, tf), lambda s, kind, e, c, t: (e[s], 0, c[s])),
                  pl.BlockSpec((1, tf, D), lambda s, kind, e, c, t: (e[s], c[s], 0))],
        out_specs=pl.BlockSpec((tm, D), lambda s, kind, e, c, t: (t[s], 0)),
        scratch_shapes=[pltpu.VMEM((nc, D, tf), BF16), pltpu.VMEM((nc, D, tf), BF16),
                        pltpu.VMEM((nc, tf, D), BF16)],
    )
    return pl.pallas_call(
        _moe_kernel,
        grid_spec=grid_spec,
        out_shape=jax.ShapeDtypeStruct((N, D), F32),
        compiler_params=_cparams("arbitrary"),
        name="moe_experts",
    )(kind, e, chunk, tile, xs, w1, w3, w2)


def _combine_kernel(x_ref, y1_ref, y2_ref, w_ref, mod_ref, o_ref):
    w = w_ref[...]
    y = w[:, 0:1] * y1_ref[...] + w[:, 1:2] * y2_ref[...]
    o_ref[...] = x_ref[...] + mod_ref[0][5:6] * y


def _moe_combine(x2, y1, y2, w12, mod, seq):
    T, D = x2.shape
    tm = ROW_TILE
    per_b = seq // tm
    return pl.pallas_call(
        _combine_kernel,
        grid=(T // tm,),
        in_specs=[pl.BlockSpec((tm, D), lambda i: (i, 0)),
                  pl.BlockSpec((tm, D), lambda i: (i, 0)),
                  pl.BlockSpec((tm, D), lambda i: (i, 0)),
                  pl.BlockSpec((tm, 2), lambda i: (i, 0)),
                  pl.BlockSpec((1, 6, D), lambda i: (i // per_b, 0, 0))],
        out_specs=pl.BlockSpec((tm, D), lambda i: (i, 0)),
        out_shape=jax.ShapeDtypeStruct((T, D), F32),
        compiler_params=_cparams("parallel"),
        name="moe_combine",
    )(x2, y1, y2, w12, mod)


def _moe_ffn(x2, mod, g, router_w, router_b, w1, w3, w2, seq):
    T, D = x2.shape
    E = router_w.shape[1]
    tm = ROW_TILE
    h, comb, sel = _router(x2, mod, g, router_w, router_b, seq)

    counts = jnp.sum(sel, axis=0)
    padded = ((counts + tm - 1) // tm) * tm
    group_end = jnp.cumsum(padded)
    group_start = group_end - padded
    rank = jnp.cumsum(sel, axis=0) - sel
    dest = group_start[None, :] + rank
    n_rows = TOP_K * T + E * tm
    n_tiles = n_rows // tm
    tile_start = jnp.arange(n_tiles, dtype=jnp.int32) * tm
    tile_expert = jnp.sum((group_end[None, :] <= tile_start[:, None]).astype(jnp.int32), axis=1)
    tile_expert = jnp.minimum(tile_expert, jnp.max(jnp.where(counts > 0, jnp.arange(E), 0))).astype(jnp.int32)

    lane = jnp.arange(E, dtype=jnp.int32)[None, :]
    e_lo = jnp.min(jnp.where(sel > 0, lane, E), axis=1)
    e_hi = jnp.max(jnp.where(sel > 0, lane, -1), axis=1)
    pick = lambda a, e: jnp.sum(jnp.where(lane == e[:, None], a, 0), axis=1)
    w12 = jnp.stack([pick(comb, e_lo), pick(comb, e_hi)], axis=1)

    tok = jnp.arange(T, dtype=jnp.int32)
    tok_sorted = jnp.sort(jnp.concatenate([e_lo * T + tok, e_hi * T + tok])) % T
    row = jnp.arange(n_rows, dtype=jnp.int32)
    row_expert = jnp.repeat(tile_expert, tm)
    first_sorted = (jnp.cumsum(counts) - counts)[row_expert]
    src = tok_sorted[jnp.clip(first_sorted + row - group_start[row_expert], 0, TOP_K * T - 1)]

    take_rows = lambda a, idx: a.at[idx].get(mode="promise_in_bounds")
    ys = _moe_grouped(take_rows(h, src), padded // tm, w1, w3, w2, tm)
    y1 = take_rows(ys, pick(dest, e_lo))
    y2 = take_rows(ys, pick(dest, e_hi))
    return _moe_combine(x2, y1, y2, w12, mod, seq)


def kernel(x, c, positions, ada_w, ada_b, norm1_g, norm2_g, w_in, gate_b, conv_w, conv_b, lru_wa, lru_ba,
           lru_wx, lru_bx, lru_lambda, diff_qn, diff_kn, diff_lq1, diff_lk1, diff_lq2, diff_lk2, diff_subln,
           moba_qn, moba_kn, w_br_a, w_br_b, w_br_c, w_out, ffn_w1, ffn_w3, ffn_w2, router_w, router_b,
           moe_w1, moe_w3, moe_w2):
    B, S, D = x.shape
    L = ada_w.shape[0]
    T = B * S
    assert S % MOBA_BLOCK == 0 and S // MOBA_BLOCK <= 56 and S % 1024 == 0
    x2 = x.reshape(T, D)
    mod_all = _ada_mod(c, ada_w, ada_b).reshape(L, B, 6, D)
    cos, sin = _rope_tables(positions)
    bf = lambda w: w.astype(BF16)
    tile2 = lambda v: jnp.tile(v, LANES // HEAD_DIM)

    for l in range(L):
        mod = mod_all[l]
        lam_init = 0.8 - 0.6 * math.exp(-0.3 * l)
        proj = _in_proj(x2, mod, norm1_g[l], bf(w_in[l]), S)
        gains = jnp.stack([tile2(diff_qn[l]), tile2(diff_kn[l]), tile2(moba_qn[l]), tile2(moba_kn[l])])
        qk = _qk_prep(proj, gains, cos, sin)
        y_a = _rg_lru_branch(proj, S, conv_w[l], conv_b[l], _pair_block_diag(lru_wa[l]), lru_ba[l],
                             _pair_block_diag(lru_wx[l]), lru_bx[l], lru_lambda[l])
        lam_params = jnp.stack([diff_lq1[l], diff_lk1[l], diff_lq2[l], diff_lk2[l]])
        y_b = _diff_attention(qk, proj, S, lam_params, diff_subln[l], lam_init)
        y_c = _moba_attention(qk, proj, S)
        x2 = _merge_out(x2, y_a, y_b, y_c, proj, gate_b[l], mod, bf(w_br_a[l]), bf(w_br_b[l]),
                        bf(w_br_c[l]), bf(w_out[l]), S)
        if l % 2 == 0:
            x2 = _dense_ffn(x2, mod, norm2_g[l], ffn_w1[l // 2], ffn_w3[l // 2], ffn_w2[l // 2], S)
        else:
            x2 = _moe_ffn(x2, mod, norm2_g[l], router_w[l // 2], router_b[l // 2], moe_w1[l // 2],
                          moe_w3[l // 2], moe_w2[l // 2], S)
    return x2.reshape(B, S, D)
```

```python
import functools
import math

import jax
import jax.numpy as jnp
from jax import lax
from jax.experimental import pallas as pl
from jax.experimental.pallas import tpu as pltpu

F32 = jnp.float32
BF16 = jnp.bfloat16
HIGHEST = lax.Precision.HIGHEST

HEAD_DIM = 64
ROPE_THETA = 10000.0
RNN_BLOCKS = 16
CONV_W = 4
LRU_C = 8.0
DIFF_HEADS = 4
MOBA_HEADS = 8
MOBA_BLOCK = 256
MOBA_TOPK = 3
N_BRANCH = 3
N_EXPERTS = 8
TOP_K = 2
EPS = 1e-6
NEG = -1e30

LANES = 128
VMEM_LIMIT = 56 * 1024 * 1024

ROW_TILE = 512


def _cparams(*sem):
    return pltpu.CompilerParams(dimension_semantics=sem, vmem_limit_bytes=VMEM_LIMIT)


def _modulated_norm(x, g, mod, base):
    ms = jnp.mean(x * x, axis=-1, keepdims=True)
    y = x * lax.rsqrt(ms + EPS) * g
    return y * (1.0 + mod[base + 1:base + 2]) + mod[base:base + 1]


def _dot_nt(a, b, **kw):
    return lax.dot_general(a, b, (((1,), (1,)), ((), ())), preferred_element_type=F32, **kw)


def _ada_kernel(c_ref, w_ref, b_ref, o_ref):
    o_ref[0] = jnp.dot(c_ref[...], w_ref[0], preferred_element_type=F32, precision=HIGHEST) + b_ref[0]


def _ada_mod(c, ada_w, ada_b):
    L, D, N = ada_w.shape
    B = c.shape[0]
    tn = 1536
    return pl.pallas_call(
        _ada_kernel,
        grid=(L, N // tn),
        in_specs=[pl.BlockSpec((B, D), lambda l, j: (0, 0)),
                  pl.BlockSpec((1, D, tn), lambda l, j: (l, 0, j)),
                  pl.BlockSpec((1, 1, tn), lambda l, j: (l, 0, j))],
        out_specs=pl.BlockSpec((1, B, tn), lambda l, j: (l, 0, j)),
        out_shape=jax.ShapeDtypeStruct((L, B, N), F32),
        compiler_params=_cparams("parallel", "parallel"),
        name="ada_mod",
    )(c, ada_w, ada_b.reshape(L, 1, N))


def _inproj_kernel(x_ref, mod_ref, g_ref, w_ref, o_ref, h_ref):
    @pl.when(pl.program_id(1) == 0)
    def _():
        h_ref[...] = _modulated_norm(x_ref[...], g_ref[...], mod_ref[0], 0).astype(BF16)

    o_ref[...] = jnp.dot(h_ref[...], w_ref[...], preferred_element_type=F32).astype(BF16)


def _in_proj(x2, mod, g, w_bf16, seq):
    T, D = x2.shape
    N = w_bf16.shape[1]
    tm, tn = min(seq, 2048), 1024
    per_b = seq // tm
    return pl.pallas_call(
        _inproj_kernel,
        grid=(T // tm, N // tn),
        in_specs=[pl.BlockSpec((tm, D), lambda i, j: (i, 0)),
                  pl.BlockSpec((1, 6, D), lambda i, j: (i // per_b, 0, 0)),
                  pl.BlockSpec((1, D), lambda i, j: (0, 0)),
                  pl.BlockSpec((D, tn), lambda i, j: (0, j))],
        out_specs=pl.BlockSpec((tm, tn), lambda i, j: (i, j)),
        out_shape=jax.ShapeDtypeStruct((T, N), BF16),
        scratch_shapes=[pltpu.VMEM((tm, D), BF16)],
        compiler_params=_cparams("parallel", "arbitrary"),
        name="in_proj",
    )(x2, mod, g.reshape(1, D), w_bf16)


def _rope_kernel(pos_ref, inv_ref, sign_ref, cos_ref, sin_ref):
    ang = pos_ref[...] * inv_ref[...]
    cos_ref[...] = jnp.cos(ang)
    sin_ref[...] = jnp.sin(ang) * sign_ref[...]


def _rope_tables(positions):
    T = positions.size
    pos = positions.reshape(T, 1).astype(F32)
    inv = 1.0 / (ROPE_THETA ** (jnp.arange(0, HEAD_DIM, 2, dtype=F32) / HEAD_DIM))
    half = HEAD_DIM // 2
    inv128 = jnp.tile(inv, LANES // half).reshape(1, LANES)
    sign = jnp.tile(jnp.concatenate([-jnp.ones((half,), F32), jnp.ones((half,), F32)]),
                    LANES // HEAD_DIM).reshape(1, LANES)
    tm = 1024
    return pl.pallas_call(
        _rope_kernel,
        grid=(T // tm,),
        in_specs=[pl.BlockSpec((tm, 1), lambda i: (i, 0)),
                  pl.BlockSpec((1, LANES), lambda i: (0, 0)),
                  pl.BlockSpec((1, LANES), lambda i: (0, 0))],
        out_specs=[pl.BlockSpec((tm, LANES), lambda i: (i, 0))] * 2,
        out_shape=[jax.ShapeDtypeStruct((T, LANES), F32)] * 2,
        compiler_params=_cparams("parallel"),
        name="rope_tables",
    )(pos, inv128, sign)


def _qkprep_kernel(dq_ref, dk_ref, mq_ref, mk_ref, gain_ref, cos_ref, sin_ref, seg_ref, o_ref):
    cos = cos_ref[...]
    sin = sin_ref[...]
    seg = seg_ref[...]
    lane = lax.broadcasted_iota(jnp.int32, cos.shape, 1)
    first_half = (lane % HEAD_DIM) < (HEAD_DIM // 2)
    width = dq_ref.shape[1]
    for gi, ref in enumerate((dq_ref, dk_ref, mq_ref, mk_ref)):
        gain = gain_ref[gi:gi + 1, :]
        for cb in range(width // LANES):
            x = ref[:, cb * LANES:(cb + 1) * LANES].astype(F32)
            sq = x * x
            sq_hi = sq.astype(BF16)
            sq_lo = (sq - sq_hi.astype(F32)).astype(BF16)
            ms = (jnp.dot(sq_hi, seg, preferred_element_type=F32)
                  + jnp.dot(sq_lo, seg, preferred_element_type=F32))
            y = x * lax.rsqrt(ms + EPS) * gain
            swapped = jnp.where(first_half, pltpu.roll(y, LANES - HEAD_DIM // 2, 1),
                                pltpu.roll(y, HEAD_DIM // 2, 1))
            r = y * cos + swapped * sin
            if gi % 2 == 0:
                r = r * (1.0 / math.sqrt(HEAD_DIM))
            col = gi * width + cb * LANES
            o_ref[:, col:col + LANES] = r.astype(BF16)


def _qk_prep(proj, gains, cos, sin):
    T = proj.shape[0]
    width = DIFF_HEADS * 2 * HEAD_DIM
    seg = jnp.kron(jnp.eye(LANES // HEAD_DIM, dtype=F32),
                   jnp.full((HEAD_DIM, HEAD_DIM), 1.0 / HEAD_DIM, F32)).astype(BF16)
    tm = ROW_TILE
    col_blocks = (4, 5, 7, 8)
    in_specs = [pl.BlockSpec((tm, width), functools.partial(lambda i, c: (i, c), c=c)) for c in col_blocks]
    in_specs += [pl.BlockSpec((4, LANES), lambda i: (0, 0)),
                 pl.BlockSpec((tm, LANES), lambda i: (i, 0)),
                 pl.BlockSpec((tm, LANES), lambda i: (i, 0)),
                 pl.BlockSpec((LANES, LANES), lambda i: (0, 0))]
    return pl.pallas_call(
        _qkprep_kernel,
        grid=(T // tm,),
        in_specs=in_specs,
        out_specs=pl.BlockSpec((tm, 4 * width), lambda i: (i, 0)),
        out_shape=jax.ShapeDtypeStruct((T, 4 * width), BF16),
        compiler_params=_cparams("parallel"),
        name="qk_prep",
    )(proj, proj, proj, proj, gains, cos, sin, seg)


def _gelu_tanh(x):
    return 0.5 * x * (1.0 + jnp.tanh(math.sqrt(2.0 / math.pi) * (x + 0.044715 * x * x * x)))


SUBLANES = 8
LRU_LANE_BLOCKS = 2


def _rglru_kernel(x_ref, g_ref, cw_ref, cb_ref, wa_ref, ba_ref, wx_ref, bx_ref, lam_ref, o_ref, *scratch):
    S = x_ref.shape[0]
    steps = S // SUBLANES
    pitch = steps + SUBLANES
    nblk = x_ref.shape[1] // LANES
    per_blk = len(scratch) // nblk
    sub = lax.broadcasted_iota(jnp.int32, (SUBLANES, LANES), 0)

    for blk in range(nblk):
        slab, fold_x, a_buf, u_buf, p_buf = scratch[blk * per_blk:(blk + 1) * per_blk]
        lanes = slice(blk * LANES, (blk + 1) * LANES)

        for s in range(SUBLANES):
            slab[s * pitch:s * pitch + steps, :] = x_ref[s * steps:(s + 1) * steps, lanes].astype(F32)

        def fold(step, carry):
            fold_x[pl.ds(pl.multiple_of(step * SUBLANES, SUBLANES), SUBLANES), :] = (
                slab[pl.ds(step, SUBLANES, stride=pitch), :])
            return carry

        lax.fori_loop(0, steps, fold, 0, unroll=8)
        xf = fold_x[...]

        def head(j):
            tail = xf[(steps - j) * SUBLANES:, :]
            vregs = [jnp.where(sub == 0, 0.0, pltpu.roll(tail[v * SUBLANES:(v + 1) * SUBLANES, :], 1, 0))
                     for v in range(j)]
            return jnp.concatenate(vregs + [xf[:(steps - j) * SUBLANES, :]], axis=0)

        xc = cb_ref[:, lanes] + cw_ref[CONV_W - 1:CONV_W, lanes] * xf
        for j in range(1, CONV_W):
            xc = xc + cw_ref[CONV_W - 1 - j:CONV_W - j, lanes] * head(j)

        xcb = xc.astype(BF16)
        r = jax.nn.sigmoid(jnp.dot(xcb, wa_ref[blk], preferred_element_type=F32) + ba_ref[:, lanes])
        gi = jax.nn.sigmoid(jnp.dot(xcb, wx_ref[blk], preferred_element_type=F32) + bx_ref[:, lanes])
        neg_lam = -lam_ref[:, lanes]
        softplus = jnp.maximum(neg_lam, 0.0) + jnp.log1p(jnp.exp(-jnp.abs(neg_lam)))
        a = jnp.exp((-LRU_C) * r * softplus)
        gap = 1.0 - a * a
        mult = jnp.where(gap > 0.0, gap * lax.rsqrt(gap), 0.0)
        a_buf[...] = a
        u_buf[...] = mult * gi * xc

    def scan(step, carry):
        rows = pl.ds(pl.multiple_of(step * SUBLANES, SUBLANES), SUBLANES)
        out = []
        for blk in range(nblk):
            _, _, a_buf, u_buf, p_buf = scratch[blk * per_blk:(blk + 1) * per_blk]
            h, p = carry[blk]
            a = a_buf[rows, :]
            h = a * h + u_buf[rows, :]
            p = a * p
            u_buf[rows, :] = h
            p_buf[rows, :] = p
            out.append((h, p))
        return tuple(out)

    ones = jnp.ones((SUBLANES, LANES), F32)
    last = lax.fori_loop(0, steps, scan, ((jnp.zeros_like(ones), ones),) * nblk, unroll=8)

    for blk in range(nblk):
        slab, fold_x, a_buf, u_buf, p_buf = scratch[blk * per_blk:(blk + 1) * per_blk]
        lanes = slice(blk * LANES, (blk + 1) * LANES)
        h_end, p_end = last[blk]
        h_in = [jnp.zeros((1, LANES), F32)]
        for s in range(SUBLANES - 1):
            h_in.append(h_end[s:s + 1, :] + p_end[s:s + 1, :] * h_in[s])
        h_in = jnp.concatenate(h_in, axis=0)
        u_buf[...] = u_buf[...] + p_buf[...] * jnp.concatenate([h_in] * steps, axis=0)

        def unfold(step, carry):
            slab[pl.ds(step, SUBLANES, stride=pitch), :] = (
                u_buf[pl.ds(pl.multiple_of(step * SUBLANES, SUBLANES), SUBLANES), :])
            return carry

        lax.fori_loop(0, steps, unfold, 0, unroll=8)
        for s in range(SUBLANES):
            rows = slice(s * steps, (s + 1) * steps)
            h = slab[s * pitch:s * pitch + steps, :]
            o_ref[rows, lanes] = (_gelu_tanh(g_ref[rows, lanes].astype(F32)) * h).astype(BF16)


def _rg_lru_branch(proj, seq, conv_w, conv_b, wa2, ba, wx2, bx, lam):
    T = proj.shape[0]
    C = conv_w.shape[1]
    B = T // seq
    nblk = LRU_LANE_BLOCKS
    W = nblk * LANES
    g_off = C // W
    steps = seq // SUBLANES
    assert seq % (SUBLANES * SUBLANES) == 0 and C % W == 0
    vec = lambda v: v.reshape(1, C)
    cols = lambda rows: pl.BlockSpec((rows, W), lambda b, n: (0, n))
    per_blk = [pltpu.VMEM((SUBLANES * (steps + SUBLANES), LANES), F32)] + [pltpu.VMEM((seq, LANES), F32)] * 4
    return pl.pallas_call(
        _rglru_kernel,
        grid=(B, C // W),
        in_specs=[pl.BlockSpec((seq, W), lambda b, n: (b, n)),
                  pl.BlockSpec((seq, W), lambda b, n: (b, g_off + n)),
                  cols(CONV_W), cols(1),
                  pl.BlockSpec((nblk, LANES, LANES), lambda b, n: (n, 0, 0)), cols(1),
                  pl.BlockSpec((nblk, LANES, LANES), lambda b, n: (n, 0, 0)), cols(1), cols(1)],
        out_specs=pl.BlockSpec((seq, W), lambda b, n: (b, n)),
        out_shape=jax.ShapeDtypeStruct((T, C), BF16),
        scratch_shapes=per_blk * nblk,
        compiler_params=_cparams("parallel", "parallel"),
        name="rg_lru",
    )(proj, proj, conv_w, vec(conv_b), wa2, vec(ba), wx2, vec(bx), vec(lam))


def _pair_block_diag(w):
    n, d, _ = w.shape
    z = jnp.zeros((n // 2, 2, d, 2, d), w.dtype)
    z = z.at[:, 0, :, 0, :].set(w[0::2]).at[:, 1, :, 1, :].set(w[1::2])
    return z.reshape(n // 2, 2 * d, 2 * d).astype(BF16)


ATTN_TILE = 512


def _lane_tile(x, n):
    return jnp.concatenate([x] * n, axis=1)


def _softmax_update(s, state, v_aug):
    row_max = jnp.max(s, axis=-1, keepdims=True)
    if state is None:
        m_new = jnp.broadcast_to(row_max, (s.shape[0], LANES))
        p = jnp.exp(s - _lane_tile(m_new, s.shape[1] // LANES)).astype(BF16)
        return m_new, jnp.dot(p, v_aug, preferred_element_type=F32)
    m_old, acc = state
    m_new = jnp.maximum(m_old, row_max)
    p = jnp.exp(s - _lane_tile(m_new, s.shape[1] // LANES)).astype(BF16)
    alpha = _lane_tile(jnp.exp(m_old - m_new), acc.shape[1] // LANES)
    return m_new, alpha * acc + jnp.dot(p, v_aug, preferred_element_type=F32)


def _softmax_finish(state):
    _, acc = state
    return acc[:, :LANES] / acc[:, LANES:]


def _causal_mask(t):
    return lax.broadcasted_iota(jnp.int32, (t, t), 1) <= lax.broadcasted_iota(jnp.int32, (t, t), 0)


def _with_ones(v):
    return jnp.concatenate([v, jnp.ones_like(v)], axis=1)


def _diffattn_kernel(q_ref, k_ref, v_ref, lam_ref, g_ref, o_ref, *, lam_init):
    t = ATTN_TILE
    nq = q_ref.shape[0] // t
    lane = lax.broadcasted_iota(jnp.int32, (t, LANES), 1)
    causal = _causal_mask(t)
    lp = lam_ref[...]
    lam = (jnp.exp(jnp.sum(lp[0:1] * lp[1:2], axis=-1, keepdims=True))
           - jnp.exp(jnp.sum(lp[2:3] * lp[3:4], axis=-1, keepdims=True)) + lam_init)
    for i in range(nq):
        q = q_ref[i * t:(i + 1) * t, :]
        zero = jnp.zeros_like(q)
        q1 = jnp.where(lane < HEAD_DIM, q, zero)
        q2 = jnp.where(lane < HEAD_DIM, zero, q)
        st1 = st2 = None
        for j in range(i + 1):
            k = k_ref[j * t:(j + 1) * t, :]
            v_aug = _with_ones(v_ref[j * t:(j + 1) * t, :])
            s1 = _dot_nt(q1, k)
            s2 = _dot_nt(q2, k)
            if j == i:
                s1 = jnp.where(causal, s1, NEG)
                s2 = jnp.where(causal, s2, NEG)
            st1 = _softmax_update(s1, st1, v_aug)
            st2 = _softmax_update(s2, st2, v_aug)
        o = _softmax_finish(st1) - lam * _softmax_finish(st2)
        ms = jnp.mean(o * o, axis=-1, keepdims=True)
        o_ref[i * t:(i + 1) * t, :] = (o * lax.rsqrt(ms + EPS) * g_ref[...] * (1.0 - lam_init)).astype(BF16)


def _diff_attention(qk, proj, seq, lam_params, subln_g, lam_init):
    T = qk.shape[0]
    B = T // seq
    H = DIFF_HEADS
    return pl.pallas_call(
        functools.partial(_diffattn_kernel, lam_init=lam_init),
        grid=(B, H),
        in_specs=[pl.BlockSpec((seq, LANES), lambda b, h: (b, h)),
                  pl.BlockSpec((seq, LANES), lambda b, h: (b, H + h)),
                  pl.BlockSpec((seq, LANES), lambda b, h: (b, 24 + h)),
                  pl.BlockSpec((4, HEAD_DIM), lambda b, h: (0, 0)),
                  pl.BlockSpec((1, LANES), lambda b, h: (0, 0))],
        out_specs=pl.BlockSpec((seq, LANES), lambda b, h: (b, h)),
        out_shape=jax.ShapeDtypeStruct((T, H * LANES), BF16),
        compiler_params=_cparams("parallel", "parallel"),
        name="diff_attn",
    )(qk, qk, proj, lam_params, subln_g.reshape(1, LANES))


def _moba_kernel(q_ref, k_ref, v_ref, o_ref, kmean):
    t = ATTN_TILE
    nq = q_ref.shape[0] // t
    nb = k_ref.shape[0] // MOBA_BLOCK
    nb8 = -(-nb // 8) * 8
    lane_row = lax.broadcasted_iota(jnp.int32, (1, LANES), 1)

    kmean[...] = jnp.zeros_like(kmean)
    for blk in range(nb):
        mean = jnp.mean(k_ref[blk * MOBA_BLOCK:(blk + 1) * MOBA_BLOCK, :].astype(F32), axis=0, keepdims=True)
        kmean[blk:blk + 1, :] = jnp.where(lane_row < HEAD_DIM, mean, 0.0)
        kmean[HEAD_DIM + blk:HEAD_DIM + blk + 1, :] = jnp.where(lane_row < HEAD_DIM, 0.0, mean)

    lane = lax.broadcasted_iota(jnp.int32, (t, LANES), 1)
    row = lax.broadcasted_iota(jnp.int32, (t, LANES), 0)
    causal = _causal_mask(t)
    gate_blk = lax.broadcasted_iota(jnp.int32, (nb8, t), 0)
    gate_col = lax.broadcasted_iota(jnp.int32, (nb8, t), 1)
    unused_rows = jnp.full((HEAD_DIM - nb8, t), NEG, F32)

    def keys(j):
        key_blk = (j * t + row) // MOBA_BLOCK
        k_aug = jnp.concatenate([k_ref[j * t:(j + 1) * t, :], (lane == key_blk).astype(BF16)], axis=1)
        return k_aug, _with_ones(v_ref[j * t:(j + 1) * t, :])

    for i in range(nq):
        q = q_ref[i * t:(i + 1) * t, :]
        zero = jnp.zeros_like(q)
        q_a = jnp.where(lane < HEAD_DIM, q, zero)
        q_b = jnp.where(lane < HEAD_DIM, zero, q)

        gate = _dot_nt(kmean[...], q.astype(F32), precision=HIGHEST)
        own = (i * t + gate_col) // MOBA_BLOCK
        past = gate_blk < own
        last_past = min(nb, (i * t + t - 1) // MOBA_BLOCK)
        bias_rows = []
        for head in range(2):
            g = jnp.where(past, gate[head * HEAD_DIM:head * HEAD_DIM + nb8, :], NEG)
            rank = jnp.zeros(g.shape, jnp.int32)
            for jb in range(last_past):
                other = g[jb:jb + 1, :]
                beats = (other > g) | ((other == g) & (jb < gate_blk))
                rank = rank + beats.astype(jnp.int32)
            allowed = (past & (rank < MOBA_TOPK)) | (gate_blk == own)
            bias_rows += [jnp.where(allowed, 0.0, NEG), unused_rows]
        bias_a = jnp.concatenate(bias_rows, axis=0).T
        bias_b = pltpu.roll(bias_a, HEAD_DIM, 1)
        qa_aug = jnp.concatenate([q_a, bias_a.astype(BF16)], axis=1)
        qb_aug = jnp.concatenate([q_b, bias_b.astype(BF16)], axis=1)

        st_a = st_b = None
        for j in [i] + list(range(i)):
            k_aug, v_aug = keys(j)
            s_a = _dot_nt(qa_aug, k_aug)
            s_b = _dot_nt(qb_aug, k_aug)
            if j == i:
                s_a = jnp.where(causal, s_a, NEG)
                s_b = jnp.where(causal, s_b, NEG)
            st_a = _softmax_update(s_a, st_a, v_aug)
            st_b = _softmax_update(s_b, st_b, v_aug)
        o = jnp.where(lane < HEAD_DIM, _softmax_finish(st_a), _softmax_finish(st_b))
        o_ref[i * t:(i + 1) * t, :] = o.astype(BF16)


def _moba_attention(qk, proj, seq):
    T = qk.shape[0]
    B = T // seq
    HP = MOBA_HEADS // 2
    return pl.pallas_call(
        _moba_kernel,
        grid=(B, HP),
        in_specs=[pl.BlockSpec((seq, LANES), lambda b, h: (b, 8 + h)),
                  pl.BlockSpec((seq, LANES), lambda b, h: (b, 12 + h)),
                  pl.BlockSpec((seq, LANES), lambda b, h: (b, 36 + h))],
        out_specs=pl.BlockSpec((seq, LANES), lambda b, h: (b, h)),
        out_shape=jax.ShapeDtypeStruct((T, HP * LANES), BF16),
        scratch_shapes=[pltpu.VMEM((LANES, LANES), F32)],
        compiler_params=_cparams("parallel", "parallel"),
        name="moba_attn",
    )(qk, qk, proj)


def _merge_kernel(x_ref, ya_ref, yb_ref, yc_ref, ga_ref, gbb_ref, gc_ref, gb_ref, mod_ref,
                  wa_ref, wb_ref, wc_ref, wo_ref, o_ref):
    merged = None
    branches = ((ya_ref, wa_ref, ga_ref), (yb_ref, wb_ref, gbb_ref), (yc_ref, wc_ref, gc_ref))
    for n, (y_ref, w_ref, gl_ref) in enumerate(branches):
        gate = jax.nn.sigmoid(gl_ref[...].astype(F32) + gb_ref[n:n + 1, :])
        term = gate * jnp.dot(y_ref[...], w_ref[...], preferred_element_type=F32)
        merged = term if merged is None else merged + term
    mix = jnp.dot(merged.astype(BF16), wo_ref[...], preferred_element_type=F32)
    o_ref[...] = x_ref[...] + mod_ref[0][2:3] * mix


def _merge_out(x2, ya, yb, yc, proj, gate_b, mod, wa, wb, wc, wo, seq):
    T, D = x2.shape
    tm = ROW_TILE
    per_b = seq // tm
    gbr_block = 5120 // D
    assert gbr_block * D == 5120
    wspec = lambda w: pl.BlockSpec(w.shape, lambda i: (0, 0))
    return pl.pallas_call(
        _merge_kernel,
        grid=(T // tm,),
        in_specs=[pl.BlockSpec((tm, D), lambda i: (i, 0)),
                  pl.BlockSpec((tm, ya.shape[1]), lambda i: (i, 0)),
                  pl.BlockSpec((tm, yb.shape[1]), lambda i: (i, 0)),
                  pl.BlockSpec((tm, yc.shape[1]), lambda i: (i, 0)),
                  pl.BlockSpec((tm, D), lambda i: (i, gbr_block)),
                  pl.BlockSpec((tm, D), lambda i: (i, gbr_block + 1)),
                  pl.BlockSpec((tm, D), lambda i: (i, gbr_block + 2)),
                  pl.BlockSpec((N_BRANCH, D), lambda i: (0, 0)),
                  pl.BlockSpec((1, 6, D), lambda i: (i // per_b, 0, 0)),
                  wspec(wa), wspec(wb), wspec(wc), wspec(wo)],
        out_specs=pl.BlockSpec((tm, D), lambda i: (i, 0)),
        out_shape=jax.ShapeDtypeStruct((T, D), F32),
        compiler_params=_cparams("parallel"),
        name="merge_out",
    )(x2, ya, yb, yc, proj, proj, proj, gate_b, mod, wa, wb, wc, wo)


FF_CHUNK = 256
FF_LOADS = 8


def _stash_weight_slab(k, w1_ref, w3_ref, w2_ref, wb1, wb3, wb2):
    for w_ref, wb in ((w1_ref, wb1), (w3_ref, wb3), (w2_ref, wb2)):
        rows, cols = w_ref.shape[-2:]
        start = pl.multiple_of(k * rows, rows)
        wb[pl.ds(start, rows), :] = w_ref[...].reshape(rows, cols).astype(BF16)


def _swiglu_resident(h, wb1, wb3, wb2):
    acc = None
    for f in range(wb2.shape[0] // FF_CHUNK):
        cols = slice(f * FF_CHUNK, (f + 1) * FF_CHUNK)
        g = jnp.dot(h, wb1[:, cols], preferred_element_type=F32)
        u = jnp.dot(h, wb3[:, cols], preferred_element_type=F32)
        act = (g * jax.nn.sigmoid(g) * u).astype(BF16)
        part = jnp.dot(act, wb2[cols, :], preferred_element_type=F32)
        acc = part if acc is None else acc + part
    return acc


def _ffn_weight_specs(D, F, lead, slab_index):
    assert D % (16 * FF_LOADS) == 0 and F % (16 * FF_LOADS) == 0 and F % FF_CHUNK == 0
    index = lambda *a: slab_index(*a) + (0,)
    return [pl.BlockSpec(lead + (D // FF_LOADS, F), index), pl.BlockSpec(lead + (D // FF_LOADS, F), index),
            pl.BlockSpec(lead + (F // FF_LOADS, D), index)]


def _ffn_kernel(x_ref, mod_ref, g_ref, w1_ref, w3_ref, w2_ref, o_ref, wb1, wb3, wb2):
    s = pl.program_id(0)

    @pl.when(s < FF_LOADS)
    def _():
        _stash_weight_slab(s, w1_ref, w3_ref, w2_ref, wb1, wb3, wb2)

    @pl.when(s >= FF_LOADS)
    def _():
        h = _modulated_norm(x_ref[...], g_ref[...], mod_ref[0], 3).astype(BF16)
        o_ref[...] = x_ref[...] + mod_ref[0][5:6] * _swiglu_resident(h, wb1, wb3, wb2)


def _dense_ffn(x2, mod, g, w1, w3, w2, seq):
    T, D = x2.shape
    F = w1.shape[1]
    tm = ROW_TILE
    per_b = seq // tm
    tile = lambda s: jnp.maximum(s - FF_LOADS, 0)
    return pl.pallas_call(
        _ffn_kernel,
        grid=(FF_LOADS + T // tm,),
        in_specs=[pl.BlockSpec((tm, D), lambda s: (tile(s), 0)),
                  pl.BlockSpec((1, 6, D), lambda s: (tile(s) // per_b, 0, 0)),
                  pl.BlockSpec((1, D), lambda s: (0, 0))]
        + _ffn_weight_specs(D, F, (), lambda s: (jnp.minimum(s, FF_LOADS - 1),)),
        out_specs=pl.BlockSpec((tm, D), lambda s: (tile(s), 0)),
        out_shape=jax.ShapeDtypeStruct((T, D), F32),
        scratch_shapes=[pltpu.VMEM((D, F), BF16), pltpu.VMEM((D, F), BF16), pltpu.VMEM((F, D), BF16)],
        compiler_params=_cparams("arbitrary"),
        name="dense_ffn",
    )(x2, mod, g.reshape(1, D), w1, w3, w2)


def _router_kernel(x_ref, mod_ref, g_ref, rw_ref, rb_ref, h_ref, comb_ref, sel_ref):
    h = _modulated_norm(x_ref[...], g_ref[...], mod_ref[0], 3)
    h_ref[...] = h
    logits = jnp.dot(h, rw_ref[...], preferred_element_type=F32, precision=HIGHEST) + rb_ref[...]
    E = logits.shape[1]
    lane = lax.broadcasted_iota(jnp.int32, logits.shape, 1)
    v1 = jnp.max(logits, axis=-1, keepdims=True)
    i1 = jnp.min(jnp.where(logits == v1, lane, E), axis=-1, keepdims=True)
    rest = jnp.where(lane == i1, -jnp.inf, logits)
    v2 = jnp.max(rest, axis=-1, keepdims=True)
    i2 = jnp.min(jnp.where(rest == v2, lane, E), axis=-1, keepdims=True)
    e2 = jnp.exp(v2 - v1)
    w1 = 1.0 / (1.0 + e2)
    w2 = e2 / (1.0 + e2)
    comb_ref[...] = jnp.where(lane == i1, w1, 0.0) + jnp.where(lane == i2, w2, 0.0)
    sel_ref[...] = ((lane == i1) | (lane == i2)).astype(jnp.int32)


def _router(x2, mod, g, router_w, router_b, seq):
    T, D = x2.shape
    E = router_w.shape[1]
    tm = ROW_TILE
    per_b = seq // tm
    return pl.pallas_call(
        _router_kernel,
        grid=(T // tm,),
        in_specs=[pl.BlockSpec((tm, D), lambda i: (i, 0)),
                  pl.BlockSpec((1, 6, D), lambda i: (i // per_b, 0, 0)),
                  pl.BlockSpec((1, D), lambda i: (0, 0)),
                  pl.BlockSpec((D, E), lambda i: (0, 0)),
                  pl.BlockSpec((1, E), lambda i: (0, 0))],
        out_specs=[pl.BlockSpec((tm, D), lambda i: (i, 0)),
                   pl.BlockSpec((tm, E), lambda i: (i, 0)),
                   pl.BlockSpec((tm, E), lambda i: (i, 0))],
        out_shape=[jax.ShapeDtypeStruct((T, D), F32),
                   jax.ShapeDtypeStruct((T, E), F32),
                   jax.ShapeDtypeStruct((T, E), jnp.int32)],
        compiler_params=_cparams("parallel"),
        name="moe_router",
    )(x2, mod, g.reshape(1, D), router_w, router_b.reshape(1, E))


STEP_LOAD, STEP_TILE, STEP_IDLE = 0, 1, 2


def _moe_kernel(kind_ref, e_ref, c_ref, t_ref, xs_ref, w1_ref, w3_ref, w2_ref, o_ref, wb1, wb3, wb2):
    s = pl.program_id(0)

    @pl.when(kind_ref[s] == STEP_LOAD)
    def _():
        _stash_weight_slab(c_ref[s], w1_ref, w3_ref, w2_ref, wb1, wb3, wb2)

    @pl.when(kind_ref[s] == STEP_TILE)
    def _():
        o_ref[...] = _swiglu_resident(xs_ref[...].astype(BF16), wb1, wb3, wb2)


def _moe_schedule(tiles_per_expert, n_tiles, nc):
    E = tiles_per_expert.shape[0]
    first_tile = jnp.cumsum(tiles_per_expert) - tiles_per_expert
    first_step = nc * jnp.arange(E, dtype=jnp.int32) + first_tile
    step = jnp.arange(nc * E + n_tiles, dtype=jnp.int32)
    e = jnp.sum((first_step[None, :] <= step[:, None]).astype(jnp.int32), axis=1) - 1
    off = step - first_step[e]
    kind = jnp.where(off < nc, STEP_LOAD, jnp.where(off < nc + tiles_per_expert[e], STEP_TILE, STEP_IDLE))
    chunk = jnp.clip(off, 0, nc - 1)
    last_used = jnp.maximum(jnp.sum(tiles_per_expert) - 1, 0)
    tile = jnp.clip(first_tile[e] + jnp.clip(off - nc, 0, tiles_per_expert[e] - 1), 0, last_used)
    as_i32 = lambda a: a.astype(jnp.int32)
    return as_i32(kind), as_i32(e), as_i32(chunk), as_i32(tile)


def _moe_grouped(xs, tiles_per_expert, w1, w3, w2, tm):
    N, D = xs.shape
    E, _, F = w1.shape
    n_tiles = N // tm
    kind, e, slab, tile = _moe_schedule(tiles_per_expert, n_tiles, FF_LOADS)
    grid_spec = pltpu.PrefetchScalarGridSpec(
        num_scalar_prefetch=4,
        grid=(FF_LOADS * E + n_tiles,),
        in_specs=[pl.BlockSpec((tm, D), lambda s, kind, e, c, t: (t[s], 0))]
        + _ffn_weight_specs(D, F, (1,), lambda s, kind, e, c, t: (e[s], c[s])),
        out_specs=pl.BlockSpec((tm, D), lambda s, kind, e, c, t: (t[s], 0)),
        scratch_shapes=[pltpu.VMEM((D, F), BF16), pltpu.VMEM((D, F), BF16), pltpu.VMEM((F, D), BF16)],
    )
    return pl.pallas_call(
        _moe_kernel,
        grid_spec=grid_spec,
        out_shape=jax.ShapeDtypeStruct((N, D), F32),
        compiler_params=_cparams("arbitrary"),
        name="moe_experts",
    )(kind, e, slab, tile, xs, w1, w3, w2)


def _combine_kernel(x_ref, y1_ref, y2_ref, w_ref, mod_ref, o_ref):
    w = w_ref[...]
    y = w[:, 0:1] * y1_ref[...] + w[:, 1:2] * y2_ref[...]
    o_ref[...] = x_ref[...] + mod_ref[0][5:6] * y


def _moe_combine(x2, y1, y2, w12, mod, seq):
    T, D = x2.shape
    tm = ROW_TILE
    per_b = seq // tm
    return pl.pallas_call(
        _combine_kernel,
        grid=(T // tm,),
        in_specs=[pl.BlockSpec((tm, D), lambda i: (i, 0)),
                  pl.BlockSpec((tm, D), lambda i: (i, 0)),
                  pl.BlockSpec((tm, D), lambda i: (i, 0)),
                  pl.BlockSpec((tm, 2), lambda i: (i, 0)),
                  pl.BlockSpec((1, 6, D), lambda i: (i // per_b, 0, 0))],
        out_specs=pl.BlockSpec((tm, D), lambda i: (i, 0)),
        out_shape=jax.ShapeDtypeStruct((T, D), F32),
        compiler_params=_cparams("parallel"),
        name="moe_combine",
    )(x2, y1, y2, w12, mod)


def _moe_ffn(x2, mod, g, router_w, router_b, w1, w3, w2, seq):
    T, D = x2.shape
    E = router_w.shape[1]
    tm = ROW_TILE
    h, comb, sel = _router(x2, mod, g, router_w, router_b, seq)

    counts = jnp.sum(sel, axis=0)
    padded = ((counts + tm - 1) // tm) * tm
    group_end = jnp.cumsum(padded)
    group_start = group_end - padded
    rank = jnp.cumsum(sel, axis=0) - sel
    dest = group_start[None, :] + rank
    n_rows = TOP_K * T + E * tm
    n_tiles = n_rows // tm
    tile_start = jnp.arange(n_tiles, dtype=jnp.int32) * tm
    tile_expert = jnp.sum((group_end[None, :] <= tile_start[:, None]).astype(jnp.int32), axis=1)
    tile_expert = jnp.minimum(tile_expert, jnp.max(jnp.where(counts > 0, jnp.arange(E), 0))).astype(jnp.int32)

    lane = jnp.arange(E, dtype=jnp.int32)[None, :]
    e_lo = jnp.min(jnp.where(sel > 0, lane, E), axis=1)
    e_hi = jnp.max(jnp.where(sel > 0, lane, -1), axis=1)
    pick = lambda a, e: jnp.sum(jnp.where(lane == e[:, None], a, 0), axis=1)
    w12 = jnp.stack([pick(comb, e_lo), pick(comb, e_hi)], axis=1)

    tok = jnp.arange(T, dtype=jnp.int32)
    tok_sorted = jnp.sort(jnp.concatenate([e_lo * T + tok, e_hi * T + tok])) % T
    row = jnp.arange(n_rows, dtype=jnp.int32)
    row_expert = jnp.repeat(tile_expert, tm)
    first_sorted = (jnp.cumsum(counts) - counts)[row_expert]
    src = tok_sorted[jnp.clip(first_sorted + row - group_start[row_expert], 0, TOP_K * T - 1)]

    take_rows = lambda a, idx: a.at[idx].get(mode="promise_in_bounds")
    ys = _moe_grouped(take_rows(h, src), padded // tm, w1, w3, w2, tm)
    y1 = take_rows(ys, pick(dest, e_lo))
    y2 = take_rows(ys, pick(dest, e_hi))
    return _moe_combine(x2, y1, y2, w12, mod, seq)


def kernel(x, c, positions, ada_w, ada_b, norm1_g, norm2_g, w_in, gate_b, conv_w, conv_b, lru_wa, lru_ba,
           lru_wx, lru_bx, lru_lambda, diff_qn, diff_kn, diff_lq1, diff_lk1, diff_lq2, diff_lk2, diff_subln,
           moba_qn, moba_kn, w_br_a, w_br_b, w_br_c, w_out, ffn_w1, ffn_w3, ffn_w2, router_w, router_b,
           moe_w1, moe_w3, moe_w2):
    B, S, D = x.shape
    L = ada_w.shape[0]
    T = B * S
    assert S % MOBA_BLOCK == 0 and S // MOBA_BLOCK <= 56 and S % 1024 == 0
    x2 = x.reshape(T, D)
    mod_all = _ada_mod(c, ada_w, ada_b).reshape(L, B, 6, D)
    cos, sin = _rope_tables(positions)
    bf = lambda w: w.astype(BF16)
    tile2 = lambda v: jnp.tile(v, LANES // HEAD_DIM)

    for l in range(L):
        mod = mod_all[l]
        lam_init = 0.8 - 0.6 * math.exp(-0.3 * l)
        proj = _in_proj(x2, mod, norm1_g[l], bf(w_in[l]), S)
        gains = jnp.stack([tile2(diff_qn[l]), tile2(diff_kn[l]), tile2(moba_qn[l]), tile2(moba_kn[l])])
        qk = _qk_prep(proj, gains, cos, sin)
        y_a = _rg_lru_branch(proj, S, conv_w[l], conv_b[l], _pair_block_diag(lru_wa[l]), lru_ba[l],
                             _pair_block_diag(lru_wx[l]), lru_bx[l], lru_lambda[l])
        lam_params = jnp.stack([diff_lq1[l], diff_lk1[l], diff_lq2[l], diff_lk2[l]])
        y_b = _diff_attention(qk, proj, S, lam_params, diff_subln[l], lam_init)
        y_c = _moba_attention(qk, proj, S)
        x2 = _merge_out(x2, y_a, y_b, y_c, proj, gate_b[l], mod, bf(w_br_a[l]), bf(w_br_b[l]),
                        bf(w_br_c[l]), bf(w_out[l]), S)
        if l % 2 == 0:
            x2 = _dense_ffn(x2, mod, norm2_g[l], ffn_w1[l // 2], ffn_w3[l // 2], ffn_w2[l // 2], S)
        else:
            x2 = _moe_ffn(x2, mod, norm2_g[l], router_w[l // 2], router_b[l // 2], moe_w1[l // 2],
                          moe_w3[l // 2], moe_w2[l // 2], S)
    return x2.reshape(B, S, D)
```

```python
import functools
import math

import jax
import jax.numpy as jnp
from jax import lax
from jax.experimental import pallas as pl
from jax.experimental.pallas import tpu as pltpu

F32 = jnp.float32
BF16 = jnp.bfloat16
HIGHEST = lax.Precision.HIGHEST

HEAD_DIM = 64
ROPE_THETA = 10000.0
RNN_BLOCKS = 16
CONV_W = 4
LRU_C = 8.0
DIFF_HEADS = 4
MOBA_HEADS = 8
MOBA_BLOCK = 256
MOBA_TOPK = 3
N_BRANCH = 3
N_EXPERTS = 8
TOP_K = 2
EPS = 1e-6
NEG = -1e30

LANES = 128
VMEM_LIMIT = 56 * 1024 * 1024

ROW_TILE = 512


def _cparams(*sem):
    return pltpu.CompilerParams(dimension_semantics=sem, vmem_limit_bytes=VMEM_LIMIT)


def _modulated_norm(x, g, mod, base):
    ms = jnp.mean(x * x, axis=-1, keepdims=True)
    y = x * lax.rsqrt(ms + EPS) * g
    return y * (1.0 + mod[base + 1:base + 2]) + mod[base:base + 1]


def _dot_nt(a, b, **kw):
    return lax.dot_general(a, b, (((1,), (1,)), ((), ())), preferred_element_type=F32, **kw)


def _ada_kernel(c_ref, w_ref, b_ref, o_ref):
    o_ref[0] = jnp.dot(c_ref[...], w_ref[0], preferred_element_type=F32, precision=HIGHEST) + b_ref[0]


def _ada_mod(c, ada_w, ada_b):
    L, D, N = ada_w.shape
    B = c.shape[0]
    tn = 1536
    return pl.pallas_call(
        _ada_kernel,
        grid=(L, N // tn),
        in_specs=[pl.BlockSpec((B, D), lambda l, j: (0, 0)),
                  pl.BlockSpec((1, D, tn), lambda l, j: (l, 0, j)),
                  pl.BlockSpec((1, 1, tn), lambda l, j: (l, 0, j))],
        out_specs=pl.BlockSpec((1, B, tn), lambda l, j: (l, 0, j)),
        out_shape=jax.ShapeDtypeStruct((L, B, N), F32),
        compiler_params=_cparams("parallel", "parallel"),
        name="ada_mod",
    )(c, ada_w, ada_b.reshape(L, 1, N))


IN_LOADS = 8
IN_COL_CHUNK = 1024


def _inproj_kernel(x_ref, mod_ref, g_ref, w_ref, o_ref, wb):
    s = pl.program_id(0)

    @pl.when(s < IN_LOADS)
    def _():
        rows = w_ref.shape[1]
        wb[pl.ds(pl.multiple_of(s * rows, rows), rows), :] = w_ref[0].astype(BF16)

    @pl.when(s >= IN_LOADS)
    def _():
        h = _modulated_norm(x_ref[...], g_ref[...], mod_ref[0], 0).astype(BF16)
        for c in range(wb.shape[1] // IN_COL_CHUNK):
            cols = slice(c * IN_COL_CHUNK, (c + 1) * IN_COL_CHUNK)
            o_ref[:, cols] = jnp.dot(h, wb[:, cols], preferred_element_type=F32).astype(BF16)


def _in_proj(x2, mod, g, w_in, layer, seq):
    T, D = x2.shape
    N = w_in.shape[2]
    tm = ROW_TILE
    per_b = seq // tm
    assert D % (16 * IN_LOADS) == 0 and N % IN_COL_CHUNK == 0
    tile = lambda s: jnp.maximum(s - IN_LOADS, 0)
    return pl.pallas_call(
        _inproj_kernel,
        grid=(IN_LOADS + T // tm,),
        in_specs=[pl.BlockSpec((tm, D), lambda s: (tile(s), 0)),
                  pl.BlockSpec((1, 6, D), lambda s: (tile(s) // per_b, 0, 0)),
                  pl.BlockSpec((1, D), lambda s: (0, 0)),
                  pl.BlockSpec((1, D // IN_LOADS, N), lambda s: (layer, jnp.minimum(s, IN_LOADS - 1), 0))],
        out_specs=pl.BlockSpec((tm, N), lambda s: (tile(s), 0)),
        out_shape=jax.ShapeDtypeStruct((T, N), BF16),
        scratch_shapes=[pltpu.VMEM((D, N), BF16)],
        compiler_params=_cparams("arbitrary"),
        name="in_proj",
    )(x2, mod, g.reshape(1, D), w_in)


def _rope_kernel(pos_ref, inv_ref, sign_ref, cos_ref, sin_ref):
    ang = pos_ref[...] * inv_ref[...]
    cos_ref[...] = jnp.cos(ang)
    sin_ref[...] = jnp.sin(ang) * sign_ref[...]


def _rope_tables(positions):
    T = positions.size
    pos = positions.reshape(T, 1).astype(F32)
    inv = 1.0 / (ROPE_THETA ** (jnp.arange(0, HEAD_DIM, 2, dtype=F32) / HEAD_DIM))
    half = HEAD_DIM // 2
    inv128 = jnp.tile(inv, LANES // half).reshape(1, LANES)
    sign = jnp.tile(jnp.concatenate([-jnp.ones((half,), F32), jnp.ones((half,), F32)]),
                    LANES // HEAD_DIM).reshape(1, LANES)
    tm = 1024
    return pl.pallas_call(
        _rope_kernel,
        grid=(T // tm,),
        in_specs=[pl.BlockSpec((tm, 1), lambda i: (i, 0)),
                  pl.BlockSpec((1, LANES), lambda i: (0, 0)),
                  pl.BlockSpec((1, LANES), lambda i: (0, 0))],
        out_specs=[pl.BlockSpec((tm, LANES), lambda i: (i, 0))] * 2,
        out_shape=[jax.ShapeDtypeStruct((T, LANES), F32)] * 2,
        compiler_params=_cparams("parallel"),
        name="rope_tables",
    )(pos, inv128, sign)


def _qkprep_kernel(dq_ref, dk_ref, mq_ref, mk_ref, gain_ref, cos_ref, sin_ref, seg_ref, o_ref):
    cos = cos_ref[...]
    sin = sin_ref[...]
    seg = seg_ref[...]
    lane = lax.broadcasted_iota(jnp.int32, cos.shape, 1)
    first_half = (lane % HEAD_DIM) < (HEAD_DIM // 2)
    width = dq_ref.shape[1]
    for gi, ref in enumerate((dq_ref, dk_ref, mq_ref, mk_ref)):
        gain = gain_ref[gi:gi + 1, :]
        for cb in range(width // LANES):
            x = ref[:, cb * LANES:(cb + 1) * LANES].astype(F32)
            sq = x * x
            sq_hi = sq.astype(BF16)
            sq_lo = (sq - sq_hi.astype(F32)).astype(BF16)
            ms = (jnp.dot(sq_hi, seg, preferred_element_type=F32)
                  + jnp.dot(sq_lo, seg, preferred_element_type=F32))
            y = x * lax.rsqrt(ms + EPS) * gain
            swapped = jnp.where(first_half, pltpu.roll(y, LANES - HEAD_DIM // 2, 1),
                                pltpu.roll(y, HEAD_DIM // 2, 1))
            r = y * cos + swapped * sin
            if gi % 2 == 0:
                r = r * (math.log2(math.e) / math.sqrt(HEAD_DIM))
            col = gi * width + cb * LANES
            o_ref[:, col:col + LANES] = r.astype(BF16)


def _qk_prep(proj, gains, cos, sin):
    T = proj.shape[0]
    width = DIFF_HEADS * 2 * HEAD_DIM
    seg = jnp.kron(jnp.eye(LANES // HEAD_DIM, dtype=F32),
                   jnp.full((HEAD_DIM, HEAD_DIM), 1.0 / HEAD_DIM, F32)).astype(BF16)
    tm = ROW_TILE
    col_blocks = (4, 5, 7, 8)
    in_specs = [pl.BlockSpec((tm, width), functools.partial(lambda i, c: (i, c), c=c)) for c in col_blocks]
    in_specs += [pl.BlockSpec((4, LANES), lambda i: (0, 0)),
                 pl.BlockSpec((tm, LANES), lambda i: (i, 0)),
                 pl.BlockSpec((tm, LANES), lambda i: (i, 0)),
                 pl.BlockSpec((LANES, LANES), lambda i: (0, 0))]
    return pl.pallas_call(
        _qkprep_kernel,
        grid=(T // tm,),
        in_specs=in_specs,
        out_specs=pl.BlockSpec((tm, 4 * width), lambda i: (i, 0)),
        out_shape=jax.ShapeDtypeStruct((T, 4 * width), BF16),
        compiler_params=_cparams("parallel"),
        name="qk_prep",
    )(proj, proj, proj, proj, gains, cos, sin, seg)


def _gelu_tanh(x):
    return 0.5 * x * (1.0 + jnp.tanh(math.sqrt(2.0 / math.pi) * (x + 0.044715 * x * x * x)))


SUBLANES = 8
LRU_LANE_BLOCKS = 2


def _rglru_kernel(x_ref, g_ref, cw_ref, cb_ref, wa_ref, ba_ref, wx_ref, bx_ref, lam_ref, o_ref, *scratch):
    S = x_ref.shape[0]
    steps = S // SUBLANES
    pitch = steps + SUBLANES
    nblk = x_ref.shape[1] // LANES
    per_blk = len(scratch) // nblk
    sub = lax.broadcasted_iota(jnp.int32, (SUBLANES, LANES), 0)

    for blk in range(nblk):
        slab, fold_x, a_buf, u_buf, p_buf = scratch[blk * per_blk:(blk + 1) * per_blk]
        lanes = slice(blk * LANES, (blk + 1) * LANES)

        for s in range(SUBLANES):
            slab[s * pitch:s * pitch + steps, :] = x_ref[s * steps:(s + 1) * steps, lanes].astype(F32)

        def fold(step, carry):
            fold_x[pl.ds(pl.multiple_of(step * SUBLANES, SUBLANES), SUBLANES), :] = (
                slab[pl.ds(step, SUBLANES, stride=pitch), :])
            return carry

        lax.fori_loop(0, steps, fold, 0, unroll=8)
        xf = fold_x[...]

        def head(j):
            tail = xf[(steps - j) * SUBLANES:, :]
            vregs = [jnp.where(sub == 0, 0.0, pltpu.roll(tail[v * SUBLANES:(v + 1) * SUBLANES, :], 1, 0))
                     for v in range(j)]
            return jnp.concatenate(vregs + [xf[:(steps - j) * SUBLANES, :]], axis=0)

        xc = cb_ref[:, lanes] + cw_ref[CONV_W - 1:CONV_W, lanes] * xf
        for j in range(1, CONV_W):
            xc = xc + cw_ref[CONV_W - 1 - j:CONV_W - j, lanes] * head(j)

        xcb = xc.astype(BF16)
        r = jax.nn.sigmoid(jnp.dot(xcb, wa_ref[blk], preferred_element_type=F32) + ba_ref[:, lanes])
        gi = jax.nn.sigmoid(jnp.dot(xcb, wx_ref[blk], preferred_element_type=F32) + bx_ref[:, lanes])
        neg_lam = -lam_ref[:, lanes]
        softplus = jnp.maximum(neg_lam, 0.0) + jnp.log1p(jnp.exp(-jnp.abs(neg_lam)))
        a = jnp.exp((-LRU_C) * r * softplus)
        gap = 1.0 - a * a
        mult = jnp.where(gap > 0.0, gap * lax.rsqrt(gap), 0.0)
        a_buf[...] = a
        u_buf[...] = mult * gi * xc

    def scan(step, carry):
        rows = pl.ds(pl.multiple_of(step * SUBLANES, SUBLANES), SUBLANES)
        out = []
        for blk in range(nblk):
            _, _, a_buf, u_buf, p_buf = scratch[blk * per_blk:(blk + 1) * per_blk]
            h, p = carry[blk]
            a = a_buf[rows, :]
            h = a * h + u_buf[rows, :]
            p = a * p
            u_buf[rows, :] = h
            p_buf[rows, :] = p
            out.append((h, p))
        return tuple(out)

    ones = jnp.ones((SUBLANES, LANES), F32)
    last = lax.fori_loop(0, steps, scan, ((jnp.zeros_like(ones), ones),) * nblk, unroll=8)

    for blk in range(nblk):
        slab, fold_x, a_buf, u_buf, p_buf = scratch[blk * per_blk:(blk + 1) * per_blk]
        lanes = slice(blk * LANES, (blk + 1) * LANES)
        h_end, p_end = last[blk]
        h_in = [jnp.zeros((1, LANES), F32)]
        for s in range(SUBLANES - 1):
            h_in.append(h_end[s:s + 1, :] + p_end[s:s + 1, :] * h_in[s])
        h_in = jnp.concatenate(h_in, axis=0)
        u_buf[...] = u_buf[...] + p_buf[...] * jnp.concatenate([h_in] * steps, axis=0)

        def unfold(step, carry):
            slab[pl.ds(step, SUBLANES, stride=pitch), :] = (
                u_buf[pl.ds(pl.multiple_of(step * SUBLANES, SUBLANES), SUBLANES), :])
            return carry

        lax.fori_loop(0, steps, unfold, 0, unroll=8)
        for s in range(SUBLANES):
            rows = slice(s * steps, (s + 1) * steps)
            h = slab[s * pitch:s * pitch + steps, :]
            o_ref[rows, lanes] = (_gelu_tanh(g_ref[rows, lanes].astype(F32)) * h).astype(BF16)


def _rg_lru_branch(proj, seq, conv_w, conv_b, wa2, ba, wx2, bx, lam):
    T = proj.shape[0]
    C = conv_w.shape[1]
    B = T // seq
    nblk = LRU_LANE_BLOCKS
    W = nblk * LANES
    g_off = C // W
    steps = seq // SUBLANES
    assert seq % (SUBLANES * SUBLANES) == 0 and C % W == 0
    vec = lambda v: v.reshape(1, C)
    cols = lambda rows: pl.BlockSpec((rows, W), lambda b, n: (0, n))
    per_blk = [pltpu.VMEM((SUBLANES * (steps + SUBLANES), LANES), F32)] + [pltpu.VMEM((seq, LANES), F32)] * 4
    return pl.pallas_call(
        _rglru_kernel,
        grid=(B, C // W),
        in_specs=[pl.BlockSpec((seq, W), lambda b, n: (b, n)),
                  pl.BlockSpec((seq, W), lambda b, n: (b, g_off + n)),
                  cols(CONV_W), cols(1),
                  pl.BlockSpec((nblk, LANES, LANES), lambda b, n: (n, 0, 0)), cols(1),
                  pl.BlockSpec((nblk, LANES, LANES), lambda b, n: (n, 0, 0)), cols(1), cols(1)],
        out_specs=pl.BlockSpec((seq, W), lambda b, n: (b, n)),
        out_shape=jax.ShapeDtypeStruct((T, C), BF16),
        scratch_shapes=per_blk * nblk,
        compiler_params=_cparams("parallel", "parallel"),
        name="rg_lru",
    )(proj, proj, conv_w, vec(conv_b), wa2, vec(ba), wx2, vec(bx), vec(lam))


def _pair_block_diag(w):
    n, d, _ = w.shape
    z = jnp.zeros((n // 2, 2, d, 2, d), w.dtype)
    z = z.at[:, 0, :, 0, :].set(w[0::2]).at[:, 1, :, 1, :].set(w[1::2])
    return z.reshape(n // 2, 2 * d, 2 * d).astype(BF16)


ATTN_TILE = 512


def _lane_tile(x, n):
    return jnp.concatenate([x] * n, axis=1)


def _softmax_update(s, state, v_aug):
    row_max = jnp.max(s, axis=-1, keepdims=True)
    if state is None:
        m_new = jnp.broadcast_to(row_max, (s.shape[0], LANES))
        p = jnp.exp2(s - _lane_tile(m_new, s.shape[1] // LANES)).astype(BF16)
        return m_new, jnp.dot(p, v_aug, preferred_element_type=F32)
    m_old, acc = state
    m_new = jnp.maximum(m_old, row_max)
    p = jnp.exp2(s - _lane_tile(m_new, s.shape[1] // LANES)).astype(BF16)
    alpha = _lane_tile(jnp.exp2(m_old - m_new), acc.shape[1] // LANES)
    return m_new, alpha * acc + jnp.dot(p, v_aug, preferred_element_type=F32)


def _softmax_finish(state):
    _, acc = state
    return acc[:, :LANES] / acc[:, LANES:]


def _causal_mask(t):
    return lax.broadcasted_iota(jnp.int32, (t, t), 1) <= lax.broadcasted_iota(jnp.int32, (t, t), 0)


def _with_ones(v):
    return jnp.concatenate([v, jnp.ones_like(v)], axis=1)


def _diffattn_kernel(q_ref, k_ref, v_ref, lam_ref, g_ref, o_ref, *, lam_init):
    t = ATTN_TILE
    nq = q_ref.shape[0] // t
    lane = lax.broadcasted_iota(jnp.int32, (t, LANES), 1)
    causal = _causal_mask(t)
    lp = lam_ref[...]
    lam = (jnp.exp(jnp.sum(lp[0:1] * lp[1:2], axis=-1, keepdims=True))
           - jnp.exp(jnp.sum(lp[2:3] * lp[3:4], axis=-1, keepdims=True)) + lam_init)
    for i in range(nq):
        q = q_ref[i * t:(i + 1) * t, :]
        zero = jnp.zeros_like(q)
        q1 = jnp.where(lane < HEAD_DIM, q, zero)
        q2 = jnp.where(lane < HEAD_DIM, zero, q)
        st1 = st2 = None
        for j in range(i + 1):
            k = k_ref[j * t:(j + 1) * t, :]
            v_aug = _with_ones(v_ref[j * t:(j + 1) * t, :])
            s1 = _dot_nt(q1, k)
            s2 = _dot_nt(q2, k)
            if j == i:
                s1 = jnp.where(causal, s1, NEG)
                s2 = jnp.where(causal, s2, NEG)
            st1 = _softmax_update(s1, st1, v_aug)
            st2 = _softmax_update(s2, st2, v_aug)
        o = _softmax_finish(st1) - lam * _softmax_finish(st2)
        ms = jnp.mean(o * o, axis=-1, keepdims=True)
        o_ref[i * t:(i + 1) * t, :] = (o * lax.rsqrt(ms + EPS) * g_ref[...] * (1.0 - lam_init)).astype(BF16)


def _diff_attention(qk, proj, seq, lam_params, subln_g, lam_init):
    T = qk.shape[0]
    B = T // seq
    H = DIFF_HEADS
    return pl.pallas_call(
        functools.partial(_diffattn_kernel, lam_init=lam_init),
        grid=(B, H),
        in_specs=[pl.BlockSpec((seq, LANES), lambda b, h: (b, h)),
                  pl.BlockSpec((seq, LANES), lambda b, h: (b, H + h)),
                  pl.BlockSpec((seq, LANES), lambda b, h: (b, 24 + h)),
                  pl.BlockSpec((4, HEAD_DIM), lambda b, h: (0, 0)),
                  pl.BlockSpec((1, LANES), lambda b, h: (0, 0))],
        out_specs=pl.BlockSpec((seq, LANES), lambda b, h: (b, h)),
        out_shape=jax.ShapeDtypeStruct((T, H * LANES), BF16),
        compiler_params=_cparams("parallel", "parallel"),
        name="diff_attn",
    )(qk, qk, proj, lam_params, subln_g.reshape(1, LANES))


def _moba_kernel(q_ref, k_ref, v_ref, o_ref, kmean):
    t = ATTN_TILE
    nq = q_ref.shape[0] // t
    nb = k_ref.shape[0] // MOBA_BLOCK
    nb8 = -(-nb // 8) * 8
    lane_row = lax.broadcasted_iota(jnp.int32, (1, LANES), 1)

    kmean[...] = jnp.zeros_like(kmean)
    for blk in range(nb):
        mean = jnp.mean(k_ref[blk * MOBA_BLOCK:(blk + 1) * MOBA_BLOCK, :].astype(F32), axis=0, keepdims=True)
        kmean[blk:blk + 1, :] = jnp.where(lane_row < HEAD_DIM, mean, 0.0)
        kmean[HEAD_DIM + blk:HEAD_DIM + blk + 1, :] = jnp.where(lane_row < HEAD_DIM, 0.0, mean)

    lane = lax.broadcasted_iota(jnp.int32, (t, LANES), 1)
    row = lax.broadcasted_iota(jnp.int32, (t, LANES), 0)
    causal = _causal_mask(t)
    gate_blk = lax.broadcasted_iota(jnp.int32, (nb8, t), 0)
    gate_col = lax.broadcasted_iota(jnp.int32, (nb8, t), 1)
    unused_rows = jnp.full((HEAD_DIM - nb8, t), NEG, F32)

    def keys(j):
        key_blk = (j * t + row) // MOBA_BLOCK
        k_aug = jnp.concatenate([k_ref[j * t:(j + 1) * t, :], (lane == key_blk).astype(BF16)], axis=1)
        return k_aug, _with_ones(v_ref[j * t:(j + 1) * t, :])

    for i in range(nq):
        q = q_ref[i * t:(i + 1) * t, :]
        zero = jnp.zeros_like(q)
        q_a = jnp.where(lane < HEAD_DIM, q, zero)
        q_b = jnp.where(lane < HEAD_DIM, zero, q)

        gate = _dot_nt(kmean[...], q.astype(F32), precision=HIGHEST)
        own = (i * t + gate_col) // MOBA_BLOCK
        past = gate_blk < own
        last_past = min(nb, (i * t + t - 1) // MOBA_BLOCK)
        bias_rows = []
        for head in range(2):
            g = jnp.where(past, gate[head * HEAD_DIM:head * HEAD_DIM + nb8, :], NEG)
            rank = jnp.zeros(g.shape, jnp.int32)
            for jb in range(last_past):
                other = g[jb:jb + 1, :]
                beats = (other > g) | ((other == g) & (jb < gate_blk))
                rank = rank + beats.astype(jnp.int32)
            allowed = (past & (rank < MOBA_TOPK)) | (gate_blk == own)
            bias_rows += [jnp.where(allowed, 0.0, NEG), unused_rows]
        bias_a = jnp.concatenate(bias_rows, axis=0).T
        bias_b = pltpu.roll(bias_a, HEAD_DIM, 1)
        qa_aug = jnp.concatenate([q_a, bias_a.astype(BF16)], axis=1)
        qb_aug = jnp.concatenate([q_b, bias_b.astype(BF16)], axis=1)

        st_a = st_b = None
        for j in [i] + list(range(i)):
            k_aug, v_aug = keys(j)
            s_a = _dot_nt(qa_aug, k_aug)
            s_b = _dot_nt(qb_aug, k_aug)
            if j == i:
                s_a = jnp.where(causal, s_a, NEG)
                s_b = jnp.where(causal, s_b, NEG)
            st_a = _softmax_update(s_a, st_a, v_aug)
            st_b = _softmax_update(s_b, st_b, v_aug)
        o = jnp.where(lane < HEAD_DIM, _softmax_finish(st_a), _softmax_finish(st_b))
        o_ref[i * t:(i + 1) * t, :] = o.astype(BF16)


def _moba_attention(qk, proj, seq):
    T = qk.shape[0]
    B = T // seq
    HP = MOBA_HEADS // 2
    return pl.pallas_call(
        _moba_kernel,
        grid=(B, HP),
        in_specs=[pl.BlockSpec((seq, LANES), lambda b, h: (b, 8 + h)),
                  pl.BlockSpec((seq, LANES), lambda b, h: (b, 12 + h)),
                  pl.BlockSpec((seq, LANES), lambda b, h: (b, 36 + h))],
        out_specs=pl.BlockSpec((seq, LANES), lambda b, h: (b, h)),
        out_shape=jax.ShapeDtypeStruct((T, HP * LANES), BF16),
        scratch_shapes=[pltpu.VMEM((LANES, LANES), F32)],
        compiler_params=_cparams("parallel", "parallel"),
        name="moba_attn",
    )(qk, qk, proj)


def _merge_kernel(x_ref, ya_ref, yb_ref, yc_ref, ga_ref, gbb_ref, gc_ref, gb_ref, mod_ref,
                  wa_ref, wb_ref, wc_ref, wo_ref, o_ref):
    merged = None
    branches = ((ya_ref, wa_ref, ga_ref), (yb_ref, wb_ref, gbb_ref), (yc_ref, wc_ref, gc_ref))
    for n, (y_ref, w_ref, gl_ref) in enumerate(branches):
        gate = jax.nn.sigmoid(gl_ref[...].astype(F32) + gb_ref[n:n + 1, :])
        term = gate * jnp.dot(y_ref[...], w_ref[...], preferred_element_type=F32)
        merged = term if merged is None else merged + term
    mix = jnp.dot(merged.astype(BF16), wo_ref[...], preferred_element_type=F32)
    o_ref[...] = x_ref[...] + mod_ref[0][2:3] * mix


def _merge_out(x2, ya, yb, yc, proj, gate_b, mod, wa, wb, wc, wo, seq):
    T, D = x2.shape
    tm = ROW_TILE
    per_b = seq // tm
    gbr_block = 5120 // D
    assert gbr_block * D == 5120
    wspec = lambda w: pl.BlockSpec(w.shape, lambda i: (0, 0))
    return pl.pallas_call(
        _merge_kernel,
        grid=(T // tm,),
        in_specs=[pl.BlockSpec((tm, D), lambda i: (i, 0)),
                  pl.BlockSpec((tm, ya.shape[1]), lambda i: (i, 0)),
                  pl.BlockSpec((tm, yb.shape[1]), lambda i: (i, 0)),
                  pl.BlockSpec((tm, yc.shape[1]), lambda i: (i, 0)),
                  pl.BlockSpec((tm, D), lambda i: (i, gbr_block)),
                  pl.BlockSpec((tm, D), lambda i: (i, gbr_block + 1)),
                  pl.BlockSpec((tm, D), lambda i: (i, gbr_block + 2)),
                  pl.BlockSpec((N_BRANCH, D), lambda i: (0, 0)),
                  pl.BlockSpec((1, 6, D), lambda i: (i // per_b, 0, 0)),
                  wspec(wa), wspec(wb), wspec(wc), wspec(wo)],
        out_specs=pl.BlockSpec((tm, D), lambda i: (i, 0)),
        out_shape=jax.ShapeDtypeStruct((T, D), F32),
        compiler_params=_cparams("parallel"),
        name="merge_out",
    )(x2, ya, yb, yc, proj, proj, proj, gate_b, mod, wa, wb, wc, wo)


FF_CHUNK = 256
FF_LOADS = 8


def _stash_weight_slab(k, w1_ref, w3_ref, w2_ref, wb1, wb3, wb2):
    for w_ref, wb in ((w1_ref, wb1), (w3_ref, wb3), (w2_ref, wb2)):
        rows, cols = w_ref.shape[-2:]
        start = pl.multiple_of(k * rows, rows)
        wb[pl.ds(start, rows), :] = w_ref[...].reshape(rows, cols).astype(BF16)


def _swiglu_resident(h, wb1, wb3, wb2):
    acc = None
    for f in range(wb2.shape[0] // FF_CHUNK):
        cols = slice(f * FF_CHUNK, (f + 1) * FF_CHUNK)
        g = jnp.dot(h, wb1[:, cols], preferred_element_type=F32)
        u = jnp.dot(h, wb3[:, cols], preferred_element_type=F32)
        act = (g * jax.nn.sigmoid(g) * u).astype(BF16)
        part = jnp.dot(act, wb2[cols, :], preferred_element_type=F32)
        acc = part if acc is None else acc + part
    return acc


def _ffn_weight_specs(D, F, lead, slab_index):
    assert D % (16 * FF_LOADS) == 0 and F % (16 * FF_LOADS) == 0 and F % FF_CHUNK == 0
    index = lambda *a: slab_index(*a) + (0,)
    return [pl.BlockSpec(lead + (D // FF_LOADS, F), index), pl.BlockSpec(lead + (D // FF_LOADS, F), index),
            pl.BlockSpec(lead + (F // FF_LOADS, D), index)]


def _ffn_kernel(x_ref, mod_ref, g_ref, w1_ref, w3_ref, w2_ref, o_ref, wb1, wb3, wb2):
    s = pl.program_id(0)

    @pl.when(s < FF_LOADS)
    def _():
        _stash_weight_slab(s, w1_ref, w3_ref, w2_ref, wb1, wb3, wb2)

    @pl.when(s >= FF_LOADS)
    def _():
        h = _modulated_norm(x_ref[...], g_ref[...], mod_ref[0], 3).astype(BF16)
        o_ref[...] = x_ref[...] + mod_ref[0][5:6] * _swiglu_resident(h, wb1, wb3, wb2)


def _dense_ffn(x2, mod, g, w1, w3, w2, seq):
    T, D = x2.shape
    F = w1.shape[1]
    tm = ROW_TILE
    per_b = seq // tm
    tile = lambda s: jnp.maximum(s - FF_LOADS, 0)
    return pl.pallas_call(
        _ffn_kernel,
        grid=(FF_LOADS + T // tm,),
        in_specs=[pl.BlockSpec((tm, D), lambda s: (tile(s), 0)),
                  pl.BlockSpec((1, 6, D), lambda s: (tile(s) // per_b, 0, 0)),
                  pl.BlockSpec((1, D), lambda s: (0, 0))]
        + _ffn_weight_specs(D, F, (), lambda s: (jnp.minimum(s, FF_LOADS - 1),)),
        out_specs=pl.BlockSpec((tm, D), lambda s: (tile(s), 0)),
        out_shape=jax.ShapeDtypeStruct((T, D), F32),
        scratch_shapes=[pltpu.VMEM((D, F), BF16), pltpu.VMEM((D, F), BF16), pltpu.VMEM((F, D), BF16)],
        compiler_params=_cparams("arbitrary"),
        name="dense_ffn",
    )(x2, mod, g.reshape(1, D), w1, w3, w2)


def _router_kernel(x_ref, mod_ref, g_ref, rw_ref, rb_ref, h_ref, comb_ref, sel_ref):
    h = _modulated_norm(x_ref[...], g_ref[...], mod_ref[0], 3)
    h_ref[...] = h
    logits = jnp.dot(h, rw_ref[...], preferred_element_type=F32, precision=HIGHEST) + rb_ref[...]
    E = logits.shape[1]
    lane = lax.broadcasted_iota(jnp.int32, logits.shape, 1)
    v1 = jnp.max(logits, axis=-1, keepdims=True)
    i1 = jnp.min(jnp.where(logits == v1, lane, E), axis=-1, keepdims=True)
    rest = jnp.where(lane == i1, -jnp.inf, logits)
    v2 = jnp.max(rest, axis=-1, keepdims=True)
    i2 = jnp.min(jnp.where(rest == v2, lane, E), axis=-1, keepdims=True)
    e2 = jnp.exp(v2 - v1)
    w1 = 1.0 / (1.0 + e2)
    w2 = e2 / (1.0 + e2)
    comb_ref[...] = jnp.where(lane == i1, w1, 0.0) + jnp.where(lane == i2, w2, 0.0)
    sel_ref[...] = ((lane == i1) | (lane == i2)).astype(jnp.int32)


def _router(x2, mod, g, router_w, router_b, seq):
    T, D = x2.shape
    E = router_w.shape[1]
    tm = ROW_TILE
    per_b = seq // tm
    return pl.pallas_call(
        _router_kernel,
        grid=(T // tm,),
        in_specs=[pl.BlockSpec((tm, D), lambda i: (i, 0)),
                  pl.BlockSpec((1, 6, D), lambda i: (i // per_b, 0, 0)),
                  pl.BlockSpec((1, D), lambda i: (0, 0)),
                  pl.BlockSpec((D, E), lambda i: (0, 0)),
                  pl.BlockSpec((1, E), lambda i: (0, 0))],
        out_specs=[pl.BlockSpec((tm, D), lambda i: (i, 0)),
                   pl.BlockSpec((tm, E), lambda i: (i, 0)),
                   pl.BlockSpec((tm, E), lambda i: (i, 0))],
        out_shape=[jax.ShapeDtypeStruct((T, D), F32),
                   jax.ShapeDtypeStruct((T, E), F32),
                   jax.ShapeDtypeStruct((T, E), jnp.int32)],
        compiler_params=_cparams("parallel"),
        name="moe_router",
    )(x2, mod, g.reshape(1, D), router_w, router_b.reshape(1, E))


STEP_LOAD, STEP_TILE, STEP_IDLE = 0, 1, 2


def _moe_kernel(kind_ref, e_ref, c_ref, t_ref, xs_ref, w1_ref, w3_ref, w2_ref, o_ref, wb1, wb3, wb2):
    s = pl.program_id(0)

    @pl.when(kind_ref[s] == STEP_LOAD)
    def _():
        _stash_weight_slab(c_ref[s], w1_ref, w3_ref, w2_ref, wb1, wb3, wb2)

    @pl.when(kind_ref[s] == STEP_TILE)
    def _():
        o_ref[...] = _swiglu_resident(xs_ref[...].astype(BF16), wb1, wb3, wb2)


def _moe_schedule(tiles_per_expert, n_tiles, nc):
    E = tiles_per_expert.shape[0]
    first_tile = jnp.cumsum(tiles_per_expert) - tiles_per_expert
    first_step = nc * jnp.arange(E, dtype=jnp.int32) + first_tile
    step = jnp.arange(nc * E + n_tiles, dtype=jnp.int32)
    e = jnp.sum((first_step[None, :] <= step[:, None]).astype(jnp.int32), axis=1) - 1
    off = step - first_step[e]
    kind = jnp.where(off < nc, STEP_LOAD, jnp.where(off < nc + tiles_per_expert[e], STEP_TILE, STEP_IDLE))
    chunk = jnp.clip(off, 0, nc - 1)
    last_used = jnp.maximum(jnp.sum(tiles_per_expert) - 1, 0)
    tile = jnp.clip(first_tile[e] + jnp.clip(off - nc, 0, tiles_per_expert[e] - 1), 0, last_used)
    as_i32 = lambda a: a.astype(jnp.int32)
    return as_i32(kind), as_i32(e), as_i32(chunk), as_i32(tile)


def _moe_grouped(xs, tiles_per_expert, w1, w3, w2, tm):
    N, D = xs.shape
    E, _, F = w1.shape
    n_tiles = N // tm
    kind, e, slab, tile = _moe_schedule(tiles_per_expert, n_tiles, FF_LOADS)
    grid_spec = pltpu.PrefetchScalarGridSpec(
        num_scalar_prefetch=4,
        grid=(FF_LOADS * E + n_tiles,),
        in_specs=[pl.BlockSpec((tm, D), lambda s, kind, e, c, t: (t[s], 0))]
        + _ffn_weight_specs(D, F, (1,), lambda s, kind, e, c, t: (e[s], c[s])),
        out_specs=pl.BlockSpec((tm, D), lambda s, kind, e, c, t: (t[s], 0)),
        scratch_shapes=[pltpu.VMEM((D, F), BF16), pltpu.VMEM((D, F), BF16), pltpu.VMEM((F, D), BF16)],
    )
    return pl.pallas_call(
        _moe_kernel,
        grid_spec=grid_spec,
        out_shape=jax.ShapeDtypeStruct((N, D), F32),
        compiler_params=_cparams("arbitrary"),
        name="moe_experts",
    )(kind, e, slab, tile, xs, w1, w3, w2)


def _combine_kernel(x_ref, y1_ref, y2_ref, w_ref, mod_ref, o_ref):
    w = w_ref[...]
    y = w[:, 0:1] * y1_ref[...] + w[:, 1:2] * y2_ref[...]
    o_ref[...] = x_ref[...] + mod_ref[0][5:6] * y


def _moe_combine(x2, y1, y2, w12, mod, seq):
    T, D = x2.shape
    tm = ROW_TILE
    per_b = seq // tm
    return pl.pallas_call(
        _combine_kernel,
        grid=(T // tm,),
        in_specs=[pl.BlockSpec((tm, D), lambda i: (i, 0)),
                  pl.BlockSpec((tm, D), lambda i: (i, 0)),
                  pl.BlockSpec((tm, D), lambda i: (i, 0)),
                  pl.BlockSpec((tm, 2), lambda i: (i, 0)),
                  pl.BlockSpec((1, 6, D), lambda i: (i // per_b, 0, 0))],
        out_specs=pl.BlockSpec((tm, D), lambda i: (i, 0)),
        out_shape=jax.ShapeDtypeStruct((T, D), F32),
        compiler_params=_cparams("parallel"),
        name="moe_combine",
    )(x2, y1, y2, w12, mod)


def _moe_ffn(x2, mod, g, router_w, router_b, w1, w3, w2, seq):
    T, D = x2.shape
    E = router_w.shape[1]
    tm = ROW_TILE
    h, comb, sel = _router(x2, mod, g, router_w, router_b, seq)

    counts = jnp.sum(sel, axis=0)
    padded = ((counts + tm - 1) // tm) * tm
    group_end = jnp.cumsum(padded)
    group_start = group_end - padded
    rank = jnp.cumsum(sel, axis=0) - sel
    dest = group_start[None, :] + rank
    n_rows = TOP_K * T + E * tm
    n_tiles = n_rows // tm
    tile_start = jnp.arange(n_tiles, dtype=jnp.int32) * tm
    tile_expert = jnp.sum((group_end[None, :] <= tile_start[:, None]).astype(jnp.int32), axis=1)
    tile_expert = jnp.minimum(tile_expert, jnp.max(jnp.where(counts > 0, jnp.arange(E), 0))).astype(jnp.int32)

    lane = jnp.arange(E, dtype=jnp.int32)[None, :]
    e_lo = jnp.min(jnp.where(sel > 0, lane, E), axis=1)
    e_hi = jnp.max(jnp.where(sel > 0, lane, -1), axis=1)
    pick = lambda a, e: jnp.sum(jnp.where(lane == e[:, None], a, 0), axis=1)
    w12 = jnp.stack([pick(comb, e_lo), pick(comb, e_hi)], axis=1)

    tok = jnp.arange(T, dtype=jnp.int32)
    tok_sorted = jnp.sort(jnp.concatenate([e_lo * T + tok, e_hi * T + tok])) % T
    row = jnp.arange(n_rows, dtype=jnp.int32)
    row_expert = jnp.repeat(tile_expert, tm)
    first_sorted = (jnp.cumsum(counts) - counts)[row_expert]
    src = tok_sorted[jnp.clip(first_sorted + row - group_start[row_expert], 0, TOP_K * T - 1)]

    take_rows = lambda a, idx: a.at[idx].get(mode="promise_in_bounds")
    ys = _moe_grouped(take_rows(h, src), padded // tm, w1, w3, w2, tm)
    y1 = take_rows(ys, pick(dest, e_lo))
    y2 = take_rows(ys, pick(dest, e_hi))
    return _moe_combine(x2, y1, y2, w12, mod, seq)


def kernel(x, c, positions, ada_w, ada_b, norm1_g, norm2_g, w_in, gate_b, conv_w, conv_b, lru_wa, lru_ba,
           lru_wx, lru_bx, lru_lambda, diff_qn, diff_kn, diff_lq1, diff_lk1, diff_lq2, diff_lk2, diff_subln,
           moba_qn, moba_kn, w_br_a, w_br_b, w_br_c, w_out, ffn_w1, ffn_w3, ffn_w2, router_w, router_b,
           moe_w1, moe_w3, moe_w2):
    B, S, D = x.shape
    L = ada_w.shape[0]
    T = B * S
    assert S % MOBA_BLOCK == 0 and S // MOBA_BLOCK <= 56 and S % 1024 == 0
    x2 = x.reshape(T, D)
    mod_all = _ada_mod(c, ada_w, ada_b).reshape(L, B, 6, D)
    cos, sin = _rope_tables(positions)
    bf = lambda w: w.astype(BF16)
    tile2 = lambda v: jnp.tile(v, LANES // HEAD_DIM)

    for l in range(L):
        mod = mod_all[l]
        lam_init = 0.8 - 0.6 * math.exp(-0.3 * l)
        proj = _in_proj(x2, mod, norm1_g[l], w_in, l, S)
        gains = jnp.stack([tile2(diff_qn[l]), tile2(diff_kn[l]), tile2(moba_qn[l]), tile2(moba_kn[l])])
        qk = _qk_prep(proj, gains, cos, sin)
        y_a = _rg_lru_branch(proj, S, conv_w[l], conv_b[l], _pair_block_diag(lru_wa[l]), lru_ba[l],
                             _pair_block_diag(lru_wx[l]), lru_bx[l], lru_lambda[l])
        lam_params = jnp.stack([diff_lq1[l], diff_lk1[l], diff_lq2[l], diff_lk2[l]])
        y_b = _diff_attention(qk, proj, S, lam_params, diff_subln[l], lam_init)
        y_c = _moba_attention(qk, proj, S)
        x2 = _merge_out(x2, y_a, y_b, y_c, proj, gate_b[l], mod, bf(w_br_a[l]), bf(w_br_b[l]),
                        bf(w_br_c[l]), bf(w_out[l]), S)
        if l % 2 == 0:
            x2 = _dense_ffn(x2, mod, norm2_g[l], ffn_w1[l // 2], ffn_w3[l // 2], ffn_w2[l // 2], S)
        else:
            x2 = _moe_ffn(x2, mod, norm2_g[l], router_w[l // 2], router_b[l // 2], moe_w1[l // 2],
                          moe_w3[l // 2], moe_w2[l // 2], S)
    return x2.reshape(B, S, D)
```

```python
import functools
import math

import jax
import jax.numpy as jnp
from jax import lax
from jax.experimental import pallas as pl
from jax.experimental.pallas import tpu as pltpu

F32 = jnp.float32
BF16 = jnp.bfloat16
HIGHEST = lax.Precision.HIGHEST

HEAD_DIM = 64
ROPE_THETA = 10000.0
RNN_BLOCKS = 16
CONV_W = 4
LRU_C = 8.0
DIFF_HEADS = 4
MOBA_HEADS = 8
MOBA_BLOCK = 256
MOBA_TOPK = 3
N_BRANCH = 3
N_EXPERTS = 8
TOP_K = 2
EPS = 1e-6
NEG = -1e30

LANES = 128
VMEM_LIMIT = 56 * 1024 * 1024

ROW_TILE = 512


def _cparams(*sem):
    return pltpu.CompilerParams(dimension_semantics=sem, vmem_limit_bytes=VMEM_LIMIT)


def _modulated_norm(x, g, mod, base):
    ms = jnp.mean(x * x, axis=-1, keepdims=True)
    y = x * lax.rsqrt(ms + EPS) * g
    return y * (1.0 + mod[base + 1:base + 2]) + mod[base:base + 1]


def _dot_nt(a, b, **kw):
    return lax.dot_general(a, b, (((1,), (1,)), ((), ())), preferred_element_type=F32, **kw)


def _ada_kernel(c_ref, w_ref, b_ref, o_ref):
    o_ref[0] = jnp.dot(c_ref[...], w_ref[0], preferred_element_type=F32, precision=HIGHEST) + b_ref[0]


def _ada_mod(c, ada_w, ada_b):
    L, D, N = ada_w.shape
    B = c.shape[0]
    tn = 1536
    return pl.pallas_call(
        _ada_kernel,
        grid=(L, N // tn),
        in_specs=[pl.BlockSpec((B, D), lambda l, j: (0, 0)),
                  pl.BlockSpec((1, D, tn), lambda l, j: (l, 0, j)),
                  pl.BlockSpec((1, 1, tn), lambda l, j: (l, 0, j))],
        out_specs=pl.BlockSpec((1, B, tn), lambda l, j: (l, 0, j)),
        out_shape=jax.ShapeDtypeStruct((L, B, N), F32),
        compiler_params=_cparams("parallel", "parallel"),
        name="ada_mod",
    )(c, ada_w, ada_b.reshape(L, 1, N))


IN_LOADS = 8
IN_COL_CHUNK = 1024


def _inproj_kernel(x_ref, mod_ref, g_ref, w_ref, o_ref, wb):
    s = pl.program_id(0)

    @pl.when(s < IN_LOADS)
    def _():
        rows = w_ref.shape[1]
        wb[pl.ds(pl.multiple_of(s * rows, rows), rows), :] = w_ref[0].astype(BF16)

    @pl.when(s >= IN_LOADS)
    def _():
        h = _modulated_norm(x_ref[...], g_ref[...], mod_ref[0], 0).astype(BF16)
        for c in range(wb.shape[1] // IN_COL_CHUNK):
            cols = slice(c * IN_COL_CHUNK, (c + 1) * IN_COL_CHUNK)
            o_ref[:, cols] = jnp.dot(h, wb[:, cols], preferred_element_type=F32).astype(BF16)


def _in_proj(x2, mod, g, w_in, layer, seq):
    T, D = x2.shape
    N = w_in.shape[2]
    tm = ROW_TILE
    per_b = seq // tm
    assert D % (16 * IN_LOADS) == 0 and N % IN_COL_CHUNK == 0
    tile = lambda s: jnp.maximum(s - IN_LOADS, 0)
    return pl.pallas_call(
        _inproj_kernel,
        grid=(IN_LOADS + T // tm,),
        in_specs=[pl.BlockSpec((tm, D), lambda s: (tile(s), 0)),
                  pl.BlockSpec((1, 6, D), lambda s: (tile(s) // per_b, 0, 0)),
                  pl.BlockSpec((1, D), lambda s: (0, 0)),
                  pl.BlockSpec((1, D // IN_LOADS, N), lambda s: (layer, jnp.minimum(s, IN_LOADS - 1), 0))],
        out_specs=pl.BlockSpec((tm, N), lambda s: (tile(s), 0)),
        out_shape=jax.ShapeDtypeStruct((T, N), BF16),
        scratch_shapes=[pltpu.VMEM((D, N), BF16)],
        compiler_params=_cparams("arbitrary"),
        name="in_proj",
    )(x2, mod, g.reshape(1, D), w_in)


def _rope_kernel(pos_ref, inv_ref, sign_ref, cos_ref, sin_ref):
    ang = pos_ref[...] * inv_ref[...]
    cos_ref[...] = jnp.cos(ang)
    sin_ref[...] = jnp.sin(ang) * sign_ref[...]


def _rope_tables(positions):
    T = positions.size
    pos = positions.reshape(T, 1).astype(F32)
    inv = 1.0 / (ROPE_THETA ** (jnp.arange(0, HEAD_DIM, 2, dtype=F32) / HEAD_DIM))
    half = HEAD_DIM // 2
    inv128 = jnp.tile(inv, LANES // half).reshape(1, LANES)
    sign = jnp.tile(jnp.concatenate([-jnp.ones((half,), F32), jnp.ones((half,), F32)]),
                    LANES // HEAD_DIM).reshape(1, LANES)
    tm = 1024
    return pl.pallas_call(
        _rope_kernel,
        grid=(T // tm,),
        in_specs=[pl.BlockSpec((tm, 1), lambda i: (i, 0)),
                  pl.BlockSpec((1, LANES), lambda i: (0, 0)),
                  pl.BlockSpec((1, LANES), lambda i: (0, 0))],
        out_specs=[pl.BlockSpec((tm, LANES), lambda i: (i, 0))] * 2,
        out_shape=[jax.ShapeDtypeStruct((T, LANES), F32)] * 2,
        compiler_params=_cparams("parallel"),
        name="rope_tables",
    )(pos, inv128, sign)


def _qkprep_kernel(dq_ref, dk_ref, mq_ref, mk_ref, gain_ref, cos_ref, sin_ref, seg_ref, o_ref):
    cos = cos_ref[...]
    sin = sin_ref[...]
    seg = seg_ref[...]
    lane = lax.broadcasted_iota(jnp.int32, cos.shape, 1)
    first_half = (lane % HEAD_DIM) < (HEAD_DIM // 2)
    width = dq_ref.shape[1]
    for gi, ref in enumerate((dq_ref, dk_ref, mq_ref, mk_ref)):
        gain = gain_ref[gi:gi + 1, :]
        for cb in range(width // LANES):
            x = ref[:, cb * LANES:(cb + 1) * LANES].astype(F32)
            sq = x * x
            sq_hi = sq.astype(BF16)
            sq_lo = (sq - sq_hi.astype(F32)).astype(BF16)
            ms = (jnp.dot(sq_hi, seg, preferred_element_type=F32)
                  + jnp.dot(sq_lo, seg, preferred_element_type=F32))
            y = x * lax.rsqrt(ms + EPS) * gain
            swapped = jnp.where(first_half, pltpu.roll(y, LANES - HEAD_DIM // 2, 1),
                                pltpu.roll(y, HEAD_DIM // 2, 1))
            r = y * cos + swapped * sin
            if gi % 2 == 0:
                r = r * (math.log2(math.e) / math.sqrt(HEAD_DIM))
            col = gi * width + cb * LANES
            o_ref[:, col:col + LANES] = r.astype(BF16)


def _qk_prep(proj, gains, cos, sin):
    T = proj.shape[0]
    width = DIFF_HEADS * 2 * HEAD_DIM
    seg = jnp.kron(jnp.eye(LANES // HEAD_DIM, dtype=F32),
                   jnp.full((HEAD_DIM, HEAD_DIM), 1.0 / HEAD_DIM, F32)).astype(BF16)
    tm = ROW_TILE
    col_blocks = (4, 5, 7, 8)
    in_specs = [pl.BlockSpec((tm, width), functools.partial(lambda i, c: (i, c), c=c)) for c in col_blocks]
    in_specs += [pl.BlockSpec((4, LANES), lambda i: (0, 0)),
                 pl.BlockSpec((tm, LANES), lambda i: (i, 0)),
                 pl.BlockSpec((tm, LANES), lambda i: (i, 0)),
                 pl.BlockSpec((LANES, LANES), lambda i: (0, 0))]
    return pl.pallas_call(
        _qkprep_kernel,
        grid=(T // tm,),
        in_specs=in_specs,
        out_specs=pl.BlockSpec((tm, 4 * width), lambda i: (i, 0)),
        out_shape=jax.ShapeDtypeStruct((T, 4 * width), BF16),
        compiler_params=_cparams("parallel"),
        name="qk_prep",
    )(proj, proj, proj, proj, gains, cos, sin, seg)


def _gelu_tanh(x):
    return 0.5 * x * (1.0 + jnp.tanh(math.sqrt(2.0 / math.pi) * (x + 0.044715 * x * x * x)))


SUBLANES = 8
LRU_LANE_BLOCKS = 2


def _rglru_kernel(x_ref, g_ref, cw_ref, cb_ref, wa_ref, ba_ref, wx_ref, bx_ref, lam_ref, o_ref, *scratch):
    S = x_ref.shape[0]
    steps = S // SUBLANES
    pitch = steps + SUBLANES
    nblk = x_ref.shape[1] // LANES
    per_blk = len(scratch) // nblk
    sub = lax.broadcasted_iota(jnp.int32, (SUBLANES, LANES), 0)

    for blk in range(nblk):
        slab, fold_x, a_buf, u_buf, p_buf = scratch[blk * per_blk:(blk + 1) * per_blk]
        lanes = slice(blk * LANES, (blk + 1) * LANES)

        for s in range(SUBLANES):
            slab[s * pitch:s * pitch + steps, :] = x_ref[s * steps:(s + 1) * steps, lanes].astype(F32)

        def fold(step, carry):
            fold_x[pl.ds(pl.multiple_of(step * SUBLANES, SUBLANES), SUBLANES), :] = (
                slab[pl.ds(step, SUBLANES, stride=pitch), :])
            return carry

        lax.fori_loop(0, steps, fold, 0, unroll=8)
        xf = fold_x[...]

        def head(j):
            tail = xf[(steps - j) * SUBLANES:, :]
            vregs = [jnp.where(sub == 0, 0.0, pltpu.roll(tail[v * SUBLANES:(v + 1) * SUBLANES, :], 1, 0))
                     for v in range(j)]
            return jnp.concatenate(vregs + [xf[:(steps - j) * SUBLANES, :]], axis=0)

        xc = cb_ref[:, lanes] + cw_ref[CONV_W - 1:CONV_W, lanes] * xf
        for j in range(1, CONV_W):
            xc = xc + cw_ref[CONV_W - 1 - j:CONV_W - j, lanes] * head(j)

        xcb = xc.astype(BF16)
        r = jax.nn.sigmoid(jnp.dot(xcb, wa_ref[blk], preferred_element_type=F32) + ba_ref[:, lanes])
        gi = jax.nn.sigmoid(jnp.dot(xcb, wx_ref[blk], preferred_element_type=F32) + bx_ref[:, lanes])
        neg_lam = -lam_ref[:, lanes]
        softplus = jnp.maximum(neg_lam, 0.0) + jnp.log1p(jnp.exp(-jnp.abs(neg_lam)))
        a = jnp.exp((-LRU_C) * r * softplus)
        gap = 1.0 - a * a
        mult = jnp.where(gap > 0.0, gap * lax.rsqrt(gap), 0.0)
        a_buf[...] = a
        u_buf[...] = mult * gi * xc

    def scan(step, carry):
        rows = pl.ds(pl.multiple_of(step * SUBLANES, SUBLANES), SUBLANES)
        out = []
        for blk in range(nblk):
            _, _, a_buf, u_buf, p_buf = scratch[blk * per_blk:(blk + 1) * per_blk]
            h, p = carry[blk]
            a = a_buf[rows, :]
            h = a * h + u_buf[rows, :]
            p = a * p
            u_buf[rows, :] = h
            p_buf[rows, :] = p
            out.append((h, p))
        return tuple(out)

    ones = jnp.ones((SUBLANES, LANES), F32)
    last = lax.fori_loop(0, steps, scan, ((jnp.zeros_like(ones), ones),) * nblk, unroll=8)

    for blk in range(nblk):
        slab, fold_x, a_buf, u_buf, p_buf = scratch[blk * per_blk:(blk + 1) * per_blk]
        lanes = slice(blk * LANES, (blk + 1) * LANES)
        h_end, p_end = last[blk]
        h_in = [jnp.zeros((1, LANES), F32)]
        for s in range(SUBLANES - 1):
            h_in.append(h_end[s:s + 1, :] + p_end[s:s + 1, :] * h_in[s])
        h_in = jnp.concatenate(h_in, axis=0)
        u_buf[...] = u_buf[...] + p_buf[...] * jnp.concatenate([h_in] * steps, axis=0)

        def unfold(step, carry):
            slab[pl.ds(step, SUBLANES, stride=pitch), :] = (
                u_buf[pl.ds(pl.multiple_of(step * SUBLANES, SUBLANES), SUBLANES), :])
            return carry

        lax.fori_loop(0, steps, unfold, 0, unroll=8)
        for s in range(SUBLANES):
            rows = slice(s * steps, (s + 1) * steps)
            h = slab[s * pitch:s * pitch + steps, :]
            o_ref[rows, lanes] = (_gelu_tanh(g_ref[rows, lanes].astype(F32)) * h).astype(BF16)


def _rg_lru_branch(proj, seq, conv_w, conv_b, wa2, ba, wx2, bx, lam):
    T = proj.shape[0]
    C = conv_w.shape[1]
    B = T // seq
    nblk = LRU_LANE_BLOCKS
    W = nblk * LANES
    g_off = C // W
    steps = seq // SUBLANES
    assert seq % (SUBLANES * SUBLANES) == 0 and C % W == 0
    vec = lambda v: v.reshape(1, C)
    cols = lambda rows: pl.BlockSpec((rows, W), lambda b, n: (0, n))
    per_blk = [pltpu.VMEM((SUBLANES * (steps + SUBLANES), LANES), F32)] + [pltpu.VMEM((seq, LANES), F32)] * 4
    return pl.pallas_call(
        _rglru_kernel,
        grid=(B, C // W),
        in_specs=[pl.BlockSpec((seq, W), lambda b, n: (b, n)),
                  pl.BlockSpec((seq, W), lambda b, n: (b, g_off + n)),
                  cols(CONV_W), cols(1),
                  pl.BlockSpec((nblk, LANES, LANES), lambda b, n: (n, 0, 0)), cols(1),
                  pl.BlockSpec((nblk, LANES, LANES), lambda b, n: (n, 0, 0)), cols(1), cols(1)],
        out_specs=pl.BlockSpec((seq, W), lambda b, n: (b, n)),
        out_shape=jax.ShapeDtypeStruct((T, C), BF16),
        scratch_shapes=per_blk * nblk,
        compiler_params=_cparams("parallel", "parallel"),
        name="rg_lru",
    )(proj, proj, conv_w, vec(conv_b), wa2, vec(ba), wx2, vec(bx), vec(lam))


def _pair_block_diag(w):
    n, d, _ = w.shape
    z = jnp.zeros((n // 2, 2, d, 2, d), w.dtype)
    z = z.at[:, 0, :, 0, :].set(w[0::2]).at[:, 1, :, 1, :].set(w[1::2])
    return z.reshape(n // 2, 2 * d, 2 * d).astype(BF16)


ATTN_TILE = 512


def _lane_tile(x, n):
    return jnp.concatenate([x] * n, axis=1)


def _softmax_update(s, state, v_aug):
    row_max = jnp.max(s, axis=-1, keepdims=True)
    if state is None:
        m_new = jnp.broadcast_to(row_max, (s.shape[0], LANES))
        p = jnp.exp2(s - _lane_tile(m_new, s.shape[1] // LANES)).astype(BF16)
        return m_new, jnp.dot(p, v_aug, preferred_element_type=F32)
    m_old, acc = state
    m_new = jnp.maximum(m_old, row_max)
    p = jnp.exp2(s - _lane_tile(m_new, s.shape[1] // LANES)).astype(BF16)
    alpha = _lane_tile(jnp.exp2(m_old - m_new), acc.shape[1] // LANES)
    return m_new, alpha * acc + jnp.dot(p, v_aug, preferred_element_type=F32)


def _softmax_finish(state):
    _, acc = state
    return acc[:, :LANES] / acc[:, LANES:]


def _state_rows(state, start, stop):
    m, acc = state
    return m[start:stop], acc[start:stop]


def _causal_mask(t):
    return lax.broadcasted_iota(jnp.int32, (t, t), 1) <= lax.broadcasted_iota(jnp.int32, (t, t), 0)


def _mask_diagonal(s, causal):
    t = causal.shape[0]
    if s.shape[0] == t:
        return jnp.where(causal, s, NEG)
    return jnp.concatenate([jnp.where(causal, s[:t], NEG), s[t:]], axis=0)


def _with_ones(v):
    return jnp.concatenate([v, jnp.ones_like(v)], axis=1)


def _diffattn_kernel(q_ref, k_ref, v_ref, lam_ref, g_ref, o_ref, *, lam_init):
    t = ATTN_TILE
    S = q_ref.shape[0]
    q = q_ref[...]
    lane = lax.broadcasted_iota(jnp.int32, q.shape, 1)
    zero = jnp.zeros_like(q)
    q1 = jnp.where(lane < HEAD_DIM, q, zero)
    q2 = jnp.where(lane < HEAD_DIM, zero, q)
    causal = _causal_mask(t)
    lp = lam_ref[...]
    lam = (jnp.exp(jnp.sum(lp[0:1] * lp[1:2], axis=-1, keepdims=True))
           - jnp.exp(jnp.sum(lp[2:3] * lp[3:4], axis=-1, keepdims=True)) + lam_init)
    st1 = st2 = None
    for j in range(S // t):
        k = k_ref[j * t:(j + 1) * t, :]
        v_aug = _with_ones(v_ref[j * t:(j + 1) * t, :])
        st1 = _softmax_update(_mask_diagonal(_dot_nt(q1[j * t:, :], k), causal), st1, v_aug)
        st2 = _softmax_update(_mask_diagonal(_dot_nt(q2[j * t:, :], k), causal), st2, v_aug)
        o = _softmax_finish(_state_rows(st1, 0, t)) - lam * _softmax_finish(_state_rows(st2, 0, t))
        ms = jnp.mean(o * o, axis=-1, keepdims=True)
        o_ref[j * t:(j + 1) * t, :] = (o * lax.rsqrt(ms + EPS) * g_ref[...] * (1.0 - lam_init)).astype(BF16)
        if (j + 1) * t < S:
            st1 = _state_rows(st1, t, None)
            st2 = _state_rows(st2, t, None)


def _diff_attention(qk, proj, seq, lam_params, subln_g, lam_init):
    T = qk.shape[0]
    B = T // seq
    H = DIFF_HEADS
    return pl.pallas_call(
        functools.partial(_diffattn_kernel, lam_init=lam_init),
        grid=(B, H),
        in_specs=[pl.BlockSpec((seq, LANES), lambda b, h: (b, h)),
                  pl.BlockSpec((seq, LANES), lambda b, h: (b, H + h)),
                  pl.BlockSpec((seq, LANES), lambda b, h: (b, 24 + h)),
                  pl.BlockSpec((4, HEAD_DIM), lambda b, h: (0, 0)),
                  pl.BlockSpec((1, LANES), lambda b, h: (0, 0))],
        out_specs=pl.BlockSpec((seq, LANES), lambda b, h: (b, h)),
        out_shape=jax.ShapeDtypeStruct((T, H * LANES), BF16),
        compiler_params=_cparams("parallel", "parallel"),
        name="diff_attn",
    )(qk, qk, proj, lam_params, subln_g.reshape(1, LANES))


def _moba_kernel(q_ref, k_ref, v_ref, o_ref, kmean):
    t = ATTN_TILE
    S = q_ref.shape[0]
    nb = S // MOBA_BLOCK
    nb8 = -(-nb // 8) * 8
    lane_row = lax.broadcasted_iota(jnp.int32, (1, LANES), 1)

    kmean[...] = jnp.zeros_like(kmean)
    for blk in range(nb):
        mean = jnp.mean(k_ref[blk * MOBA_BLOCK:(blk + 1) * MOBA_BLOCK, :].astype(F32), axis=0, keepdims=True)
        kmean[blk:blk + 1, :] = jnp.where(lane_row < HEAD_DIM, mean, 0.0)
        kmean[HEAD_DIM + blk:HEAD_DIM + blk + 1, :] = jnp.where(lane_row < HEAD_DIM, 0.0, mean)

    q = q_ref[...]
    lane = lax.broadcasted_iota(jnp.int32, q.shape, 1)
    zero = jnp.zeros_like(q)
    q_a = jnp.where(lane < HEAD_DIM, q, zero)
    q_b = jnp.where(lane < HEAD_DIM, zero, q)

    km = kmean[...]
    km_hi = km.astype(BF16)
    km_mid = (km - km_hi.astype(F32)).astype(BF16)
    km_lo = (km - km_hi.astype(F32) - km_mid.astype(F32)).astype(BF16)
    gate = _dot_nt(km_hi, q) + _dot_nt(km_mid, q) + _dot_nt(km_lo, q)
    gate_blk = lax.broadcasted_iota(jnp.int32, (nb8, S), 0)
    own = lax.broadcasted_iota(jnp.int32, (nb8, S), 1) // MOBA_BLOCK
    past = gate_blk < own
    unused_rows = jnp.full((HEAD_DIM - nb8, S), NEG, F32)
    bias_rows = []
    for head in range(2):
        g = jnp.where(past, gate[head * HEAD_DIM:head * HEAD_DIM + nb8, :], NEG)
        rank = jnp.zeros(g.shape, jnp.int32)
        for jb in range(nb - 1):
            other = g[jb:jb + 1, :]
            beats = (other > g) | ((other == g) & (jb < gate_blk))
            rank = rank + beats.astype(jnp.int32)
        allowed = (past & (rank < MOBA_TOPK)) | (gate_blk == own)
        bias_rows += [jnp.where(allowed, 0.0, NEG), unused_rows]
    bias_a = jnp.concatenate(bias_rows, axis=0).T
    bias_b = pltpu.roll(bias_a, HEAD_DIM, 1)
    qa_aug = jnp.concatenate([q_a, bias_a.astype(BF16)], axis=1)
    qb_aug = jnp.concatenate([q_b, bias_b.astype(BF16)], axis=1)

    causal = _causal_mask(t)
    key_lane = lax.broadcasted_iota(jnp.int32, (t, LANES), 1)
    key_row = lax.broadcasted_iota(jnp.int32, (t, LANES), 0)
    st_a = st_b = None
    for j in range(S // t):
        key_blk = (j * t + key_row) // MOBA_BLOCK
        k_aug = jnp.concatenate([k_ref[j * t:(j + 1) * t, :], (key_lane == key_blk).astype(BF16)], axis=1)
        v_aug = _with_ones(v_ref[j * t:(j + 1) * t, :])
        st_a = _softmax_update(_mask_diagonal(_dot_nt(qa_aug[j * t:, :], k_aug), causal), st_a, v_aug)
        st_b = _softmax_update(_mask_diagonal(_dot_nt(qb_aug[j * t:, :], k_aug), causal), st_b, v_aug)
        o = jnp.where(key_lane < HEAD_DIM, _softmax_finish(_state_rows(st_a, 0, t)),
                      _softmax_finish(_state_rows(st_b, 0, t)))
        o_ref[j * t:(j + 1) * t, :] = o.astype(BF16)
        if (j + 1) * t < S:
            st_a = _state_rows(st_a, t, None)
            st_b = _state_rows(st_b, t, None)


def _moba_attention(qk, proj, seq):
    T = qk.shape[0]
    B = T // seq
    HP = MOBA_HEADS // 2
    return pl.pallas_call(
        _moba_kernel,
        grid=(B, HP),
        in_specs=[pl.BlockSpec((seq, LANES), lambda b, h: (b, 8 + h)),
                  pl.BlockSpec((seq, LANES), lambda b, h: (b, 12 + h)),
                  pl.BlockSpec((seq, LANES), lambda b, h: (b, 36 + h))],
        out_specs=pl.BlockSpec((seq, LANES), lambda b, h: (b, h)),
        out_shape=jax.ShapeDtypeStruct((T, HP * LANES), BF16),
        scratch_shapes=[pltpu.VMEM((LANES, LANES), F32)],
        compiler_params=_cparams("parallel", "parallel"),
        name="moba_attn",
    )(qk, qk, proj)


def _merge_kernel(x_ref, ya_ref, yb_ref, yc_ref, ga_ref, gbb_ref, gc_ref, gb_ref, mod_ref,
                  wa_ref, wb_ref, wc_ref, wo_ref, o_ref):
    merged = None
    branches = ((ya_ref, wa_ref, ga_ref), (yb_ref, wb_ref, gbb_ref), (yc_ref, wc_ref, gc_ref))
    for n, (y_ref, w_ref, gl_ref) in enumerate(branches):
        gate = jax.nn.sigmoid(gl_ref[...].astype(F32) + gb_ref[n:n + 1, :])
        term = gate * jnp.dot(y_ref[...], w_ref[...], preferred_element_type=F32)
        merged = term if merged is None else merged + term
    mix = jnp.dot(merged.astype(BF16), wo_ref[...], preferred_element_type=F32)
    o_ref[...] = x_ref[...] + mod_ref[0][2:3] * mix


def _merge_out(x2, ya, yb, yc, proj, gate_b, mod, wa, wb, wc, wo, seq):
    T, D = x2.shape
    tm = ROW_TILE
    per_b = seq // tm
    gbr_block = 5120 // D
    assert gbr_block * D == 5120
    wspec = lambda w: pl.BlockSpec(w.shape, lambda i: (0, 0))
    return pl.pallas_call(
        _merge_kernel,
        grid=(T // tm,),
        in_specs=[pl.BlockSpec((tm, D), lambda i: (i, 0)),
                  pl.BlockSpec((tm, ya.shape[1]), lambda i: (i, 0)),
                  pl.BlockSpec((tm, yb.shape[1]), lambda i: (i, 0)),
                  pl.BlockSpec((tm, yc.shape[1]), lambda i: (i, 0)),
                  pl.BlockSpec((tm, D), lambda i: (i, gbr_block)),
                  pl.BlockSpec((tm, D), lambda i: (i, gbr_block + 1)),
                  pl.BlockSpec((tm, D), lambda i: (i, gbr_block + 2)),
                  pl.BlockSpec((N_BRANCH, D), lambda i: (0, 0)),
                  pl.BlockSpec((1, 6, D), lambda i: (i // per_b, 0, 0)),
                  wspec(wa), wspec(wb), wspec(wc), wspec(wo)],
        out_specs=pl.BlockSpec((tm, D), lambda i: (i, 0)),
        out_shape=jax.ShapeDtypeStruct((T, D), F32),
        compiler_params=_cparams("parallel"),
        name="merge_out",
    )(x2, ya, yb, yc, proj, proj, proj, gate_b, mod, wa, wb, wc, wo)


FF_CHUNK = 256
FF_LOADS = 8


def _stash_weight_slab(k, w1_ref, w3_ref, w2_ref, wb1, wb3, wb2):
    for w_ref, wb in ((w1_ref, wb1), (w3_ref, wb3), (w2_ref, wb2)):
        rows, cols = w_ref.shape[-2:]
        start = pl.multiple_of(k * rows, rows)
        wb[pl.ds(start, rows), :] = w_ref[...].reshape(rows, cols).astype(BF16)


def _swiglu_resident(h, wb1, wb3, wb2):
    acc = None
    for f in range(wb2.shape[0] // FF_CHUNK):
        cols = slice(f * FF_CHUNK, (f + 1) * FF_CHUNK)
        g = jnp.dot(h, wb1[:, cols], preferred_element_type=F32)
        u = jnp.dot(h, wb3[:, cols], preferred_element_type=F32)
        act = (g * jax.nn.sigmoid(g) * u).astype(BF16)
        part = jnp.dot(act, wb2[cols, :], preferred_element_type=F32)
        acc = part if acc is None else acc + part
    return acc


def _ffn_weight_specs(D, F, lead, slab_index):
    assert D % (16 * FF_LOADS) == 0 and F % (16 * FF_LOADS) == 0 and F % FF_CHUNK == 0
    index = lambda *a: slab_index(*a) + (0,)
    return [pl.BlockSpec(lead + (D // FF_LOADS, F), index), pl.BlockSpec(lead + (D // FF_LOADS, F), index),
            pl.BlockSpec(lead + (F // FF_LOADS, D), index)]


def _ffn_kernel(x_ref, mod_ref, g_ref, w1_ref, w3_ref, w2_ref, o_ref, wb1, wb3, wb2):
    s = pl.program_id(0)

    @pl.when(s < FF_LOADS)
    def _():
        _stash_weight_slab(s, w1_ref, w3_ref, w2_ref, wb1, wb3, wb2)

    @pl.when(s >= FF_LOADS)
    def _():
        h = _modulated_norm(x_ref[...], g_ref[...], mod_ref[0], 3).astype(BF16)
        o_ref[...] = x_ref[...] + mod_ref[0][5:6] * _swiglu_resident(h, wb1, wb3, wb2)


def _dense_ffn(x2, mod, g, w1, w3, w2, seq):
    T, D = x2.shape
    F = w1.shape[1]
    tm = ROW_TILE
    per_b = seq // tm
    tile = lambda s: jnp.maximum(s - FF_LOADS, 0)
    return pl.pallas_call(
        _ffn_kernel,
        grid=(FF_LOADS + T // tm,),
        in_specs=[pl.BlockSpec((tm, D), lambda s: (tile(s), 0)),
                  pl.BlockSpec((1, 6, D), lambda s: (tile(s) // per_b, 0, 0)),
                  pl.BlockSpec((1, D), lambda s: (0, 0))]
        + _ffn_weight_specs(D, F, (), lambda s: (jnp.minimum(s, FF_LOADS - 1),)),
        out_specs=pl.BlockSpec((tm, D), lambda s: (tile(s), 0)),
        out_shape=jax.ShapeDtypeStruct((T, D), F32),
        scratch_shapes=[pltpu.VMEM((D, F), BF16), pltpu.VMEM((D, F), BF16), pltpu.VMEM((F, D), BF16)],
        compiler_params=_cparams("arbitrary"),
        name="dense_ffn",
    )(x2, mod, g.reshape(1, D), w1, w3, w2)


def _router_kernel(x_ref, mod_ref, g_ref, rw_ref, rb_ref, h_ref, comb_ref, sel_ref):
    h = _modulated_norm(x_ref[...], g_ref[...], mod_ref[0], 3)
    h_ref[...] = h
    logits = jnp.dot(h, rw_ref[...], preferred_element_type=F32, precision=HIGHEST) + rb_ref[...]
    E = logits.shape[1]
    lane = lax.broadcasted_iota(jnp.int32, logits.shape, 1)
    v1 = jnp.max(logits, axis=-1, keepdims=True)
    i1 = jnp.min(jnp.where(logits == v1, lane, E), axis=-1, keepdims=True)
    rest = jnp.where(lane == i1, -jnp.inf, logits)
    v2 = jnp.max(rest, axis=-1, keepdims=True)
    i2 = jnp.min(jnp.where(rest == v2, lane, E), axis=-1, keepdims=True)
    e2 = jnp.exp(v2 - v1)
    w1 = 1.0 / (1.0 + e2)
    w2 = e2 / (1.0 + e2)
    comb_ref[...] = jnp.where(lane == i1, w1, 0.0) + jnp.where(lane == i2, w2, 0.0)
    sel_ref[...] = ((lane == i1) | (lane == i2)).astype(jnp.int32)


def _router(x2, mod, g, router_w, router_b, seq):
    T, D = x2.shape
    E = router_w.shape[1]
    tm = ROW_TILE
    per_b = seq // tm
    return pl.pallas_call(
        _router_kernel,
        grid=(T // tm,),
        in_specs=[pl.BlockSpec((tm, D), lambda i: (i, 0)),
                  pl.BlockSpec((1, 6, D), lambda i: (i // per_b, 0, 0)),
                  pl.BlockSpec((1, D), lambda i: (0, 0)),
                  pl.BlockSpec((D, E), lambda i: (0, 0)),
                  pl.BlockSpec((1, E), lambda i: (0, 0))],
        out_specs=[pl.BlockSpec((tm, D), lambda i: (i, 0)),
                   pl.BlockSpec((tm, E), lambda i: (i, 0)),
                   pl.BlockSpec((tm, E), lambda i: (i, 0))],
        out_shape=[jax.ShapeDtypeStruct((T, D), F32),
                   jax.ShapeDtypeStruct((T, E), F32),
                   jax.ShapeDtypeStruct((T, E), jnp.int32)],
        compiler_params=_cparams("parallel"),
        name="moe_router",
    )(x2, mod, g.reshape(1, D), router_w, router_b.reshape(1, E))


STEP_LOAD, STEP_TILE, STEP_IDLE = 0, 1, 2


def _moe_kernel(kind_ref, e_ref, c_ref, t_ref, xs_ref, w1_ref, w3_ref, w2_ref, o_ref, wb1, wb3, wb2):
    s = pl.program_id(0)

    @pl.when(kind_ref[s] == STEP_LOAD)
    def _():
        _stash_weight_slab(c_ref[s], w1_ref, w3_ref, w2_ref, wb1, wb3, wb2)

    @pl.when(kind_ref[s] == STEP_TILE)
    def _():
        o_ref[...] = _swiglu_resident(xs_ref[...].astype(BF16), wb1, wb3, wb2)

    @pl.when(kind_ref[s] == STEP_IDLE)
    def _():
        o_ref[...] = jnp.zeros_like(o_ref)


def _moe_schedule(tiles_per_expert, n_tiles, nc):
    E = tiles_per_expert.shape[0]
    first_tile = jnp.cumsum(tiles_per_expert) - tiles_per_expert
    first_step = nc * jnp.arange(E, dtype=jnp.int32) + first_tile
    step = jnp.arange(nc * E + n_tiles, dtype=jnp.int32)
    e = jnp.sum((first_step[None, :] <= step[:, None]).astype(jnp.int32), axis=1) - 1
    off = step - first_step[e]
    kind = jnp.where(off < nc, STEP_LOAD, jnp.where(off < nc + tiles_per_expert[e], STEP_TILE, STEP_IDLE))
    chunk = jnp.clip(off, 0, nc - 1)
    n_used = jnp.sum(tiles_per_expert)
    tile = jnp.clip(first_tile[e] + jnp.clip(off - nc, 0, tiles_per_expert[e] - 1), 0, jnp.maximum(n_used - 1, 0))
    tile = jnp.where(kind == STEP_IDLE, jnp.minimum(n_used + off - nc - tiles_per_expert[e], n_tiles - 1), tile)
    as_i32 = lambda a: a.astype(jnp.int32)
    return as_i32(kind), as_i32(e), as_i32(chunk), as_i32(tile)


def _moe_grouped(xs, tiles_per_expert, w1, w3, w2, tm):
    N, D = xs.shape
    E, _, F = w1.shape
    n_tiles = N // tm
    kind, e, slab, tile = _moe_schedule(tiles_per_expert, n_tiles, FF_LOADS)
    grid_spec = pltpu.PrefetchScalarGridSpec(
        num_scalar_prefetch=4,
        grid=(FF_LOADS * E + n_tiles,),
        in_specs=[pl.BlockSpec((tm, D), lambda s, kind, e, c, t: (t[s], 0))]
        + _ffn_weight_specs(D, F, (1,), lambda s, kind, e, c, t: (e[s], c[s])),
        out_specs=pl.BlockSpec((tm, D), lambda s, kind, e, c, t: (t[s], 0)),
        scratch_shapes=[pltpu.VMEM((D, F), BF16), pltpu.VMEM((D, F), BF16), pltpu.VMEM((F, D), BF16)],
    )
    return pl.pallas_call(
        _moe_kernel,
        grid_spec=grid_spec,
        out_shape=jax.ShapeDtypeStruct((N, D), F32),
        compiler_params=_cparams("arbitrary"),
        name="moe_experts",
    )(kind, e, slab, tile, xs, w1, w3, w2)


def _combine_kernel(x_ref, y1_ref, y2_ref, w_ref, mod_ref, o_ref):
    w = w_ref[...]
    y = w[:, 0:1] * y1_ref[...] + w[:, 1:2] * y2_ref[...]
    o_ref[...] = x_ref[...] + mod_ref[0][5:6] * y


def _moe_combine(x2, y1, y2, w12, mod, seq):
    T, D = x2.shape
    tm = ROW_TILE
    per_b = seq // tm
    return pl.pallas_call(
        _combine_kernel,
        grid=(T // tm,),
        in_specs=[pl.BlockSpec((tm, D), lambda i: (i, 0)),
                  pl.BlockSpec((tm, D), lambda i: (i, 0)),
                  pl.BlockSpec((tm, D), lambda i: (i, 0)),
                  pl.BlockSpec((tm, 2), lambda i: (i, 0)),
                  pl.BlockSpec((1, 6, D), lambda i: (i // per_b, 0, 0))],
        out_specs=pl.BlockSpec((tm, D), lambda i: (i, 0)),
        out_shape=jax.ShapeDtypeStruct((T, D), F32),
        compiler_params=_cparams("parallel"),
        name="moe_combine",
    )(x2, y1, y2, w12, mod)


def _moe_ffn(x2, mod, g, router_w, router_b, w1, w3, w2, seq):
    T, D = x2.shape
    E = router_w.shape[1]
    tm = ROW_TILE
    h, comb, sel = _router(x2, mod, g, router_w, router_b, seq)

    counts = jnp.sum(sel, axis=0)
    padded = ((counts + tm - 1) // tm) * tm
    group_end = jnp.cumsum(padded)
    group_start = group_end - padded
    rank = jnp.cumsum(sel, axis=0) - sel
    dest = group_start[None, :] + rank
    n_rows = TOP_K * T + E * tm
    n_tiles = n_rows // tm
    tile_start = jnp.arange(n_tiles, dtype=jnp.int32) * tm
    tile_expert = jnp.sum((group_end[None, :] <= tile_start[:, None]).astype(jnp.int32), axis=1)
    tile_expert = jnp.minimum(tile_expert, jnp.max(jnp.where(counts > 0, jnp.arange(E), 0))).astype(jnp.int32)

    lane = jnp.arange(E, dtype=jnp.int32)[None, :]
    e_lo = jnp.min(jnp.where(sel > 0, lane, E), axis=1)
    e_hi = jnp.max(jnp.where(sel > 0, lane, -1), axis=1)
    pick = lambda a, e: jnp.sum(jnp.where(lane == e[:, None], a, 0), axis=1)
    w12 = jnp.stack([pick(comb, e_lo), pick(comb, e_hi)], axis=1)

    tok = jnp.arange(T, dtype=jnp.int32)
    tok_sorted = jnp.sort(jnp.concatenate([e_lo * T + tok, e_hi * T + tok])) % T
    row = jnp.arange(n_rows, dtype=jnp.int32)
    row_expert = jnp.repeat(tile_expert, tm)
    first_sorted = (jnp.cumsum(counts) - counts)[row_expert]
    src = tok_sorted[jnp.clip(first_sorted + row - group_start[row_expert], 0, TOP_K * T - 1)]

    take_rows = lambda a, idx: a.at[idx].get(mode="promise_in_bounds")
    ys = _moe_grouped(take_rows(h, src), padded // tm, w1, w3, w2, tm)
    y1 = take_rows(ys, pick(dest, e_lo))
    y2 = take_rows(ys, pick(dest, e_hi))
    return _moe_combine(x2, y1, y2, w12, mod, seq)


def kernel(x, c, positions, ada_w, ada_b, norm1_g, norm2_g, w_in, gate_b, conv_w, conv_b, lru_wa, lru_ba,
           lru_wx, lru_bx, lru_lambda, diff_qn, diff_kn, diff_lq1, diff_lk1, diff_lq2, diff_lk2, diff_subln,
           moba_qn, moba_kn, w_br_a, w_br_b, w_br_c, w_out, ffn_w1, ffn_w3, ffn_w2, router_w, router_b,
           moe_w1, moe_w3, moe_w2):
    B, S, D = x.shape
    L = ada_w.shape[0]
    T = B * S
    assert S % MOBA_BLOCK == 0 and S // MOBA_BLOCK <= 56 and S % 1024 == 0
    x2 = x.reshape(T, D)
    mod_all = _ada_mod(c, ada_w, ada_b).reshape(L, B, 6, D)
    cos, sin = _rope_tables(positions)
    bf = lambda w: w.astype(BF16)
    tile2 = lambda v: jnp.tile(v, LANES // HEAD_DIM)

    for l in range(L):
        mod = mod_all[l]
        lam_init = 0.8 - 0.6 * math.exp(-0.3 * l)
        proj = _in_proj(x2, mod, norm1_g[l], w_in, l, S)
        gains = jnp.stack([tile2(diff_qn[l]), tile2(diff_kn[l]), tile2(moba_qn[l]), tile2(moba_kn[l])])
        qk = _qk_prep(proj, gains, cos, sin)
        y_a = _rg_lru_branch(proj, S, conv_w[l], conv_b[l], _pair_block_diag(lru_wa[l]), lru_ba[l],
                             _pair_block_diag(lru_wx[l]), lru_bx[l], lru_lambda[l])
        lam_params = jnp.stack([diff_lq1[l], diff_lk1[l], diff_lq2[l], diff_lk2[l]])
        y_b = _diff_attention(qk, proj, S, lam_params, diff_subln[l], lam_init)
        y_c = _moba_attention(qk, proj, S)
        x2 = _merge_out(x2, y_a, y_b, y_c, proj, gate_b[l], mod, bf(w_br_a[l]), bf(w_br_b[l]),
                        bf(w_br_c[l]), bf(w_out[l]), S)
        if l % 2 == 0:
            x2 = _dense_ffn(x2, mod, norm2_g[l], ffn_w1[l // 2], ffn_w3[l // 2], ffn_w2[l // 2], S)
        else:
            x2 = _moe_ffn(x2, mod, norm2_g[l], router_w[l // 2], router_b[l // 2], moe_w1[l // 2],
                          moe_w3[l // 2], moe_w2[l // 2], S)
    return x2.reshape(B, S, D)
```

```python
import functools
import math

import jax
import jax.numpy as jnp
from jax import lax
from jax.experimental import pallas as pl
from jax.experimental.pallas import tpu as pltpu

F32 = jnp.float32
BF16 = jnp.bfloat16
HIGHEST = lax.Precision.HIGHEST

HEAD_DIM = 64
ROPE_THETA = 10000.0
RNN_BLOCKS = 16
CONV_W = 4
LRU_C = 8.0
DIFF_HEADS = 4
MOBA_HEADS = 8
MOBA_BLOCK = 256
MOBA_TOPK = 3
N_BRANCH = 3
N_EXPERTS = 8
TOP_K = 2
EPS = 1e-6
NEG = -1e30

LANES = 128
VMEM_LIMIT = 56 * 1024 * 1024

ROW_TILE = 512


def _cparams(*sem):
    return pltpu.CompilerParams(dimension_semantics=sem, vmem_limit_bytes=VMEM_LIMIT)


def _modulated_norm(x, g, mod, base):
    ms = jnp.mean(x * x, axis=-1, keepdims=True)
    y = x * lax.rsqrt(ms + EPS) * g
    return y * (1.0 + mod[base + 1:base + 2]) + mod[base:base + 1]


def _dot_nt(a, b, **kw):
    return lax.dot_general(a, b, (((1,), (1,)), ((), ())), preferred_element_type=F32, **kw)


def _ada_kernel(c_ref, w_ref, b_ref, o_ref):
    o_ref[0] = jnp.dot(c_ref[...], w_ref[0], preferred_element_type=F32, precision=HIGHEST) + b_ref[0]


def _ada_mod(c, ada_w, ada_b):
    L, D, N = ada_w.shape
    B = c.shape[0]
    tn = 1536
    return pl.pallas_call(
        _ada_kernel,
        grid=(L, N // tn),
        in_specs=[pl.BlockSpec((B, D), lambda l, j: (0, 0)),
                  pl.BlockSpec((1, D, tn), lambda l, j: (l, 0, j)),
                  pl.BlockSpec((1, 1, tn), lambda l, j: (l, 0, j))],
        out_specs=pl.BlockSpec((1, B, tn), lambda l, j: (l, 0, j)),
        out_shape=jax.ShapeDtypeStruct((L, B, N), F32),
        compiler_params=_cparams("parallel", "parallel"),
        name="ada_mod",
    )(c, ada_w, ada_b.reshape(L, 1, N))


IN_LOADS = 8
IN_COL_CHUNK = 1024


def _inproj_kernel(x_ref, mod_ref, g_ref, w_ref, o_ref, wb):
    s = pl.program_id(0)

    @pl.when(s < IN_LOADS)
    def _():
        rows = w_ref.shape[1]
        wb[pl.ds(pl.multiple_of(s * rows, rows), rows), :] = w_ref[0].astype(BF16)

    @pl.when(s >= IN_LOADS)
    def _():
        h = _modulated_norm(x_ref[...], g_ref[...], mod_ref[0], 0).astype(BF16)
        for c in range(wb.shape[1] // IN_COL_CHUNK):
            cols = slice(c * IN_COL_CHUNK, (c + 1) * IN_COL_CHUNK)
            o_ref[:, cols] = jnp.dot(h, wb[:, cols], preferred_element_type=F32).astype(BF16)


def _in_proj(x2, mod, g, w_in, layer, seq):
    T, D = x2.shape
    N = w_in.shape[2]
    tm = ROW_TILE
    per_b = seq // tm
    assert D % (16 * IN_LOADS) == 0 and N % IN_COL_CHUNK == 0
    tile = lambda s: jnp.maximum(s - IN_LOADS, 0)
    return pl.pallas_call(
        _inproj_kernel,
        grid=(IN_LOADS + T // tm,),
        in_specs=[pl.BlockSpec((tm, D), lambda s: (tile(s), 0)),
                  pl.BlockSpec((1, 6, D), lambda s: (tile(s) // per_b, 0, 0)),
                  pl.BlockSpec((1, D), lambda s: (0, 0)),
                  pl.BlockSpec((1, D // IN_LOADS, N), lambda s: (layer, jnp.minimum(s, IN_LOADS - 1), 0))],
        out_specs=pl.BlockSpec((tm, N), lambda s: (tile(s), 0)),
        out_shape=jax.ShapeDtypeStruct((T, N), BF16),
        scratch_shapes=[pltpu.VMEM((D, N), BF16)],
        compiler_params=_cparams("arbitrary"),
        name="in_proj",
    )(x2, mod, g.reshape(1, D), w_in)


def _rope_kernel(pos_ref, inv_ref, sign_ref, cos_ref, sin_ref):
    ang = pos_ref[...] * inv_ref[...]
    cos_ref[...] = jnp.cos(ang)
    sin_ref[...] = jnp.sin(ang) * sign_ref[...]


def _rope_tables(positions):
    T = positions.size
    pos = positions.reshape(T, 1).astype(F32)
    inv = 1.0 / (ROPE_THETA ** (jnp.arange(0, HEAD_DIM, 2, dtype=F32) / HEAD_DIM))
    half = HEAD_DIM // 2
    inv128 = jnp.tile(inv, LANES // half).reshape(1, LANES)
    sign = jnp.tile(jnp.concatenate([-jnp.ones((half,), F32), jnp.ones((half,), F32)]),
                    LANES // HEAD_DIM).reshape(1, LANES)
    tm = 1024
    return pl.pallas_call(
        _rope_kernel,
        grid=(T // tm,),
        in_specs=[pl.BlockSpec((tm, 1), lambda i: (i, 0)),
                  pl.BlockSpec((1, LANES), lambda i: (0, 0)),
                  pl.BlockSpec((1, LANES), lambda i: (0, 0))],
        out_specs=[pl.BlockSpec((tm, LANES), lambda i: (i, 0))] * 2,
        out_shape=[jax.ShapeDtypeStruct((T, LANES), F32)] * 2,
        compiler_params=_cparams("parallel"),
        name="rope_tables",
    )(pos, inv128, sign)


def _qkprep_kernel(dq_ref, dk_ref, mq_ref, mk_ref, gain_ref, cos_ref, sin_ref, seg_ref, o_ref):
    cos = cos_ref[...]
    sin = sin_ref[...]
    seg = seg_ref[...]
    lane = lax.broadcasted_iota(jnp.int32, cos.shape, 1)
    first_half = (lane % HEAD_DIM) < (HEAD_DIM // 2)
    width = dq_ref.shape[1]
    for gi, ref in enumerate((dq_ref, dk_ref, mq_ref, mk_ref)):
        gain = gain_ref[gi:gi + 1, :]
        for cb in range(width // LANES):
            x = ref[:, cb * LANES:(cb + 1) * LANES].astype(F32)
            sq = x * x
            sq_hi = sq.astype(BF16)
            sq_lo = (sq - sq_hi.astype(F32)).astype(BF16)
            ms = (jnp.dot(sq_hi, seg, preferred_element_type=F32)
                  + jnp.dot(sq_lo, seg, preferred_element_type=F32))
            y = x * lax.rsqrt(ms + EPS) * gain
            swapped = jnp.where(first_half, pltpu.roll(y, LANES - HEAD_DIM // 2, 1),
                                pltpu.roll(y, HEAD_DIM // 2, 1))
            r = y * cos + swapped * sin
            if gi % 2 == 0:
                r = r * (math.log2(math.e) / math.sqrt(HEAD_DIM))
            col = gi * width + cb * LANES
            o_ref[:, col:col + LANES] = r.astype(BF16)


def _qk_prep(proj, gains, cos, sin):
    T = proj.shape[0]
    width = DIFF_HEADS * 2 * HEAD_DIM
    seg = jnp.kron(jnp.eye(LANES // HEAD_DIM, dtype=F32),
                   jnp.full((HEAD_DIM, HEAD_DIM), 1.0 / HEAD_DIM, F32)).astype(BF16)
    tm = ROW_TILE
    col_blocks = (4, 5, 7, 8)
    in_specs = [pl.BlockSpec((tm, width), functools.partial(lambda i, c: (i, c), c=c)) for c in col_blocks]
    in_specs += [pl.BlockSpec((4, LANES), lambda i: (0, 0)),
                 pl.BlockSpec((tm, LANES), lambda i: (i, 0)),
                 pl.BlockSpec((tm, LANES), lambda i: (i, 0)),
                 pl.BlockSpec((LANES, LANES), lambda i: (0, 0))]
    return pl.pallas_call(
        _qkprep_kernel,
        grid=(T // tm,),
        in_specs=in_specs,
        out_specs=pl.BlockSpec((tm, 4 * width), lambda i: (i, 0)),
        out_shape=jax.ShapeDtypeStruct((T, 4 * width), BF16),
        compiler_params=_cparams("parallel"),
        name="qk_prep",
    )(proj, proj, proj, proj, gains, cos, sin, seg)


def _gelu_tanh(x):
    return 0.5 * x * (1.0 + jnp.tanh(math.sqrt(2.0 / math.pi) * (x + 0.044715 * x * x * x)))


SUBLANES = 8
LRU_LANE_BLOCKS = 2


def _rglru_kernel(x_ref, g_ref, cw_ref, cb_ref, wa_ref, ba_ref, wx_ref, bx_ref, lam_ref, o_ref, *scratch):
    S = x_ref.shape[0]
    steps = S // SUBLANES
    pitch = steps + SUBLANES
    nblk = x_ref.shape[1] // LANES
    per_blk = len(scratch) // nblk
    sub = lax.broadcasted_iota(jnp.int32, (SUBLANES, LANES), 0)

    for blk in range(nblk):
        slab, fold_x, a_buf, u_buf, p_buf = scratch[blk * per_blk:(blk + 1) * per_blk]
        lanes = slice(blk * LANES, (blk + 1) * LANES)

        for s in range(SUBLANES):
            slab[s * pitch:s * pitch + steps, :] = x_ref[s * steps:(s + 1) * steps, lanes].astype(F32)

        def fold(step, carry):
            fold_x[pl.ds(pl.multiple_of(step * SUBLANES, SUBLANES), SUBLANES), :] = (
                slab[pl.ds(step, SUBLANES, stride=pitch), :])
            return carry

        lax.fori_loop(0, steps, fold, 0, unroll=8)
        xf = fold_x[...]

        def head(j):
            tail = xf[(steps - j) * SUBLANES:, :]
            vregs = [jnp.where(sub == 0, 0.0, pltpu.roll(tail[v * SUBLANES:(v + 1) * SUBLANES, :], 1, 0))
                     for v in range(j)]
            return jnp.concatenate(vregs + [xf[:(steps - j) * SUBLANES, :]], axis=0)

        xc = cb_ref[:, lanes] + cw_ref[CONV_W - 1:CONV_W, lanes] * xf
        for j in range(1, CONV_W):
            xc = xc + cw_ref[CONV_W - 1 - j:CONV_W - j, lanes] * head(j)

        xcb = xc.astype(BF16)
        r = jax.nn.sigmoid(jnp.dot(xcb, wa_ref[blk], preferred_element_type=F32) + ba_ref[:, lanes])
        gi = jax.nn.sigmoid(jnp.dot(xcb, wx_ref[blk], preferred_element_type=F32) + bx_ref[:, lanes])
        neg_lam = -lam_ref[:, lanes]
        softplus = jnp.maximum(neg_lam, 0.0) + jnp.log1p(jnp.exp(-jnp.abs(neg_lam)))
        a = jnp.exp((-LRU_C) * r * softplus)
        gap = 1.0 - a * a
        mult = jnp.where(gap > 0.0, gap * lax.rsqrt(gap), 0.0)
        a_buf[...] = a
        u_buf[...] = mult * gi * xc

    def scan(step, carry):
        rows = pl.ds(pl.multiple_of(step * SUBLANES, SUBLANES), SUBLANES)
        out = []
        for blk in range(nblk):
            _, _, a_buf, u_buf, p_buf = scratch[blk * per_blk:(blk + 1) * per_blk]
            h, p = carry[blk]
            a = a_buf[rows, :]
            h = a * h + u_buf[rows, :]
            p = a * p
            u_buf[rows, :] = h
            p_buf[rows, :] = p
            out.append((h, p))
        return tuple(out)

    ones = jnp.ones((SUBLANES, LANES), F32)
    last = lax.fori_loop(0, steps, scan, ((jnp.zeros_like(ones), ones),) * nblk, unroll=8)

    for blk in range(nblk):
        slab, fold_x, a_buf, u_buf, p_buf = scratch[blk * per_blk:(blk + 1) * per_blk]
        lanes = slice(blk * LANES, (blk + 1) * LANES)
        h_end, p_end = last[blk]
        h_in = [jnp.zeros((1, LANES), F32)]
        for s in range(SUBLANES - 1):
            h_in.append(h_end[s:s + 1, :] + p_end[s:s + 1, :] * h_in[s])
        h_in = jnp.concatenate(h_in, axis=0)
        u_buf[...] = u_buf[...] + p_buf[...] * jnp.concatenate([h_in] * steps, axis=0)

        def unfold(step, carry):
            slab[pl.ds(step, SUBLANES, stride=pitch), :] = (
                u_buf[pl.ds(pl.multiple_of(step * SUBLANES, SUBLANES), SUBLANES), :])
            return carry

        lax.fori_loop(0, steps, unfold, 0, unroll=8)
        for s in range(SUBLANES):
            rows = slice(s * steps, (s + 1) * steps)
            h = slab[s * pitch:s * pitch + steps, :]
            o_ref[rows, lanes] = (_gelu_tanh(g_ref[rows, lanes].astype(F32)) * h).astype(BF16)


def _rg_lru_branch(proj, seq, conv_w, conv_b, wa2, ba, wx2, bx, lam):
    T = proj.shape[0]
    C = conv_w.shape[1]
    B = T // seq
    nblk = LRU_LANE_BLOCKS
    W = nblk * LANES
    g_off = C // W
    steps = seq // SUBLANES
    assert seq % (SUBLANES * SUBLANES) == 0 and C % W == 0
    vec = lambda v: v.reshape(1, C)
    cols = lambda rows: pl.BlockSpec((rows, W), lambda b, n: (0, n))
    per_blk = [pltpu.VMEM((SUBLANES * (steps + SUBLANES), LANES), F32)] + [pltpu.VMEM((seq, LANES), F32)] * 4
    return pl.pallas_call(
        _rglru_kernel,
        grid=(B, C // W),
        in_specs=[pl.BlockSpec((seq, W), lambda b, n: (b, n)),
                  pl.BlockSpec((seq, W), lambda b, n: (b, g_off + n)),
                  cols(CONV_W), cols(1),
                  pl.BlockSpec((nblk, LANES, LANES), lambda b, n: (n, 0, 0)), cols(1),
                  pl.BlockSpec((nblk, LANES, LANES), lambda b, n: (n, 0, 0)), cols(1), cols(1)],
        out_specs=pl.BlockSpec((seq, W), lambda b, n: (b, n)),
        out_shape=jax.ShapeDtypeStruct((T, C), BF16),
        scratch_shapes=per_blk * nblk,
        compiler_params=_cparams("parallel", "parallel"),
        name="rg_lru",
    )(proj, proj, conv_w, vec(conv_b), wa2, vec(ba), wx2, vec(bx), vec(lam))


def _pair_block_diag(w):
    n, d, _ = w.shape
    z = jnp.zeros((n // 2, 2, d, 2, d), w.dtype)
    z = z.at[:, 0, :, 0, :].set(w[0::2]).at[:, 1, :, 1, :].set(w[1::2])
    return z.reshape(n // 2, 2 * d, 2 * d).astype(BF16)


ATTN_TILE = 512


def _lane_tile(x, n):
    return jnp.concatenate([x] * n, axis=1)


def _softmax_update(s, state, v_aug):
    row_max = jnp.max(s, axis=-1, keepdims=True)
    if state is None:
        m_new = jnp.broadcast_to(row_max, (s.shape[0], LANES))
        p = jnp.exp2(s - _lane_tile(m_new, s.shape[1] // LANES)).astype(BF16)
        return m_new, jnp.dot(p, v_aug, preferred_element_type=F32)
    m_old, acc = state
    m_new = jnp.maximum(m_old, row_max)
    p = jnp.exp2(s - _lane_tile(m_new, s.shape[1] // LANES)).astype(BF16)
    alpha = _lane_tile(jnp.exp2(m_old - m_new), acc.shape[1] // LANES)
    return m_new, alpha * acc + jnp.dot(p, v_aug, preferred_element_type=F32)


def _softmax_finish(state):
    _, acc = state
    return acc[:, :LANES] / acc[:, LANES:]


def _state_rows(state, start, stop):
    m, acc = state
    return m[start:stop], acc[start:stop]


def _causal_mask(t):
    return lax.broadcasted_iota(jnp.int32, (t, t), 1) <= lax.broadcasted_iota(jnp.int32, (t, t), 0)


def _mask_diagonal(s, causal):
    t = causal.shape[0]
    if s.shape[0] == t:
        return jnp.where(causal, s, NEG)
    return jnp.concatenate([jnp.where(causal, s[:t], NEG), s[t:]], axis=0)


def _with_ones(v):
    return jnp.concatenate([v, jnp.ones_like(v)], axis=1)


def _diffattn_kernel(q_ref, k_ref, v_ref, lam_ref, g_ref, o_ref, *, lam_init):
    t = ATTN_TILE
    S = q_ref.shape[0]
    q = q_ref[...]
    lane = lax.broadcasted_iota(jnp.int32, q.shape, 1)
    zero = jnp.zeros_like(q)
    q1 = jnp.where(lane < HEAD_DIM, q, zero)
    q2 = jnp.where(lane < HEAD_DIM, zero, q)
    causal = _causal_mask(t)
    lp = lam_ref[...]
    lam = (jnp.exp(jnp.sum(lp[0:1] * lp[1:2], axis=-1, keepdims=True))
           - jnp.exp(jnp.sum(lp[2:3] * lp[3:4], axis=-1, keepdims=True)) + lam_init)
    st1 = st2 = None
    for j in range(S // t):
        k = k_ref[j * t:(j + 1) * t, :]
        v_aug = _with_ones(v_ref[j * t:(j + 1) * t, :])
        st1 = _softmax_update(_mask_diagonal(_dot_nt(q1[j * t:, :], k), causal), st1, v_aug)
        st2 = _softmax_update(_mask_diagonal(_dot_nt(q2[j * t:, :], k), causal), st2, v_aug)
        o = _softmax_finish(_state_rows(st1, 0, t)) - lam * _softmax_finish(_state_rows(st2, 0, t))
        ms = jnp.mean(o * o, axis=-1, keepdims=True)
        o_ref[j * t:(j + 1) * t, :] = (o * lax.rsqrt(ms + EPS) * g_ref[...] * (1.0 - lam_init)).astype(BF16)
        if (j + 1) * t < S:
            st1 = _state_rows(st1, t, None)
            st2 = _state_rows(st2, t, None)


def _diff_attention(qk, proj, seq, lam_params, subln_g, lam_init):
    T = qk.shape[0]
    B = T // seq
    H = DIFF_HEADS
    return pl.pallas_call(
        functools.partial(_diffattn_kernel, lam_init=lam_init),
        grid=(B, H),
        in_specs=[pl.BlockSpec((seq, LANES), lambda b, h: (b, h)),
                  pl.BlockSpec((seq, LANES), lambda b, h: (b, H + h)),
                  pl.BlockSpec((seq, LANES), lambda b, h: (b, 24 + h)),
                  pl.BlockSpec((4, HEAD_DIM), lambda b, h: (0, 0)),
                  pl.BlockSpec((1, LANES), lambda b, h: (0, 0))],
        out_specs=pl.BlockSpec((seq, LANES), lambda b, h: (b, h)),
        out_shape=jax.ShapeDtypeStruct((T, H * LANES), BF16),
        compiler_params=_cparams("parallel", "parallel"),
        name="diff_attn",
    )(qk, qk, proj, lam_params, subln_g.reshape(1, LANES))


def _moba_kernel(q_ref, k_ref, v_ref, o_ref, kmean):
    t = ATTN_TILE
    S = q_ref.shape[0]
    nb = S // MOBA_BLOCK
    nb8 = -(-nb // 8) * 8
    lane_row = lax.broadcasted_iota(jnp.int32, (1, LANES), 1)

    kmean[...] = jnp.zeros_like(kmean)
    for blk in range(nb):
        mean = jnp.mean(k_ref[blk * MOBA_BLOCK:(blk + 1) * MOBA_BLOCK, :].astype(F32), axis=0, keepdims=True)
        kmean[blk:blk + 1, :] = jnp.where(lane_row < HEAD_DIM, mean, 0.0)
        kmean[HEAD_DIM + blk:HEAD_DIM + blk + 1, :] = jnp.where(lane_row < HEAD_DIM, 0.0, mean)

    q = q_ref[...]
    lane = lax.broadcasted_iota(jnp.int32, q.shape, 1)
    zero = jnp.zeros_like(q)
    q_a = jnp.where(lane < HEAD_DIM, q, zero)
    q_b = jnp.where(lane < HEAD_DIM, zero, q)

    km = kmean[...]
    km_hi = km.astype(BF16)
    km_mid = (km - km_hi.astype(F32)).astype(BF16)
    km_lo = (km - km_hi.astype(F32) - km_mid.astype(F32)).astype(BF16)
    gate = _dot_nt(km_hi, q) + _dot_nt(km_mid, q) + _dot_nt(km_lo, q)
    gate_blk = lax.broadcasted_iota(jnp.int32, (nb8, S), 0)
    own = lax.broadcasted_iota(jnp.int32, (nb8, S), 1) // MOBA_BLOCK
    past = gate_blk < own
    unused_rows = jnp.full((HEAD_DIM - nb8, S), NEG, F32)
    bias_rows = []
    for head in range(2):
        g = jnp.where(past, gate[head * HEAD_DIM:head * HEAD_DIM + nb8, :], NEG)
        rank = jnp.zeros(g.shape, jnp.int32)
        for jb in range(nb - 1):
            other = g[jb:jb + 1, :]
            beats = (other > g) | ((other == g) & (jb < gate_blk))
            rank = rank + beats.astype(jnp.int32)
        allowed = (past & (rank < MOBA_TOPK)) | (gate_blk == own)
        bias_rows += [jnp.where(allowed, 0.0, NEG), unused_rows]
    bias_a = jnp.concatenate(bias_rows, axis=0).T
    bias_b = pltpu.roll(bias_a, HEAD_DIM, 1)
    qa_aug = jnp.concatenate([q_a, bias_a.astype(BF16)], axis=1)
    qb_aug = jnp.concatenate([q_b, bias_b.astype(BF16)], axis=1)

    causal = _causal_mask(t)
    key_lane = lax.broadcasted_iota(jnp.int32, (t, LANES), 1)
    key_row = lax.broadcasted_iota(jnp.int32, (t, LANES), 0)
    st_a = st_b = None
    for j in range(S // t):
        key_blk = (j * t + key_row) // MOBA_BLOCK
        k_aug = jnp.concatenate([k_ref[j * t:(j + 1) * t, :], (key_lane == key_blk).astype(BF16)], axis=1)
        v_aug = _with_ones(v_ref[j * t:(j + 1) * t, :])
        st_a = _softmax_update(_mask_diagonal(_dot_nt(qa_aug[j * t:, :], k_aug), causal), st_a, v_aug)
        st_b = _softmax_update(_mask_diagonal(_dot_nt(qb_aug[j * t:, :], k_aug), causal), st_b, v_aug)
        o = jnp.where(key_lane < HEAD_DIM, _softmax_finish(_state_rows(st_a, 0, t)),
                      _softmax_finish(_state_rows(st_b, 0, t)))
        o_ref[j * t:(j + 1) * t, :] = o.astype(BF16)
        if (j + 1) * t < S:
            st_a = _state_rows(st_a, t, None)
            st_b = _state_rows(st_b, t, None)


def _moba_attention(qk, proj, seq):
    T = qk.shape[0]
    B = T // seq
    HP = MOBA_HEADS // 2
    return pl.pallas_call(
        _moba_kernel,
        grid=(B, HP),
        in_specs=[pl.BlockSpec((seq, LANES), lambda b, h: (b, 8 + h)),
                  pl.BlockSpec((seq, LANES), lambda b, h: (b, 12 + h)),
                  pl.BlockSpec((seq, LANES), lambda b, h: (b, 36 + h))],
        out_specs=pl.BlockSpec((seq, LANES), lambda b, h: (b, h)),
        out_shape=jax.ShapeDtypeStruct((T, HP * LANES), BF16),
        scratch_shapes=[pltpu.VMEM((LANES, LANES), F32)],
        compiler_params=_cparams("parallel", "parallel"),
        name="moba_attn",
    )(qk, qk, proj)


def _merge_kernel(x_ref, ya_ref, yb_ref, yc_ref, ga_ref, gbb_ref, gc_ref, gb_ref, mod_ref,
                  wa_ref, wb_ref, wc_ref, wo_ref, o_ref):
    merged = None
    branches = ((ya_ref, wa_ref, ga_ref), (yb_ref, wb_ref, gbb_ref), (yc_ref, wc_ref, gc_ref))
    for n, (y_ref, w_ref, gl_ref) in enumerate(branches):
        gate = jax.nn.sigmoid(gl_ref[...].astype(F32) + gb_ref[n:n + 1, :])
        term = gate * jnp.dot(y_ref[...], w_ref[...], preferred_element_type=F32)
        merged = term if merged is None else merged + term
    mix = jnp.dot(merged.astype(BF16), wo_ref[...], preferred_element_type=F32)
    o_ref[...] = x_ref[...] + mod_ref[0][2:3] * mix


def _merge_out(x2, ya, yb, yc, proj, gate_b, mod, wa, wb, wc, wo, seq):
    T, D = x2.shape
    tm = ROW_TILE
    per_b = seq // tm
    gbr_block = 5120 // D
    assert gbr_block * D == 5120
    wspec = lambda w: pl.BlockSpec(w.shape, lambda i: (0, 0))
    return pl.pallas_call(
        _merge_kernel,
        grid=(T // tm,),
        in_specs=[pl.BlockSpec((tm, D), lambda i: (i, 0)),
                  pl.BlockSpec((tm, ya.shape[1]), lambda i: (i, 0)),
                  pl.BlockSpec((tm, yb.shape[1]), lambda i: (i, 0)),
                  pl.BlockSpec((tm, yc.shape[1]), lambda i: (i, 0)),
                  pl.BlockSpec((tm, D), lambda i: (i, gbr_block)),
                  pl.BlockSpec((tm, D), lambda i: (i, gbr_block + 1)),
                  pl.BlockSpec((tm, D), lambda i: (i, gbr_block + 2)),
                  pl.BlockSpec((N_BRANCH, D), lambda i: (0, 0)),
                  pl.BlockSpec((1, 6, D), lambda i: (i // per_b, 0, 0)),
                  wspec(wa), wspec(wb), wspec(wc), wspec(wo)],
        out_specs=pl.BlockSpec((tm, D), lambda i: (i, 0)),
        out_shape=jax.ShapeDtypeStruct((T, D), F32),
        compiler_params=_cparams("parallel"),
        name="merge_out",
    )(x2, ya, yb, yc, proj, proj, proj, gate_b, mod, wa, wb, wc, wo)


FF_CHUNK = 256
FF_LOADS = 8


def _stash_weight_slab(k, w1_ref, w3_ref, w2_ref, wb1, wb3, wb2):
    for w_ref, wb in ((w1_ref, wb1), (w3_ref, wb3), (w2_ref, wb2)):
        rows, cols = w_ref.shape[-2:]
        start = pl.multiple_of(k * rows, rows)
        wb[pl.ds(start, rows), :] = w_ref[...].reshape(rows, cols).astype(BF16)


def _swiglu_resident(h, wb1, wb3, wb2):
    acc = None
    for f in range(wb2.shape[0] // FF_CHUNK):
        cols = slice(f * FF_CHUNK, (f + 1) * FF_CHUNK)
        g = jnp.dot(h, wb1[:, cols], preferred_element_type=F32)
        u = jnp.dot(h, wb3[:, cols], preferred_element_type=F32)
        act = (g * jax.nn.sigmoid(g) * u).astype(BF16)
        part = jnp.dot(act, wb2[cols, :], preferred_element_type=F32)
        acc = part if acc is None else acc + part
    return acc


def _ffn_weight_specs(D, F, lead, slab_index):
    assert D % (16 * FF_LOADS) == 0 and F % (16 * FF_LOADS) == 0 and F % FF_CHUNK == 0
    index = lambda *a: slab_index(*a) + (0,)
    return [pl.BlockSpec(lead + (D // FF_LOADS, F), index), pl.BlockSpec(lead + (D // FF_LOADS, F), index),
            pl.BlockSpec(lead + (F // FF_LOADS, D), index)]


def _ffn_kernel(x_ref, mod_ref, g_ref, w1_ref, w3_ref, w2_ref, o_ref, wb1, wb3, wb2):
    s = pl.program_id(0)

    @pl.when(s < FF_LOADS)
    def _():
        _stash_weight_slab(s, w1_ref, w3_ref, w2_ref, wb1, wb3, wb2)

    @pl.when(s >= FF_LOADS)
    def _():
        h = _modulated_norm(x_ref[...], g_ref[...], mod_ref[0], 3).astype(BF16)
        o_ref[...] = x_ref[...] + mod_ref[0][5:6] * _swiglu_resident(h, wb1, wb3, wb2)


def _dense_ffn(x2, mod, g, w1, w3, w2, seq):
    T, D = x2.shape
    F = w1.shape[1]
    tm = ROW_TILE
    per_b = seq // tm
    tile = lambda s: jnp.maximum(s - FF_LOADS, 0)
    return pl.pallas_call(
        _ffn_kernel,
        grid=(FF_LOADS + T // tm,),
        in_specs=[pl.BlockSpec((tm, D), lambda s: (tile(s), 0)),
                  pl.BlockSpec((1, 6, D), lambda s: (tile(s) // per_b, 0, 0)),
                  pl.BlockSpec((1, D), lambda s: (0, 0))]
        + _ffn_weight_specs(D, F, (), lambda s: (jnp.minimum(s, FF_LOADS - 1),)),
        out_specs=pl.BlockSpec((tm, D), lambda s: (tile(s), 0)),
        out_shape=jax.ShapeDtypeStruct((T, D), F32),
        scratch_shapes=[pltpu.VMEM((D, F), BF16), pltpu.VMEM((D, F), BF16), pltpu.VMEM((F, D), BF16)],
        compiler_params=_cparams("arbitrary"),
        name="dense_ffn",
    )(x2, mod, g.reshape(1, D), w1, w3, w2)


def _router_kernel(x_ref, mod_ref, g_ref, rw_ref, rb_ref, h_ref, comb_ref, sel_ref):
    h = _modulated_norm(x_ref[...], g_ref[...], mod_ref[0], 3)
    h_ref[...] = h
    logits = jnp.dot(h, rw_ref[...], preferred_element_type=F32, precision=HIGHEST) + rb_ref[...]
    E = logits.shape[1]
    lane = lax.broadcasted_iota(jnp.int32, logits.shape, 1)
    v1 = jnp.max(logits, axis=-1, keepdims=True)
    i1 = jnp.min(jnp.where(logits == v1, lane, E), axis=-1, keepdims=True)
    rest = jnp.where(lane == i1, -jnp.inf, logits)
    v2 = jnp.max(rest, axis=-1, keepdims=True)
    i2 = jnp.min(jnp.where(rest == v2, lane, E), axis=-1, keepdims=True)
    e2 = jnp.exp(v2 - v1)
    w1 = 1.0 / (1.0 + e2)
    w2 = e2 / (1.0 + e2)
    comb_ref[...] = jnp.where(lane == i1, w1, 0.0) + jnp.where(lane == i2, w2, 0.0)
    sel_ref[...] = ((lane == i1) | (lane == i2)).astype(jnp.int32)


def _router(x2, mod, g, router_w, router_b, seq):
    T, D = x2.shape
    E = router_w.shape[1]
    tm = ROW_TILE
    per_b = seq // tm
    return pl.pallas_call(
        _router_kernel,
        grid=(T // tm,),
        in_specs=[pl.BlockSpec((tm, D), lambda i: (i, 0)),
                  pl.BlockSpec((1, 6, D), lambda i: (i // per_b, 0, 0)),
                  pl.BlockSpec((1, D), lambda i: (0, 0)),
                  pl.BlockSpec((D, E), lambda i: (0, 0)),
                  pl.BlockSpec((1, E), lambda i: (0, 0))],
        out_specs=[pl.BlockSpec((tm, D), lambda i: (i, 0)),
                   pl.BlockSpec((tm, E), lambda i: (i, 0)),
                   pl.BlockSpec((tm, E), lambda i: (i, 0))],
        out_shape=[jax.ShapeDtypeStruct((T, D), F32),
                   jax.ShapeDtypeStruct((T, E), F32),
                   jax.ShapeDtypeStruct((T, E), jnp.int32)],
        compiler_params=_cparams("parallel"),
        name="moe_router",
    )(x2, mod, g.reshape(1, D), router_w, router_b.reshape(1, E))


STEP_LOAD, STEP_TILE, STEP_ZERO, STEP_NOOP = 0, 1, 2, 3
MOE_SLOTS = 2


def _moe_kernel(kind_ref, slot_ref, slab_ref, we_ref, wk_ref, t_ref, xs_ref, w1_ref, w3_ref, w2_ref, *rest):
    o_ref, wb1, wb3, wb2 = rest[-4:]
    s = pl.program_id(0)
    slot = slot_ref[s]

    @pl.when(kind_ref[s] == STEP_LOAD)
    def _():
        _stash_weight_slab(slab_ref[s], w1_ref, w3_ref, w2_ref, wb1.at[slot], wb3.at[slot], wb2.at[slot])

    @pl.when(kind_ref[s] == STEP_TILE)
    def _():
        o_ref[...] = _swiglu_resident(xs_ref[...].astype(BF16), wb1.at[slot], wb3.at[slot], wb2.at[slot])

    @pl.when(kind_ref[s] == STEP_ZERO)
    def _():
        o_ref[...] = jnp.zeros_like(o_ref)


def _moe_schedule(first_tile, tiles, t0, nt):
    E = tiles.shape[0]
    L = FF_LOADS
    i32 = jnp.int32
    ids = jnp.arange(E, dtype=i32)
    lo = jnp.clip(first_tile - t0, 0, nt)
    cnt = jnp.clip(first_tile + tiles - t0, 0, nt) - lo
    active = cnt > 0
    n_act = jnp.sum(active.astype(i32))
    order = jnp.cumsum(active.astype(i32)) - 1
    pick = ((order[None, :] == ids[:, None]) & active[None, :]).astype(i32)
    act_e = jnp.sum(pick * ids[None, :], axis=1)
    act_cnt = jnp.sum(pick * cnt[None, :], axis=1)
    act_lo = jnp.sum(pick * lo[None, :], axis=1)
    has_next = ids + 1 < n_act
    blk_len = jnp.where(ids < n_act, act_cnt + jnp.where(has_next, L, 0), 0)
    blk_start = L + jnp.cumsum(blk_len) - blk_len
    n_sched = jnp.where(n_act > 0, L + jnp.sum(blk_len), 0)
    n_used = jnp.sum(cnt)

    s = jnp.arange(L * E + nt, dtype=i32)
    k = jnp.clip(jnp.sum(((blk_start[None, :] <= s[:, None]) & (ids[None, :] < n_act)).astype(i32), axis=1) - 1,
                 0, E - 1)
    p = s - blk_start[k]
    c = act_cnt[k]
    pairs = jnp.where(has_next[k], jnp.minimum(c, L), 0)
    in_pairs = p < 2 * pairs
    tail = p - 2 * pairs
    in_block = (s >= L) & (s < n_sched)
    tile_step = in_block & jnp.where(in_pairs, p % 2 == 0, c > pairs)
    load_step = (in_block & ~tile_step) | ((s < L) & (n_act > 0))
    nth_tile = jnp.where(in_pairs, p // 2, pairs + tail)
    nth_slab = jnp.where(s < L, s, jnp.where(in_pairs, p // 2, pairs + tail))
    load_k = jnp.where(s < L, 0, k + 1)
    zero_step = (s >= n_sched) & (s - n_sched < nt - n_used)

    kind = jnp.where(tile_step, STEP_TILE, jnp.where(load_step, STEP_LOAD,
                                                     jnp.where(zero_step, STEP_ZERO, STEP_NOOP)))
    slot = jnp.where(load_step, load_k, k) % MOE_SLOTS
    slab = jnp.clip(nth_slab, 0, L - 1)
    last_load = lax.cummax(jnp.where(load_step, s, 0), axis=0)
    we = act_e[jnp.clip(load_k, 0, E - 1)][last_load]
    wk = slab[last_load]
    own = jnp.where(tile_step, act_lo[k] + nth_tile, jnp.where(zero_step, n_used + s - n_sched, nt - 1))
    tile = jnp.clip(lax.cummin(own, axis=0, reverse=True), 0, nt - 1)
    return tuple(a.astype(i32) for a in (kind, slot, slab, we, wk, tile))


def _moe_grouped(xs, n_rows, first_tile, tiles, t0, w1, w3, w2, tm, ys=None):
    D = xs.shape[1]
    nt = xs.shape[0] // tm
    E, _, F = w1.shape
    table = _moe_schedule(first_tile, tiles, t0, nt)
    n_pre = len(table)
    in_specs = [pl.BlockSpec((tm, D), lambda s, *pre: (pre[5][s], 0))]
    in_specs += _ffn_weight_specs(D, F, (1,), lambda s, *pre: (pre[3][s], pre[4][s]))
    operands = [xs, w1, w3, w2]
    aliases = {}
    if ys is not None:
        in_specs.append(pl.BlockSpec(memory_space=pl.ANY))
        operands.append(ys)
        aliases = {n_pre + len(operands) - 1: 0}
    grid_spec = pltpu.PrefetchScalarGridSpec(
        num_scalar_prefetch=n_pre,
        grid=(FF_LOADS * E + nt,),
        in_specs=in_specs,
        out_specs=pl.BlockSpec((tm, D), lambda s, *pre: (t0 + pre[5][s], 0)),
        scratch_shapes=[pltpu.VMEM((MOE_SLOTS, D, F), BF16), pltpu.VMEM((MOE_SLOTS, D, F), BF16),
                        pltpu.VMEM((MOE_SLOTS, F, D), BF16)],
    )
    return pl.pallas_call(
        _moe_kernel,
        grid_spec=grid_spec,
        out_shape=jax.ShapeDtypeStruct((n_rows, D), F32),
        input_output_aliases=aliases,
        compiler_params=_cparams("arbitrary"),
        name="moe_experts",
    )(*table, *operands)


def _combine_kernel(x_ref, y1_ref, y2_ref, w_ref, mod_ref, o_ref):
    w = w_ref[...]
    y = w[:, 0:1] * y1_ref[...] + w[:, 1:2] * y2_ref[...]
    o_ref[...] = x_ref[...] + mod_ref[0][5:6] * y


def _moe_combine(x2, y1, y2, w12, mod, seq):
    T, D = x2.shape
    tm = ROW_TILE
    per_b = seq // tm
    return pl.pallas_call(
        _combine_kernel,
        grid=(T // tm,),
        in_specs=[pl.BlockSpec((tm, D), lambda i: (i, 0)),
                  pl.BlockSpec((tm, D), lambda i: (i, 0)),
                  pl.BlockSpec((tm, D), lambda i: (i, 0)),
                  pl.BlockSpec((tm, 2), lambda i: (i, 0)),
                  pl.BlockSpec((1, 6, D), lambda i: (i // per_b, 0, 0))],
        out_specs=pl.BlockSpec((tm, D), lambda i: (i, 0)),
        out_shape=jax.ShapeDtypeStruct((T, D), F32),
        compiler_params=_cparams("parallel"),
        name="moe_combine",
    )(x2, y1, y2, w12, mod)


def _moe_ffn(x2, mod, g, router_w, router_b, w1, w3, w2, seq):
    T, D = x2.shape
    E = router_w.shape[1]
    tm = ROW_TILE
    h, comb, sel = _router(x2, mod, g, router_w, router_b, seq)

    counts = jnp.sum(sel, axis=0)
    padded = ((counts + tm - 1) // tm) * tm
    group_end = jnp.cumsum(padded)
    group_start = group_end - padded
    rank = jnp.cumsum(sel, axis=0) - sel
    dest = group_start[None, :] + rank
    n_rows = TOP_K * T + E * tm
    n_tiles = n_rows // tm
    tile_start = jnp.arange(n_tiles, dtype=jnp.int32) * tm
    tile_expert = jnp.sum((group_end[None, :] <= tile_start[:, None]).astype(jnp.int32), axis=1)
    tile_expert = jnp.minimum(tile_expert, jnp.max(jnp.where(counts > 0, jnp.arange(E), 0))).astype(jnp.int32)

    lane = jnp.arange(E, dtype=jnp.int32)[None, :]
    e_lo = jnp.min(jnp.where(sel > 0, lane, E), axis=1)
    e_hi = jnp.max(jnp.where(sel > 0, lane, -1), axis=1)
    pick = lambda a, e: jnp.sum(jnp.where(lane == e[:, None], a, 0), axis=1)
    w12 = jnp.stack([pick(comb, e_lo), pick(comb, e_hi)], axis=1)

    tok = jnp.arange(T, dtype=jnp.int32)
    tok_sorted = jnp.sort(jnp.concatenate([e_lo * T + tok, e_hi * T + tok])) % T
    row = jnp.arange(n_rows, dtype=jnp.int32)
    row_expert = jnp.repeat(tile_expert, tm)
    first_sorted = (jnp.cumsum(counts) - counts)[row_expert]
    src = tok_sorted[jnp.clip(first_sorted + row - group_start[row_expert], 0, TOP_K * T - 1)]

    take_rows = lambda a, idx: a.at[idx].get(mode="promise_in_bounds")
    tiles = padded // tm
    first_tile = group_start // tm
    half = (n_tiles // 2) * tm
    ys = _moe_grouped(take_rows(h, src[:half]), n_rows, first_tile, tiles, 0, w1, w3, w2, tm)
    ys = _moe_grouped(take_rows(h, src[half:]), n_rows, first_tile, tiles, half // tm, w1, w3, w2, tm, ys=ys)
    y1 = take_rows(ys, pick(dest, e_lo))
    y2 = take_rows(ys, pick(dest, e_hi))
    return _moe_combine(x2, y1, y2, w12, mod, seq)


def kernel(x, c, positions, ada_w, ada_b, norm1_g, norm2_g, w_in, gate_b, conv_w, conv_b, lru_wa, lru_ba,
           lru_wx, lru_bx, lru_lambda, diff_qn, diff_kn, diff_lq1, diff_lk1, diff_lq2, diff_lk2, diff_subln,
           moba_qn, moba_kn, w_br_a, w_br_b, w_br_c, w_out, ffn_w1, ffn_w3, ffn_w2, router_w, router_b,
           moe_w1, moe_w3, moe_w2):
    B, S, D = x.shape
    L = ada_w.shape[0]
    T = B * S
    assert S % MOBA_BLOCK == 0 and S // MOBA_BLOCK <= 56 and S % 1024 == 0
    x2 = x.reshape(T, D)
    mod_all = _ada_mod(c, ada_w, ada_b).reshape(L, B, 6, D)
    cos, sin = _rope_tables(positions)
    bf = lambda w: w.astype(BF16)
    tile2 = lambda v: jnp.tile(v, LANES // HEAD_DIM)

    for l in range(L):
        mod = mod_all[l]
        lam_init = 0.8 - 0.6 * math.exp(-0.3 * l)
        proj = _in_proj(x2, mod, norm1_g[l], w_in, l, S)
        gains = jnp.stack([tile2(diff_qn[l]), tile2(diff_kn[l]), tile2(moba_qn[l]), tile2(moba_kn[l])])
        qk = _qk_prep(proj, gains, cos, sin)
        y_a = _rg_lru_branch(proj, S, conv_w[l], conv_b[l], _pair_block_diag(lru_wa[l]), lru_ba[l],
                             _pair_block_diag(lru_wx[l]), lru_bx[l], lru_lambda[l])
        lam_params = jnp.stack([diff_lq1[l], diff_lk1[l], diff_lq2[l], diff_lk2[l]])
        y_b = _diff_attention(qk, proj, S, lam_params, diff_subln[l], lam_init)
        y_c = _moba_attention(qk, proj, S)
        x2 = _merge_out(x2, y_a, y_b, y_c, proj, gate_b[l], mod, bf(w_br_a[l]), bf(w_br_b[l]),
                        bf(w_br_c[l]), bf(w_out[l]), S)
        if l % 2 == 0:
            x2 = _dense_ffn(x2, mod, norm2_g[l], ffn_w1[l // 2], ffn_w3[l // 2], ffn_w2[l // 2], S)
        else:
            x2 = _moe_ffn(x2, mod, norm2_g[l], router_w[l // 2], router_b[l // 2], moe_w1[l // 2],
                          moe_w3[l // 2], moe_w2[l // 2], S)
    return x2.reshape(B, S, D)
```

```python
import functools
import math

import jax
import jax.numpy as jnp
from jax import lax
from jax.experimental import pallas as pl
from jax.experimental.pallas import tpu as pltpu

F32 = jnp.float32
BF16 = jnp.bfloat16
HIGHEST = lax.Precision.HIGHEST

HEAD_DIM = 64
ROPE_THETA = 10000.0
RNN_BLOCKS = 16
CONV_W = 4
LRU_C = 8.0
DIFF_HEADS = 4
MOBA_HEADS = 8
MOBA_BLOCK = 256
MOBA_TOPK = 3
N_BRANCH = 3
N_EXPERTS = 8
TOP_K = 2
EPS = 1e-6
NEG = -1e30

LANES = 128
VMEM_LIMIT = 56 * 1024 * 1024

ROW_TILE = 512


def _cparams(*sem):
    return pltpu.CompilerParams(dimension_semantics=sem, vmem_limit_bytes=VMEM_LIMIT)


def _modulated_norm(x, g, mod, base):
    ms = jnp.mean(x * x, axis=-1, keepdims=True)
    y = x * lax.rsqrt(ms + EPS) * g
    return y * (1.0 + mod[base + 1:base + 2]) + mod[base:base + 1]


def _dot_nt(a, b, **kw):
    return lax.dot_general(a, b, (((1,), (1,)), ((), ())), preferred_element_type=F32, **kw)


def _ada_kernel(c_ref, w_ref, b_ref, o_ref):
    o_ref[0] = jnp.dot(c_ref[...], w_ref[0], preferred_element_type=F32, precision=HIGHEST) + b_ref[0]


def _ada_mod(c, ada_w, ada_b):
    L, D, N = ada_w.shape
    B = c.shape[0]
    tn = 1536
    return pl.pallas_call(
        _ada_kernel,
        grid=(L, N // tn),
        in_specs=[pl.BlockSpec((B, D), lambda l, j: (0, 0)),
                  pl.BlockSpec((1, D, tn), lambda l, j: (l, 0, j)),
                  pl.BlockSpec((1, 1, tn), lambda l, j: (l, 0, j))],
        out_specs=pl.BlockSpec((1, B, tn), lambda l, j: (l, 0, j)),
        out_shape=jax.ShapeDtypeStruct((L, B, N), F32),
        compiler_params=_cparams("parallel", "parallel"),
        name="ada_mod",
    )(c, ada_w, ada_b.reshape(L, 1, N))


IN_LOADS = 8
IN_COL_CHUNK = 1024


def _inproj_kernel(x_ref, mod_ref, g_ref, w_ref, o_ref, wb):
    s = pl.program_id(0)

    @pl.when(s < IN_LOADS)
    def _():
        rows = w_ref.shape[1]
        wb[pl.ds(pl.multiple_of(s * rows, rows), rows), :] = w_ref[0].astype(BF16)

    @pl.when(s >= IN_LOADS)
    def _():
        h = _modulated_norm(x_ref[...], g_ref[...], mod_ref[0], 0).astype(BF16)
        for c in range(wb.shape[1] // IN_COL_CHUNK):
            cols = slice(c * IN_COL_CHUNK, (c + 1) * IN_COL_CHUNK)
            o_ref[:, cols] = jnp.dot(h, wb[:, cols], preferred_element_type=F32).astype(BF16)


def _in_proj(x2, mod, g, w_in, layer, seq):
    T, D = x2.shape
    N = w_in.shape[2]
    tm = ROW_TILE
    per_b = seq // tm
    assert D % (16 * IN_LOADS) == 0 and N % IN_COL_CHUNK == 0
    tile = lambda s: jnp.maximum(s - IN_LOADS, 0)
    return pl.pallas_call(
        _inproj_kernel,
        grid=(IN_LOADS + T // tm,),
        in_specs=[pl.BlockSpec((tm, D), lambda s: (tile(s), 0)),
                  pl.BlockSpec((1, 6, D), lambda s: (tile(s) // per_b, 0, 0)),
                  pl.BlockSpec((1, D), lambda s: (0, 0)),
                  pl.BlockSpec((1, D // IN_LOADS, N), lambda s: (layer, jnp.minimum(s, IN_LOADS - 1), 0))],
        out_specs=pl.BlockSpec((tm, N), lambda s: (tile(s), 0)),
        out_shape=jax.ShapeDtypeStruct((T, N), BF16),
        scratch_shapes=[pltpu.VMEM((D, N), BF16)],
        compiler_params=_cparams("arbitrary"),
        name="in_proj",
    )(x2, mod, g.reshape(1, D), w_in)


def _rope_kernel(pos_ref, inv_ref, sign_ref, cos_ref, sin_ref):
    ang = pos_ref[...] * inv_ref[...]
    cos_ref[...] = jnp.cos(ang)
    sin_ref[...] = jnp.sin(ang) * sign_ref[...]


def _rope_tables(positions):
    T = positions.size
    pos = positions.reshape(T, 1).astype(F32)
    inv = 1.0 / (ROPE_THETA ** (jnp.arange(0, HEAD_DIM, 2, dtype=F32) / HEAD_DIM))
    half = HEAD_DIM // 2
    inv128 = jnp.tile(inv, LANES // half).reshape(1, LANES)
    sign = jnp.tile(jnp.concatenate([-jnp.ones((half,), F32), jnp.ones((half,), F32)]),
                    LANES // HEAD_DIM).reshape(1, LANES)
    tm = 1024
    return pl.pallas_call(
        _rope_kernel,
        grid=(T // tm,),
        in_specs=[pl.BlockSpec((tm, 1), lambda i: (i, 0)),
                  pl.BlockSpec((1, LANES), lambda i: (0, 0)),
                  pl.BlockSpec((1, LANES), lambda i: (0, 0))],
        out_specs=[pl.BlockSpec((tm, LANES), lambda i: (i, 0))] * 2,
        out_shape=[jax.ShapeDtypeStruct((T, LANES), F32)] * 2,
        compiler_params=_cparams("parallel"),
        name="rope_tables",
    )(pos, inv128, sign)


def _qkprep_kernel(dq_ref, dk_ref, mq_ref, mk_ref, gain_ref, cos_ref, sin_ref, seg_ref, o_ref):
    cos = cos_ref[...]
    sin = sin_ref[...]
    seg = seg_ref[...]
    lane = lax.broadcasted_iota(jnp.int32, cos.shape, 1)
    first_half = (lane % HEAD_DIM) < (HEAD_DIM // 2)
    width = dq_ref.shape[1]
    for gi, ref in enumerate((dq_ref, dk_ref, mq_ref, mk_ref)):
        gain = gain_ref[gi:gi + 1, :]
        for cb in range(width // LANES):
            x = ref[:, cb * LANES:(cb + 1) * LANES].astype(F32)
            sq = x * x
            sq_hi = sq.astype(BF16)
            sq_lo = (sq - sq_hi.astype(F32)).astype(BF16)
            ms = (jnp.dot(sq_hi, seg, preferred_element_type=F32)
                  + jnp.dot(sq_lo, seg, preferred_element_type=F32))
            y = x * lax.rsqrt(ms + EPS) * gain
            swapped = jnp.where(first_half, pltpu.roll(y, LANES - HEAD_DIM // 2, 1),
                                pltpu.roll(y, HEAD_DIM // 2, 1))
            r = y * cos + swapped * sin
            if gi % 2 == 0:
                r = r * (math.log2(math.e) / math.sqrt(HEAD_DIM))
            col = gi * width + cb * LANES
            o_ref[:, col:col + LANES] = r.astype(BF16)


def _qk_prep(proj, gains, cos, sin):
    T = proj.shape[0]
    width = DIFF_HEADS * 2 * HEAD_DIM
    seg = jnp.kron(jnp.eye(LANES // HEAD_DIM, dtype=F32),
                   jnp.full((HEAD_DIM, HEAD_DIM), 1.0 / HEAD_DIM, F32)).astype(BF16)
    tm = ROW_TILE
    col_blocks = (4, 5, 7, 8)
    in_specs = [pl.BlockSpec((tm, width), functools.partial(lambda i, c: (i, c), c=c)) for c in col_blocks]
    in_specs += [pl.BlockSpec((4, LANES), lambda i: (0, 0)),
                 pl.BlockSpec((tm, LANES), lambda i: (i, 0)),
                 pl.BlockSpec((tm, LANES), lambda i: (i, 0)),
                 pl.BlockSpec((LANES, LANES), lambda i: (0, 0))]
    return pl.pallas_call(
        _qkprep_kernel,
        grid=(T // tm,),
        in_specs=in_specs,
        out_specs=pl.BlockSpec((tm, 4 * width), lambda i: (i, 0)),
        out_shape=jax.ShapeDtypeStruct((T, 4 * width), BF16),
        compiler_params=_cparams("parallel"),
        name="qk_prep",
    )(proj, proj, proj, proj, gains, cos, sin, seg)


def _gelu_tanh(x):
    return 0.5 * x * (1.0 + jnp.tanh(math.sqrt(2.0 / math.pi) * (x + 0.044715 * x * x * x)))


SUBLANES = 8
LRU_LANE_BLOCKS = 2


def _rglru_kernel(x_ref, g_ref, cw_ref, cb_ref, wa_ref, ba_ref, wx_ref, bx_ref, lam_ref, o_ref, *scratch):
    S = x_ref.shape[0]
    steps = S // SUBLANES
    pitch = steps + SUBLANES
    nblk = x_ref.shape[1] // LANES
    per_blk = len(scratch) // nblk
    sub = lax.broadcasted_iota(jnp.int32, (SUBLANES, LANES), 0)

    for blk in range(nblk):
        slab, fold_x, a_buf, u_buf, p_buf = scratch[blk * per_blk:(blk + 1) * per_blk]
        lanes = slice(blk * LANES, (blk + 1) * LANES)

        for s in range(SUBLANES):
            slab[s * pitch:s * pitch + steps, :] = x_ref[s * steps:(s + 1) * steps, lanes].astype(F32)

        def fold(step, carry):
            fold_x[pl.ds(pl.multiple_of(step * SUBLANES, SUBLANES), SUBLANES), :] = (
                slab[pl.ds(step, SUBLANES, stride=pitch), :])
            return carry

        lax.fori_loop(0, steps, fold, 0, unroll=8)
        xf = fold_x[...]

        def head(j):
            tail = xf[(steps - j) * SUBLANES:, :]
            vregs = [jnp.where(sub == 0, 0.0, pltpu.roll(tail[v * SUBLANES:(v + 1) * SUBLANES, :], 1, 0))
                     for v in range(j)]
            return jnp.concatenate(vregs + [xf[:(steps - j) * SUBLANES, :]], axis=0)

        xc = cb_ref[:, lanes] + cw_ref[CONV_W - 1:CONV_W, lanes] * xf
        for j in range(1, CONV_W):
            xc = xc + cw_ref[CONV_W - 1 - j:CONV_W - j, lanes] * head(j)

        xcb = xc.astype(BF16)
        r = jax.nn.sigmoid(jnp.dot(xcb, wa_ref[blk], preferred_element_type=F32) + ba_ref[:, lanes])
        gi = jax.nn.sigmoid(jnp.dot(xcb, wx_ref[blk], preferred_element_type=F32) + bx_ref[:, lanes])
        neg_lam = -lam_ref[:, lanes]
        softplus = jnp.maximum(neg_lam, 0.0) + jnp.log1p(jnp.exp(-jnp.abs(neg_lam)))
        a = jnp.exp((-LRU_C) * r * softplus)
        gap = 1.0 - a * a
        mult = jnp.where(gap > 0.0, gap * lax.rsqrt(gap), 0.0)
        a_buf[...] = a
        u_buf[...] = mult * gi * xc

    def scan(step, carry):
        rows = pl.ds(pl.multiple_of(step * SUBLANES, SUBLANES), SUBLANES)
        out = []
        for blk in range(nblk):
            _, _, a_buf, u_buf, p_buf = scratch[blk * per_blk:(blk + 1) * per_blk]
            h, p = carry[blk]
            a = a_buf[rows, :]
            h = a * h + u_buf[rows, :]
            p = a * p
            u_buf[rows, :] = h
            p_buf[rows, :] = p
            out.append((h, p))
        return tuple(out)

    ones = jnp.ones((SUBLANES, LANES), F32)
    last = lax.fori_loop(0, steps, scan, ((jnp.zeros_like(ones), ones),) * nblk, unroll=8)

    for blk in range(nblk):
        slab, fold_x, a_buf, u_buf, p_buf = scratch[blk * per_blk:(blk + 1) * per_blk]
        lanes = slice(blk * LANES, (blk + 1) * LANES)
        h_end, p_end = last[blk]
        h_in = [jnp.zeros((1, LANES), F32)]
        for s in range(SUBLANES - 1):
            h_in.append(h_end[s:s + 1, :] + p_end[s:s + 1, :] * h_in[s])
        h_in = jnp.concatenate(h_in, axis=0)
        u_buf[...] = u_buf[...] + p_buf[...] * jnp.concatenate([h_in] * steps, axis=0)

        def unfold(step, carry):
            slab[pl.ds(step, SUBLANES, stride=pitch), :] = (
                u_buf[pl.ds(pl.multiple_of(step * SUBLANES, SUBLANES), SUBLANES), :])
            return carry

        lax.fori_loop(0, steps, unfold, 0, unroll=8)
        for s in range(SUBLANES):
            rows = slice(s * steps, (s + 1) * steps)
            h = slab[s * pitch:s * pitch + steps, :]
            o_ref[rows, lanes] = (_gelu_tanh(g_ref[rows, lanes].astype(F32)) * h).astype(BF16)


def _rg_lru_branch(proj, seq, conv_w, conv_b, wa2, ba, wx2, bx, lam):
    T = proj.shape[0]
    C = conv_w.shape[1]
    B = T // seq
    nblk = LRU_LANE_BLOCKS
    W = nblk * LANES
    g_off = C // W
    steps = seq // SUBLANES
    assert seq % (SUBLANES * SUBLANES) == 0 and C % W == 0
    vec = lambda v: v.reshape(1, C)
    cols = lambda rows: pl.BlockSpec((rows, W), lambda b, n: (0, n))
    per_blk = [pltpu.VMEM((SUBLANES * (steps + SUBLANES), LANES), F32)] + [pltpu.VMEM((seq, LANES), F32)] * 4
    return pl.pallas_call(
        _rglru_kernel,
        grid=(B, C // W),
        in_specs=[pl.BlockSpec((seq, W), lambda b, n: (b, n)),
                  pl.BlockSpec((seq, W), lambda b, n: (b, g_off + n)),
                  cols(CONV_W), cols(1),
                  pl.BlockSpec((nblk, LANES, LANES), lambda b, n: (n, 0, 0)), cols(1),
                  pl.BlockSpec((nblk, LANES, LANES), lambda b, n: (n, 0, 0)), cols(1), cols(1)],
        out_specs=pl.BlockSpec((seq, W), lambda b, n: (b, n)),
        out_shape=jax.ShapeDtypeStruct((T, C), BF16),
        scratch_shapes=per_blk * nblk,
        compiler_params=_cparams("parallel", "parallel"),
        name="rg_lru",
    )(proj, proj, conv_w, vec(conv_b), wa2, vec(ba), wx2, vec(bx), vec(lam))


def _pair_block_diag(w):
    n, d, _ = w.shape
    z = jnp.zeros((n // 2, 2, d, 2, d), w.dtype)
    z = z.at[:, 0, :, 0, :].set(w[0::2]).at[:, 1, :, 1, :].set(w[1::2])
    return z.reshape(n // 2, 2 * d, 2 * d).astype(BF16)


ATTN_TILE = 512


def _lane_tile(x, n):
    return jnp.concatenate([x] * n, axis=1)


def _softmax_update(s, state, v_aug):
    row_max = jnp.max(s, axis=-1, keepdims=True)
    if state is None:
        m_new = jnp.broadcast_to(row_max, (s.shape[0], LANES))
        p = jnp.exp2(s - _lane_tile(m_new, s.shape[1] // LANES)).astype(BF16)
        return m_new, jnp.dot(p, v_aug, preferred_element_type=F32)
    m_old, acc = state
    m_new = jnp.maximum(m_old, row_max)
    p = jnp.exp2(s - _lane_tile(m_new, s.shape[1] // LANES)).astype(BF16)
    alpha = _lane_tile(jnp.exp2(m_old - m_new), acc.shape[1] // LANES)
    return m_new, alpha * acc + jnp.dot(p, v_aug, preferred_element_type=F32)


def _softmax_finish(state):
    _, acc = state
    return acc[:, :LANES] / acc[:, LANES:]


def _state_rows(state, start, stop):
    m, acc = state
    return m[start:stop], acc[start:stop]


def _causal_mask(t):
    return lax.broadcasted_iota(jnp.int32, (t, t), 1) <= lax.broadcasted_iota(jnp.int32, (t, t), 0)


def _mask_diagonal(s, causal):
    t = causal.shape[0]
    if s.shape[0] == t:
        return jnp.where(causal, s, NEG)
    return jnp.concatenate([jnp.where(causal, s[:t], NEG), s[t:]], axis=0)


def _with_ones(v):
    return jnp.concatenate([v, jnp.ones_like(v)], axis=1)


def _diffattn_kernel(q_ref, k_ref, v_ref, lam_ref, g_ref, o_ref, *, lam_init):
    t = ATTN_TILE
    S = q_ref.shape[0]
    q = q_ref[...]
    lane = lax.broadcasted_iota(jnp.int32, q.shape, 1)
    zero = jnp.zeros_like(q)
    q1 = jnp.where(lane < HEAD_DIM, q, zero)
    q2 = jnp.where(lane < HEAD_DIM, zero, q)
    causal = _causal_mask(t)
    lp = lam_ref[...]
    lam = (jnp.exp(jnp.sum(lp[0:1] * lp[1:2], axis=-1, keepdims=True))
           - jnp.exp(jnp.sum(lp[2:3] * lp[3:4], axis=-1, keepdims=True)) + lam_init)
    st1 = st2 = None
    for j in range(S // t):
        k = k_ref[j * t:(j + 1) * t, :]
        v_aug = _with_ones(v_ref[j * t:(j + 1) * t, :])
        st1 = _softmax_update(_mask_diagonal(_dot_nt(q1[j * t:, :], k), causal), st1, v_aug)
        st2 = _softmax_update(_mask_diagonal(_dot_nt(q2[j * t:, :], k), causal), st2, v_aug)
        o = _softmax_finish(_state_rows(st1, 0, t)) - lam * _softmax_finish(_state_rows(st2, 0, t))
        ms = jnp.mean(o * o, axis=-1, keepdims=True)
        o_ref[j * t:(j + 1) * t, :] = (o * lax.rsqrt(ms + EPS) * g_ref[...] * (1.0 - lam_init)).astype(BF16)
        if (j + 1) * t < S:
            st1 = _state_rows(st1, t, None)
            st2 = _state_rows(st2, t, None)


def _diff_attention(qk, proj, seq, lam_params, subln_g, lam_init):
    T = qk.shape[0]
    B = T // seq
    H = DIFF_HEADS
    return pl.pallas_call(
        functools.partial(_diffattn_kernel, lam_init=lam_init),
        grid=(B, H),
        in_specs=[pl.BlockSpec((seq, LANES), lambda b, h: (b, h)),
                  pl.BlockSpec((seq, LANES), lambda b, h: (b, H + h)),
                  pl.BlockSpec((seq, LANES), lambda b, h: (b, 24 + h)),
                  pl.BlockSpec((4, HEAD_DIM), lambda b, h: (0, 0)),
                  pl.BlockSpec((1, LANES), lambda b, h: (0, 0))],
        out_specs=pl.BlockSpec((seq, LANES), lambda b, h: (b, h)),
        out_shape=jax.ShapeDtypeStruct((T, H * LANES), BF16),
        compiler_params=_cparams("parallel", "parallel"),
        name="diff_attn",
    )(qk, qk, proj, lam_params, subln_g.reshape(1, LANES))


def _moba_kernel(q_ref, k_ref, v_ref, o_ref, kmean):
    t = ATTN_TILE
    S = q_ref.shape[0]
    nb = S // MOBA_BLOCK
    nb8 = -(-nb // 8) * 8
    lane_row = lax.broadcasted_iota(jnp.int32, (1, LANES), 1)

    kmean[...] = jnp.zeros_like(kmean)
    for blk in range(nb):
        mean = jnp.mean(k_ref[blk * MOBA_BLOCK:(blk + 1) * MOBA_BLOCK, :].astype(F32), axis=0, keepdims=True)
        kmean[blk:blk + 1, :] = jnp.where(lane_row < HEAD_DIM, mean, 0.0)
        kmean[HEAD_DIM + blk:HEAD_DIM + blk + 1, :] = jnp.where(lane_row < HEAD_DIM, 0.0, mean)

    q = q_ref[...]
    lane = lax.broadcasted_iota(jnp.int32, q.shape, 1)
    zero = jnp.zeros_like(q)
    q_a = jnp.where(lane < HEAD_DIM, q, zero)
    q_b = jnp.where(lane < HEAD_DIM, zero, q)

    km = kmean[...]
    km_hi = km.astype(BF16)
    km_mid = (km - km_hi.astype(F32)).astype(BF16)
    km_lo = (km - km_hi.astype(F32) - km_mid.astype(F32)).astype(BF16)
    gate = _dot_nt(km_hi, q) + _dot_nt(km_mid, q) + _dot_nt(km_lo, q)
    gate_blk = lax.broadcasted_iota(jnp.int32, (nb8, S), 0)
    own = lax.broadcasted_iota(jnp.int32, (nb8, S), 1) // MOBA_BLOCK
    past = gate_blk < own
    unused_rows = jnp.full((HEAD_DIM - nb8, S), NEG, F32)
    bias_rows = []
    for head in range(2):
        g = jnp.where(past, gate[head * HEAD_DIM:head * HEAD_DIM + nb8, :], NEG)
        rank = jnp.zeros(g.shape, jnp.int32)
        for jb in range(nb - 1):
            other = g[jb:jb + 1, :]
            beats = (other > g) | ((other == g) & (jb < gate_blk))
            rank = rank + beats.astype(jnp.int32)
        allowed = (past & (rank < MOBA_TOPK)) | (gate_blk == own)
        bias_rows += [jnp.where(allowed, 0.0, NEG), unused_rows]
    bias_a = jnp.concatenate(bias_rows, axis=0).T
    bias_b = pltpu.roll(bias_a, HEAD_DIM, 1)
    qa_aug = jnp.concatenate([q_a, bias_a.astype(BF16)], axis=1)
    qb_aug = jnp.concatenate([q_b, bias_b.astype(BF16)], axis=1)

    causal = _causal_mask(t)
    key_lane = lax.broadcasted_iota(jnp.int32, (t, LANES), 1)
    key_row = lax.broadcasted_iota(jnp.int32, (t, LANES), 0)
    st_a = st_b = None
    for j in range(S // t):
        key_blk = (j * t + key_row) // MOBA_BLOCK
        k_aug = jnp.concatenate([k_ref[j * t:(j + 1) * t, :], (key_lane == key_blk).astype(BF16)], axis=1)
        v_aug = _with_ones(v_ref[j * t:(j + 1) * t, :])
        st_a = _softmax_update(_mask_diagonal(_dot_nt(qa_aug[j * t:, :], k_aug), causal), st_a, v_aug)
        st_b = _softmax_update(_mask_diagonal(_dot_nt(qb_aug[j * t:, :], k_aug), causal), st_b, v_aug)
        o = jnp.where(key_lane < HEAD_DIM, _softmax_finish(_state_rows(st_a, 0, t)),
                      _softmax_finish(_state_rows(st_b, 0, t)))
        o_ref[j * t:(j + 1) * t, :] = o.astype(BF16)
        if (j + 1) * t < S:
            st_a = _state_rows(st_a, t, None)
            st_b = _state_rows(st_b, t, None)


def _moba_attention(qk, proj, seq):
    T = qk.shape[0]
    B = T // seq
    HP = MOBA_HEADS // 2
    return pl.pallas_call(
        _moba_kernel,
        grid=(B, HP),
        in_specs=[pl.BlockSpec((seq, LANES), lambda b, h: (b, 8 + h)),
                  pl.BlockSpec((seq, LANES), lambda b, h: (b, 12 + h)),
                  pl.BlockSpec((seq, LANES), lambda b, h: (b, 36 + h))],
        out_specs=pl.BlockSpec((seq, LANES), lambda b, h: (b, h)),
        out_shape=jax.ShapeDtypeStruct((T, HP * LANES), BF16),
        scratch_shapes=[pltpu.VMEM((LANES, LANES), F32)],
        compiler_params=_cparams("parallel", "parallel"),
        name="moba_attn",
    )(qk, qk, proj)


def _merge_kernel(x_ref, ya_ref, yb_ref, yc_ref, ga_ref, gbb_ref, gc_ref, gb_ref, mod_ref,
                  wa_ref, wb_ref, wc_ref, wo_ref, o_ref):
    merged = None
    branches = ((ya_ref, wa_ref, ga_ref), (yb_ref, wb_ref, gbb_ref), (yc_ref, wc_ref, gc_ref))
    for n, (y_ref, w_ref, gl_ref) in enumerate(branches):
        gate = jax.nn.sigmoid(gl_ref[...].astype(F32) + gb_ref[n:n + 1, :])
        term = gate * jnp.dot(y_ref[...], w_ref[...], preferred_element_type=F32)
        merged = term if merged is None else merged + term
    mix = jnp.dot(merged.astype(BF16), wo_ref[...], preferred_element_type=F32)
    o_ref[...] = x_ref[...] + mod_ref[0][2:3] * mix


def _merge_out(x2, ya, yb, yc, proj, gate_b, mod, wa, wb, wc, wo, seq):
    T, D = x2.shape
    tm = ROW_TILE
    per_b = seq // tm
    gbr_block = 5120 // D
    assert gbr_block * D == 5120
    wspec = lambda w: pl.BlockSpec(w.shape, lambda i: (0, 0))
    return pl.pallas_call(
        _merge_kernel,
        grid=(T // tm,),
        in_specs=[pl.BlockSpec((tm, D), lambda i: (i, 0)),
                  pl.BlockSpec((tm, ya.shape[1]), lambda i: (i, 0)),
                  pl.BlockSpec((tm, yb.shape[1]), lambda i: (i, 0)),
                  pl.BlockSpec((tm, yc.shape[1]), lambda i: (i, 0)),
                  pl.BlockSpec((tm, D), lambda i: (i, gbr_block)),
                  pl.BlockSpec((tm, D), lambda i: (i, gbr_block + 1)),
                  pl.BlockSpec((tm, D), lambda i: (i, gbr_block + 2)),
                  pl.BlockSpec((N_BRANCH, D), lambda i: (0, 0)),
                  pl.BlockSpec((1, 6, D), lambda i: (i // per_b, 0, 0)),
                  wspec(wa), wspec(wb), wspec(wc), wspec(wo)],
        out_specs=pl.BlockSpec((tm, D), lambda i: (i, 0)),
        out_shape=jax.ShapeDtypeStruct((T, D), F32),
        compiler_params=_cparams("parallel"),
        name="merge_out",
    )(x2, ya, yb, yc, proj, proj, proj, gate_b, mod, wa, wb, wc, wo)


FF_CHUNK = 256
FF_LOADS = 8


def _stash_weight_slab(k, w1_ref, w3_ref, w2_ref, wb1, wb3, wb2):
    for w_ref, wb in ((w1_ref, wb1), (w3_ref, wb3), (w2_ref, wb2)):
        rows, cols = w_ref.shape[-2:]
        start = pl.multiple_of(k * rows, rows)
        wb[pl.ds(start, rows), :] = w_ref[...].reshape(rows, cols).astype(BF16)


def _swiglu_resident(h, wb1, wb3, wb2):
    acc = None
    for f in range(wb2.shape[0] // FF_CHUNK):
        cols = slice(f * FF_CHUNK, (f + 1) * FF_CHUNK)
        g = jnp.dot(h, wb1[:, cols], preferred_element_type=F32)
        u = jnp.dot(h, wb3[:, cols], preferred_element_type=F32)
        act = (g * jax.nn.sigmoid(g) * u).astype(BF16)
        part = jnp.dot(act, wb2[cols, :], preferred_element_type=F32)
        acc = part if acc is None else acc + part
    return acc


def _ffn_weight_specs(D, F, lead, slab_index):
    assert D % (16 * FF_LOADS) == 0 and F % (16 * FF_LOADS) == 0 and F % FF_CHUNK == 0
    index = lambda *a: slab_index(*a) + (0,)
    return [pl.BlockSpec(lead + (D // FF_LOADS, F), index), pl.BlockSpec(lead + (D // FF_LOADS, F), index),
            pl.BlockSpec(lead + (F // FF_LOADS, D), index)]


def _ffn_kernel(x_ref, mod_ref, g_ref, w1_ref, w3_ref, w2_ref, o_ref, wb1, wb3, wb2):
    s = pl.program_id(0)

    @pl.when(s < FF_LOADS)
    def _():
        _stash_weight_slab(s, w1_ref, w3_ref, w2_ref, wb1, wb3, wb2)

    @pl.when(s >= FF_LOADS)
    def _():
        h = _modulated_norm(x_ref[...], g_ref[...], mod_ref[0], 3).astype(BF16)
        o_ref[...] = x_ref[...] + mod_ref[0][5:6] * _swiglu_resident(h, wb1, wb3, wb2)


def _dense_ffn(x2, mod, g, w1, w3, w2, seq):
    T, D = x2.shape
    F = w1.shape[1]
    tm = ROW_TILE
    per_b = seq // tm
    tile = lambda s: jnp.maximum(s - FF_LOADS, 0)
    return pl.pallas_call(
        _ffn_kernel,
        grid=(FF_LOADS + T // tm,),
        in_specs=[pl.BlockSpec((tm, D), lambda s: (tile(s), 0)),
                  pl.BlockSpec((1, 6, D), lambda s: (tile(s) // per_b, 0, 0)),
                  pl.BlockSpec((1, D), lambda s: (0, 0))]
        + _ffn_weight_specs(D, F, (), lambda s: (jnp.minimum(s, FF_LOADS - 1),)),
        out_specs=pl.BlockSpec((tm, D), lambda s: (tile(s), 0)),
        out_shape=jax.ShapeDtypeStruct((T, D), F32),
        scratch_shapes=[pltpu.VMEM((D, F), BF16), pltpu.VMEM((D, F), BF16), pltpu.VMEM((F, D), BF16)],
        compiler_params=_cparams("arbitrary"),
        name="dense_ffn",
    )(x2, mod, g.reshape(1, D), w1, w3, w2)


def _router_kernel(x_ref, mod_ref, g_ref, rw_ref, rb_ref, h_ref, comb_ref, sel_ref):
    h = _modulated_norm(x_ref[...], g_ref[...], mod_ref[0], 3)
    h_ref[...] = h
    logits = _dot_nt(rw_ref[...], h, precision=HIGHEST) + rb_ref[...]
    E = logits.shape[0]
    eid = lax.broadcasted_iota(jnp.int32, logits.shape, 0)
    v1 = jnp.max(logits, axis=0, keepdims=True)
    i1 = jnp.min(jnp.where(logits == v1, eid, E), axis=0, keepdims=True)
    rest = jnp.where(eid == i1, -jnp.inf, logits)
    v2 = jnp.max(rest, axis=0, keepdims=True)
    i2 = jnp.min(jnp.where(rest == v2, eid, E), axis=0, keepdims=True)
    e2 = jnp.exp(v2 - v1)
    w1 = 1.0 / (1.0 + e2)
    w2 = e2 / (1.0 + e2)
    comb_ref[...] = jnp.where(eid == i1, w1, 0.0) + jnp.where(eid == i2, w2, 0.0)
    sel_ref[...] = ((eid == i1) | (eid == i2)).astype(jnp.int32)


def _router(x2, mod, g, router_w, router_b, seq):
    T, D = x2.shape
    E = router_w.shape[1]
    tm = ROW_TILE
    per_b = seq // tm
    return pl.pallas_call(
        _router_kernel,
        grid=(T // tm,),
        in_specs=[pl.BlockSpec((tm, D), lambda i: (i, 0)),
                  pl.BlockSpec((1, 6, D), lambda i: (i // per_b, 0, 0)),
                  pl.BlockSpec((1, D), lambda i: (0, 0)),
                  pl.BlockSpec((E, D), lambda i: (0, 0)),
                  pl.BlockSpec((E, 1), lambda i: (0, 0))],
        out_specs=[pl.BlockSpec((tm, D), lambda i: (i, 0)),
                   pl.BlockSpec((E, tm), lambda i: (0, i)),
                   pl.BlockSpec((E, tm), lambda i: (0, i))],
        out_shape=[jax.ShapeDtypeStruct((T, D), F32),
                   jax.ShapeDtypeStruct((E, T), F32),
                   jax.ShapeDtypeStruct((E, T), jnp.int32)],
        compiler_params=_cparams("parallel"),
        name="moe_router",
    )(x2, mod, g.reshape(1, D), router_w.T, router_b.reshape(E, 1))


STEP_LOAD, STEP_TILE, STEP_ZERO, STEP_NOOP = 0, 1, 2, 3
MOE_SLOTS = 2


def _moe_kernel(kind_ref, slot_ref, slab_ref, we_ref, wk_ref, t_ref, xs_ref, w1_ref, w3_ref, w2_ref, *rest):
    o_ref, wb1, wb3, wb2 = rest[-4:]
    s = pl.program_id(0)
    slot = slot_ref[s]

    @pl.when(kind_ref[s] == STEP_LOAD)
    def _():
        _stash_weight_slab(slab_ref[s], w1_ref, w3_ref, w2_ref, wb1.at[slot], wb3.at[slot], wb2.at[slot])

    @pl.when(kind_ref[s] == STEP_TILE)
    def _():
        o_ref[...] = _swiglu_resident(xs_ref[...].astype(BF16), wb1.at[slot], wb3.at[slot], wb2.at[slot])

    @pl.when(kind_ref[s] == STEP_ZERO)
    def _():
        o_ref[...] = jnp.zeros_like(o_ref)


def _moe_schedule(first_tile, tiles, t0, nt):
    E = tiles.shape[0]
    L = FF_LOADS
    i32 = jnp.int32
    ids = jnp.arange(E, dtype=i32)
    lo = jnp.clip(first_tile - t0, 0, nt)
    cnt = jnp.clip(first_tile + tiles - t0, 0, nt) - lo
    active = cnt > 0
    n_act = jnp.sum(active.astype(i32))
    order = jnp.cumsum(active.astype(i32)) - 1
    pick = ((order[None, :] == ids[:, None]) & active[None, :]).astype(i32)
    act_e = jnp.sum(pick * ids[None, :], axis=1)
    act_cnt = jnp.sum(pick * cnt[None, :], axis=1)
    act_lo = jnp.sum(pick * lo[None, :], axis=1)
    has_next = ids + 1 < n_act
    blk_len = jnp.where(ids < n_act, act_cnt + jnp.where(has_next, L, 0), 0)
    blk_start = L + jnp.cumsum(blk_len) - blk_len
    n_sched = jnp.where(n_act > 0, L + jnp.sum(blk_len), 0)
    n_used = jnp.sum(cnt)

    s = jnp.arange(L * E + nt, dtype=i32)
    k = jnp.clip(jnp.sum(((blk_start[None, :] <= s[:, None]) & (ids[None, :] < n_act)).astype(i32), axis=1) - 1,
                 0, E - 1)
    p = s - blk_start[k]
    c = act_cnt[k]
    pairs = jnp.where(has_next[k], jnp.minimum(c, L), 0)
    in_pairs = p < 2 * pairs
    tail = p - 2 * pairs
    in_block = (s >= L) & (s < n_sched)
    tile_step = in_block & jnp.where(in_pairs, p % 2 == 0, c > pairs)
    load_step = (in_block & ~tile_step) | ((s < L) & (n_act > 0))
    nth_tile = jnp.where(in_pairs, p // 2, pairs + tail)
    nth_slab = jnp.where(s < L, s, jnp.where(in_pairs, p // 2, pairs + tail))
    load_k = jnp.where(s < L, 0, k + 1)
    zero_step = (s >= n_sched) & (s - n_sched < nt - n_used)

    kind = jnp.where(tile_step, STEP_TILE, jnp.where(load_step, STEP_LOAD,
                                                     jnp.where(zero_step, STEP_ZERO, STEP_NOOP)))
    slot = jnp.where(load_step, load_k, k) % MOE_SLOTS
    slab = jnp.clip(nth_slab, 0, L - 1)
    last_load = lax.cummax(jnp.where(load_step, s, 0), axis=0)
    we = act_e[jnp.clip(load_k, 0, E - 1)][last_load]
    wk = slab[last_load]
    own = jnp.where(tile_step, act_lo[k] + nth_tile, jnp.where(zero_step, n_used + s - n_sched, nt - 1))
    tile = jnp.clip(lax.cummin(own, axis=0, reverse=True), 0, nt - 1)
    return tuple(a.astype(i32) for a in (kind, slot, slab, we, wk, tile))


def _moe_grouped(xs, n_rows, first_tile, tiles, t0, w1, w3, w2, tm, ys=None):
    D = xs.shape[1]
    nt = xs.shape[0] // tm
    E, _, F = w1.shape
    table = _moe_schedule(first_tile, tiles, t0, nt)
    n_pre = len(table)
    in_specs = [pl.BlockSpec((tm, D), lambda s, *pre: (pre[5][s], 0))]
    in_specs += _ffn_weight_specs(D, F, (1,), lambda s, *pre: (pre[3][s], pre[4][s]))
    operands = [xs, w1, w3, w2]
    aliases = {}
    if ys is not None:
        in_specs.append(pl.BlockSpec(memory_space=pl.ANY))
        operands.append(ys)
        aliases = {n_pre + len(operands) - 1: 0}
    grid_spec = pltpu.PrefetchScalarGridSpec(
        num_scalar_prefetch=n_pre,
        grid=(FF_LOADS * E + nt,),
        in_specs=in_specs,
        out_specs=pl.BlockSpec((tm, D), lambda s, *pre: (t0 + pre[5][s], 0)),
        scratch_shapes=[pltpu.VMEM((MOE_SLOTS, D, F), BF16), pltpu.VMEM((MOE_SLOTS, D, F), BF16),
                        pltpu.VMEM((MOE_SLOTS, F, D), BF16)],
    )
    return pl.pallas_call(
        _moe_kernel,
        grid_spec=grid_spec,
        out_shape=jax.ShapeDtypeStruct((n_rows, D), F32),
        input_output_aliases=aliases,
        compiler_params=_cparams("arbitrary"),
        name="moe_experts",
    )(*table, *operands)


def _combine_kernel(x_ref, y1_ref, y2_ref, w_ref, mod_ref, o_ref):
    w = w_ref[...]
    y = w[:, 0:1] * y1_ref[...] + w[:, 1:2] * y2_ref[...]
    o_ref[...] = x_ref[...] + mod_ref[0][5:6] * y


def _moe_combine(x2, y1, y2, w12, mod, seq):
    T, D = x2.shape
    tm = ROW_TILE
    per_b = seq // tm
    return pl.pallas_call(
        _combine_kernel,
        grid=(T // tm,),
        in_specs=[pl.BlockSpec((tm, D), lambda i: (i, 0)),
                  pl.BlockSpec((tm, D), lambda i: (i, 0)),
                  pl.BlockSpec((tm, D), lambda i: (i, 0)),
                  pl.BlockSpec((tm, 2), lambda i: (i, 0)),
                  pl.BlockSpec((1, 6, D), lambda i: (i // per_b, 0, 0))],
        out_specs=pl.BlockSpec((tm, D), lambda i: (i, 0)),
        out_shape=jax.ShapeDtypeStruct((T, D), F32),
        compiler_params=_cparams("parallel"),
        name="moe_combine",
    )(x2, y1, y2, w12, mod)


def _moe_ffn(x2, mod, g, router_w, router_b, w1, w3, w2, seq):
    T, D = x2.shape
    E = router_w.shape[1]
    tm = ROW_TILE
    h, comb, sel = _router(x2, mod, g, router_w, router_b, seq)

    counts = jnp.sum(sel, axis=1)
    padded = ((counts + tm - 1) // tm) * tm
    group_end = jnp.cumsum(padded)
    group_start = group_end - padded
    rank = jnp.cumsum(sel, axis=1) - sel
    dest = group_start[:, None] + rank
    n_rows = TOP_K * T + E * tm
    n_tiles = n_rows // tm
    tile_start = jnp.arange(n_tiles, dtype=jnp.int32) * tm
    tile_expert = jnp.sum((group_end[None, :] <= tile_start[:, None]).astype(jnp.int32), axis=1)
    tile_expert = jnp.minimum(tile_expert, jnp.max(jnp.where(counts > 0, jnp.arange(E), 0))).astype(jnp.int32)

    eid = jnp.arange(E, dtype=jnp.int32)[:, None]
    e_lo = jnp.min(jnp.where(sel > 0, eid, E), axis=0)
    e_hi = jnp.max(jnp.where(sel > 0, eid, -1), axis=0)
    pick = lambda a, e: jnp.sum(jnp.where(eid == e[None, :], a, 0), axis=0)
    w12 = jnp.stack([pick(comb, e_lo), pick(comb, e_hi)], axis=1)

    tok = jnp.arange(T, dtype=jnp.int32)
    tok_sorted = jnp.sort(jnp.concatenate([e_lo * T + tok, e_hi * T + tok])) % T
    row = jnp.arange(n_rows, dtype=jnp.int32)
    row_expert = jnp.repeat(tile_expert, tm)
    first_sorted = (jnp.cumsum(counts) - counts)[row_expert]
    src = tok_sorted[jnp.clip(first_sorted + row - group_start[row_expert], 0, TOP_K * T - 1)]

    take_rows = lambda a, idx: a.at[idx].get(mode="promise_in_bounds")
    ys = _moe_grouped(take_rows(h, src), n_rows, group_start // tm, padded // tm, 0, w1, w3, w2, tm)
    y1 = take_rows(ys, pick(dest, e_lo))
    y2 = take_rows(ys, pick(dest, e_hi))
    return _moe_combine(x2, y1, y2, w12, mod, seq)


def kernel(x, c, positions, ada_w, ada_b, norm1_g, norm2_g, w_in, gate_b, conv_w, conv_b, lru_wa, lru_ba,
           lru_wx, lru_bx, lru_lambda, diff_qn, diff_kn, diff_lq1, diff_lk1, diff_lq2, diff_lk2, diff_subln,
           moba_qn, moba_kn, w_br_a, w_br_b, w_br_c, w_out, ffn_w1, ffn_w3, ffn_w2, router_w, router_b,
           moe_w1, moe_w3, moe_w2):
    B, S, D = x.shape
    L = ada_w.shape[0]
    T = B * S
    assert S % MOBA_BLOCK == 0 and S // MOBA_BLOCK <= 56 and S % 1024 == 0
    x2 = x.reshape(T, D)
    mod_all = _ada_mod(c, ada_w, ada_b).reshape(L, B, 6, D)
    cos, sin = _rope_tables(positions)
    bf = lambda w: w.astype(BF16)
    tile2 = lambda v: jnp.tile(v, LANES // HEAD_DIM)

    for l in range(L):
        mod = mod_all[l]
        lam_init = 0.8 - 0.6 * math.exp(-0.3 * l)
        proj = _in_proj(x2, mod, norm1_g[l], w_in, l, S)
        gains = jnp.stack([tile2(diff_qn[l]), tile2(diff_kn[l]), tile2(moba_qn[l]), tile2(moba_kn[l])])
        qk = _qk_prep(proj, gains, cos, sin)
        y_a = _rg_lru_branch(proj, S, conv_w[l], conv_b[l], _pair_block_diag(lru_wa[l]), lru_ba[l],
                             _pair_block_diag(lru_wx[l]), lru_bx[l], lru_lambda[l])
        lam_params = jnp.stack([diff_lq1[l], diff_lk1[l], diff_lq2[l], diff_lk2[l]])
        y_b = _diff_attention(qk, proj, S, lam_params, diff_subln[l], lam_init)
        y_c = _moba_attention(qk, proj, S)
        x2 = _merge_out(x2, y_a, y_b, y_c, proj, gate_b[l], mod, bf(w_br_a[l]), bf(w_br_b[l]),
                        bf(w_br_c[l]), bf(w_out[l]), S)
        if l % 2 == 0:
            x2 = _dense_ffn(x2, mod, norm2_g[l], ffn_w1[l // 2], ffn_w3[l // 2], ffn_w2[l // 2], S)
        else:
            x2 = _moe_ffn(x2, mod, norm2_g[l], router_w[l // 2], router_b[l // 2], moe_w1[l // 2],
                          moe_w3[l // 2], moe_w2[l // 2], S)
    return x2.reshape(B, S, D)
```

```python
import functools
import math

import jax
import jax.numpy as jnp
from jax import lax
from jax.experimental import pallas as pl
from jax.experimental.pallas import tpu as pltpu

F32 = jnp.float32
BF16 = jnp.bfloat16
HIGHEST = lax.Precision.HIGHEST

HEAD_DIM = 64
ROPE_THETA = 10000.0
RNN_BLOCKS = 16
CONV_W = 4
LRU_C = 8.0
DIFF_HEADS = 4
MOBA_HEADS = 8
MOBA_BLOCK = 256
MOBA_TOPK = 3
N_BRANCH = 3
N_EXPERTS = 8
TOP_K = 2
EPS = 1e-6
NEG = -1e30

LANES = 128
VMEM_LIMIT = 56 * 1024 * 1024

ROW_TILE = 512


def _cparams(*sem):
    return pltpu.CompilerParams(dimension_semantics=sem, vmem_limit_bytes=VMEM_LIMIT)


def _modulated_norm(x, g, mod, base):
    ms = jnp.mean(x * x, axis=-1, keepdims=True)
    y = x * lax.rsqrt(ms + EPS) * g
    return y * (1.0 + mod[base + 1:base + 2]) + mod[base:base + 1]


def _dot_nt(a, b, **kw):
    return lax.dot_general(a, b, (((1,), (1,)), ((), ())), preferred_element_type=F32, **kw)


def _ada_kernel(c_ref, w_ref, b_ref, o_ref):
    o_ref[0] = jnp.dot(c_ref[...], w_ref[0], preferred_element_type=F32, precision=HIGHEST) + b_ref[0]


def _ada_mod(c, ada_w, ada_b):
    L, D, N = ada_w.shape
    B = c.shape[0]
    tn = 1536
    return pl.pallas_call(
        _ada_kernel,
        grid=(L, N // tn),
        in_specs=[pl.BlockSpec((B, D), lambda l, j: (0, 0)),
                  pl.BlockSpec((1, D, tn), lambda l, j: (l, 0, j)),
                  pl.BlockSpec((1, 1, tn), lambda l, j: (l, 0, j))],
        out_specs=pl.BlockSpec((1, B, tn), lambda l, j: (l, 0, j)),
        out_shape=jax.ShapeDtypeStruct((L, B, N), F32),
        compiler_params=_cparams("parallel", "parallel"),
        name="ada_mod",
    )(c, ada_w, ada_b.reshape(L, 1, N))


IN_LOADS = 8
IN_COL_CHUNK = 1024


def _inproj_kernel(x_ref, mod_ref, g_ref, w_ref, o_ref, wb):
    s = pl.program_id(0)

    @pl.when(s < IN_LOADS)
    def _():
        rows = w_ref.shape[1]
        wb[pl.ds(pl.multiple_of(s * rows, rows), rows), :] = w_ref[0].astype(BF16)

    @pl.when(s >= IN_LOADS)
    def _():
        h = _modulated_norm(x_ref[...], g_ref[...], mod_ref[0], 0).astype(BF16)
        for c in range(wb.shape[1] // IN_COL_CHUNK):
            cols = slice(c * IN_COL_CHUNK, (c + 1) * IN_COL_CHUNK)
            o_ref[:, cols] = jnp.dot(h, wb[:, cols], preferred_element_type=F32).astype(BF16)


def _in_proj(x2, mod, g, w_in, layer, seq):
    T, D = x2.shape
    N = w_in.shape[2]
    tm = ROW_TILE
    per_b = seq // tm
    assert D % (16 * IN_LOADS) == 0 and N % IN_COL_CHUNK == 0
    tile = lambda s: jnp.maximum(s - IN_LOADS, 0)
    return pl.pallas_call(
        _inproj_kernel,
        grid=(IN_LOADS + T // tm,),
        in_specs=[pl.BlockSpec((tm, D), lambda s: (tile(s), 0)),
                  pl.BlockSpec((1, 6, D), lambda s: (tile(s) // per_b, 0, 0)),
                  pl.BlockSpec((1, D), lambda s: (0, 0)),
                  pl.BlockSpec((1, D // IN_LOADS, N), lambda s: (layer, jnp.minimum(s, IN_LOADS - 1), 0))],
        out_specs=pl.BlockSpec((tm, N), lambda s: (tile(s), 0)),
        out_shape=jax.ShapeDtypeStruct((T, N), BF16),
        scratch_shapes=[pltpu.VMEM((D, N), BF16)],
        compiler_params=_cparams("arbitrary"),
        name="in_proj",
    )(x2, mod, g.reshape(1, D), w_in)


def _rope_kernel(pos_ref, inv_ref, sign_ref, cos_ref, sin_ref):
    ang = pos_ref[...] * inv_ref[...]
    cos_ref[...] = jnp.cos(ang)
    sin_ref[...] = jnp.sin(ang) * sign_ref[...]


def _rope_tables(positions):
    T = positions.size
    pos = positions.reshape(T, 1).astype(F32)
    inv = 1.0 / (ROPE_THETA ** (jnp.arange(0, HEAD_DIM, 2, dtype=F32) / HEAD_DIM))
    half = HEAD_DIM // 2
    inv128 = jnp.tile(inv, LANES // half).reshape(1, LANES)
    sign = jnp.tile(jnp.concatenate([-jnp.ones((half,), F32), jnp.ones((half,), F32)]),
                    LANES // HEAD_DIM).reshape(1, LANES)
    tm = 1024
    return pl.pallas_call(
        _rope_kernel,
        grid=(T // tm,),
        in_specs=[pl.BlockSpec((tm, 1), lambda i: (i, 0)),
                  pl.BlockSpec((1, LANES), lambda i: (0, 0)),
                  pl.BlockSpec((1, LANES), lambda i: (0, 0))],
        out_specs=[pl.BlockSpec((tm, LANES), lambda i: (i, 0))] * 2,
        out_shape=[jax.ShapeDtypeStruct((T, LANES), F32)] * 2,
        compiler_params=_cparams("parallel"),
        name="rope_tables",
    )(pos, inv128, sign)


def _qkprep_kernel(dq_ref, dk_ref, mq_ref, mk_ref, gain_ref, cos_ref, sin_ref, seg_ref, o_ref):
    cos = cos_ref[...]
    sin = sin_ref[...]
    seg = seg_ref[...]
    lane = lax.broadcasted_iota(jnp.int32, cos.shape, 1)
    first_half = (lane % HEAD_DIM) < (HEAD_DIM // 2)
    width = dq_ref.shape[1]
    for gi, ref in enumerate((dq_ref, dk_ref, mq_ref, mk_ref)):
        gain = gain_ref[gi:gi + 1, :]
        for cb in range(width // LANES):
            x = ref[:, cb * LANES:(cb + 1) * LANES].astype(F32)
            sq = x * x
            sq_hi = sq.astype(BF16)
            sq_lo = (sq - sq_hi.astype(F32)).astype(BF16)
            ms = (jnp.dot(sq_hi, seg, preferred_element_type=F32)
                  + jnp.dot(sq_lo, seg, preferred_element_type=F32))
            y = x * lax.rsqrt(ms + EPS) * gain
            swapped = jnp.where(first_half, pltpu.roll(y, LANES - HEAD_DIM // 2, 1),
                                pltpu.roll(y, HEAD_DIM // 2, 1))
            r = y * cos + swapped * sin
            if gi % 2 == 0:
                r = r * (math.log2(math.e) / math.sqrt(HEAD_DIM))
            col = gi * width + cb * LANES
            o_ref[:, col:col + LANES] = r.astype(BF16)


def _qk_prep(proj, gains, cos, sin):
    T = proj.shape[0]
    width = DIFF_HEADS * 2 * HEAD_DIM
    seg = jnp.kron(jnp.eye(LANES // HEAD_DIM, dtype=F32),
                   jnp.full((HEAD_DIM, HEAD_DIM), 1.0 / HEAD_DIM, F32)).astype(BF16)
    tm = ROW_TILE
    col_blocks = (4, 5, 7, 8)
    in_specs = [pl.BlockSpec((tm, width), functools.partial(lambda i, c: (i, c), c=c)) for c in col_blocks]
    in_specs += [pl.BlockSpec((4, LANES), lambda i: (0, 0)),
                 pl.BlockSpec((tm, LANES), lambda i: (i, 0)),
                 pl.BlockSpec((tm, LANES), lambda i: (i, 0)),
                 pl.BlockSpec((LANES, LANES), lambda i: (0, 0))]
    return pl.pallas_call(
        _qkprep_kernel,
        grid=(T // tm,),
        in_specs=in_specs,
        out_specs=pl.BlockSpec((tm, 4 * width), lambda i: (i, 0)),
        out_shape=jax.ShapeDtypeStruct((T, 4 * width), BF16),
        compiler_params=_cparams("parallel"),
        name="qk_prep",
    )(proj, proj, proj, proj, gains, cos, sin, seg)


def _gelu_tanh(x):
    return 0.5 * x * (1.0 + jnp.tanh(math.sqrt(2.0 / math.pi) * (x + 0.044715 * x * x * x)))


SUBLANES = 8
LRU_LANE_BLOCKS = 2


def _rglru_kernel(x_ref, g_ref, cw_ref, cb_ref, wa_ref, ba_ref, wx_ref, bx_ref, lam_ref, o_ref, *scratch):
    S = x_ref.shape[0]
    steps = S // SUBLANES
    pitch = steps + SUBLANES
    nblk = x_ref.shape[1] // LANES
    per_blk = len(scratch) // nblk
    sub = lax.broadcasted_iota(jnp.int32, (SUBLANES, LANES), 0)

    for blk in range(nblk):
        slab, fold_x, a_buf, u_buf, p_buf = scratch[blk * per_blk:(blk + 1) * per_blk]
        lanes = slice(blk * LANES, (blk + 1) * LANES)

        for s in range(SUBLANES):
            slab[s * pitch:s * pitch + steps, :] = x_ref[s * steps:(s + 1) * steps, lanes].astype(F32)

        def fold(step, carry):
            fold_x[pl.ds(pl.multiple_of(step * SUBLANES, SUBLANES), SUBLANES), :] = (
                slab[pl.ds(step, SUBLANES, stride=pitch), :])
            return carry

        lax.fori_loop(0, steps, fold, 0, unroll=8)
        xf = fold_x[...]

        def head(j):
            tail = xf[(steps - j) * SUBLANES:, :]
            vregs = [jnp.where(sub == 0, 0.0, pltpu.roll(tail[v * SUBLANES:(v + 1) * SUBLANES, :], 1, 0))
                     for v in range(j)]
            return jnp.concatenate(vregs + [xf[:(steps - j) * SUBLANES, :]], axis=0)

        xc = cb_ref[:, lanes] + cw_ref[CONV_W - 1:CONV_W, lanes] * xf
        for j in range(1, CONV_W):
            xc = xc + cw_ref[CONV_W - 1 - j:CONV_W - j, lanes] * head(j)

        xcb = xc.astype(BF16)
        r = jax.nn.sigmoid(jnp.dot(xcb, wa_ref[blk], preferred_element_type=F32) + ba_ref[:, lanes])
        gi = jax.nn.sigmoid(jnp.dot(xcb, wx_ref[blk], preferred_element_type=F32) + bx_ref[:, lanes])
        neg_lam = -lam_ref[:, lanes]
        softplus = jnp.maximum(neg_lam, 0.0) + jnp.log1p(jnp.exp(-jnp.abs(neg_lam)))
        a = jnp.exp((-LRU_C) * r * softplus)
        gap = 1.0 - a * a
        mult = jnp.where(gap > 0.0, gap * lax.rsqrt(gap), 0.0)
        a_buf[...] = a
        u_buf[...] = mult * gi * xc

    def scan(step, carry):
        rows = pl.ds(pl.multiple_of(step * SUBLANES, SUBLANES), SUBLANES)
        out = []
        for blk in range(nblk):
            _, _, a_buf, u_buf, p_buf = scratch[blk * per_blk:(blk + 1) * per_blk]
            h, p = carry[blk]
            a = a_buf[rows, :]
            h = a * h + u_buf[rows, :]
            p = a * p
            u_buf[rows, :] = h
            p_buf[rows, :] = p
            out.append((h, p))
        return tuple(out)

    ones = jnp.ones((SUBLANES, LANES), F32)
    last = lax.fori_loop(0, steps, scan, ((jnp.zeros_like(ones), ones),) * nblk, unroll=8)

    for blk in range(nblk):
        slab, fold_x, a_buf, u_buf, p_buf = scratch[blk * per_blk:(blk + 1) * per_blk]
        lanes = slice(blk * LANES, (blk + 1) * LANES)
        h_end, p_end = last[blk]
        h_in = [jnp.zeros((1, LANES), F32)]
        for s in range(SUBLANES - 1):
            h_in.append(h_end[s:s + 1, :] + p_end[s:s + 1, :] * h_in[s])
        h_in = jnp.concatenate(h_in, axis=0)
        u_buf[...] = u_buf[...] + p_buf[...] * jnp.concatenate([h_in] * steps, axis=0)

        def unfold(step, carry):
            slab[pl.ds(step, SUBLANES, stride=pitch), :] = (
                u_buf[pl.ds(pl.multiple_of(step * SUBLANES, SUBLANES), SUBLANES), :])
            return carry

        lax.fori_loop(0, steps, unfold, 0, unroll=8)
        for s in range(SUBLANES):
            rows = slice(s * steps, (s + 1) * steps)
            h = slab[s * pitch:s * pitch + steps, :]
            o_ref[rows, lanes] = (_gelu_tanh(g_ref[rows, lanes].astype(F32)) * h).astype(BF16)


def _rg_lru_branch(proj, seq, conv_w, conv_b, wa2, ba, wx2, bx, lam):
    T = proj.shape[0]
    C = conv_w.shape[1]
    B = T // seq
    nblk = LRU_LANE_BLOCKS
    W = nblk * LANES
    g_off = C // W
    steps = seq // SUBLANES
    assert seq % (SUBLANES * SUBLANES) == 0 and C % W == 0
    vec = lambda v: v.reshape(1, C)
    cols = lambda rows: pl.BlockSpec((rows, W), lambda b, n: (0, n))
    per_blk = [pltpu.VMEM((SUBLANES * (steps + SUBLANES), LANES), F32)] + [pltpu.VMEM((seq, LANES), F32)] * 4
    return pl.pallas_call(
        _rglru_kernel,
        grid=(B, C // W),
        in_specs=[pl.BlockSpec((seq, W), lambda b, n: (b, n)),
                  pl.BlockSpec((seq, W), lambda b, n: (b, g_off + n)),
                  cols(CONV_W), cols(1),
                  pl.BlockSpec((nblk, LANES, LANES), lambda b, n: (n, 0, 0)), cols(1),
                  pl.BlockSpec((nblk, LANES, LANES), lambda b, n: (n, 0, 0)), cols(1), cols(1)],
        out_specs=pl.BlockSpec((seq, W), lambda b, n: (b, n)),
        out_shape=jax.ShapeDtypeStruct((T, C), BF16),
        scratch_shapes=per_blk * nblk,
        compiler_params=_cparams("parallel", "parallel"),
        name="rg_lru",
    )(proj, proj, conv_w, vec(conv_b), wa2, vec(ba), wx2, vec(bx), vec(lam))


def _pair_block_diag(w):
    n, d, _ = w.shape
    z = jnp.zeros((n // 2, 2, d, 2, d), w.dtype)
    z = z.at[:, 0, :, 0, :].set(w[0::2]).at[:, 1, :, 1, :].set(w[1::2])
    return z.reshape(n // 2, 2 * d, 2 * d).astype(BF16)


ATTN_TILE = 512


def _lane_tile(x, n):
    return jnp.concatenate([x] * n, axis=1)


def _softmax_update(s, state, v_aug):
    row_max = jnp.max(s, axis=-1, keepdims=True)
    if state is None:
        m_new = jnp.broadcast_to(row_max, (s.shape[0], LANES))
        p = jnp.exp2(s - _lane_tile(m_new, s.shape[1] // LANES)).astype(BF16)
        return m_new, jnp.dot(p, v_aug, preferred_element_type=F32)
    m_old, acc = state
    m_new = jnp.maximum(m_old, row_max)
    p = jnp.exp2(s - _lane_tile(m_new, s.shape[1] // LANES)).astype(BF16)
    alpha = _lane_tile(jnp.exp2(m_old - m_new), acc.shape[1] // LANES)
    return m_new, alpha * acc + jnp.dot(p, v_aug, preferred_element_type=F32)


def _softmax_finish(state):
    _, acc = state
    return acc[:, :LANES] / acc[:, LANES:]


def _state_rows(state, start, stop):
    m, acc = state
    return m[start:stop], acc[start:stop]


def _causal_mask(t):
    return lax.broadcasted_iota(jnp.int32, (t, t), 1) <= lax.broadcasted_iota(jnp.int32, (t, t), 0)


def _mask_diagonal(s, causal):
    t = causal.shape[0]
    if s.shape[0] == t:
        return jnp.where(causal, s, NEG)
    return jnp.concatenate([jnp.where(causal, s[:t], NEG), s[t:]], axis=0)


def _with_ones(v):
    return jnp.concatenate([v, jnp.ones_like(v)], axis=1)


def _diffattn_kernel(q_ref, k_ref, v_ref, lam_ref, g_ref, o_ref, *, lam_init):
    t = ATTN_TILE
    S = q_ref.shape[0]
    q = q_ref[...]
    lane = lax.broadcasted_iota(jnp.int32, q.shape, 1)
    zero = jnp.zeros_like(q)
    q1 = jnp.where(lane < HEAD_DIM, q, zero)
    q2 = jnp.where(lane < HEAD_DIM, zero, q)
    causal = _causal_mask(t)
    lp = lam_ref[...]
    lam = (jnp.exp(jnp.sum(lp[0:1] * lp[1:2], axis=-1, keepdims=True))
           - jnp.exp(jnp.sum(lp[2:3] * lp[3:4], axis=-1, keepdims=True)) + lam_init)
    st1 = st2 = None
    for j in range(S // t):
        k = k_ref[j * t:(j + 1) * t, :]
        v_aug = _with_ones(v_ref[j * t:(j + 1) * t, :])
        st1 = _softmax_update(_mask_diagonal(_dot_nt(q1[j * t:, :], k), causal), st1, v_aug)
        st2 = _softmax_update(_mask_diagonal(_dot_nt(q2[j * t:, :], k), causal), st2, v_aug)
        o = _softmax_finish(_state_rows(st1, 0, t)) - lam * _softmax_finish(_state_rows(st2, 0, t))
        ms = jnp.mean(o * o, axis=-1, keepdims=True)
        o_ref[j * t:(j + 1) * t, :] = (o * lax.rsqrt(ms + EPS) * g_ref[...] * (1.0 - lam_init)).astype(BF16)
        if (j + 1) * t < S:
            st1 = _state_rows(st1, t, None)
            st2 = _state_rows(st2, t, None)


def _diff_attention(qk, proj, seq, lam_params, subln_g, lam_init):
    T = qk.shape[0]
    B = T // seq
    H = DIFF_HEADS
    return pl.pallas_call(
        functools.partial(_diffattn_kernel, lam_init=lam_init),
        grid=(B, H),
        in_specs=[pl.BlockSpec((seq, LANES), lambda b, h: (b, h)),
                  pl.BlockSpec((seq, LANES), lambda b, h: (b, H + h)),
                  pl.BlockSpec((seq, LANES), lambda b, h: (b, 24 + h)),
                  pl.BlockSpec((4, HEAD_DIM), lambda b, h: (0, 0)),
                  pl.BlockSpec((1, LANES), lambda b, h: (0, 0))],
        out_specs=pl.BlockSpec((seq, LANES), lambda b, h: (b, h)),
        out_shape=jax.ShapeDtypeStruct((T, H * LANES), BF16),
        compiler_params=_cparams("parallel", "parallel"),
        name="diff_attn",
    )(qk, qk, proj, lam_params, subln_g.reshape(1, LANES))


def _moba_kernel(q_ref, k_ref, v_ref, o_ref, kmean):
    t = ATTN_TILE
    S = q_ref.shape[0]
    nb = S // MOBA_BLOCK
    nb8 = -(-nb // 8) * 8
    lane_row = lax.broadcasted_iota(jnp.int32, (1, LANES), 1)

    kmean[...] = jnp.zeros_like(kmean)
    for blk in range(nb):
        mean = jnp.mean(k_ref[blk * MOBA_BLOCK:(blk + 1) * MOBA_BLOCK, :].astype(F32), axis=0, keepdims=True)
        kmean[blk:blk + 1, :] = jnp.where(lane_row < HEAD_DIM, mean, 0.0)
        kmean[HEAD_DIM + blk:HEAD_DIM + blk + 1, :] = jnp.where(lane_row < HEAD_DIM, 0.0, mean)

    q = q_ref[...]
    lane = lax.broadcasted_iota(jnp.int32, q.shape, 1)
    zero = jnp.zeros_like(q)
    q_a = jnp.where(lane < HEAD_DIM, q, zero)
    q_b = jnp.where(lane < HEAD_DIM, zero, q)

    km = kmean[...]
    km_hi = km.astype(BF16)
    km_mid = (km - km_hi.astype(F32)).astype(BF16)
    km_lo = (km - km_hi.astype(F32) - km_mid.astype(F32)).astype(BF16)
    gate = _dot_nt(km_hi, q) + _dot_nt(km_mid, q) + _dot_nt(km_lo, q)
    gate_blk = lax.broadcasted_iota(jnp.int32, (nb8, S), 0)
    own = lax.broadcasted_iota(jnp.int32, (nb8, S), 1) // MOBA_BLOCK
    past = gate_blk < own
    unused_rows = jnp.full((HEAD_DIM - nb8, S), NEG, F32)
    bias_rows = []
    for head in range(2):
        g = jnp.where(past, gate[head * HEAD_DIM:head * HEAD_DIM + nb8, :], NEG)
        rank = jnp.zeros(g.shape, jnp.int32)
        for jb in range(nb - 1):
            other = g[jb:jb + 1, :]
            beats = (other > g) | ((other == g) & (jb < gate_blk))
            rank = rank + beats.astype(jnp.int32)
        allowed = (past & (rank < MOBA_TOPK)) | (gate_blk == own)
        bias_rows += [jnp.where(allowed, 0.0, NEG), unused_rows]
    bias_a = jnp.concatenate(bias_rows, axis=0).T
    bias_b = pltpu.roll(bias_a, HEAD_DIM, 1)
    qa_aug = jnp.concatenate([q_a, bias_a.astype(BF16)], axis=1)
    qb_aug = jnp.concatenate([q_b, bias_b.astype(BF16)], axis=1)

    causal = _causal_mask(t)
    key_lane = lax.broadcasted_iota(jnp.int32, (t, LANES), 1)
    key_row = lax.broadcasted_iota(jnp.int32, (t, LANES), 0)
    st_a = st_b = None
    for j in range(S // t):
        key_blk = (j * t + key_row) // MOBA_BLOCK
        k_aug = jnp.concatenate([k_ref[j * t:(j + 1) * t, :], (key_lane == key_blk).astype(BF16)], axis=1)
        v_aug = _with_ones(v_ref[j * t:(j + 1) * t, :])
        st_a = _softmax_update(_mask_diagonal(_dot_nt(qa_aug[j * t:, :], k_aug), causal), st_a, v_aug)
        st_b = _softmax_update(_mask_diagonal(_dot_nt(qb_aug[j * t:, :], k_aug), causal), st_b, v_aug)
        o = jnp.where(key_lane < HEAD_DIM, _softmax_finish(_state_rows(st_a, 0, t)),
                      _softmax_finish(_state_rows(st_b, 0, t)))
        o_ref[j * t:(j + 1) * t, :] = o.astype(BF16)
        if (j + 1) * t < S:
            st_a = _state_rows(st_a, t, None)
            st_b = _state_rows(st_b, t, None)


def _moba_attention(qk, proj, seq):
    T = qk.shape[0]
    B = T // seq
    HP = MOBA_HEADS // 2
    return pl.pallas_call(
        _moba_kernel,
        grid=(B, HP),
        in_specs=[pl.BlockSpec((seq, LANES), lambda b, h: (b, 8 + h)),
                  pl.BlockSpec((seq, LANES), lambda b, h: (b, 12 + h)),
                  pl.BlockSpec((seq, LANES), lambda b, h: (b, 36 + h))],
        out_specs=pl.BlockSpec((seq, LANES), lambda b, h: (b, h)),
        out_shape=jax.ShapeDtypeStruct((T, HP * LANES), BF16),
        scratch_shapes=[pltpu.VMEM((LANES, LANES), F32)],
        compiler_params=_cparams("parallel", "parallel"),
        name="moba_attn",
    )(qk, qk, proj)


def _merge_kernel(x_ref, ya_ref, yb_ref, yc_ref, ga_ref, gbb_ref, gc_ref, gb_ref, mod_ref,
                  wa_ref, wb_ref, wc_ref, wo_ref, o_ref):
    merged = None
    branches = ((ya_ref, wa_ref, ga_ref), (yb_ref, wb_ref, gbb_ref), (yc_ref, wc_ref, gc_ref))
    for n, (y_ref, w_ref, gl_ref) in enumerate(branches):
        gate = jax.nn.sigmoid(gl_ref[...].astype(F32) + gb_ref[n:n + 1, :])
        term = gate * jnp.dot(y_ref[...], w_ref[...], preferred_element_type=F32)
        merged = term if merged is None else merged + term
    mix = jnp.dot(merged.astype(BF16), wo_ref[...], preferred_element_type=F32)
    o_ref[...] = x_ref[...] + mod_ref[0][2:3] * mix


def _merge_out(x2, ya, yb, yc, proj, gate_b, mod, wa, wb, wc, wo, seq):
    T, D = x2.shape
    tm = ROW_TILE
    per_b = seq // tm
    gbr_block = 5120 // D
    assert gbr_block * D == 5120
    wspec = lambda w: pl.BlockSpec(w.shape, lambda i: (0, 0))
    return pl.pallas_call(
        _merge_kernel,
        grid=(T // tm,),
        in_specs=[pl.BlockSpec((tm, D), lambda i: (i, 0)),
                  pl.BlockSpec((tm, ya.shape[1]), lambda i: (i, 0)),
                  pl.BlockSpec((tm, yb.shape[1]), lambda i: (i, 0)),
                  pl.BlockSpec((tm, yc.shape[1]), lambda i: (i, 0)),
                  pl.BlockSpec((tm, D), lambda i: (i, gbr_block)),
                  pl.BlockSpec((tm, D), lambda i: (i, gbr_block + 1)),
                  pl.BlockSpec((tm, D), lambda i: (i, gbr_block + 2)),
                  pl.BlockSpec((N_BRANCH, D), lambda i: (0, 0)),
                  pl.BlockSpec((1, 6, D), lambda i: (i // per_b, 0, 0)),
                  wspec(wa), wspec(wb), wspec(wc), wspec(wo)],
        out_specs=pl.BlockSpec((tm, D), lambda i: (i, 0)),
        out_shape=jax.ShapeDtypeStruct((T, D), F32),
        compiler_params=_cparams("parallel"),
        name="merge_out",
    )(x2, ya, yb, yc, proj, proj, proj, gate_b, mod, wa, wb, wc, wo)


FF_CHUNK = 256
FF_LOADS = 8


def _stash_weight_slab(k, w1_ref, w3_ref, w2_ref, wb1, wb3, wb2):
    for w_ref, wb in ((w1_ref, wb1), (w3_ref, wb3), (w2_ref, wb2)):
        rows, cols = w_ref.shape[-2:]
        start = pl.multiple_of(k * rows, rows)
        wb[pl.ds(start, rows), :] = w_ref[...].reshape(rows, cols).astype(BF16)


def _swiglu_resident(h, wb1, wb3, wb2):
    acc = None
    for f in range(wb2.shape[0] // FF_CHUNK):
        cols = slice(f * FF_CHUNK, (f + 1) * FF_CHUNK)
        g = jnp.dot(h, wb1[:, cols], preferred_element_type=F32)
        u = jnp.dot(h, wb3[:, cols], preferred_element_type=F32)
        act = (g * jax.nn.sigmoid(g) * u).astype(BF16)
        part = jnp.dot(act, wb2[cols, :], preferred_element_type=F32)
        acc = part if acc is None else acc + part
    return acc


def _ffn_weight_specs(D, F, lead, slab_index):
    assert D % (16 * FF_LOADS) == 0 and F % (16 * FF_LOADS) == 0 and F % FF_CHUNK == 0
    index = lambda *a: slab_index(*a) + (0,)
    return [pl.BlockSpec(lead + (D // FF_LOADS, F), index), pl.BlockSpec(lead + (D // FF_LOADS, F), index),
            pl.BlockSpec(lead + (F // FF_LOADS, D), index)]


def _ffn_kernel(x_ref, mod_ref, g_ref, w1_ref, w3_ref, w2_ref, o_ref, wb1, wb3, wb2):
    s = pl.program_id(0)

    @pl.when(s < FF_LOADS)
    def _():
        _stash_weight_slab(s, w1_ref, w3_ref, w2_ref, wb1, wb3, wb2)

    @pl.when(s >= FF_LOADS)
    def _():
        h = _modulated_norm(x_ref[...], g_ref[...], mod_ref[0], 3).astype(BF16)
        o_ref[...] = x_ref[...] + mod_ref[0][5:6] * _swiglu_resident(h, wb1, wb3, wb2)


def _dense_ffn(x2, mod, g, w1, w3, w2, seq):
    T, D = x2.shape
    F = w1.shape[1]
    tm = ROW_TILE
    per_b = seq // tm
    tile = lambda s: jnp.maximum(s - FF_LOADS, 0)
    return pl.pallas_call(
        _ffn_kernel,
        grid=(FF_LOADS + T // tm,),
        in_specs=[pl.BlockSpec((tm, D), lambda s: (tile(s), 0)),
                  pl.BlockSpec((1, 6, D), lambda s: (tile(s) // per_b, 0, 0)),
                  pl.BlockSpec((1, D), lambda s: (0, 0))]
        + _ffn_weight_specs(D, F, (), lambda s: (jnp.minimum(s, FF_LOADS - 1),)),
        out_specs=pl.BlockSpec((tm, D), lambda s: (tile(s), 0)),
        out_shape=jax.ShapeDtypeStruct((T, D), F32),
        scratch_shapes=[pltpu.VMEM((D, F), BF16), pltpu.VMEM((D, F), BF16), pltpu.VMEM((F, D), BF16)],
        compiler_params=_cparams("arbitrary"),
        name="dense_ffn",
    )(x2, mod, g.reshape(1, D), w1, w3, w2)


def _router_kernel(x_ref, mod_ref, g_ref, rw_ref, rb_ref, h_ref, comb_ref, sel_ref):
    h = _modulated_norm(x_ref[...], g_ref[...], mod_ref[0], 3)
    h_ref[...] = h
    logits = _dot_nt(rw_ref[...], h, precision=HIGHEST) + rb_ref[...]
    E = logits.shape[0]
    eid = lax.broadcasted_iota(jnp.int32, logits.shape, 0)
    v1 = jnp.max(logits, axis=0, keepdims=True)
    i1 = jnp.min(jnp.where(logits == v1, eid, E), axis=0, keepdims=True)
    rest = jnp.where(eid == i1, -jnp.inf, logits)
    v2 = jnp.max(rest, axis=0, keepdims=True)
    i2 = jnp.min(jnp.where(rest == v2, eid, E), axis=0, keepdims=True)
    e2 = jnp.exp(v2 - v1)
    w1 = 1.0 / (1.0 + e2)
    w2 = e2 / (1.0 + e2)
    comb_ref[...] = jnp.where(eid == i1, w1, 0.0) + jnp.where(eid == i2, w2, 0.0)
    sel_ref[...] = ((eid == i1) | (eid == i2)).astype(jnp.int32)


def _router(x2, mod, g, router_w, router_b, seq):
    T, D = x2.shape
    E = router_w.shape[1]
    tm = ROW_TILE
    per_b = seq // tm
    return pl.pallas_call(
        _router_kernel,
        grid=(T // tm,),
        in_specs=[pl.BlockSpec((tm, D), lambda i: (i, 0)),
                  pl.BlockSpec((1, 6, D), lambda i: (i // per_b, 0, 0)),
                  pl.BlockSpec((1, D), lambda i: (0, 0)),
                  pl.BlockSpec((E, D), lambda i: (0, 0)),
                  pl.BlockSpec((E, 1), lambda i: (0, 0))],
        out_specs=[pl.BlockSpec((tm, D), lambda i: (i, 0)),
                   pl.BlockSpec((E, tm), lambda i: (0, i)),
                   pl.BlockSpec((E, tm), lambda i: (0, i))],
        out_shape=[jax.ShapeDtypeStruct((T, D), F32),
                   jax.ShapeDtypeStruct((E, T), F32),
                   jax.ShapeDtypeStruct((E, T), jnp.int32)],
        compiler_params=_cparams("parallel"),
        name="moe_router",
    )(x2, mod, g.reshape(1, D), router_w.T, router_b.reshape(E, 1))


STEP_LOAD, STEP_TILE, STEP_BOTH, STEP_ZERO, STEP_NOOP = 0, 1, 2, 3, 4
MOE_SLOTS = 2


def _moe_kernel(kind_ref, slot_ref, slab_ref, we_ref, wk_ref, t_ref, xs_ref, w1_ref, w3_ref, w2_ref, o_ref,
                wb1, wb3, wb2):
    s = pl.program_id(0)
    kind = kind_ref[s]
    slot = slot_ref[s]

    @pl.when(kind == STEP_LOAD)
    def _():
        _stash_weight_slab(slab_ref[s], w1_ref, w3_ref, w2_ref, wb1.at[slot], wb3.at[slot], wb2.at[slot])

    @pl.when(kind == STEP_TILE)
    def _():
        o_ref[...] = _swiglu_resident(xs_ref[...].astype(BF16), wb1.at[slot], wb3.at[slot], wb2.at[slot])

    for use in range(MOE_SLOTS):
        fill = (use + 1) % MOE_SLOTS

        @pl.when((kind == STEP_BOTH) & (slot == use))
        def _():
            _stash_weight_slab(slab_ref[s], w1_ref, w3_ref, w2_ref, wb1.at[fill], wb3.at[fill], wb2.at[fill])
            o_ref[...] = _swiglu_resident(xs_ref[...].astype(BF16), wb1.at[use], wb3.at[use], wb2.at[use])

    @pl.when(kind == STEP_ZERO)
    def _():
        o_ref[...] = jnp.zeros_like(o_ref)


def _moe_schedule(first_tile, tiles, t0, nt):
    E = tiles.shape[0]
    L = FF_LOADS
    i32 = jnp.int32
    ids = jnp.arange(E, dtype=i32)
    lo = jnp.clip(first_tile - t0, 0, nt)
    cnt = jnp.clip(first_tile + tiles - t0, 0, nt) - lo
    active = cnt > 0
    n_act = jnp.sum(active.astype(i32))
    order = jnp.cumsum(active.astype(i32)) - 1
    pick = ((order[None, :] == ids[:, None]) & active[None, :]).astype(i32)
    act_e = jnp.sum(pick * ids[None, :], axis=1)
    act_cnt = jnp.sum(pick * cnt[None, :], axis=1)
    act_lo = jnp.sum(pick * lo[None, :], axis=1)
    has_next = ids + 1 < n_act
    blk_len = jnp.where(ids < n_act, jnp.maximum(act_cnt, jnp.where(has_next, L, 0)), 0)
    blk_start = L + jnp.cumsum(blk_len) - blk_len
    n_sched = jnp.where(n_act > 0, L + jnp.sum(blk_len), 0)
    n_used = jnp.sum(cnt)

    s = jnp.arange(L * E + nt, dtype=i32)
    k = jnp.clip(jnp.sum(((blk_start[None, :] <= s[:, None]) & (ids[None, :] < n_act)).astype(i32), axis=1) - 1,
                 0, E - 1)
    p = s - blk_start[k]
    in_block = (s >= L) & (s < n_sched)
    first_loads = (s < L) & (n_act > 0)
    tile_step = in_block & (p < act_cnt[k])
    load_step = first_loads | (in_block & has_next[k] & (p < L))
    load_k = jnp.where(first_loads, 0, k + 1)
    zero_step = (s >= n_sched) & (s - n_sched < nt - n_used)

    kind = jnp.where(tile_step & load_step, STEP_BOTH,
                     jnp.where(tile_step, STEP_TILE,
                               jnp.where(load_step, STEP_LOAD, jnp.where(zero_step, STEP_ZERO, STEP_NOOP))))
    slot = jnp.where(tile_step, k, load_k) % MOE_SLOTS
    slab = jnp.clip(jnp.where(first_loads, s, p), 0, L - 1)
    last_load = lax.cummax(jnp.where(load_step, s, 0), axis=0)
    we = act_e[jnp.clip(load_k, 0, E - 1)][last_load]
    wk = slab[last_load]
    own = jnp.where(tile_step, act_lo[k] + p, jnp.where(zero_step, n_used + s - n_sched, nt - 1))
    tile = jnp.clip(lax.cummin(own, axis=0, reverse=True), 0, nt - 1)
    return tuple(a.astype(i32) for a in (kind, slot, slab, we, wk, tile))


def _moe_grouped(xs, first_tile, tiles, w1, w3, w2, tm):
    N, D = xs.shape
    nt = N // tm
    E, _, F = w1.shape
    table = _moe_schedule(first_tile, tiles, 0, nt)
    grid_spec = pltpu.PrefetchScalarGridSpec(
        num_scalar_prefetch=len(table),
        grid=(FF_LOADS * E + nt,),
        in_specs=[pl.BlockSpec((tm, D), lambda s, *pre: (pre[5][s], 0))]
        + _ffn_weight_specs(D, F, (1,), lambda s, *pre: (pre[3][s], pre[4][s])),
        out_specs=pl.BlockSpec((tm, D), lambda s, *pre: (pre[5][s], 0)),
        scratch_shapes=[pltpu.VMEM((MOE_SLOTS, D, F), BF16), pltpu.VMEM((MOE_SLOTS, D, F), BF16),
                        pltpu.VMEM((MOE_SLOTS, F, D), BF16)],
    )
    return pl.pallas_call(
        _moe_kernel,
        grid_spec=grid_spec,
        out_shape=jax.ShapeDtypeStruct((N, D), F32),
        compiler_params=_cparams("arbitrary"),
        name="moe_experts",
    )(*table, xs, w1, w3, w2)


def _combine_kernel(x_ref, y1_ref, y2_ref, w_ref, mod_ref, o_ref):
    w = w_ref[...]
    y = w[:, 0:1] * y1_ref[...] + w[:, 1:2] * y2_ref[...]
    o_ref[...] = x_ref[...] + mod_ref[0][5:6] * y


def _moe_combine(x2, y1, y2, w12, mod, seq):
    T, D = x2.shape
    tm = ROW_TILE
    per_b = seq // tm
    return pl.pallas_call(
        _combine_kernel,
        grid=(T // tm,),
        in_specs=[pl.BlockSpec((tm, D), lambda i: (i, 0)),
                  pl.BlockSpec((tm, D), lambda i: (i, 0)),
                  pl.BlockSpec((tm, D), lambda i: (i, 0)),
                  pl.BlockSpec((tm, 2), lambda i: (i, 0)),
                  pl.BlockSpec((1, 6, D), lambda i: (i // per_b, 0, 0))],
        out_specs=pl.BlockSpec((tm, D), lambda i: (i, 0)),
        out_shape=jax.ShapeDtypeStruct((T, D), F32),
        compiler_params=_cparams("parallel"),
        name="moe_combine",
    )(x2, y1, y2, w12, mod)


def _moe_ffn(x2, mod, g, router_w, router_b, w1, w3, w2, seq):
    T, D = x2.shape
    E = router_w.shape[1]
    tm = ROW_TILE
    h, comb, sel = _router(x2, mod, g, router_w, router_b, seq)

    counts = jnp.sum(sel, axis=1)
    padded = ((counts + tm - 1) // tm) * tm
    group_end = jnp.cumsum(padded)
    group_start = group_end - padded
    rank = jnp.cumsum(sel, axis=1) - sel
    dest = group_start[:, None] + rank
    n_rows = TOP_K * T + E * tm
    n_tiles = n_rows // tm
    tile_start = jnp.arange(n_tiles, dtype=jnp.int32) * tm
    tile_expert = jnp.sum((group_end[None, :] <= tile_start[:, None]).astype(jnp.int32), axis=1)
    tile_expert = jnp.minimum(tile_expert, jnp.max(jnp.where(counts > 0, jnp.arange(E), 0))).astype(jnp.int32)

    eid = jnp.arange(E, dtype=jnp.int32)[:, None]
    e_lo = jnp.min(jnp.where(sel > 0, eid, E), axis=0)
    e_hi = jnp.max(jnp.where(sel > 0, eid, -1), axis=0)
    pick = lambda a, e: jnp.sum(jnp.where(eid == e[None, :], a, 0), axis=0)
    w12 = jnp.stack([pick(comb, e_lo), pick(comb, e_hi)], axis=1)

    tok = jnp.arange(T, dtype=jnp.int32)
    tok_sorted = jnp.sort(jnp.concatenate([e_lo * T + tok, e_hi * T + tok])) % T
    row = jnp.arange(n_rows, dtype=jnp.int32)
    row_expert = jnp.repeat(tile_expert, tm)
    first_sorted = (jnp.cumsum(counts) - counts)[row_expert]
    src = tok_sorted[jnp.clip(first_sorted + row - group_start[row_expert], 0, TOP_K * T - 1)]

    take_rows = lambda a, idx: a.at[idx].get(mode="promise_in_bounds")
    ys = _moe_grouped(take_rows(h, src), group_start // tm, padded // tm, w1, w3, w2, tm)
    y1 = take_rows(ys, pick(dest, e_lo))
    y2 = take_rows(ys, pick(dest, e_hi))
    return _moe_combine(x2, y1, y2, w12, mod, seq)


def kernel(x, c, positions, ada_w, ada_b, norm1_g, norm2_g, w_in, gate_b, conv_w, conv_b, lru_wa, lru_ba,
           lru_wx, lru_bx, lru_lambda, diff_qn, diff_kn, diff_lq1, diff_lk1, diff_lq2, diff_lk2, diff_subln,
           moba_qn, moba_kn, w_br_a, w_br_b, w_br_c, w_out, ffn_w1, ffn_w3, ffn_w2, router_w, router_b,
           moe_w1, moe_w3, moe_w2):
    B, S, D = x.shape
    L = ada_w.shape[0]
    T = B * S
    assert S % MOBA_BLOCK == 0 and S // MOBA_BLOCK <= 56 and S % 1024 == 0
    x2 = x.reshape(T, D)
    mod_all = _ada_mod(c, ada_w, ada_b).reshape(L, B, 6, D)
    cos, sin = _rope_tables(positions)
    bf = lambda w: w.astype(BF16)
    tile2 = lambda v: jnp.tile(v, LANES // HEAD_DIM)

    for l in range(L):
        mod = mod_all[l]
        lam_init = 0.8 - 0.6 * math.exp(-0.3 * l)
        proj = _in_proj(x2, mod, norm1_g[l], w_in, l, S)
        gains = jnp.stack([tile2(diff_qn[l]), tile2(diff_kn[l]), tile2(moba_qn[l]), tile2(moba_kn[l])])
        qk = _qk_prep(proj, gains, cos, sin)
        y_a = _rg_lru_branch(proj, S, conv_w[l], conv_b[l], _pair_block_diag(lru_wa[l]), lru_ba[l],
                             _pair_block_diag(lru_wx[l]), lru_bx[l], lru_lambda[l])
        lam_params = jnp.stack([diff_lq1[l], diff_lk1[l], diff_lq2[l], diff_lk2[l]])
        y_b = _diff_attention(qk, proj, S, lam_params, diff_subln[l], lam_init)
        y_c = _moba_attention(qk, proj, S)
        x2 = _merge_out(x2, y_a, y_b, y_c, proj, gate_b[l], mod, bf(w_br_a[l]), bf(w_br_b[l]),
                        bf(w_br_c[l]), bf(w_out[l]), S)
        if l % 2 == 0:
            x2 = _dense_ffn(x2, mod, norm2_g[l], ffn_w1[l // 2], ffn_w3[l // 2], ffn_w2[l // 2], S)
        else:
            x2 = _moe_ffn(x2, mod, norm2_g[l], router_w[l // 2], router_b[l // 2], moe_w1[l // 2],
                          moe_w3[l // 2], moe_w2[l // 2], S)
    return x2.reshape(B, S, D)
```

```python
import functools
import math

import jax
import jax.numpy as jnp
from jax import lax
from jax.experimental import pallas as pl
from jax.experimental.pallas import tpu as pltpu

F32 = jnp.float32
BF16 = jnp.bfloat16
HIGHEST = lax.Precision.HIGHEST

HEAD_DIM = 64
ROPE_THETA = 10000.0
RNN_BLOCKS = 16
CONV_W = 4
LRU_C = 8.0
DIFF_HEADS = 4
MOBA_HEADS = 8
MOBA_BLOCK = 256
MOBA_TOPK = 3
N_BRANCH = 3
N_EXPERTS = 8
TOP_K = 2
EPS = 1e-6
NEG = -1e30

LANES = 128
VMEM_LIMIT = 56 * 1024 * 1024

ROW_TILE = 512


def _cparams(*sem):
    return pltpu.CompilerParams(dimension_semantics=sem, vmem_limit_bytes=VMEM_LIMIT)


def _modulated_norm(x, g, mod, base):
    ms = jnp.mean(x * x, axis=-1, keepdims=True)
    y = x * lax.rsqrt(ms + EPS) * g
    return y * (1.0 + mod[base + 1:base + 2]) + mod[base:base + 1]


def _dot_nt(a, b, **kw):
    return lax.dot_general(a, b, (((1,), (1,)), ((), ())), preferred_element_type=F32, **kw)


def _ada_kernel(c_ref, w_ref, b_ref, o_ref):
    o_ref[0] = jnp.dot(c_ref[...], w_ref[0], preferred_element_type=F32, precision=HIGHEST) + b_ref[0]


def _ada_mod(c, ada_w, ada_b):
    L, D, N = ada_w.shape
    B = c.shape[0]
    tn = 1536
    return pl.pallas_call(
        _ada_kernel,
        grid=(L, N // tn),
        in_specs=[pl.BlockSpec((B, D), lambda l, j: (0, 0)),
                  pl.BlockSpec((1, D, tn), lambda l, j: (l, 0, j)),
                  pl.BlockSpec((1, 1, tn), lambda l, j: (l, 0, j))],
        out_specs=pl.BlockSpec((1, B, tn), lambda l, j: (l, 0, j)),
        out_shape=jax.ShapeDtypeStruct((L, B, N), F32),
        compiler_params=_cparams("parallel", "parallel"),
        name="ada_mod",
    )(c, ada_w, ada_b.reshape(L, 1, N))


IN_LOADS = 8
IN_COL_CHUNK = 1024


def _inproj_kernel(x_ref, mod_ref, g_ref, w_ref, o_ref, wb):
    s = pl.program_id(0)

    @pl.when(s < IN_LOADS)
    def _():
        rows = w_ref.shape[1]
        wb[pl.ds(pl.multiple_of(s * rows, rows), rows), :] = w_ref[0].astype(BF16)

    @pl.when(s >= IN_LOADS)
    def _():
        h = _modulated_norm(x_ref[...], g_ref[...], mod_ref[0], 0).astype(BF16)
        for c in range(wb.shape[1] // IN_COL_CHUNK):
            cols = slice(c * IN_COL_CHUNK, (c + 1) * IN_COL_CHUNK)
            o_ref[:, cols] = jnp.dot(h, wb[:, cols], preferred_element_type=F32).astype(BF16)


def _in_proj(x2, mod, g, w_in, layer, seq):
    T, D = x2.shape
    N = w_in.shape[2]
    tm = ROW_TILE
    per_b = seq // tm
    assert D % (16 * IN_LOADS) == 0 and N % IN_COL_CHUNK == 0
    tile = lambda s: jnp.maximum(s - IN_LOADS, 0)
    return pl.pallas_call(
        _inproj_kernel,
        grid=(IN_LOADS + T // tm,),
        in_specs=[pl.BlockSpec((tm, D), lambda s: (tile(s), 0)),
                  pl.BlockSpec((1, 6, D), lambda s: (tile(s) // per_b, 0, 0)),
                  pl.BlockSpec((1, D), lambda s: (0, 0)),
                  pl.BlockSpec((1, D // IN_LOADS, N), lambda s: (layer, jnp.minimum(s, IN_LOADS - 1), 0))],
        out_specs=pl.BlockSpec((tm, N), lambda s: (tile(s), 0)),
        out_shape=jax.ShapeDtypeStruct((T, N), BF16),
        scratch_shapes=[pltpu.VMEM((D, N), BF16)],
        compiler_params=_cparams("arbitrary"),
        name="in_proj",
    )(x2, mod, g.reshape(1, D), w_in)


def _rope_kernel(pos_ref, inv_ref, sign_ref, cos_ref, sin_ref):
    ang = pos_ref[...] * inv_ref[...]
    cos_ref[...] = jnp.cos(ang)
    sin_ref[...] = jnp.sin(ang) * sign_ref[...]


def _rope_tables(positions):
    T = positions.size
    pos = positions.reshape(T, 1).astype(F32)
    inv = 1.0 / (ROPE_THETA ** (jnp.arange(0, HEAD_DIM, 2, dtype=F32) / HEAD_DIM))
    half = HEAD_DIM // 2
    inv128 = jnp.tile(inv, LANES // half).reshape(1, LANES)
    sign = jnp.tile(jnp.concatenate([-jnp.ones((half,), F32), jnp.ones((half,), F32)]),
                    LANES // HEAD_DIM).reshape(1, LANES)
    tm = 1024
    return pl.pallas_call(
        _rope_kernel,
        grid=(T // tm,),
        in_specs=[pl.BlockSpec((tm, 1), lambda i: (i, 0)),
                  pl.BlockSpec((1, LANES), lambda i: (0, 0)),
                  pl.BlockSpec((1, LANES), lambda i: (0, 0))],
        out_specs=[pl.BlockSpec((tm, LANES), lambda i: (i, 0))] * 2,
        out_shape=[jax.ShapeDtypeStruct((T, LANES), F32)] * 2,
        compiler_params=_cparams("parallel"),
        name="rope_tables",
    )(pos, inv128, sign)


def _qkprep_kernel(dq_ref, dk_ref, mq_ref, mk_ref, gain_ref, cos_ref, sin_ref, seg_ref, o_ref):
    cos = cos_ref[...]
    sin = sin_ref[...]
    seg = seg_ref[...]
    lane = lax.broadcasted_iota(jnp.int32, cos.shape, 1)
    first_half = (lane % HEAD_DIM) < (HEAD_DIM // 2)
    width = dq_ref.shape[1]
    for gi, ref in enumerate((dq_ref, dk_ref, mq_ref, mk_ref)):
        gain = gain_ref[gi:gi + 1, :]
        for cb in range(width // LANES):
            x = ref[:, cb * LANES:(cb + 1) * LANES].astype(F32)
            sq = x * x
            sq_hi = sq.astype(BF16)
            sq_lo = (sq - sq_hi.astype(F32)).astype(BF16)
            ms = (jnp.dot(sq_hi, seg, preferred_element_type=F32)
                  + jnp.dot(sq_lo, seg, preferred_element_type=F32))
            y = x * lax.rsqrt(ms + EPS) * gain
            swapped = jnp.where(first_half, pltpu.roll(y, LANES - HEAD_DIM // 2, 1),
                                pltpu.roll(y, HEAD_DIM // 2, 1))
            r = y * cos + swapped * sin
            if gi % 2 == 0:
                r = r * (math.log2(math.e) / math.sqrt(HEAD_DIM))
            col = gi * width + cb * LANES
            o_ref[:, col:col + LANES] = r.astype(BF16)


def _qk_prep(proj, gains, cos, sin):
    T = proj.shape[0]
    width = DIFF_HEADS * 2 * HEAD_DIM
    seg = jnp.kron(jnp.eye(LANES // HEAD_DIM, dtype=F32),
                   jnp.full((HEAD_DIM, HEAD_DIM), 1.0 / HEAD_DIM, F32)).astype(BF16)
    tm = ROW_TILE
    col_blocks = (4, 5, 7, 8)
    in_specs = [pl.BlockSpec((tm, width), functools.partial(lambda i, c: (i, c), c=c)) for c in col_blocks]
    in_specs += [pl.BlockSpec((4, LANES), lambda i: (0, 0)),
                 pl.BlockSpec((tm, LANES), lambda i: (i, 0)),
                 pl.BlockSpec((tm, LANES), lambda i: (i, 0)),
                 pl.BlockSpec((LANES, LANES), lambda i: (0, 0))]
    return pl.pallas_call(
        _qkprep_kernel,
        grid=(T // tm,),
        in_specs=in_specs,
        out_specs=pl.BlockSpec((tm, 4 * width), lambda i: (i, 0)),
        out_shape=jax.ShapeDtypeStruct((T, 4 * width), BF16),
        compiler_params=_cparams("parallel"),
        name="qk_prep",
    )(proj, proj, proj, proj, gains, cos, sin, seg)


def _gelu_tanh(x):
    return 0.5 * x * (1.0 + jnp.tanh(math.sqrt(2.0 / math.pi) * (x + 0.044715 * x * x * x)))


SUBLANES = 8
LRU_LANE_BLOCKS = 2


def _rglru_kernel(x_ref, g_ref, cw_ref, cb_ref, wa_ref, ba_ref, wx_ref, bx_ref, lam_ref, o_ref, *scratch):
    S = x_ref.shape[0]
    steps = S // SUBLANES
    pitch = steps + SUBLANES
    nblk = x_ref.shape[1] // LANES
    per_blk = len(scratch) // nblk
    sub = lax.broadcasted_iota(jnp.int32, (SUBLANES, LANES), 0)

    for blk in range(nblk):
        slab, fold_x, a_buf, u_buf, p_buf = scratch[blk * per_blk:(blk + 1) * per_blk]
        lanes = slice(blk * LANES, (blk + 1) * LANES)

        for s in range(SUBLANES):
            slab[s * pitch:s * pitch + steps, :] = x_ref[s * steps:(s + 1) * steps, lanes].astype(F32)

        def fold(step, carry):
            fold_x[pl.ds(pl.multiple_of(step * SUBLANES, SUBLANES), SUBLANES), :] = (
                slab[pl.ds(step, SUBLANES, stride=pitch), :])
            return carry

        lax.fori_loop(0, steps, fold, 0, unroll=8)
        xf = fold_x[...]

        def head(j):
            tail = xf[(steps - j) * SUBLANES:, :]
            vregs = [jnp.where(sub == 0, 0.0, pltpu.roll(tail[v * SUBLANES:(v + 1) * SUBLANES, :], 1, 0))
                     for v in range(j)]
            return jnp.concatenate(vregs + [xf[:(steps - j) * SUBLANES, :]], axis=0)

        xc = cb_ref[:, lanes] + cw_ref[CONV_W - 1:CONV_W, lanes] * xf
        for j in range(1, CONV_W):
            xc = xc + cw_ref[CONV_W - 1 - j:CONV_W - j, lanes] * head(j)

        xcb = xc.astype(BF16)
        r = jax.nn.sigmoid(jnp.dot(xcb, wa_ref[blk], preferred_element_type=F32) + ba_ref[:, lanes])
        gi = jax.nn.sigmoid(jnp.dot(xcb, wx_ref[blk], preferred_element_type=F32) + bx_ref[:, lanes])
        neg_lam = -lam_ref[:, lanes]
        softplus = jnp.maximum(neg_lam, 0.0) + jnp.log1p(jnp.exp(-jnp.abs(neg_lam)))
        a = jnp.exp2(r * ((-LRU_C * math.log2(math.e)) * softplus))
        gap = 1.0 - a * a
        mult = jnp.where(gap > 0.0, gap * lax.rsqrt(gap), 0.0)
        a_buf[...] = a
        u_buf[...] = mult * gi * xc

    def scan(step, carry):
        rows = pl.ds(pl.multiple_of(step * SUBLANES, SUBLANES), SUBLANES)
        out = []
        for blk in range(nblk):
            _, _, a_buf, u_buf, p_buf = scratch[blk * per_blk:(blk + 1) * per_blk]
            h, p = carry[blk]
            a = a_buf[rows, :]
            h = a * h + u_buf[rows, :]
            p = a * p
            u_buf[rows, :] = h
            p_buf[rows, :] = p
            out.append((h, p))
        return tuple(out)

    ones = jnp.ones((SUBLANES, LANES), F32)
    last = lax.fori_loop(0, steps, scan, ((jnp.zeros_like(ones), ones),) * nblk, unroll=8)

    for blk in range(nblk):
        slab, fold_x, a_buf, u_buf, p_buf = scratch[blk * per_blk:(blk + 1) * per_blk]
        lanes = slice(blk * LANES, (blk + 1) * LANES)
        h_end, p_end = last[blk]
        h_in = [jnp.zeros((1, LANES), F32)]
        for s in range(SUBLANES - 1):
            h_in.append(h_end[s:s + 1, :] + p_end[s:s + 1, :] * h_in[s])
        h_in = jnp.concatenate(h_in, axis=0)
        u_buf[...] = u_buf[...] + p_buf[...] * jnp.concatenate([h_in] * steps, axis=0)

        def unfold(step, carry):
            slab[pl.ds(step, SUBLANES, stride=pitch), :] = (
                u_buf[pl.ds(pl.multiple_of(step * SUBLANES, SUBLANES), SUBLANES), :])
            return carry

        lax.fori_loop(0, steps, unfold, 0, unroll=8)
        for s in range(SUBLANES):
            rows = slice(s * steps, (s + 1) * steps)
            h = slab[s * pitch:s * pitch + steps, :]
            o_ref[rows, lanes] = (_gelu_tanh(g_ref[rows, lanes].astype(F32)) * h).astype(BF16)


def _rg_lru_branch(proj, seq, conv_w, conv_b, wa2, ba, wx2, bx, lam):
    T = proj.shape[0]
    C = conv_w.shape[1]
    B = T // seq
    nblk = LRU_LANE_BLOCKS
    W = nblk * LANES
    g_off = C // W
    steps = seq // SUBLANES
    assert seq % (SUBLANES * SUBLANES) == 0 and C % W == 0
    vec = lambda v: v.reshape(1, C)
    cols = lambda rows: pl.BlockSpec((rows, W), lambda b, n: (0, n))
    per_blk = [pltpu.VMEM((SUBLANES * (steps + SUBLANES), LANES), F32)] + [pltpu.VMEM((seq, LANES), F32)] * 4
    return pl.pallas_call(
        _rglru_kernel,
        grid=(B, C // W),
        in_specs=[pl.BlockSpec((seq, W), lambda b, n: (b, n)),
                  pl.BlockSpec((seq, W), lambda b, n: (b, g_off + n)),
                  cols(CONV_W), cols(1),
                  pl.BlockSpec((nblk, LANES, LANES), lambda b, n: (n, 0, 0)), cols(1),
                  pl.BlockSpec((nblk, LANES, LANES), lambda b, n: (n, 0, 0)), cols(1), cols(1)],
        out_specs=pl.BlockSpec((seq, W), lambda b, n: (b, n)),
        out_shape=jax.ShapeDtypeStruct((T, C), BF16),
        scratch_shapes=per_blk * nblk,
        compiler_params=_cparams("parallel", "parallel"),
        name="rg_lru",
    )(proj, proj, conv_w, vec(conv_b), wa2, vec(ba), wx2, vec(bx), vec(lam))


def _pair_block_diag(w):
    n, d, _ = w.shape
    z = jnp.zeros((n // 2, 2, d, 2, d), w.dtype)
    z = z.at[:, 0, :, 0, :].set(w[0::2]).at[:, 1, :, 1, :].set(w[1::2])
    return z.reshape(n // 2, 2 * d, 2 * d).astype(BF16)


ATTN_TILE = 256


def _lane_tile(x, n):
    return jnp.concatenate([x] * n, axis=1)


def _softmax_update(s, state, v_aug):
    row_max = jnp.max(s, axis=-1, keepdims=True)
    if state is None:
        m_new = jnp.broadcast_to(row_max, (s.shape[0], LANES))
        p = jnp.exp2(s - _lane_tile(m_new, s.shape[1] // LANES)).astype(BF16)
        return m_new, jnp.dot(p, v_aug, preferred_element_type=F32)
    m_old, acc = state
    m_new = jnp.maximum(m_old, row_max)
    p = jnp.exp2(s - _lane_tile(m_new, s.shape[1] // LANES)).astype(BF16)
    alpha = _lane_tile(jnp.exp2(m_old - m_new), acc.shape[1] // LANES)
    return m_new, alpha * acc + jnp.dot(p, v_aug, preferred_element_type=F32)


def _softmax_finish(state):
    _, acc = state
    return acc[:, :LANES] / acc[:, LANES:]


def _state_rows(state, start, stop):
    m, acc = state
    return m[start:stop], acc[start:stop]


def _causal_mask(t):
    return lax.broadcasted_iota(jnp.int32, (t, t), 1) <= lax.broadcasted_iota(jnp.int32, (t, t), 0)


def _mask_diagonal(s, causal):
    t = causal.shape[0]
    if s.shape[0] == t:
        return jnp.where(causal, s, NEG)
    return jnp.concatenate([jnp.where(causal, s[:t], NEG), s[t:]], axis=0)


def _with_ones(v):
    return jnp.concatenate([v, jnp.ones_like(v)], axis=1)


def _diffattn_kernel(q_ref, k_ref, v_ref, lam_ref, g_ref, o_ref, *, lam_init):
    t = ATTN_TILE
    S = q_ref.shape[0]
    q = q_ref[...]
    lane = lax.broadcasted_iota(jnp.int32, q.shape, 1)
    zero = jnp.zeros_like(q)
    q1 = jnp.where(lane < HEAD_DIM, q, zero)
    q2 = jnp.where(lane < HEAD_DIM, zero, q)
    causal = _causal_mask(t)
    lp = lam_ref[...]
    lam = (jnp.exp(jnp.sum(lp[0:1] * lp[1:2], axis=-1, keepdims=True))
           - jnp.exp(jnp.sum(lp[2:3] * lp[3:4], axis=-1, keepdims=True)) + lam_init)
    st1 = st2 = None
    for j in range(S // t):
        k = k_ref[j * t:(j + 1) * t, :]
        v_aug = _with_ones(v_ref[j * t:(j + 1) * t, :])
        st1 = _softmax_update(_mask_diagonal(_dot_nt(q1[j * t:, :], k), causal), st1, v_aug)
        st2 = _softmax_update(_mask_diagonal(_dot_nt(q2[j * t:, :], k), causal), st2, v_aug)
        o = _softmax_finish(_state_rows(st1, 0, t)) - lam * _softmax_finish(_state_rows(st2, 0, t))
        ms = jnp.mean(o * o, axis=-1, keepdims=True)
        o_ref[j * t:(j + 1) * t, :] = (o * lax.rsqrt(ms + EPS) * g_ref[...] * (1.0 - lam_init)).astype(BF16)
        if (j + 1) * t < S:
            st1 = _state_rows(st1, t, None)
            st2 = _state_rows(st2, t, None)


def _diff_attention(qk, proj, seq, lam_params, subln_g, lam_init):
    T = qk.shape[0]
    B = T // seq
    H = DIFF_HEADS
    return pl.pallas_call(
        functools.partial(_diffattn_kernel, lam_init=lam_init),
        grid=(B, H),
        in_specs=[pl.BlockSpec((seq, LANES), lambda b, h: (b, h)),
                  pl.BlockSpec((seq, LANES), lambda b, h: (b, H + h)),
                  pl.BlockSpec((seq, LANES), lambda b, h: (b, 24 + h)),
                  pl.BlockSpec((4, HEAD_DIM), lambda b, h: (0, 0)),
                  pl.BlockSpec((1, LANES), lambda b, h: (0, 0))],
        out_specs=pl.BlockSpec((seq, LANES), lambda b, h: (b, h)),
        out_shape=jax.ShapeDtypeStruct((T, H * LANES), BF16),
        compiler_params=_cparams("parallel", "parallel"),
        name="diff_attn",
    )(qk, qk, proj, lam_params, subln_g.reshape(1, LANES))


def _moba_kernel(q_ref, k_ref, v_ref, o_ref, kmean):
    t = ATTN_TILE
    S = q_ref.shape[0]
    nb = S // MOBA_BLOCK
    nb8 = -(-nb // 8) * 8
    lane_row = lax.broadcasted_iota(jnp.int32, (1, LANES), 1)

    kmean[...] = jnp.zeros_like(kmean)
    for blk in range(nb):
        mean = jnp.mean(k_ref[blk * MOBA_BLOCK:(blk + 1) * MOBA_BLOCK, :].astype(F32), axis=0, keepdims=True)
        kmean[blk:blk + 1, :] = jnp.where(lane_row < HEAD_DIM, mean, 0.0)
        kmean[HEAD_DIM + blk:HEAD_DIM + blk + 1, :] = jnp.where(lane_row < HEAD_DIM, 0.0, mean)

    q = q_ref[...]
    lane = lax.broadcasted_iota(jnp.int32, q.shape, 1)
    zero = jnp.zeros_like(q)
    q_a = jnp.where(lane < HEAD_DIM, q, zero)
    q_b = jnp.where(lane < HEAD_DIM, zero, q)

    km = kmean[...]
    km_hi = km.astype(BF16)
    km_mid = (km - km_hi.astype(F32)).astype(BF16)
    km_lo = (km - km_hi.astype(F32) - km_mid.astype(F32)).astype(BF16)
    gate = _dot_nt(km_hi, q) + _dot_nt(km_mid, q) + _dot_nt(km_lo, q)
    gate_blk = lax.broadcasted_iota(jnp.int32, (nb8, S), 0)
    own = lax.broadcasted_iota(jnp.int32, (nb8, S), 1) // MOBA_BLOCK
    past = gate_blk < own
    unused_rows = jnp.full((HEAD_DIM - nb8, S), NEG, F32)
    bias_rows = []
    for head in range(2):
        g = jnp.where(past, gate[head * HEAD_DIM:head * HEAD_DIM + nb8, :], NEG)
        rank = jnp.zeros(g.shape, jnp.int32)
        for jb in range(nb - 1):
            other = g[jb:jb + 1, :]
            beats = (other > g) | ((other == g) & (jb < gate_blk))
            rank = rank + beats.astype(jnp.int32)
        allowed = (past & (rank < MOBA_TOPK)) | (gate_blk == own)
        bias_rows += [jnp.where(allowed, 0.0, NEG), unused_rows]
    bias_a = jnp.concatenate(bias_rows, axis=0).T
    bias_b = pltpu.roll(bias_a, HEAD_DIM, 1)
    qa_aug = jnp.concatenate([q_a, bias_a.astype(BF16)], axis=1)
    qb_aug = jnp.concatenate([q_b, bias_b.astype(BF16)], axis=1)

    causal = _causal_mask(t)
    key_lane = lax.broadcasted_iota(jnp.int32, (t, LANES), 1)
    key_row = lax.broadcasted_iota(jnp.int32, (t, LANES), 0)
    st_a = st_b = None
    for j in range(S // t):
        key_blk = (j * t + key_row) // MOBA_BLOCK
        k_aug = jnp.concatenate([k_ref[j * t:(j + 1) * t, :], (key_lane == key_blk).astype(BF16)], axis=1)
        v_aug = _with_ones(v_ref[j * t:(j + 1) * t, :])
        st_a = _softmax_update(_mask_diagonal(_dot_nt(qa_aug[j * t:, :], k_aug), causal), st_a, v_aug)
        st_b = _softmax_update(_mask_diagonal(_dot_nt(qb_aug[j * t:, :], k_aug), causal), st_b, v_aug)
        o = jnp.where(key_lane < HEAD_DIM, _softmax_finish(_state_rows(st_a, 0, t)),
                      _softmax_finish(_state_rows(st_b, 0, t)))
        o_ref[j * t:(j + 1) * t, :] = o.astype(BF16)
        if (j + 1) * t < S:
            st_a = _state_rows(st_a, t, None)
            st_b = _state_rows(st_b, t, None)


def _moba_attention(qk, proj, seq):
    T = qk.shape[0]
    B = T // seq
    HP = MOBA_HEADS // 2
    return pl.pallas_call(
        _moba_kernel,
        grid=(B, HP),
        in_specs=[pl.BlockSpec((seq, LANES), lambda b, h: (b, 8 + h)),
                  pl.BlockSpec((seq, LANES), lambda b, h: (b, 12 + h)),
                  pl.BlockSpec((seq, LANES), lambda b, h: (b, 36 + h))],
        out_specs=pl.BlockSpec((seq, LANES), lambda b, h: (b, h)),
        out_shape=jax.ShapeDtypeStruct((T, HP * LANES), BF16),
        scratch_shapes=[pltpu.VMEM((LANES, LANES), F32)],
        compiler_params=_cparams("parallel", "parallel"),
        name="moba_attn",
    )(qk, qk, proj)


def _merge_kernel(x_ref, ya_ref, yb_ref, yc_ref, ga_ref, gbb_ref, gc_ref, gb_ref, mod_ref,
                  wa_ref, wb_ref, wc_ref, wo_ref, o_ref):
    merged = None
    branches = ((ya_ref, wa_ref, ga_ref), (yb_ref, wb_ref, gbb_ref), (yc_ref, wc_ref, gc_ref))
    for n, (y_ref, w_ref, gl_ref) in enumerate(branches):
        gate = jax.nn.sigmoid(gl_ref[...].astype(F32) + gb_ref[n:n + 1, :])
        term = gate * jnp.dot(y_ref[...], w_ref[...], preferred_element_type=F32)
        merged = term if merged is None else merged + term
    mix = jnp.dot(merged.astype(BF16), wo_ref[...], preferred_element_type=F32)
    o_ref[...] = x_ref[...] + mod_ref[0][2:3] * mix


def _merge_out(x2, ya, yb, yc, proj, gate_b, mod, wa, wb, wc, wo, seq):
    T, D = x2.shape
    tm = ROW_TILE
    per_b = seq // tm
    gbr_block = 5120 // D
    assert gbr_block * D == 5120
    wspec = lambda w: pl.BlockSpec(w.shape, lambda i: (0, 0))
    return pl.pallas_call(
        _merge_kernel,
        grid=(T // tm,),
        in_specs=[pl.BlockSpec((tm, D), lambda i: (i, 0)),
                  pl.BlockSpec((tm, ya.shape[1]), lambda i: (i, 0)),
                  pl.BlockSpec((tm, yb.shape[1]), lambda i: (i, 0)),
                  pl.BlockSpec((tm, yc.shape[1]), lambda i: (i, 0)),
                  pl.BlockSpec((tm, D), lambda i: (i, gbr_block)),
                  pl.BlockSpec((tm, D), lambda i: (i, gbr_block + 1)),
                  pl.BlockSpec((tm, D), lambda i: (i, gbr_block + 2)),
                  pl.BlockSpec((N_BRANCH, D), lambda i: (0, 0)),
                  pl.BlockSpec((1, 6, D), lambda i: (i // per_b, 0, 0)),
                  wspec(wa), wspec(wb), wspec(wc), wspec(wo)],
        out_specs=pl.BlockSpec((tm, D), lambda i: (i, 0)),
        out_shape=jax.ShapeDtypeStruct((T, D), F32),
        compiler_params=_cparams("parallel"),
        name="merge_out",
    )(x2, ya, yb, yc, proj, proj, proj, gate_b, mod, wa, wb, wc, wo)


FF_CHUNK = 256
FF_LOADS = 8


def _stash_weight_slab(k, w1_ref, w3_ref, w2_ref, wb1, wb3, wb2):
    for w_ref, wb in ((w1_ref, wb1), (w3_ref, wb3), (w2_ref, wb2)):
        rows, cols = w_ref.shape[-2:]
        start = pl.multiple_of(k * rows, rows)
        wb[pl.ds(start, rows), :] = w_ref[...].reshape(rows, cols).astype(BF16)


def _swiglu_resident(h, wb1, wb3, wb2):
    acc = None
    for f in range(wb2.shape[0] // FF_CHUNK):
        cols = slice(f * FF_CHUNK, (f + 1) * FF_CHUNK)
        g = jnp.dot(h, wb1[:, cols], preferred_element_type=F32)
        u = jnp.dot(h, wb3[:, cols], preferred_element_type=F32)
        act = (g * jax.nn.sigmoid(g) * u).astype(BF16)
        part = jnp.dot(act, wb2[cols, :], preferred_element_type=F32)
        acc = part if acc is None else acc + part
    return acc


def _ffn_weight_specs(D, F, lead, slab_index):
    assert D % (16 * FF_LOADS) == 0 and F % (16 * FF_LOADS) == 0 and F % FF_CHUNK == 0
    index = lambda *a: slab_index(*a) + (0,)
    return [pl.BlockSpec(lead + (D // FF_LOADS, F), index), pl.BlockSpec(lead + (D // FF_LOADS, F), index),
            pl.BlockSpec(lead + (F // FF_LOADS, D), index)]


def _ffn_kernel(x_ref, mod_ref, g_ref, w1_ref, w3_ref, w2_ref, o_ref, wb1, wb3, wb2):
    s = pl.program_id(0)

    @pl.when(s < FF_LOADS)
    def _():
        _stash_weight_slab(s, w1_ref, w3_ref, w2_ref, wb1, wb3, wb2)

    @pl.when(s >= FF_LOADS)
    def _():
        h = _modulated_norm(x_ref[...], g_ref[...], mod_ref[0], 3).astype(BF16)
        o_ref[...] = x_ref[...] + mod_ref[0][5:6] * _swiglu_resident(h, wb1, wb3, wb2)


def _dense_ffn(x2, mod, g, w1, w3, w2, seq):
    T, D = x2.shape
    F = w1.shape[1]
    tm = ROW_TILE
    per_b = seq // tm
    tile = lambda s: jnp.maximum(s - FF_LOADS, 0)
    return pl.pallas_call(
        _ffn_kernel,
        grid=(FF_LOADS + T // tm,),
        in_specs=[pl.BlockSpec((tm, D), lambda s: (tile(s), 0)),
                  pl.BlockSpec((1, 6, D), lambda s: (tile(s) // per_b, 0, 0)),
                  pl.BlockSpec((1, D), lambda s: (0, 0))]
        + _ffn_weight_specs(D, F, (), lambda s: (jnp.minimum(s, FF_LOADS - 1),)),
        out_specs=pl.BlockSpec((tm, D), lambda s: (tile(s), 0)),
        out_shape=jax.ShapeDtypeStruct((T, D), F32),
        scratch_shapes=[pltpu.VMEM((D, F), BF16), pltpu.VMEM((D, F), BF16), pltpu.VMEM((F, D), BF16)],
        compiler_params=_cparams("arbitrary"),
        name="dense_ffn",
    )(x2, mod, g.reshape(1, D), w1, w3, w2)


def _router_kernel(x_ref, mod_ref, g_ref, rw_ref, rb_ref, h_ref, comb_ref, sel_ref):
    h = _modulated_norm(x_ref[...], g_ref[...], mod_ref[0], 3)
    h_ref[...] = h
    logits = _dot_nt(rw_ref[...], h, precision=HIGHEST) + rb_ref[...]
    E = logits.shape[0]
    eid = lax.broadcasted_iota(jnp.int32, logits.shape, 0)
    v1 = jnp.max(logits, axis=0, keepdims=True)
    i1 = jnp.min(jnp.where(logits == v1, eid, E), axis=0, keepdims=True)
    rest = jnp.where(eid == i1, -jnp.inf, logits)
    v2 = jnp.max(rest, axis=0, keepdims=True)
    i2 = jnp.min(jnp.where(rest == v2, eid, E), axis=0, keepdims=True)
    e2 = jnp.exp(v2 - v1)
    w1 = 1.0 / (1.0 + e2)
    w2 = e2 / (1.0 + e2)
    comb_ref[...] = jnp.where(eid == i1, w1, 0.0) + jnp.where(eid == i2, w2, 0.0)
    sel_ref[...] = ((eid == i1) | (eid == i2)).astype(jnp.int32)


def _router(x2, mod, g, router_w, router_b, seq):
    T, D = x2.shape
    E = router_w.shape[1]
    tm = ROW_TILE
    per_b = seq // tm
    return pl.pallas_call(
        _router_kernel,
        grid=(T // tm,),
        in_specs=[pl.BlockSpec((tm, D), lambda i: (i, 0)),
                  pl.BlockSpec((1, 6, D), lambda i: (i // per_b, 0, 0)),
                  pl.BlockSpec((1, D), lambda i: (0, 0)),
                  pl.BlockSpec((E, D), lambda i: (0, 0)),
                  pl.BlockSpec((E, 1), lambda i: (0, 0))],
        out_specs=[pl.BlockSpec((tm, D), lambda i: (i, 0)),
                   pl.BlockSpec((E, tm), lambda i: (0, i)),
                   pl.BlockSpec((E, tm), lambda i: (0, i))],
        out_shape=[jax.ShapeDtypeStruct((T, D), F32),
                   jax.ShapeDtypeStruct((E, T), F32),
                   jax.ShapeDtypeStruct((E, T), jnp.int32)],
        compiler_params=_cparams("parallel"),
        name="moe_router",
    )(x2, mod, g.reshape(1, D), router_w.T, router_b.reshape(E, 1))


STEP_LOAD, STEP_TILE, STEP_BOTH, STEP_ZERO, STEP_NOOP = 0, 1, 2, 3, 4
MOE_SLOTS = 2


def _moe_kernel(kind_ref, slot_ref, slab_ref, we_ref, wk_ref, t_ref, xs_ref, w1_ref, w3_ref, w2_ref, o_ref,
                wb1, wb3, wb2):
    s = pl.program_id(0)
    kind = kind_ref[s]
    slot = slot_ref[s]

    @pl.when(kind == STEP_LOAD)
    def _():
        _stash_weight_slab(slab_ref[s], w1_ref, w3_ref, w2_ref, wb1.at[slot], wb3.at[slot], wb2.at[slot])

    @pl.when(kind == STEP_TILE)
    def _():
        o_ref[...] = _swiglu_resident(xs_ref[...].astype(BF16), wb1.at[slot], wb3.at[slot], wb2.at[slot])

    for use in range(MOE_SLOTS):
        fill = (use + 1) % MOE_SLOTS

        @pl.when((kind == STEP_BOTH) & (slot == use))
        def _():
            _stash_weight_slab(slab_ref[s], w1_ref, w3_ref, w2_ref, wb1.at[fill], wb3.at[fill], wb2.at[fill])
            o_ref[...] = _swiglu_resident(xs_ref[...].astype(BF16), wb1.at[use], wb3.at[use], wb2.at[use])

    @pl.when(kind == STEP_ZERO)
    def _():
        o_ref[...] = jnp.zeros_like(o_ref)


def _moe_schedule(first_tile, tiles, t0, nt):
    E = tiles.shape[0]
    L = FF_LOADS
    i32 = jnp.int32
    ids = jnp.arange(E, dtype=i32)
    lo = jnp.clip(first_tile - t0, 0, nt)
    cnt = jnp.clip(first_tile + tiles - t0, 0, nt) - lo
    active = cnt > 0
    n_act = jnp.sum(active.astype(i32))
    order = jnp.cumsum(active.astype(i32)) - 1
    pick = ((order[None, :] == ids[:, None]) & active[None, :]).astype(i32)
    act_e = jnp.sum(pick * ids[None, :], axis=1)
    act_cnt = jnp.sum(pick * cnt[None, :], axis=1)
    act_lo = jnp.sum(pick * lo[None, :], axis=1)
    has_next = ids + 1 < n_act
    blk_len = jnp.where(ids < n_act, jnp.maximum(act_cnt, jnp.where(has_next, L, 0)), 0)
    blk_start = L + jnp.cumsum(blk_len) - blk_len
    n_sched = jnp.where(n_act > 0, L + jnp.sum(blk_len), 0)
    n_used = jnp.sum(cnt)

    s = jnp.arange(L * E + nt, dtype=i32)
    k = jnp.clip(jnp.sum(((blk_start[None, :] <= s[:, None]) & (ids[None, :] < n_act)).astype(i32), axis=1) - 1,
                 0, E - 1)
    p = s - blk_start[k]
    in_block = (s >= L) & (s < n_sched)
    first_loads = (s < L) & (n_act > 0)
    tile_step = in_block & (p < act_cnt[k])
    load_step = first_loads | (in_block & has_next[k] & (p < L))
    load_k = jnp.where(first_loads, 0, k + 1)
    zero_step = (s >= n_sched) & (s - n_sched < nt - n_used)

    kind = jnp.where(tile_step & load_step, STEP_BOTH,
                     jnp.where(tile_step, STEP_TILE,
                               jnp.where(load_step, STEP_LOAD, jnp.where(zero_step, STEP_ZERO, STEP_NOOP))))
    slot = jnp.where(tile_step, k, load_k) % MOE_SLOTS
    slab = jnp.clip(jnp.where(first_loads, s, p), 0, L - 1)
    last_load = lax.cummax(jnp.where(load_step, s, 0), axis=0)
    we = act_e[jnp.clip(load_k, 0, E - 1)][last_load]
    wk = slab[last_load]
    own = jnp.where(tile_step, act_lo[k] + p, jnp.where(zero_step, n_used + s - n_sched, nt - 1))
    tile = jnp.clip(lax.cummin(own, axis=0, reverse=True), 0, nt - 1)
    return tuple(a.astype(i32) for a in (kind, slot, slab, we, wk, tile))


def _moe_grouped(xs, first_tile, tiles, w1, w3, w2, tm):
    N, D = xs.shape
    nt = N // tm
    E, _, F = w1.shape
    table = _moe_schedule(first_tile, tiles, 0, nt)
    grid_spec = pltpu.PrefetchScalarGridSpec(
        num_scalar_prefetch=len(table),
        grid=(FF_LOADS * E + nt,),
        in_specs=[pl.BlockSpec((tm, D), lambda s, *pre: (pre[5][s], 0))]
        + _ffn_weight_specs(D, F, (1,), lambda s, *pre: (pre[3][s], pre[4][s])),
        out_specs=pl.BlockSpec((tm, D), lambda s, *pre: (pre[5][s], 0)),
        scratch_shapes=[pltpu.VMEM((MOE_SLOTS, D, F), BF16), pltpu.VMEM((MOE_SLOTS, D, F), BF16),
                        pltpu.VMEM((MOE_SLOTS, F, D), BF16)],
    )
    return pl.pallas_call(
        _moe_kernel,
        grid_spec=grid_spec,
        out_shape=jax.ShapeDtypeStruct((N, D), F32),
        compiler_params=_cparams("arbitrary"),
        name="moe_experts",
    )(*table, xs, w1, w3, w2)


def _combine_kernel(x_ref, y1_ref, y2_ref, w_ref, mod_ref, o_ref):
    w = w_ref[...]
    y = w[:, 0:1] * y1_ref[...] + w[:, 1:2] * y2_ref[...]
    o_ref[...] = x_ref[...] + mod_ref[0][5:6] * y


def _moe_combine(x2, y1, y2, w12, mod, seq):
    T, D = x2.shape
    tm = ROW_TILE
    per_b = seq // tm
    return pl.pallas_call(
        _combine_kernel,
        grid=(T // tm,),
        in_specs=[pl.BlockSpec((tm, D), lambda i: (i, 0)),
                  pl.BlockSpec((tm, D), lambda i: (i, 0)),
                  pl.BlockSpec((tm, D), lambda i: (i, 0)),
                  pl.BlockSpec((tm, 2), lambda i: (i, 0)),
                  pl.BlockSpec((1, 6, D), lambda i: (i // per_b, 0, 0))],
        out_specs=pl.BlockSpec((tm, D), lambda i: (i, 0)),
        out_shape=jax.ShapeDtypeStruct((T, D), F32),
        compiler_params=_cparams("parallel"),
        name="moe_combine",
    )(x2, y1, y2, w12, mod)


def _moe_ffn(x2, mod, g, router_w, router_b, w1, w3, w2, seq):
    T, D = x2.shape
    E = router_w.shape[1]
    tm = ROW_TILE
    h, comb, sel = _router(x2, mod, g, router_w, router_b, seq)

    counts = jnp.sum(sel, axis=1)
    padded = ((counts + tm - 1) // tm) * tm
    group_end = jnp.cumsum(padded)
    group_start = group_end - padded
    rank = jnp.cumsum(sel, axis=1) - sel
    dest = group_start[:, None] + rank
    n_rows = TOP_K * T + E * tm
    n_tiles = n_rows // tm
    tile_start = jnp.arange(n_tiles, dtype=jnp.int32) * tm
    tile_expert = jnp.sum((group_end[None, :] <= tile_start[:, None]).astype(jnp.int32), axis=1)
    tile_expert = jnp.minimum(tile_expert, jnp.max(jnp.where(counts > 0, jnp.arange(E), 0))).astype(jnp.int32)

    eid = jnp.arange(E, dtype=jnp.int32)[:, None]
    e_lo = jnp.min(jnp.where(sel > 0, eid, E), axis=0)
    e_hi = jnp.max(jnp.where(sel > 0, eid, -1), axis=0)
    pick = lambda a, e: jnp.sum(jnp.where(eid == e[None, :], a, 0), axis=0)
    w12 = jnp.stack([pick(comb, e_lo), pick(comb, e_hi)], axis=1)

    tok = jnp.arange(T, dtype=jnp.int32)
    tok_sorted = jnp.sort(jnp.concatenate([e_lo * T + tok, e_hi * T + tok])) % T
    row = jnp.arange(n_rows, dtype=jnp.int32)
    row_expert = jnp.repeat(tile_expert, tm)
    first_sorted = (jnp.cumsum(counts) - counts)[row_expert]
    src = tok_sorted[jnp.clip(first_sorted + row - group_start[row_expert], 0, TOP_K * T - 1)]

    take_rows = lambda a, idx: a.at[idx].get(mode="promise_in_bounds")
    ys = _moe_grouped(take_rows(h, src), group_start // tm, padded // tm, w1, w3, w2, tm)
    y1 = take_rows(ys, pick(dest, e_lo))
    y2 = take_rows(ys, pick(dest, e_hi))
    return _moe_combine(x2, y1, y2, w12, mod, seq)


def kernel(x, c, positions, ada_w, ada_b, norm1_g, norm2_g, w_in, gate_b, conv_w, conv_b, lru_wa, lru_ba,
           lru_wx, lru_bx, lru_lambda, diff_qn, diff_kn, diff_lq1, diff_lk1, diff_lq2, diff_lk2, diff_subln,
           moba_qn, moba_kn, w_br_a, w_br_b, w_br_c, w_out, ffn_w1, ffn_w3, ffn_w2, router_w, router_b,
           moe_w1, moe_w3, moe_w2):
    B, S, D = x.shape
    L = ada_w.shape[0]
    T = B * S
    assert S % MOBA_BLOCK == 0 and S // MOBA_BLOCK <= 56 and S % 1024 == 0
    x2 = x.reshape(T, D)
    mod_all = _ada_mod(c, ada_w, ada_b).reshape(L, B, 6, D)
    cos, sin = _rope_tables(positions)
    bf = lambda w: w.astype(BF16)
    tile2 = lambda v: jnp.tile(v, LANES // HEAD_DIM)

    for l in range(L):
        mod = mod_all[l]
        lam_init = 0.8 - 0.6 * math.exp(-0.3 * l)
        proj = _in_proj(x2, mod, norm1_g[l], w_in, l, S)
        gains = jnp.stack([tile2(diff_qn[l]), tile2(diff_kn[l]), tile2(moba_qn[l]), tile2(moba_kn[l])])
        qk = _qk_prep(proj, gains, cos, sin)
        y_a = _rg_lru_branch(proj, S, conv_w[l], conv_b[l], _pair_block_diag(lru_wa[l]), lru_ba[l],
                             _pair_block_diag(lru_wx[l]), lru_bx[l], lru_lambda[l])
        lam_params = jnp.stack([diff_lq1[l], diff_lk1[l], diff_lq2[l], diff_lk2[l]])
        y_b = _diff_attention(qk, proj, S, lam_params, diff_subln[l], lam_init)
        y_c = _moba_attention(qk, proj, S)
        x2 = _merge_out(x2, y_a, y_b, y_c, proj, gate_b[l], mod, bf(w_br_a[l]), bf(w_br_b[l]),
                        bf(w_br_c[l]), bf(w_out[l]), S)
        if l % 2 == 0:
            x2 = _dense_ffn(x2, mod, norm2_g[l], ffn_w1[l // 2], ffn_w3[l // 2], ffn_w2[l // 2], S)
        else:
            x2 = _moe_ffn(x2, mod, norm2_g[l], router_w[l // 2], router_b[l // 2], moe_w1[l // 2],
                          moe_w3[l // 2], moe_w2[l // 2], S)
    return x2.reshape(B, S, D)
```

```python
import functools
import math

import jax
import jax.numpy as jnp
from jax import lax
from jax.experimental import pallas as pl
from jax.experimental.pallas import tpu as pltpu

F32 = jnp.float32
BF16 = jnp.bfloat16
HIGHEST = lax.Precision.HIGHEST

HEAD_DIM = 64
ROPE_THETA = 10000.0
RNN_BLOCKS = 16
CONV_W = 4
LRU_C = 8.0
DIFF_HEADS = 4
MOBA_HEADS = 8
MOBA_BLOCK = 256
MOBA_TOPK = 3
N_BRANCH = 3
N_EXPERTS = 8
TOP_K = 2
EPS = 1e-6
NEG = -1e30

LANES = 128
VMEM_LIMIT = 56 * 1024 * 1024

ROW_TILE = 512

D_MODEL = 1024
ATTN_W = DIFF_HEADS * 2 * HEAD_DIM


def _starts(widths):
    out, col = {}, 0
    for name, width in widths:
        out[name] = col
        col += width
    return out


PROJ_COL = _starts((("x_rnn", D_MODEL), ("g_rnn", D_MODEL), ("dq", ATTN_W), ("dk", ATTN_W), ("dv", ATTN_W),
                    ("mq", ATTN_W), ("mk", ATTN_W), ("mv", ATTN_W), ("g_br", N_BRANCH * D_MODEL)))
QK_COL = _starts((("dq", ATTN_W), ("dk", ATTN_W), ("mq", ATTN_W), ("mk", ATTN_W)))


def _cparams(*sem):
    return pltpu.CompilerParams(dimension_semantics=sem, vmem_limit_bytes=VMEM_LIMIT)


def _modulated_norm(x, g, mod, base):
    ms = jnp.mean(x * x, axis=-1, keepdims=True)
    y = x * lax.rsqrt(ms + EPS) * g
    return y * (1.0 + mod[base + 1:base + 2]) + mod[base:base + 1]


def _dot_nt(a, b, **kw):
    return lax.dot_general(a, b, (((1,), (1,)), ((), ())), preferred_element_type=F32, **kw)


def _ada_kernel(c_ref, w_ref, b_ref, o_ref):
    o_ref[0] = jnp.dot(c_ref[...], w_ref[0], preferred_element_type=F32, precision=HIGHEST) + b_ref[0]


def _ada_mod(c, ada_w, ada_b):
    L, D, N = ada_w.shape
    B = c.shape[0]
    tn = 1536
    return pl.pallas_call(
        _ada_kernel,
        grid=(L, N // tn),
        in_specs=[pl.BlockSpec((B, D), lambda l, j: (0, 0)),
                  pl.BlockSpec((1, D, tn), lambda l, j: (l, 0, j)),
                  pl.BlockSpec((1, 1, tn), lambda l, j: (l, 0, j))],
        out_specs=pl.BlockSpec((1, B, tn), lambda l, j: (l, 0, j)),
        out_shape=jax.ShapeDtypeStruct((L, B, N), F32),
        compiler_params=_cparams("parallel", "parallel"),
        name="ada_mod",
    )(c, ada_w, ada_b.reshape(L, 1, N))


IN_LOADS = 8
IN_COL_CHUNK = 1024


def _inproj_kernel(x_ref, mod_ref, g_ref, w_ref, o_ref, wb):
    s = pl.program_id(0)

    @pl.when(s < IN_LOADS)
    def _():
        rows = w_ref.shape[1]
        wb[pl.ds(pl.multiple_of(s * rows, rows), rows), :] = w_ref[0].astype(BF16)

    @pl.when(s >= IN_LOADS)
    def _():
        h = _modulated_norm(x_ref[...], g_ref[...], mod_ref[0], 0).astype(BF16)
        for c in range(wb.shape[1] // IN_COL_CHUNK):
            cols = slice(c * IN_COL_CHUNK, (c + 1) * IN_COL_CHUNK)
            o_ref[:, cols] = jnp.dot(h, wb[:, cols], preferred_element_type=F32).astype(BF16)


def _in_proj(x2, mod, g, w_in, layer, seq):
    T, D = x2.shape
    N = w_in.shape[2]
    tm = ROW_TILE
    per_b = seq // tm
    assert D % (16 * IN_LOADS) == 0 and N % IN_COL_CHUNK == 0
    tile = lambda s: jnp.maximum(s - IN_LOADS, 0)
    return pl.pallas_call(
        _inproj_kernel,
        grid=(IN_LOADS + T // tm,),
        in_specs=[pl.BlockSpec((tm, D), lambda s: (tile(s), 0)),
                  pl.BlockSpec((1, 6, D), lambda s: (tile(s) // per_b, 0, 0)),
                  pl.BlockSpec((1, D), lambda s: (0, 0)),
                  pl.BlockSpec((1, D // IN_LOADS, N), lambda s: (layer, jnp.minimum(s, IN_LOADS - 1), 0))],
        out_specs=pl.BlockSpec((tm, N), lambda s: (tile(s), 0)),
        out_shape=jax.ShapeDtypeStruct((T, N), BF16),
        scratch_shapes=[pltpu.VMEM((D, N), BF16)],
        compiler_params=_cparams("arbitrary"),
        name="in_proj",
    )(x2, mod, g.reshape(1, D), w_in)


def _rope_kernel(pos_ref, inv_ref, sign_ref, cos_ref, sin_ref):
    ang = pos_ref[...] * inv_ref[...]
    cos_ref[...] = jnp.cos(ang)
    sin_ref[...] = jnp.sin(ang) * sign_ref[...]


def _rope_tables(positions):
    T = positions.size
    pos = positions.reshape(T, 1).astype(F32)
    inv = 1.0 / (ROPE_THETA ** (jnp.arange(0, HEAD_DIM, 2, dtype=F32) / HEAD_DIM))
    half = HEAD_DIM // 2
    inv128 = jnp.tile(inv, LANES // half).reshape(1, LANES)
    sign = jnp.tile(jnp.concatenate([-jnp.ones((half,), F32), jnp.ones((half,), F32)]),
                    LANES // HEAD_DIM).reshape(1, LANES)
    tm = 1024
    return pl.pallas_call(
        _rope_kernel,
        grid=(T // tm,),
        in_specs=[pl.BlockSpec((tm, 1), lambda i: (i, 0)),
                  pl.BlockSpec((1, LANES), lambda i: (0, 0)),
                  pl.BlockSpec((1, LANES), lambda i: (0, 0))],
        out_specs=[pl.BlockSpec((tm, LANES), lambda i: (i, 0))] * 2,
        out_shape=[jax.ShapeDtypeStruct((T, LANES), F32)] * 2,
        compiler_params=_cparams("parallel"),
        name="rope_tables",
    )(pos, inv128, sign)


def _qkprep_kernel(dq_ref, dk_ref, mq_ref, mk_ref, gain_ref, cos_ref, sin_ref, seg_ref, o_ref):
    cos = cos_ref[...]
    sin = sin_ref[...]
    seg = seg_ref[...]
    lane = lax.broadcasted_iota(jnp.int32, cos.shape, 1)
    first_half = (lane % HEAD_DIM) < (HEAD_DIM // 2)
    width = dq_ref.shape[1]
    for gi, ref in enumerate((dq_ref, dk_ref, mq_ref, mk_ref)):
        gain = gain_ref[gi:gi + 1, :]
        for cb in range(width // LANES):
            x = ref[:, cb * LANES:(cb + 1) * LANES].astype(F32)
            sq = x * x
            sq_hi = sq.astype(BF16)
            sq_lo = (sq - sq_hi.astype(F32)).astype(BF16)
            ms = (jnp.dot(sq_hi, seg, preferred_element_type=F32)
                  + jnp.dot(sq_lo, seg, preferred_element_type=F32))
            y = x * lax.rsqrt(ms + EPS) * gain
            swapped = jnp.where(first_half, pltpu.roll(y, LANES - HEAD_DIM // 2, 1),
                                pltpu.roll(y, HEAD_DIM // 2, 1))
            r = y * cos + swapped * sin
            if gi % 2 == 0:
                r = r * (math.log2(math.e) / math.sqrt(HEAD_DIM))
            col = gi * width + cb * LANES
            o_ref[:, col:col + LANES] = r.astype(BF16)


def _qk_prep(proj, gains, cos, sin):
    T = proj.shape[0]
    width = ATTN_W
    seg = jnp.kron(jnp.eye(LANES // HEAD_DIM, dtype=F32),
                   jnp.full((HEAD_DIM, HEAD_DIM), 1.0 / HEAD_DIM, F32)).astype(BF16)
    tm = ROW_TILE
    col_blocks = [PROJ_COL[name] // width for name in QK_COL]
    in_specs = [pl.BlockSpec((tm, width), functools.partial(lambda i, c: (i, c), c=c)) for c in col_blocks]
    in_specs += [pl.BlockSpec((4, LANES), lambda i: (0, 0)),
                 pl.BlockSpec((tm, LANES), lambda i: (i, 0)),
                 pl.BlockSpec((tm, LANES), lambda i: (i, 0)),
                 pl.BlockSpec((LANES, LANES), lambda i: (0, 0))]
    return pl.pallas_call(
        _qkprep_kernel,
        grid=(T // tm,),
        in_specs=in_specs,
        out_specs=pl.BlockSpec((tm, 4 * width), lambda i: (i, 0)),
        out_shape=jax.ShapeDtypeStruct((T, 4 * width), BF16),
        compiler_params=_cparams("parallel"),
        name="qk_prep",
    )(proj, proj, proj, proj, gains, cos, sin, seg)


def _gelu_tanh(x):
    return 0.5 * x * (1.0 + jnp.tanh(math.sqrt(2.0 / math.pi) * (x + 0.044715 * x * x * x)))


SUBLANES = 8
LRU_LANE_BLOCKS = 2


def _rglru_kernel(x_ref, g_ref, cw_ref, cb_ref, wa_ref, ba_ref, wx_ref, bx_ref, lam_ref, o_ref, *scratch):
    S = x_ref.shape[0]
    steps = S // SUBLANES
    pitch = steps + SUBLANES
    nblk = x_ref.shape[1] // LANES
    per_blk = len(scratch) // nblk
    sub = lax.broadcasted_iota(jnp.int32, (SUBLANES, LANES), 0)

    for blk in range(nblk):
        slab, fold_x, a_buf, u_buf, p_buf = scratch[blk * per_blk:(blk + 1) * per_blk]
        lanes = slice(blk * LANES, (blk + 1) * LANES)

        for s in range(SUBLANES):
            slab[s * pitch:s * pitch + steps, :] = x_ref[s * steps:(s + 1) * steps, lanes].astype(F32)

        def fold(step, carry):
            fold_x[pl.ds(pl.multiple_of(step * SUBLANES, SUBLANES), SUBLANES), :] = (
                slab[pl.ds(step, SUBLANES, stride=pitch), :])
            return carry

        lax.fori_loop(0, steps, fold, 0, unroll=8)
        xf = fold_x[...]

        def head(j):
            tail = xf[(steps - j) * SUBLANES:, :]
            vregs = [jnp.where(sub == 0, 0.0, pltpu.roll(tail[v * SUBLANES:(v + 1) * SUBLANES, :], 1, 0))
                     for v in range(j)]
            return jnp.concatenate(vregs + [xf[:(steps - j) * SUBLANES, :]], axis=0)

        xc = cb_ref[:, lanes] + cw_ref[CONV_W - 1:CONV_W, lanes] * xf
        for j in range(1, CONV_W):
            xc = xc + cw_ref[CONV_W - 1 - j:CONV_W - j, lanes] * head(j)

        xcb = xc.astype(BF16)
        r = jax.nn.sigmoid(jnp.dot(xcb, wa_ref[blk], preferred_element_type=F32) + ba_ref[:, lanes])
        gi = jax.nn.sigmoid(jnp.dot(xcb, wx_ref[blk], preferred_element_type=F32) + bx_ref[:, lanes])
        neg_lam = -lam_ref[:, lanes]
        softplus = jnp.maximum(neg_lam, 0.0) + jnp.log1p(jnp.exp(-jnp.abs(neg_lam)))
        a = jnp.exp2(r * ((-LRU_C * math.log2(math.e)) * softplus))
        gap = 1.0 - a * a
        mult = jnp.where(gap > 0.0, gap * lax.rsqrt(gap), 0.0)
        a_buf[...] = a
        u_buf[...] = mult * gi * xc

    def scan(step, carry):
        rows = pl.ds(pl.multiple_of(step * SUBLANES, SUBLANES), SUBLANES)
        out = []
        for blk in range(nblk):
            _, _, a_buf, u_buf, p_buf = scratch[blk * per_blk:(blk + 1) * per_blk]
            h, p = carry[blk]
            a = a_buf[rows, :]
            h = a * h + u_buf[rows, :]
            p = a * p
            u_buf[rows, :] = h
            p_buf[rows, :] = p
            out.append((h, p))
        return tuple(out)

    ones = jnp.ones((SUBLANES, LANES), F32)
    last = lax.fori_loop(0, steps, scan, ((jnp.zeros_like(ones), ones),) * nblk, unroll=8)

    for blk in range(nblk):
        slab, fold_x, a_buf, u_buf, p_buf = scratch[blk * per_blk:(blk + 1) * per_blk]
        lanes = slice(blk * LANES, (blk + 1) * LANES)
        h_end, p_end = last[blk]
        h_in = [jnp.zeros((1, LANES), F32)]
        for s in range(SUBLANES - 1):
            h_in.append(h_end[s:s + 1, :] + p_end[s:s + 1, :] * h_in[s])
        h_in = jnp.concatenate(h_in, axis=0)
        u_buf[...] = u_buf[...] + p_buf[...] * jnp.concatenate([h_in] * steps, axis=0)

        def unfold(step, carry):
            slab[pl.ds(step, SUBLANES, stride=pitch), :] = (
                u_buf[pl.ds(pl.multiple_of(step * SUBLANES, SUBLANES), SUBLANES), :])
            return carry

        lax.fori_loop(0, steps, unfold, 0, unroll=8)
        for s in range(SUBLANES):
            rows = slice(s * steps, (s + 1) * steps)
            h = slab[s * pitch:s * pitch + steps, :]
            o_ref[rows, lanes] = (_gelu_tanh(g_ref[rows, lanes].astype(F32)) * h).astype(BF16)


def _rg_lru_branch(proj, seq, conv_w, conv_b, wa2, ba, wx2, bx, lam):
    T = proj.shape[0]
    C = conv_w.shape[1]
    B = T // seq
    nblk = LRU_LANE_BLOCKS
    W = nblk * LANES
    assert PROJ_COL["x_rnn"] == 0 and PROJ_COL["g_rnn"] % W == 0
    g_off = PROJ_COL["g_rnn"] // W
    steps = seq // SUBLANES
    assert seq % (SUBLANES * SUBLANES) == 0 and C % W == 0
    vec = lambda v: v.reshape(1, C)
    cols = lambda rows: pl.BlockSpec((rows, W), lambda b, n: (0, n))
    per_blk = [pltpu.VMEM((SUBLANES * (steps + SUBLANES), LANES), F32)] + [pltpu.VMEM((seq, LANES), F32)] * 4
    return pl.pallas_call(
        _rglru_kernel,
        grid=(B, C // W),
        in_specs=[pl.BlockSpec((seq, W), lambda b, n: (b, n)),
                  pl.BlockSpec((seq, W), lambda b, n: (b, g_off + n)),
                  cols(CONV_W), cols(1),
                  pl.BlockSpec((nblk, LANES, LANES), lambda b, n: (n, 0, 0)), cols(1),
                  pl.BlockSpec((nblk, LANES, LANES), lambda b, n: (n, 0, 0)), cols(1), cols(1)],
        out_specs=pl.BlockSpec((seq, W), lambda b, n: (b, n)),
        out_shape=jax.ShapeDtypeStruct((T, C), BF16),
        scratch_shapes=per_blk * nblk,
        compiler_params=_cparams("parallel", "parallel"),
        name="rg_lru",
    )(proj, proj, conv_w, vec(conv_b), wa2, vec(ba), wx2, vec(bx), vec(lam))


def _pair_block_diag(w):
    n, d, _ = w.shape
    z = jnp.zeros((n // 2, 2, d, 2, d), w.dtype)
    z = z.at[:, 0, :, 0, :].set(w[0::2]).at[:, 1, :, 1, :].set(w[1::2])
    return z.reshape(n // 2, 2 * d, 2 * d).astype(BF16)


ATTN_TILE = 256


def _lane_tile(x, n):
    return jnp.concatenate([x] * n, axis=1)


def _softmax_update(s, state, v_aug):
    row_max = jnp.max(s, axis=-1, keepdims=True)
    if state is None:
        m_new = jnp.broadcast_to(row_max, (s.shape[0], LANES))
        p = jnp.exp2(s - _lane_tile(m_new, s.shape[1] // LANES)).astype(BF16)
        return m_new, jnp.dot(p, v_aug, preferred_element_type=F32)
    m_old, acc = state
    m_new = jnp.maximum(m_old, row_max)
    p = jnp.exp2(s - _lane_tile(m_new, s.shape[1] // LANES)).astype(BF16)
    alpha = _lane_tile(jnp.exp2(m_old - m_new), acc.shape[1] // LANES)
    return m_new, alpha * acc + jnp.dot(p, v_aug, preferred_element_type=F32)


def _softmax_finish(state):
    _, acc = state
    return acc[:, :LANES] / acc[:, LANES:]


def _state_rows(state, start, stop):
    m, acc = state
    return m[start:stop], acc[start:stop]


def _causal_mask(t):
    return lax.broadcasted_iota(jnp.int32, (t, t), 1) <= lax.broadcasted_iota(jnp.int32, (t, t), 0)


def _mask_diagonal(s, causal):
    t = causal.shape[0]
    if s.shape[0] == t:
        return jnp.where(causal, s, NEG)
    return jnp.concatenate([jnp.where(causal, s[:t], NEG), s[t:]], axis=0)


def _with_ones(v):
    return jnp.concatenate([v, jnp.ones_like(v)], axis=1)


def _diffattn_kernel(q_ref, k_ref, v_ref, lam_ref, g_ref, o_ref, *, lam_init):
    t = ATTN_TILE
    S = q_ref.shape[0]
    q = q_ref[...]
    lane = lax.broadcasted_iota(jnp.int32, q.shape, 1)
    zero = jnp.zeros_like(q)
    q1 = jnp.where(lane < HEAD_DIM, q, zero)
    q2 = jnp.where(lane < HEAD_DIM, zero, q)
    causal = _causal_mask(t)
    lp = lam_ref[...]
    lam = (jnp.exp(jnp.sum(lp[0:1] * lp[1:2], axis=-1, keepdims=True))
           - jnp.exp(jnp.sum(lp[2:3] * lp[3:4], axis=-1, keepdims=True)) + lam_init)
    st1 = st2 = None
    for j in range(S // t):
        k = k_ref[j * t:(j + 1) * t, :]
        v_aug = _with_ones(v_ref[j * t:(j + 1) * t, :])
        st1 = _softmax_update(_mask_diagonal(_dot_nt(q1[j * t:, :], k), causal), st1, v_aug)
        st2 = _softmax_update(_mask_diagonal(_dot_nt(q2[j * t:, :], k), causal), st2, v_aug)
        o = _softmax_finish(_state_rows(st1, 0, t)) - lam * _softmax_finish(_state_rows(st2, 0, t))
        ms = jnp.mean(o * o, axis=-1, keepdims=True)
        o_ref[j * t:(j + 1) * t, :] = (o * lax.rsqrt(ms + EPS) * g_ref[...] * (1.0 - lam_init)).astype(BF16)
        if (j + 1) * t < S:
            st1 = _state_rows(st1, t, None)
            st2 = _state_rows(st2, t, None)


def _diff_attention(qk, proj, seq, lam_params, subln_g, lam_init):
    T = qk.shape[0]
    B = T // seq
    H = DIFF_HEADS
    return pl.pallas_call(
        functools.partial(_diffattn_kernel, lam_init=lam_init),
        grid=(B, H),
        in_specs=[pl.BlockSpec((seq, LANES), lambda b, h: (b, h)),
                  pl.BlockSpec((seq, LANES), lambda b, h: (b, QK_COL["dk"] // LANES + h)),
                  pl.BlockSpec((seq, LANES), lambda b, h: (b, PROJ_COL["dv"] // LANES + h)),
                  pl.BlockSpec((4, HEAD_DIM), lambda b, h: (0, 0)),
                  pl.BlockSpec((1, LANES), lambda b, h: (0, 0))],
        out_specs=pl.BlockSpec((seq, LANES), lambda b, h: (b, h)),
        out_shape=jax.ShapeDtypeStruct((T, H * LANES), BF16),
        compiler_params=_cparams("parallel", "parallel"),
        name="diff_attn",
    )(qk, qk, proj, lam_params, subln_g.reshape(1, LANES))


def _moba_kernel(q_ref, k_ref, v_ref, o_ref, kmean):
    t = ATTN_TILE
    S = q_ref.shape[0]
    nb = S // MOBA_BLOCK
    nb8 = -(-nb // 8) * 8
    lane_row = lax.broadcasted_iota(jnp.int32, (1, LANES), 1)

    kmean[...] = jnp.zeros_like(kmean)
    for blk in range(nb):
        mean = jnp.mean(k_ref[blk * MOBA_BLOCK:(blk + 1) * MOBA_BLOCK, :].astype(F32), axis=0, keepdims=True)
        kmean[blk:blk + 1, :] = jnp.where(lane_row < HEAD_DIM, mean, 0.0)
        kmean[HEAD_DIM + blk:HEAD_DIM + blk + 1, :] = jnp.where(lane_row < HEAD_DIM, 0.0, mean)

    q = q_ref[...]
    lane = lax.broadcasted_iota(jnp.int32, q.shape, 1)
    zero = jnp.zeros_like(q)
    q_a = jnp.where(lane < HEAD_DIM, q, zero)
    q_b = jnp.where(lane < HEAD_DIM, zero, q)

    km = kmean[...]
    km_hi = km.astype(BF16)
    km_mid = (km - km_hi.astype(F32)).astype(BF16)
    km_lo = (km - km_hi.astype(F32) - km_mid.astype(F32)).astype(BF16)
    gate = _dot_nt(km_hi, q) + _dot_nt(km_mid, q) + _dot_nt(km_lo, q)
    gate_blk = lax.broadcasted_iota(jnp.int32, (nb8, S), 0)
    own = lax.broadcasted_iota(jnp.int32, (nb8, S), 1) // MOBA_BLOCK
    past = gate_blk < own
    unused_rows = jnp.full((HEAD_DIM - nb8, S), NEG, F32)
    bias_rows = []
    for head in range(2):
        g = jnp.where(past, gate[head * HEAD_DIM:head * HEAD_DIM + nb8, :], NEG)
        rank = jnp.zeros(g.shape, jnp.int32)
        for jb in range(nb - 1):
            other = g[jb:jb + 1, :]
            beats = (other > g) | ((other == g) & (jb < gate_blk))
            rank = rank + beats.astype(jnp.int32)
        allowed = (past & (rank < MOBA_TOPK)) | (gate_blk == own)
        bias_rows += [jnp.where(allowed, 0.0, NEG), unused_rows]
    bias_a = jnp.concatenate(bias_rows, axis=0).T
    bias_b = pltpu.roll(bias_a, HEAD_DIM, 1)
    qa_aug = jnp.concatenate([q_a, bias_a.astype(BF16)], axis=1)
    qb_aug = jnp.concatenate([q_b, bias_b.astype(BF16)], axis=1)

    causal = _causal_mask(t)
    key_lane = lax.broadcasted_iota(jnp.int32, (t, LANES), 1)
    key_row = lax.broadcasted_iota(jnp.int32, (t, LANES), 0)
    st_a = st_b = None
    for j in range(S // t):
        key_blk = (j * t + key_row) // MOBA_BLOCK
        k_aug = jnp.concatenate([k_ref[j * t:(j + 1) * t, :], (key_lane == key_blk).astype(BF16)], axis=1)
        v_aug = _with_ones(v_ref[j * t:(j + 1) * t, :])
        st_a = _softmax_update(_mask_diagonal(_dot_nt(qa_aug[j * t:, :], k_aug), causal), st_a, v_aug)
        st_b = _softmax_update(_mask_diagonal(_dot_nt(qb_aug[j * t:, :], k_aug), causal), st_b, v_aug)
        o = jnp.where(key_lane < HEAD_DIM, _softmax_finish(_state_rows(st_a, 0, t)),
                      _softmax_finish(_state_rows(st_b, 0, t)))
        o_ref[j * t:(j + 1) * t, :] = o.astype(BF16)
        if (j + 1) * t < S:
            st_a = _state_rows(st_a, t, None)
            st_b = _state_rows(st_b, t, None)


def _moba_attention(qk, proj, seq):
    T = qk.shape[0]
    B = T // seq
    HP = MOBA_HEADS // 2
    return pl.pallas_call(
        _moba_kernel,
        grid=(B, HP),
        in_specs=[pl.BlockSpec((seq, LANES), lambda b, h: (b, QK_COL["mq"] // LANES + h)),
                  pl.BlockSpec((seq, LANES), lambda b, h: (b, QK_COL["mk"] // LANES + h)),
                  pl.BlockSpec((seq, LANES), lambda b, h: (b, PROJ_COL["mv"] // LANES + h))],
        out_specs=pl.BlockSpec((seq, LANES), lambda b, h: (b, h)),
        out_shape=jax.ShapeDtypeStruct((T, HP * LANES), BF16),
        scratch_shapes=[pltpu.VMEM((LANES, LANES), F32)],
        compiler_params=_cparams("parallel", "parallel"),
        name="moba_attn",
    )(qk, qk, proj)


def _merge_kernel(x_ref, ya_ref, yb_ref, yc_ref, ga_ref, gbb_ref, gc_ref, gb_ref, mod_ref,
                  wa_ref, wb_ref, wc_ref, wo_ref, o_ref):
    merged = None
    branches = ((ya_ref, wa_ref, ga_ref), (yb_ref, wb_ref, gbb_ref), (yc_ref, wc_ref, gc_ref))
    for n, (y_ref, w_ref, gl_ref) in enumerate(branches):
        gate = jax.nn.sigmoid(gl_ref[...].astype(F32) + gb_ref[n:n + 1, :])
        term = gate * jnp.dot(y_ref[...], w_ref[...], preferred_element_type=F32)
        merged = term if merged is None else merged + term
    mix = jnp.dot(merged.astype(BF16), wo_ref[...], preferred_element_type=F32)
    o_ref[...] = x_ref[...] + mod_ref[0][2:3] * mix


def _merge_out(x2, ya, yb, yc, proj, gate_b, mod, wa, wb, wc, wo, seq):
    T, D = x2.shape
    tm = ROW_TILE
    per_b = seq // tm
    assert PROJ_COL["g_br"] % D == 0
    gbr_block = PROJ_COL["g_br"] // D
    wspec = lambda w: pl.BlockSpec(w.shape, lambda i: (0, 0))
    return pl.pallas_call(
        _merge_kernel,
        grid=(T // tm,),
        in_specs=[pl.BlockSpec((tm, D), lambda i: (i, 0)),
                  pl.BlockSpec((tm, ya.shape[1]), lambda i: (i, 0)),
                  pl.BlockSpec((tm, yb.shape[1]), lambda i: (i, 0)),
                  pl.BlockSpec((tm, yc.shape[1]), lambda i: (i, 0)),
                  pl.BlockSpec((tm, D), lambda i: (i, gbr_block)),
                  pl.BlockSpec((tm, D), lambda i: (i, gbr_block + 1)),
                  pl.BlockSpec((tm, D), lambda i: (i, gbr_block + 2)),
                  pl.BlockSpec((N_BRANCH, D), lambda i: (0, 0)),
                  pl.BlockSpec((1, 6, D), lambda i: (i // per_b, 0, 0)),
                  wspec(wa), wspec(wb), wspec(wc), wspec(wo)],
        out_specs=pl.BlockSpec((tm, D), lambda i: (i, 0)),
        out_shape=jax.ShapeDtypeStruct((T, D), F32),
        compiler_params=_cparams("parallel"),
        name="merge_out",
    )(x2, ya, yb, yc, proj, proj, proj, gate_b, mod, wa, wb, wc, wo)


FF_CHUNK = 256
FF_LOADS = 8


def _stash_weight_slab(k, w1_ref, w3_ref, w2_ref, wb1, wb3, wb2):
    for w_ref, wb in ((w1_ref, wb1), (w3_ref, wb3), (w2_ref, wb2)):
        rows, cols = w_ref.shape[-2:]
        start = pl.multiple_of(k * rows, rows)
        wb[pl.ds(start, rows), :] = w_ref[...].reshape(rows, cols).astype(BF16)


def _swiglu_resident(h, wb1, wb3, wb2):
    acc = None
    for f in range(wb2.shape[0] // FF_CHUNK):
        cols = slice(f * FF_CHUNK, (f + 1) * FF_CHUNK)
        g = jnp.dot(h, wb1[:, cols], preferred_element_type=F32)
        u = jnp.dot(h, wb3[:, cols], preferred_element_type=F32)
        act = (g * jax.nn.sigmoid(g) * u).astype(BF16)
        part = jnp.dot(act, wb2[cols, :], preferred_element_type=F32)
        acc = part if acc is None else acc + part
    return acc


def _ffn_weight_specs(D, F, lead, slab_index):
    assert D % (16 * FF_LOADS) == 0 and F % (16 * FF_LOADS) == 0 and F % FF_CHUNK == 0
    index = lambda *a: slab_index(*a) + (0,)
    return [pl.BlockSpec(lead + (D // FF_LOADS, F), index), pl.BlockSpec(lead + (D // FF_LOADS, F), index),
            pl.BlockSpec(lead + (F // FF_LOADS, D), index)]


def _ffn_kernel(x_ref, mod_ref, g_ref, w1_ref, w3_ref, w2_ref, o_ref, wb1, wb3, wb2):
    s = pl.program_id(0)

    @pl.when(s < FF_LOADS)
    def _():
        _stash_weight_slab(s, w1_ref, w3_ref, w2_ref, wb1, wb3, wb2)

    @pl.when(s >= FF_LOADS)
    def _():
        h = _modulated_norm(x_ref[...], g_ref[...], mod_ref[0], 3).astype(BF16)
        o_ref[...] = x_ref[...] + mod_ref[0][5:6] * _swiglu_resident(h, wb1, wb3, wb2)


def _dense_ffn(x2, mod, g, w1, w3, w2, seq):
    T, D = x2.shape
    F = w1.shape[1]
    tm = ROW_TILE
    per_b = seq // tm
    tile = lambda s: jnp.maximum(s - FF_LOADS, 0)
    return pl.pallas_call(
        _ffn_kernel,
        grid=(FF_LOADS + T // tm,),
        in_specs=[pl.BlockSpec((tm, D), lambda s: (tile(s), 0)),
                  pl.BlockSpec((1, 6, D), lambda s: (tile(s) // per_b, 0, 0)),
                  pl.BlockSpec((1, D), lambda s: (0, 0))]
        + _ffn_weight_specs(D, F, (), lambda s: (jnp.minimum(s, FF_LOADS - 1),)),
        out_specs=pl.BlockSpec((tm, D), lambda s: (tile(s), 0)),
        out_shape=jax.ShapeDtypeStruct((T, D), F32),
        scratch_shapes=[pltpu.VMEM((D, F), BF16), pltpu.VMEM((D, F), BF16), pltpu.VMEM((F, D), BF16)],
        compiler_params=_cparams("arbitrary"),
        name="dense_ffn",
    )(x2, mod, g.reshape(1, D), w1, w3, w2)


def _router_kernel(x_ref, mod_ref, g_ref, rw_ref, rb_ref, h_ref, comb_ref, sel_ref):
    h = _modulated_norm(x_ref[...], g_ref[...], mod_ref[0], 3)
    h_ref[...] = h
    logits = _dot_nt(rw_ref[...], h, precision=HIGHEST) + rb_ref[...]
    E = logits.shape[0]
    eid = lax.broadcasted_iota(jnp.int32, logits.shape, 0)
    v1 = jnp.max(logits, axis=0, keepdims=True)
    i1 = jnp.min(jnp.where(logits == v1, eid, E), axis=0, keepdims=True)
    rest = jnp.where(eid == i1, -jnp.inf, logits)
    v2 = jnp.max(rest, axis=0, keepdims=True)
    i2 = jnp.min(jnp.where(rest == v2, eid, E), axis=0, keepdims=True)
    e2 = jnp.exp(v2 - v1)
    w1 = 1.0 / (1.0 + e2)
    w2 = e2 / (1.0 + e2)
    comb_ref[...] = jnp.where(eid == i1, w1, 0.0) + jnp.where(eid == i2, w2, 0.0)
    sel_ref[...] = ((eid == i1) | (eid == i2)).astype(jnp.int32)


def _router(x2, mod, g, router_w, router_b, seq):
    T, D = x2.shape
    E = router_w.shape[1]
    tm = ROW_TILE
    per_b = seq // tm
    return pl.pallas_call(
        _router_kernel,
        grid=(T // tm,),
        in_specs=[pl.BlockSpec((tm, D), lambda i: (i, 0)),
                  pl.BlockSpec((1, 6, D), lambda i: (i // per_b, 0, 0)),
                  pl.BlockSpec((1, D), lambda i: (0, 0)),
                  pl.BlockSpec((E, D), lambda i: (0, 0)),
                  pl.BlockSpec((E, 1), lambda i: (0, 0))],
        out_specs=[pl.BlockSpec((tm, D), lambda i: (i, 0)),
                   pl.BlockSpec((E, tm), lambda i: (0, i)),
                   pl.BlockSpec((E, tm), lambda i: (0, i))],
        out_shape=[jax.ShapeDtypeStruct((T, D), F32),
                   jax.ShapeDtypeStruct((E, T), F32),
                   jax.ShapeDtypeStruct((E, T), jnp.int32)],
        compiler_params=_cparams("parallel"),
        name="moe_router",
    )(x2, mod, g.reshape(1, D), router_w.T, router_b.reshape(E, 1))


STEP_LOAD, STEP_TILE, STEP_BOTH, STEP_ZERO, STEP_NOOP = 0, 1, 2, 3, 4
MOE_SLOTS = 2


def _moe_kernel(kind_ref, slot_ref, slab_ref, we_ref, wk_ref, t_ref, xs_ref, w1_ref, w3_ref, w2_ref, o_ref,
                wb1, wb3, wb2):
    s = pl.program_id(0)
    kind = kind_ref[s]
    slot = slot_ref[s]

    @pl.when(kind == STEP_LOAD)
    def _():
        _stash_weight_slab(slab_ref[s], w1_ref, w3_ref, w2_ref, wb1.at[slot], wb3.at[slot], wb2.at[slot])

    @pl.when(kind == STEP_TILE)
    def _():
        o_ref[...] = _swiglu_resident(xs_ref[...].astype(BF16), wb1.at[slot], wb3.at[slot], wb2.at[slot])

    for use in range(MOE_SLOTS):
        fill = (use + 1) % MOE_SLOTS

        @pl.when((kind == STEP_BOTH) & (slot == use))
        def _():
            _stash_weight_slab(slab_ref[s], w1_ref, w3_ref, w2_ref, wb1.at[fill], wb3.at[fill], wb2.at[fill])
            o_ref[...] = _swiglu_resident(xs_ref[...].astype(BF16), wb1.at[use], wb3.at[use], wb2.at[use])

    @pl.when(kind == STEP_ZERO)
    def _():
        o_ref[...] = jnp.zeros_like(o_ref)


def _moe_schedule(first_tile, tiles, t0, nt):
    E = tiles.shape[0]
    L = FF_LOADS
    i32 = jnp.int32
    ids = jnp.arange(E, dtype=i32)
    lo = jnp.clip(first_tile - t0, 0, nt)
    cnt = jnp.clip(first_tile + tiles - t0, 0, nt) - lo
    active = cnt > 0
    n_act = jnp.sum(active.astype(i32))
    order = jnp.cumsum(active.astype(i32)) - 1
    pick = ((order[None, :] == ids[:, None]) & active[None, :]).astype(i32)
    act_e = jnp.sum(pick * ids[None, :], axis=1)
    act_cnt = jnp.sum(pick * cnt[None, :], axis=1)
    act_lo = jnp.sum(pick * lo[None, :], axis=1)
    has_next = ids + 1 < n_act
    blk_len = jnp.where(ids < n_act, jnp.maximum(act_cnt, jnp.where(has_next, L, 0)), 0)
    blk_start = L + jnp.cumsum(blk_len) - blk_len
    n_sched = jnp.where(n_act > 0, L + jnp.sum(blk_len), 0)
    n_used = jnp.sum(cnt)

    s = jnp.arange(L * E + nt, dtype=i32)
    k = jnp.clip(jnp.sum(((blk_start[None, :] <= s[:, None]) & (ids[None, :] < n_act)).astype(i32), axis=1) - 1,
                 0, E - 1)
    p = s - blk_start[k]
    in_block = (s >= L) & (s < n_sched)
    first_loads = (s < L) & (n_act > 0)
    tile_step = in_block & (p < act_cnt[k])
    load_step = first_loads | (in_block & has_next[k] & (p < L))
    load_k = jnp.where(first_loads, 0, k + 1)
    zero_step = (s >= n_sched) & (s - n_sched < nt - n_used)

    kind = jnp.where(tile_step & load_step, STEP_BOTH,
                     jnp.where(tile_step, STEP_TILE,
                               jnp.where(load_step, STEP_LOAD, jnp.where(zero_step, STEP_ZERO, STEP_NOOP))))
    slot = jnp.where(tile_step, k, load_k) % MOE_SLOTS
    slab = jnp.clip(jnp.where(first_loads, s, p), 0, L - 1)
    last_load = lax.cummax(jnp.where(load_step, s, 0), axis=0)
    we = act_e[jnp.clip(load_k, 0, E - 1)][last_load]
    wk = slab[last_load]
    own = jnp.where(tile_step, act_lo[k] + p, jnp.where(zero_step, n_used + s - n_sched, nt - 1))
    tile = jnp.clip(lax.cummin(own, axis=0, reverse=True), 0, nt - 1)
    return tuple(a.astype(i32) for a in (kind, slot, slab, we, wk, tile))


def _moe_grouped(xs, first_tile, tiles, w1, w3, w2, tm):
    N, D = xs.shape
    nt = N // tm
    E, _, F = w1.shape
    table = _moe_schedule(first_tile, tiles, 0, nt)
    grid_spec = pltpu.PrefetchScalarGridSpec(
        num_scalar_prefetch=len(table),
        grid=(FF_LOADS * E + nt,),
        in_specs=[pl.BlockSpec((tm, D), lambda s, *pre: (pre[5][s], 0))]
        + _ffn_weight_specs(D, F, (1,), lambda s, *pre: (pre[3][s], pre[4][s])),
        out_specs=pl.BlockSpec((tm, D), lambda s, *pre: (pre[5][s], 0)),
        scratch_shapes=[pltpu.VMEM((MOE_SLOTS, D, F), BF16), pltpu.VMEM((MOE_SLOTS, D, F), BF16),
                        pltpu.VMEM((MOE_SLOTS, F, D), BF16)],
    )
    return pl.pallas_call(
        _moe_kernel,
        grid_spec=grid_spec,
        out_shape=jax.ShapeDtypeStruct((N, D), F32),
        compiler_params=_cparams("arbitrary"),
        name="moe_experts",
    )(*table, xs, w1, w3, w2)


def _combine_kernel(x_ref, y1_ref, y2_ref, w_ref, mod_ref, o_ref):
    w = w_ref[...]
    y = w[:, 0:1] * y1_ref[...] + w[:, 1:2] * y2_ref[...]
    o_ref[...] = x_ref[...] + mod_ref[0][5:6] * y


def _moe_combine(x2, y1, y2, w12, mod, seq):
    T, D = x2.shape
    tm = ROW_TILE
    per_b = seq // tm
    return pl.pallas_call(
        _combine_kernel,
        grid=(T // tm,),
        in_specs=[pl.BlockSpec((tm, D), lambda i: (i, 0)),
                  pl.BlockSpec((tm, D), lambda i: (i, 0)),
                  pl.BlockSpec((tm, D), lambda i: (i, 0)),
                  pl.BlockSpec((tm, 2), lambda i: (i, 0)),
                  pl.BlockSpec((1, 6, D), lambda i: (i // per_b, 0, 0))],
        out_specs=pl.BlockSpec((tm, D), lambda i: (i, 0)),
        out_shape=jax.ShapeDtypeStruct((T, D), F32),
        compiler_params=_cparams("parallel"),
        name="moe_combine",
    )(x2, y1, y2, w12, mod)


def _moe_ffn(x2, mod, g, router_w, router_b, w1, w3, w2, seq):
    T, D = x2.shape
    E = router_w.shape[1]
    tm = ROW_TILE
    h, comb, sel = _router(x2, mod, g, router_w, router_b, seq)

    counts = jnp.sum(sel, axis=1)
    padded = ((counts + tm - 1) // tm) * tm
    group_end = jnp.cumsum(padded)
    group_start = group_end - padded
    rank = jnp.cumsum(sel, axis=1) - sel
    dest = group_start[:, None] + rank
    n_rows = TOP_K * T + E * tm
    n_tiles = n_rows // tm
    tile_start = jnp.arange(n_tiles, dtype=jnp.int32) * tm
    tile_expert = jnp.sum((group_end[None, :] <= tile_start[:, None]).astype(jnp.int32), axis=1)
    tile_expert = jnp.minimum(tile_expert, jnp.max(jnp.where(counts > 0, jnp.arange(E), 0))).astype(jnp.int32)

    eid = jnp.arange(E, dtype=jnp.int32)[:, None]
    e_lo = jnp.min(jnp.where(sel > 0, eid, E), axis=0)
    e_hi = jnp.max(jnp.where(sel > 0, eid, -1), axis=0)
    pick = lambda a, e: jnp.sum(jnp.where(eid == e[None, :], a, 0), axis=0)
    w12 = jnp.stack([pick(comb, e_lo), pick(comb, e_hi)], axis=1)

    tok = jnp.arange(T, dtype=jnp.int32)
    tok_sorted = jnp.sort(jnp.concatenate([e_lo * T + tok, e_hi * T + tok])) % T
    row = jnp.arange(n_rows, dtype=jnp.int32)
    row_expert = jnp.repeat(tile_expert, tm)
    first_sorted = (jnp.cumsum(counts) - counts)[row_expert]
    src = tok_sorted[jnp.clip(first_sorted + row - group_start[row_expert], 0, TOP_K * T - 1)]

    take_rows = lambda a, idx: a.at[idx].get(mode="promise_in_bounds")
    ys = _moe_grouped(take_rows(h, src), group_start // tm, padded // tm, w1, w3, w2, tm)
    y1 = take_rows(ys, pick(dest, e_lo))
    y2 = take_rows(ys, pick(dest, e_hi))
    return _moe_combine(x2, y1, y2, w12, mod, seq)


def kernel(x, c, positions, ada_w, ada_b, norm1_g, norm2_g, w_in, gate_b, conv_w, conv_b, lru_wa, lru_ba,
           lru_wx, lru_bx, lru_lambda, diff_qn, diff_kn, diff_lq1, diff_lk1, diff_lq2, diff_lk2, diff_subln,
           moba_qn, moba_kn, w_br_a, w_br_b, w_br_c, w_out, ffn_w1, ffn_w3, ffn_w2, router_w, router_b,
           moe_w1, moe_w3, moe_w2):
    B, S, D = x.shape
    L = ada_w.shape[0]
    T = B * S
    assert D == D_MODEL and w_in.shape[2] == PROJ_COL["g_br"] + N_BRANCH * D
    assert S % MOBA_BLOCK == 0 and S // MOBA_BLOCK <= 56 and S % ROW_TILE == 0
    x2 = x.reshape(T, D)
    mod_all = _ada_mod(c, ada_w, ada_b).reshape(L, B, 6, D)
    cos, sin = _rope_tables(positions)
    bf = lambda w: w.astype(BF16)
    tile2 = lambda v: jnp.tile(v, LANES // HEAD_DIM)

    for l in range(L):
        mod = mod_all[l]
        lam_init = 0.8 - 0.6 * math.exp(-0.3 * l)
        proj = _in_proj(x2, mod, norm1_g[l], w_in, l, S)
        gains = jnp.stack([tile2(diff_qn[l]), tile2(diff_kn[l]), tile2(moba_qn[l]), tile2(moba_kn[l])])
        qk = _qk_prep(proj, gains, cos, sin)
        y_a = _rg_lru_branch(proj, S, conv_w[l], conv_b[l], _pair_block_diag(lru_wa[l]), lru_ba[l],
                             _pair_block_diag(lru_wx[l]), lru_bx[l], lru_lambda[l])
        lam_params = jnp.stack([diff_lq1[l], diff_lk1[l], diff_lq2[l], diff_lk2[l]])
        y_b = _diff_attention(qk, proj, S, lam_params, diff_subln[l], lam_init)
        y_c = _moba_attention(qk, proj, S)
        x2 = _merge_out(x2, y_a, y_b, y_c, proj, gate_b[l], mod, bf(w_br_a[l]), bf(w_br_b[l]),
                        bf(w_br_c[l]), bf(w_out[l]), S)
        if l % 2 == 0:
            x2 = _dense_ffn(x2, mod, norm2_g[l], ffn_w1[l // 2], ffn_w3[l // 2], ffn_w2[l // 2], S)
        else:
            x2 = _moe_ffn(x2, mod, norm2_g[l], router_w[l // 2], router_b[l // 2], moe_w1[l // 2],
                          moe_w3[l // 2], moe_w2[l // 2], S)
    return x2.reshape(B, S, D)
```

```python
import functools
import math

import jax
import jax.numpy as jnp
from jax import lax
from jax.experimental import pallas as pl
from jax.experimental.pallas import tpu as pltpu

F32 = jnp.float32
BF16 = jnp.bfloat16
HIGHEST = lax.Precision.HIGHEST

HEAD_DIM = 64
ROPE_THETA = 10000.0
RNN_BLOCKS = 16
CONV_W = 4
LRU_C = 8.0
DIFF_HEADS = 4
MOBA_HEADS = 8
MOBA_BLOCK = 256
MOBA_TOPK = 3
N_BRANCH = 3
N_EXPERTS = 8
TOP_K = 2
EPS = 1e-6
NEG = -1e30

LANES = 128
VMEM_LIMIT = 56 * 1024 * 1024

ROW_TILE = 512

D_MODEL = 1024
ATTN_W = DIFF_HEADS * 2 * HEAD_DIM


def _starts(widths):
    out, col = {}, 0
    for name, width in widths:
        out[name] = col
        col += width
    return out


PROJ_COL = _starts((("x_rnn", D_MODEL), ("g_rnn", D_MODEL), ("dq", ATTN_W), ("dk", ATTN_W), ("dv", ATTN_W),
                    ("mq", ATTN_W), ("mk", ATTN_W), ("mv", ATTN_W), ("g_br", N_BRANCH * D_MODEL)))
QK_COL = _starts((("dq", ATTN_W), ("dk", ATTN_W), ("mq", ATTN_W), ("mk", ATTN_W)))


def _cparams(*sem):
    return pltpu.CompilerParams(dimension_semantics=sem, vmem_limit_bytes=VMEM_LIMIT)


def _modulated_norm(x, g, mod, base):
    ms = jnp.mean(x * x, axis=-1, keepdims=True)
    y = x * lax.rsqrt(ms + EPS) * g
    return y * (1.0 + mod[base + 1:base + 2]) + mod[base:base + 1]


def _dot_nt(a, b, **kw):
    return lax.dot_general(a, b, (((1,), (1,)), ((), ())), preferred_element_type=F32, **kw)


def _ada_kernel(c_ref, w_ref, b_ref, o_ref):
    o_ref[0] = jnp.dot(c_ref[...], w_ref[0], preferred_element_type=F32, precision=HIGHEST) + b_ref[0]


def _ada_mod(c, ada_w, ada_b):
    L, D, N = ada_w.shape
    B = c.shape[0]
    tn = N // 2
    return pl.pallas_call(
        _ada_kernel,
        grid=(L, N // tn),
        in_specs=[pl.BlockSpec((B, D), lambda l, j: (0, 0)),
                  pl.BlockSpec((1, D, tn), lambda l, j: (l, 0, j)),
                  pl.BlockSpec((1, 1, tn), lambda l, j: (l, 0, j))],
        out_specs=pl.BlockSpec((1, B, tn), lambda l, j: (l, 0, j)),
        out_shape=jax.ShapeDtypeStruct((L, B, N), F32),
        compiler_params=_cparams("parallel", "parallel"),
        name="ada_mod",
    )(c, ada_w, ada_b.reshape(L, 1, N))


IN_LOADS = 8
IN_COL_CHUNK = 1024


def _inproj_kernel(x_ref, mod_ref, g_ref, w_ref, o_ref, wb):
    s = pl.program_id(0)

    @pl.when(s < IN_LOADS)
    def _():
        rows = w_ref.shape[1]
        wb[pl.ds(pl.multiple_of(s * rows, rows), rows), :] = w_ref[0].astype(BF16)

    @pl.when(s >= IN_LOADS)
    def _():
        h = _modulated_norm(x_ref[...], g_ref[...], mod_ref[0], 0).astype(BF16)
        for c in range(wb.shape[1] // IN_COL_CHUNK):
            cols = slice(c * IN_COL_CHUNK, (c + 1) * IN_COL_CHUNK)
            o_ref[:, cols] = jnp.dot(h, wb[:, cols], preferred_element_type=F32).astype(BF16)


def _in_proj(x2, mod, g, w_in, layer, seq):
    T, D = x2.shape
    N = w_in.shape[2]
    tm = ROW_TILE
    per_b = seq // tm
    assert D % (16 * IN_LOADS) == 0 and N % IN_COL_CHUNK == 0
    tile = lambda s: jnp.maximum(s - IN_LOADS, 0)
    return pl.pallas_call(
        _inproj_kernel,
        grid=(IN_LOADS + T // tm,),
        in_specs=[pl.BlockSpec((tm, D), lambda s: (tile(s), 0)),
                  pl.BlockSpec((1, 6, D), lambda s: (tile(s) // per_b, 0, 0)),
                  pl.BlockSpec((1, D), lambda s: (0, 0)),
                  pl.BlockSpec((1, D // IN_LOADS, N), lambda s: (layer, jnp.minimum(s, IN_LOADS - 1), 0))],
        out_specs=pl.BlockSpec((tm, N), lambda s: (tile(s), 0)),
        out_shape=jax.ShapeDtypeStruct((T, N), BF16),
        scratch_shapes=[pltpu.VMEM((D, N), BF16)],
        compiler_params=_cparams("arbitrary"),
        name="in_proj",
    )(x2, mod, g.reshape(1, D), w_in)


def _rope_kernel(pos_ref, inv_ref, sign_ref, cos_ref, sin_ref):
    ang = pos_ref[...] * inv_ref[...]
    cos_ref[...] = jnp.cos(ang)
    sin_ref[...] = jnp.sin(ang) * sign_ref[...]


def _rope_tables(positions):
    T = positions.size
    pos = positions.reshape(T, 1).astype(F32)
    inv = 1.0 / (ROPE_THETA ** (jnp.arange(0, HEAD_DIM, 2, dtype=F32) / HEAD_DIM))
    half = HEAD_DIM // 2
    inv128 = jnp.tile(inv, LANES // half).reshape(1, LANES)
    sign = jnp.tile(jnp.concatenate([-jnp.ones((half,), F32), jnp.ones((half,), F32)]),
                    LANES // HEAD_DIM).reshape(1, LANES)
    tm = 1024
    return pl.pallas_call(
        _rope_kernel,
        grid=(T // tm,),
        in_specs=[pl.BlockSpec((tm, 1), lambda i: (i, 0)),
                  pl.BlockSpec((1, LANES), lambda i: (0, 0)),
                  pl.BlockSpec((1, LANES), lambda i: (0, 0))],
        out_specs=[pl.BlockSpec((tm, LANES), lambda i: (i, 0))] * 2,
        out_shape=[jax.ShapeDtypeStruct((T, LANES), F32)] * 2,
        compiler_params=_cparams("parallel"),
        name="rope_tables",
    )(pos, inv128, sign)


def _qkprep_kernel(dq_ref, dk_ref, mq_ref, mk_ref, gain_ref, cos_ref, sin_ref, seg_ref, o_ref):
    cos = cos_ref[...]
    sin = sin_ref[...]
    seg = seg_ref[...]
    lane = lax.broadcasted_iota(jnp.int32, cos.shape, 1)
    first_half = (lane % HEAD_DIM) < (HEAD_DIM // 2)
    width = dq_ref.shape[1]
    for gi, ref in enumerate((dq_ref, dk_ref, mq_ref, mk_ref)):
        gain = gain_ref[gi:gi + 1, :]
        for cb in range(width // LANES):
            x = ref[:, cb * LANES:(cb + 1) * LANES].astype(F32)
            sq = x * x
            sq_hi = sq.astype(BF16)
            sq_lo = (sq - sq_hi.astype(F32)).astype(BF16)
            ms = (jnp.dot(sq_hi, seg, preferred_element_type=F32)
                  + jnp.dot(sq_lo, seg, preferred_element_type=F32))
            y = x * lax.rsqrt(ms + EPS) * gain
            swapped = jnp.where(first_half, pltpu.roll(y, LANES - HEAD_DIM // 2, 1),
                                pltpu.roll(y, HEAD_DIM // 2, 1))
            r = y * cos + swapped * sin
            if gi % 2 == 0:
                r = r * (math.log2(math.e) / math.sqrt(HEAD_DIM))
            col = gi * width + cb * LANES
            o_ref[:, col:col + LANES] = r.astype(BF16)


def _qk_prep(proj, gains, cos, sin):
    T = proj.shape[0]
    width = ATTN_W
    seg = jnp.kron(jnp.eye(LANES // HEAD_DIM, dtype=F32),
                   jnp.full((HEAD_DIM, HEAD_DIM), 1.0 / HEAD_DIM, F32)).astype(BF16)
    tm = ROW_TILE
    col_blocks = [PROJ_COL[name] // width for name in QK_COL]
    in_specs = [pl.BlockSpec((tm, width), functools.partial(lambda i, c: (i, c), c=c)) for c in col_blocks]
    in_specs += [pl.BlockSpec((4, LANES), lambda i: (0, 0)),
                 pl.BlockSpec((tm, LANES), lambda i: (i, 0)),
                 pl.BlockSpec((tm, LANES), lambda i: (i, 0)),
                 pl.BlockSpec((LANES, LANES), lambda i: (0, 0))]
    return pl.pallas_call(
        _qkprep_kernel,
        grid=(T // tm,),
        in_specs=in_specs,
        out_specs=pl.BlockSpec((tm, 4 * width), lambda i: (i, 0)),
        out_shape=jax.ShapeDtypeStruct((T, 4 * width), BF16),
        compiler_params=_cparams("parallel"),
        name="qk_prep",
    )(proj, proj, proj, proj, gains, cos, sin, seg)


def _gelu_tanh(x):
    return 0.5 * x * (1.0 + jnp.tanh(math.sqrt(2.0 / math.pi) * (x + 0.044715 * x * x * x)))


SUBLANES = 8
LRU_LANE_BLOCKS = 2


def _rglru_kernel(x_ref, g_ref, cw_ref, cb_ref, wa_ref, ba_ref, wx_ref, bx_ref, lam_ref, o_ref, *scratch):
    S = x_ref.shape[0]
    steps = S // SUBLANES
    pitch = steps + SUBLANES
    nblk = x_ref.shape[1] // LANES
    per_blk = len(scratch) // nblk
    sub = lax.broadcasted_iota(jnp.int32, (SUBLANES, LANES), 0)

    for blk in range(nblk):
        slab, fold_x, a_buf, u_buf, p_buf = scratch[blk * per_blk:(blk + 1) * per_blk]
        lanes = slice(blk * LANES, (blk + 1) * LANES)

        for s in range(SUBLANES):
            slab[s * pitch:s * pitch + steps, :] = x_ref[s * steps:(s + 1) * steps, lanes].astype(F32)

        def fold(step, carry):
            fold_x[pl.ds(pl.multiple_of(step * SUBLANES, SUBLANES), SUBLANES), :] = (
                slab[pl.ds(step, SUBLANES, stride=pitch), :])
            return carry

        lax.fori_loop(0, steps, fold, 0, unroll=8)
        xf = fold_x[...]

        def head(j):
            tail = xf[(steps - j) * SUBLANES:, :]
            vregs = [jnp.where(sub == 0, 0.0, pltpu.roll(tail[v * SUBLANES:(v + 1) * SUBLANES, :], 1, 0))
                     for v in range(j)]
            return jnp.concatenate(vregs + [xf[:(steps - j) * SUBLANES, :]], axis=0)

        xc = cb_ref[:, lanes] + cw_ref[CONV_W - 1:CONV_W, lanes] * xf
        for j in range(1, CONV_W):
            xc = xc + cw_ref[CONV_W - 1 - j:CONV_W - j, lanes] * head(j)

        xcb = xc.astype(BF16)
        r = jax.nn.sigmoid(jnp.dot(xcb, wa_ref[blk], preferred_element_type=F32) + ba_ref[:, lanes])
        gi = jax.nn.sigmoid(jnp.dot(xcb, wx_ref[blk], preferred_element_type=F32) + bx_ref[:, lanes])
        neg_lam = -lam_ref[:, lanes]
        softplus = jnp.maximum(neg_lam, 0.0) + jnp.log1p(jnp.exp(-jnp.abs(neg_lam)))
        a = jnp.exp2(r * ((-LRU_C * math.log2(math.e)) * softplus))
        gap = 1.0 - a * a
        mult = jnp.where(gap > 0.0, gap * lax.rsqrt(gap), 0.0)
        a_buf[...] = a
        u_buf[...] = mult * gi * xc

    def scan(step, carry):
        rows = pl.ds(pl.multiple_of(step * SUBLANES, SUBLANES), SUBLANES)
        out = []
        for blk in range(nblk):
            _, _, a_buf, u_buf, p_buf = scratch[blk * per_blk:(blk + 1) * per_blk]
            h, p = carry[blk]
            a = a_buf[rows, :]
            h = a * h + u_buf[rows, :]
            p = a * p
            u_buf[rows, :] = h
            p_buf[rows, :] = p
            out.append((h, p))
        return tuple(out)

    ones = jnp.ones((SUBLANES, LANES), F32)
    last = lax.fori_loop(0, steps, scan, ((jnp.zeros_like(ones), ones),) * nblk, unroll=8)

    for blk in range(nblk):
        slab, fold_x, a_buf, u_buf, p_buf = scratch[blk * per_blk:(blk + 1) * per_blk]
        lanes = slice(blk * LANES, (blk + 1) * LANES)
        h_end, p_end = last[blk]
        h_in = [jnp.zeros((1, LANES), F32)]
        for s in range(SUBLANES - 1):
            h_in.append(h_end[s:s + 1, :] + p_end[s:s + 1, :] * h_in[s])
        h_in = jnp.concatenate(h_in, axis=0)
        u_buf[...] = u_buf[...] + p_buf[...] * jnp.concatenate([h_in] * steps, axis=0)

        def unfold(step, carry):
            slab[pl.ds(step, SUBLANES, stride=pitch), :] = (
                u_buf[pl.ds(pl.multiple_of(step * SUBLANES, SUBLANES), SUBLANES), :])
            return carry

        lax.fori_loop(0, steps, unfold, 0, unroll=8)
        for s in range(SUBLANES):
            rows = slice(s * steps, (s + 1) * steps)
            h = slab[s * pitch:s * pitch + steps, :]
            o_ref[rows, lanes] = (_gelu_tanh(g_ref[rows, lanes].astype(F32)) * h).astype(BF16)


def _rg_lru_branch(proj, seq, conv_w, conv_b, wa2, ba, wx2, bx, lam):
    T = proj.shape[0]
    C = conv_w.shape[1]
    B = T // seq
    nblk = LRU_LANE_BLOCKS
    W = nblk * LANES
    assert PROJ_COL["x_rnn"] == 0 and PROJ_COL["g_rnn"] % W == 0
    g_off = PROJ_COL["g_rnn"] // W
    steps = seq // SUBLANES
    assert seq % (SUBLANES * SUBLANES) == 0 and C % W == 0
    vec = lambda v: v.reshape(1, C)
    cols = lambda rows: pl.BlockSpec((rows, W), lambda b, n: (0, n))
    per_blk = [pltpu.VMEM((SUBLANES * (steps + SUBLANES), LANES), F32)] + [pltpu.VMEM((seq, LANES), F32)] * 4
    return pl.pallas_call(
        _rglru_kernel,
        grid=(B, C // W),
        in_specs=[pl.BlockSpec((seq, W), lambda b, n: (b, n)),
                  pl.BlockSpec((seq, W), lambda b, n: (b, g_off + n)),
                  cols(CONV_W), cols(1),
                  pl.BlockSpec((nblk, LANES, LANES), lambda b, n: (n, 0, 0)), cols(1),
                  pl.BlockSpec((nblk, LANES, LANES), lambda b, n: (n, 0, 0)), cols(1), cols(1)],
        out_specs=pl.BlockSpec((seq, W), lambda b, n: (b, n)),
        out_shape=jax.ShapeDtypeStruct((T, C), BF16),
        scratch_shapes=per_blk * nblk,
        compiler_params=_cparams("parallel", "parallel"),
        name="rg_lru",
    )(proj, proj, conv_w, vec(conv_b), wa2, vec(ba), wx2, vec(bx), vec(lam))


def _pair_block_diag(w):
    n, d, _ = w.shape
    z = jnp.zeros((n // 2, 2, d, 2, d), w.dtype)
    z = z.at[:, 0, :, 0, :].set(w[0::2]).at[:, 1, :, 1, :].set(w[1::2])
    return z.reshape(n // 2, 2 * d, 2 * d).astype(BF16)


ATTN_TILE = 256


def _lane_tile(x, n):
    return jnp.concatenate([x] * n, axis=1)


def _softmax_update(s, state, v_aug):
    row_max = jnp.max(s, axis=-1, keepdims=True)
    if state is None:
        m_new = jnp.broadcast_to(row_max, (s.shape[0], LANES))
        p = jnp.exp2(s - _lane_tile(m_new, s.shape[1] // LANES)).astype(BF16)
        return m_new, jnp.dot(p, v_aug, preferred_element_type=F32)
    m_old, acc = state
    m_new = jnp.maximum(m_old, row_max)
    p = jnp.exp2(s - _lane_tile(m_new, s.shape[1] // LANES)).astype(BF16)
    alpha = _lane_tile(jnp.exp2(m_old - m_new), acc.shape[1] // LANES)
    return m_new, alpha * acc + jnp.dot(p, v_aug, preferred_element_type=F32)


def _softmax_finish(state):
    _, acc = state
    return acc[:, :LANES] / acc[:, LANES:]


def _state_rows(state, start, stop):
    m, acc = state
    return m[start:stop], acc[start:stop]


def _causal_mask(t):
    return lax.broadcasted_iota(jnp.int32, (t, t), 1) <= lax.broadcasted_iota(jnp.int32, (t, t), 0)


def _mask_diagonal(s, causal):
    t = causal.shape[0]
    if s.shape[0] == t:
        return jnp.where(causal, s, NEG)
    return jnp.concatenate([jnp.where(causal, s[:t], NEG), s[t:]], axis=0)


def _with_ones(v):
    return jnp.concatenate([v, jnp.ones_like(v)], axis=1)


def _diffattn_kernel(q_ref, k_ref, v_ref, lam_ref, g_ref, o_ref, *, lam_init):
    t = ATTN_TILE
    S = q_ref.shape[0]
    q = q_ref[...]
    lane = lax.broadcasted_iota(jnp.int32, q.shape, 1)
    zero = jnp.zeros_like(q)
    q1 = jnp.where(lane < HEAD_DIM, q, zero)
    q2 = jnp.where(lane < HEAD_DIM, zero, q)
    causal = _causal_mask(t)
    lp = lam_ref[...]
    lam = (jnp.exp(jnp.sum(lp[0:1] * lp[1:2], axis=-1, keepdims=True))
           - jnp.exp(jnp.sum(lp[2:3] * lp[3:4], axis=-1, keepdims=True)) + lam_init)
    st1 = st2 = None
    for j in range(S // t):
        k = k_ref[j * t:(j + 1) * t, :]
        v_aug = _with_ones(v_ref[j * t:(j + 1) * t, :])
        st1 = _softmax_update(_mask_diagonal(_dot_nt(q1[j * t:, :], k), causal), st1, v_aug)
        st2 = _softmax_update(_mask_diagonal(_dot_nt(q2[j * t:, :], k), causal), st2, v_aug)
        o = _softmax_finish(_state_rows(st1, 0, t)) - lam * _softmax_finish(_state_rows(st2, 0, t))
        ms = jnp.mean(o * o, axis=-1, keepdims=True)
        o_ref[j * t:(j + 1) * t, :] = (o * lax.rsqrt(ms + EPS) * g_ref[...] * (1.0 - lam_init)).astype(BF16)
        if (j + 1) * t < S:
            st1 = _state_rows(st1, t, None)
            st2 = _state_rows(st2, t, None)


def _diff_attention(qk, proj, seq, lam_params, subln_g, lam_init):
    T = qk.shape[0]
    B = T // seq
    H = DIFF_HEADS
    return pl.pallas_call(
        functools.partial(_diffattn_kernel, lam_init=lam_init),
        grid=(B, H),
        in_specs=[pl.BlockSpec((seq, LANES), lambda b, h: (b, h)),
                  pl.BlockSpec((seq, LANES), lambda b, h: (b, QK_COL["dk"] // LANES + h)),
                  pl.BlockSpec((seq, LANES), lambda b, h: (b, PROJ_COL["dv"] // LANES + h)),
                  pl.BlockSpec((4, HEAD_DIM), lambda b, h: (0, 0)),
                  pl.BlockSpec((1, LANES), lambda b, h: (0, 0))],
        out_specs=pl.BlockSpec((seq, LANES), lambda b, h: (b, h)),
        out_shape=jax.ShapeDtypeStruct((T, H * LANES), BF16),
        compiler_params=_cparams("parallel", "parallel"),
        name="diff_attn",
    )(qk, qk, proj, lam_params, subln_g.reshape(1, LANES))


MOBA_PAIRS_PER_STEP = 2


def _moba_kernel(q_ref, k_ref, v_ref, o_ref, kmean):
    for pair in range(q_ref.shape[1] // LANES):
        lanes = pl.ds(pair * LANES, LANES)
        _moba_pair(q_ref.at[:, lanes], k_ref.at[:, lanes], v_ref.at[:, lanes], o_ref.at[:, lanes], kmean.at[pair])


def _moba_pair(q_ref, k_ref, v_ref, o_ref, kmean):
    t = ATTN_TILE
    S = q_ref.shape[0]
    nb = S // MOBA_BLOCK
    nb8 = -(-nb // 8) * 8
    lane_row = lax.broadcasted_iota(jnp.int32, (1, LANES), 1)

    kmean[...] = jnp.zeros_like(kmean)
    for blk in range(nb):
        mean = jnp.mean(k_ref[blk * MOBA_BLOCK:(blk + 1) * MOBA_BLOCK, :].astype(F32), axis=0, keepdims=True)
        kmean[blk:blk + 1, :] = jnp.where(lane_row < HEAD_DIM, mean, 0.0)
        kmean[nb8 + blk:nb8 + blk + 1, :] = jnp.where(lane_row < HEAD_DIM, 0.0, mean)

    q = q_ref[...]
    lane = lax.broadcasted_iota(jnp.int32, q.shape, 1)
    zero = jnp.zeros_like(q)
    q_a = jnp.where(lane < HEAD_DIM, q, zero)
    q_b = jnp.where(lane < HEAD_DIM, zero, q)

    km = kmean[...]
    km_hi = km.astype(BF16)
    km_mid = (km - km_hi.astype(F32)).astype(BF16)
    km_lo = (km - km_hi.astype(F32) - km_mid.astype(F32)).astype(BF16)
    gate = _dot_nt(km_hi, q) + _dot_nt(km_mid, q) + _dot_nt(km_lo, q)
    gate_blk = lax.broadcasted_iota(jnp.int32, (nb8, S), 0)
    own = lax.broadcasted_iota(jnp.int32, (nb8, S), 1) // MOBA_BLOCK
    past = gate_blk < own
    unused_rows = jnp.full((HEAD_DIM - nb8, S), NEG, F32)
    bias_rows = []
    for head in range(2):
        g = jnp.where(past, gate[head * nb8:(head + 1) * nb8, :], NEG)
        rank = jnp.zeros(g.shape, jnp.int32)
        for jb in range(nb - 1):
            other = g[jb:jb + 1, :]
            beats = (other > g) | ((other == g) & (jb < gate_blk))
            rank = rank + beats.astype(jnp.int32)
        allowed = (past & (rank < MOBA_TOPK)) | (gate_blk == own)
        bias_rows += [jnp.where(allowed, 0.0, NEG), unused_rows]
    bias = jnp.concatenate(bias_rows, axis=0).T.astype(BF16)
    qa_aug = jnp.concatenate([q_a, bias], axis=1)
    qb_aug = jnp.concatenate([q_b, bias], axis=1)

    causal = _causal_mask(t)
    key_lane = lax.broadcasted_iota(jnp.int32, (t, LANES), 1)
    key_row = lax.broadcasted_iota(jnp.int32, (t, LANES), 0)
    st_a = st_b = None
    for j in range(S // t):
        k = k_ref[j * t:(j + 1) * t, :]
        key_blk = (j * t + key_row) // MOBA_BLOCK
        ka_aug = jnp.concatenate([k, (key_lane == key_blk).astype(BF16)], axis=1)
        kb_aug = jnp.concatenate([k, (key_lane == key_blk + HEAD_DIM).astype(BF16)], axis=1)
        v_aug = _with_ones(v_ref[j * t:(j + 1) * t, :])
        st_a = _softmax_update(_mask_diagonal(_dot_nt(qa_aug[j * t:, :], ka_aug), causal), st_a, v_aug)
        st_b = _softmax_update(_mask_diagonal(_dot_nt(qb_aug[j * t:, :], kb_aug), causal), st_b, v_aug)
        o = jnp.where(key_lane < HEAD_DIM, _softmax_finish(_state_rows(st_a, 0, t)),
                      _softmax_finish(_state_rows(st_b, 0, t)))
        o_ref[j * t:(j + 1) * t, :] = o.astype(BF16)
        if (j + 1) * t < S:
            st_a = _state_rows(st_a, t, None)
            st_b = _state_rows(st_b, t, None)


def _moba_attention(qk, proj, seq):
    T = qk.shape[0]
    B = T // seq
    HP = MOBA_HEADS // 2
    P = MOBA_PAIRS_PER_STEP
    W = P * LANES
    assert HP % P == 0 and all(col % W == 0 for col in (QK_COL["mq"], QK_COL["mk"], PROJ_COL["mv"]))
    gate_rows = 2 * SUBLANES * pl.cdiv(seq // MOBA_BLOCK, SUBLANES)
    return pl.pallas_call(
        _moba_kernel,
        grid=(B, HP // P),
        in_specs=[pl.BlockSpec((seq, W), lambda b, h: (b, QK_COL["mq"] // W + h)),
                  pl.BlockSpec((seq, W), lambda b, h: (b, QK_COL["mk"] // W + h)),
                  pl.BlockSpec((seq, W), lambda b, h: (b, PROJ_COL["mv"] // W + h))],
        out_specs=pl.BlockSpec((seq, W), lambda b, h: (b, h)),
        out_shape=jax.ShapeDtypeStruct((T, HP * LANES), BF16),
        scratch_shapes=[pltpu.VMEM((P, gate_rows, LANES), F32)],
        compiler_params=_cparams("parallel", "parallel"),
        name="moba_attn",
    )(qk, qk, proj)


def _merge_kernel(x_ref, ya_ref, yb_ref, yc_ref, ga_ref, gbb_ref, gc_ref, gb_ref, mod_ref,
                  wa_ref, wb_ref, wc_ref, wo_ref, o_ref):
    merged = None
    branches = ((ya_ref, wa_ref, ga_ref), (yb_ref, wb_ref, gbb_ref), (yc_ref, wc_ref, gc_ref))
    for n, (y_ref, w_ref, gl_ref) in enumerate(branches):
        gate = jax.nn.sigmoid(gl_ref[...].astype(F32) + gb_ref[n:n + 1, :])
        term = gate * jnp.dot(y_ref[...], w_ref[...], preferred_element_type=F32)
        merged = term if merged is None else merged + term
    mix = jnp.dot(merged.astype(BF16), wo_ref[...], preferred_element_type=F32)
    o_ref[...] = x_ref[...] + mod_ref[0][2:3] * mix


def _merge_out(x2, ya, yb, yc, proj, gate_b, mod, wa, wb, wc, wo, seq):
    T, D = x2.shape
    tm = ROW_TILE
    per_b = seq // tm
    assert PROJ_COL["g_br"] % D == 0
    gbr_block = PROJ_COL["g_br"] // D
    wspec = lambda w: pl.BlockSpec(w.shape, lambda i: (0, 0))
    return pl.pallas_call(
        _merge_kernel,
        grid=(T // tm,),
        in_specs=[pl.BlockSpec((tm, D), lambda i: (i, 0)),
                  pl.BlockSpec((tm, ya.shape[1]), lambda i: (i, 0)),
                  pl.BlockSpec((tm, yb.shape[1]), lambda i: (i, 0)),
                  pl.BlockSpec((tm, yc.shape[1]), lambda i: (i, 0)),
                  pl.BlockSpec((tm, D), lambda i: (i, gbr_block)),
                  pl.BlockSpec((tm, D), lambda i: (i, gbr_block + 1)),
                  pl.BlockSpec((tm, D), lambda i: (i, gbr_block + 2)),
                  pl.BlockSpec((N_BRANCH, D), lambda i: (0, 0)),
                  pl.BlockSpec((1, 6, D), lambda i: (i // per_b, 0, 0)),
                  wspec(wa), wspec(wb), wspec(wc), wspec(wo)],
        out_specs=pl.BlockSpec((tm, D), lambda i: (i, 0)),
        out_shape=jax.ShapeDtypeStruct((T, D), F32),
        compiler_params=_cparams("parallel"),
        name="merge_out",
    )(x2, ya, yb, yc, proj, proj, proj, gate_b, mod, wa, wb, wc, wo)


FF_CHUNK = 256
FF_LOADS = 8


def _stash_weight_slab(k, w1_ref, w3_ref, w2_ref, wb1, wb3, wb2):
    for w_ref, wb in ((w1_ref, wb1), (w3_ref, wb3), (w2_ref, wb2)):
        rows, cols = w_ref.shape[-2:]
        start = pl.multiple_of(k * rows, rows)
        wb[pl.ds(start, rows), :] = w_ref[...].reshape(rows, cols).astype(BF16)


def _swiglu_resident(h, wb1, wb3, wb2):
    acc = None
    for f in range(wb2.shape[0] // FF_CHUNK):
        cols = slice(f * FF_CHUNK, (f + 1) * FF_CHUNK)
        g = jnp.dot(h, wb1[:, cols], preferred_element_type=F32)
        u = jnp.dot(h, wb3[:, cols], preferred_element_type=F32)
        act = (g * jax.nn.sigmoid(g) * u).astype(BF16)
        part = jnp.dot(act, wb2[cols, :], preferred_element_type=F32)
        acc = part if acc is None else acc + part
    return acc


def _ffn_weight_specs(D, F, lead, slab_index):
    assert D % (16 * FF_LOADS) == 0 and F % (16 * FF_LOADS) == 0 and F % FF_CHUNK == 0
    index = lambda *a: slab_index(*a) + (0,)
    return [pl.BlockSpec(lead + (D // FF_LOADS, F), index), pl.BlockSpec(lead + (D // FF_LOADS, F), index),
            pl.BlockSpec(lead + (F // FF_LOADS, D), index)]


def _ffn_kernel(x_ref, mod_ref, g_ref, w1_ref, w3_ref, w2_ref, o_ref, wb1, wb3, wb2):
    s = pl.program_id(0)

    @pl.when(s < FF_LOADS)
    def _():
        _stash_weight_slab(s, w1_ref, w3_ref, w2_ref, wb1, wb3, wb2)

    @pl.when(s >= FF_LOADS)
    def _():
        h = _modulated_norm(x_ref[...], g_ref[...], mod_ref[0], 3).astype(BF16)
        o_ref[...] = x_ref[...] + mod_ref[0][5:6] * _swiglu_resident(h, wb1, wb3, wb2)


def _dense_ffn(x2, mod, g, w1, w3, w2, seq):
    T, D = x2.shape
    F = w1.shape[1]
    tm = ROW_TILE
    per_b = seq // tm
    tile = lambda s: jnp.maximum(s - FF_LOADS, 0)
    return pl.pallas_call(
        _ffn_kernel,
        grid=(FF_LOADS + T // tm,),
        in_specs=[pl.BlockSpec((tm, D), lambda s: (tile(s), 0)),
                  pl.BlockSpec((1, 6, D), lambda s: (tile(s) // per_b, 0, 0)),
                  pl.BlockSpec((1, D), lambda s: (0, 0))]
        + _ffn_weight_specs(D, F, (), lambda s: (jnp.minimum(s, FF_LOADS - 1),)),
        out_specs=pl.BlockSpec((tm, D), lambda s: (tile(s), 0)),
        out_shape=jax.ShapeDtypeStruct((T, D), F32),
        scratch_shapes=[pltpu.VMEM((D, F), BF16), pltpu.VMEM((D, F), BF16), pltpu.VMEM((F, D), BF16)],
        compiler_params=_cparams("arbitrary"),
        name="dense_ffn",
    )(x2, mod, g.reshape(1, D), w1, w3, w2)


def _router_kernel(x_ref, mod_ref, g_ref, rw_ref, rb_ref, h_ref, comb_ref, sel_ref):
    h = _modulated_norm(x_ref[...], g_ref[...], mod_ref[0], 3)
    h_ref[...] = h
    logits = _dot_nt(rw_ref[...], h, precision=HIGHEST) + rb_ref[...]
    E = logits.shape[0]
    eid = lax.broadcasted_iota(jnp.int32, logits.shape, 0)
    v1 = jnp.max(logits, axis=0, keepdims=True)
    i1 = jnp.min(jnp.where(logits == v1, eid, E), axis=0, keepdims=True)
    rest = jnp.where(eid == i1, -jnp.inf, logits)
    v2 = jnp.max(rest, axis=0, keepdims=True)
    i2 = jnp.min(jnp.where(rest == v2, eid, E), axis=0, keepdims=True)
    e2 = jnp.exp(v2 - v1)
    w1 = 1.0 / (1.0 + e2)
    w2 = e2 / (1.0 + e2)
    comb_ref[...] = jnp.where(eid == i1, w1, 0.0) + jnp.where(eid == i2, w2, 0.0)
    sel_ref[...] = ((eid == i1) | (eid == i2)).astype(jnp.int32)


def _router(x2, mod, g, router_w, router_b, seq):
    T, D = x2.shape
    E = router_w.shape[1]
    tm = ROW_TILE
    per_b = seq // tm
    return pl.pallas_call(
        _router_kernel,
        grid=(T // tm,),
        in_specs=[pl.BlockSpec((tm, D), lambda i: (i, 0)),
                  pl.BlockSpec((1, 6, D), lambda i: (i // per_b, 0, 0)),
                  pl.BlockSpec((1, D), lambda i: (0, 0)),
                  pl.BlockSpec((E, D), lambda i: (0, 0)),
                  pl.BlockSpec((E, 1), lambda i: (0, 0))],
        out_specs=[pl.BlockSpec((tm, D), lambda i: (i, 0)),
                   pl.BlockSpec((E, tm), lambda i: (0, i)),
                   pl.BlockSpec((E, tm), lambda i: (0, i))],
        out_shape=[jax.ShapeDtypeStruct((T, D), F32),
                   jax.ShapeDtypeStruct((E, T), F32),
                   jax.ShapeDtypeStruct((E, T), jnp.int32)],
        compiler_params=_cparams("parallel"),
        name="moe_router",
    )(x2, mod, g.reshape(1, D), router_w.T, router_b.reshape(E, 1))


STEP_LOAD, STEP_TILE, STEP_BOTH, STEP_ZERO, STEP_NOOP = 0, 1, 2, 3, 4
MOE_SLOTS = 2


def _moe_kernel(kind_ref, slot_ref, slab_ref, we_ref, wk_ref, t_ref, xs_ref, w1_ref, w3_ref, w2_ref, o_ref,
                wb1, wb3, wb2):
    s = pl.program_id(0)
    kind = kind_ref[s]
    slot = slot_ref[s]

    @pl.when(kind == STEP_LOAD)
    def _():
        _stash_weight_slab(slab_ref[s], w1_ref, w3_ref, w2_ref, wb1.at[slot], wb3.at[slot], wb2.at[slot])

    @pl.when(kind == STEP_TILE)
    def _():
        o_ref[...] = _swiglu_resident(xs_ref[...].astype(BF16), wb1.at[slot], wb3.at[slot], wb2.at[slot])

    for use in range(MOE_SLOTS):
        fill = (use + 1) % MOE_SLOTS

        @pl.when((kind == STEP_BOTH) & (slot == use))
        def _():
            _stash_weight_slab(slab_ref[s], w1_ref, w3_ref, w2_ref, wb1.at[fill], wb3.at[fill], wb2.at[fill])
            o_ref[...] = _swiglu_resident(xs_ref[...].astype(BF16), wb1.at[use], wb3.at[use], wb2.at[use])

    @pl.when(kind == STEP_ZERO)
    def _():
        o_ref[...] = jnp.zeros_like(o_ref)


def _moe_schedule(first_tile, tiles, t0, nt):
    E = tiles.shape[0]
    L = FF_LOADS
    i32 = jnp.int32
    ids = jnp.arange(E, dtype=i32)
    lo = jnp.clip(first_tile - t0, 0, nt)
    cnt = jnp.clip(first_tile + tiles - t0, 0, nt) - lo
    active = cnt > 0
    n_act = jnp.sum(active.astype(i32))
    order = jnp.cumsum(active.astype(i32)) - 1
    pick = ((order[None, :] == ids[:, None]) & active[None, :]).astype(i32)
    act_e = jnp.sum(pick * ids[None, :], axis=1)
    act_cnt = jnp.sum(pick * cnt[None, :], axis=1)
    act_lo = jnp.sum(pick * lo[None, :], axis=1)
    has_next = ids + 1 < n_act
    blk_len = jnp.where(ids < n_act, jnp.maximum(act_cnt, jnp.where(has_next, L, 0)), 0)
    blk_start = L + jnp.cumsum(blk_len) - blk_len
    n_sched = jnp.where(n_act > 0, L + jnp.sum(blk_len), 0)
    n_used = jnp.sum(cnt)

    s = jnp.arange(L * E + nt, dtype=i32)
    k = jnp.clip(jnp.sum(((blk_start[None, :] <= s[:, None]) & (ids[None, :] < n_act)).astype(i32), axis=1) - 1,
                 0, E - 1)
    p = s - blk_start[k]
    in_block = (s >= L) & (s < n_sched)
    first_loads = (s < L) & (n_act > 0)
    tile_step = in_block & (p < act_cnt[k])
    load_step = first_loads | (in_block & has_next[k] & (p < L))
    load_k = jnp.where(first_loads, 0, k + 1)
    zero_step = (s >= n_sched) & (s - n_sched < nt - n_used)

    kind = jnp.where(tile_step & load_step, STEP_BOTH,
                     jnp.where(tile_step, STEP_TILE,
                               jnp.where(load_step, STEP_LOAD, jnp.where(zero_step, STEP_ZERO, STEP_NOOP))))
    slot = jnp.where(tile_step, k, load_k) % MOE_SLOTS
    slab = jnp.clip(jnp.where(first_loads, s, p), 0, L - 1)
    last_load = lax.cummax(jnp.where(load_step, s, 0), axis=0)
    we = act_e[jnp.clip(load_k, 0, E - 1)][last_load]
    wk = slab[last_load]
    own = jnp.where(tile_step, act_lo[k] + p, jnp.where(zero_step, n_used + s - n_sched, nt - 1))
    tile = jnp.clip(lax.cummin(own, axis=0, reverse=True), 0, nt - 1)
    return tuple(a.astype(i32) for a in (kind, slot, slab, we, wk, tile))


def _moe_grouped(xs, first_tile, tiles, w1, w3, w2, tm):
    N, D = xs.shape
    nt = N // tm
    E, _, F = w1.shape
    table = _moe_schedule(first_tile, tiles, 0, nt)
    grid_spec = pltpu.PrefetchScalarGridSpec(
        num_scalar_prefetch=len(table),
        grid=(FF_LOADS * E + nt,),
        in_specs=[pl.BlockSpec((tm, D), lambda s, *pre: (pre[5][s], 0))]
        + _ffn_weight_specs(D, F, (1,), lambda s, *pre: (pre[3][s], pre[4][s])),
        out_specs=pl.BlockSpec((tm, D), lambda s, *pre: (pre[5][s], 0)),
        scratch_shapes=[pltpu.VMEM((MOE_SLOTS, D, F), BF16), pltpu.VMEM((MOE_SLOTS, D, F), BF16),
                        pltpu.VMEM((MOE_SLOTS, F, D), BF16)],
    )
    return pl.pallas_call(
        _moe_kernel,
        grid_spec=grid_spec,
        out_shape=jax.ShapeDtypeStruct((N, D), F32),
        compiler_params=_cparams("arbitrary"),
        name="moe_experts",
    )(*table, xs, w1, w3, w2)


def _combine_kernel(x_ref, y1_ref, y2_ref, w_ref, mod_ref, o_ref):
    w = w_ref[...]
    y = w[:, 0:1] * y1_ref[...] + w[:, 1:2] * y2_ref[...]
    o_ref[...] = x_ref[...] + mod_ref[0][5:6] * y


def _moe_combine(x2, y1, y2, w12, mod, seq):
    T, D = x2.shape
    tm = ROW_TILE
    per_b = seq // tm
    return pl.pallas_call(
        _combine_kernel,
        grid=(T // tm,),
        in_specs=[pl.BlockSpec((tm, D), lambda i: (i, 0)),
                  pl.BlockSpec((tm, D), lambda i: (i, 0)),
                  pl.BlockSpec((tm, D), lambda i: (i, 0)),
                  pl.BlockSpec((tm, 2), lambda i: (i, 0)),
                  pl.BlockSpec((1, 6, D), lambda i: (i // per_b, 0, 0))],
        out_specs=pl.BlockSpec((tm, D), lambda i: (i, 0)),
        out_shape=jax.ShapeDtypeStruct((T, D), F32),
        compiler_params=_cparams("parallel"),
        name="moe_combine",
    )(x2, y1, y2, w12, mod)


def _moe_ffn(x2, mod, g, router_w, router_b, w1, w3, w2, seq):
    T, D = x2.shape
    E = router_w.shape[1]
    tm = ROW_TILE
    h, comb, sel = _router(x2, mod, g, router_w, router_b, seq)

    counts = jnp.sum(sel, axis=1)
    padded = ((counts + tm - 1) // tm) * tm
    group_end = jnp.cumsum(padded)
    group_start = group_end - padded
    rank = jnp.cumsum(sel, axis=1) - sel
    dest = group_start[:, None] + rank
    n_rows = TOP_K * T + E * tm
    n_tiles = n_rows // tm
    tile_start = jnp.arange(n_tiles, dtype=jnp.int32) * tm
    tile_expert = jnp.sum((group_end[None, :] <= tile_start[:, None]).astype(jnp.int32), axis=1)
    tile_expert = jnp.minimum(tile_expert, jnp.max(jnp.where(counts > 0, jnp.arange(E), 0))).astype(jnp.int32)

    eid = jnp.arange(E, dtype=jnp.int32)[:, None]
    e_lo = jnp.min(jnp.where(sel > 0, eid, E), axis=0)
    e_hi = jnp.max(jnp.where(sel > 0, eid, -1), axis=0)
    pick = lambda a, e: jnp.sum(jnp.where(eid == e[None, :], a, 0), axis=0)
    w12 = jnp.stack([pick(comb, e_lo), pick(comb, e_hi)], axis=1)

    tok = jnp.arange(T, dtype=jnp.int32)
    tok_sorted = jnp.sort(jnp.concatenate([e_lo * T + tok, e_hi * T + tok])) % T
    row = jnp.arange(n_rows, dtype=jnp.int32)
    row_expert = jnp.repeat(tile_expert, tm)
    first_sorted = (jnp.cumsum(counts) - counts)[row_expert]
    src = tok_sorted[jnp.clip(first_sorted + row - group_start[row_expert], 0, TOP_K * T - 1)]

    take_rows = lambda a, idx: a.at[idx].get(mode="promise_in_bounds")
    ys = _moe_grouped(take_rows(h, src), group_start // tm, padded // tm, w1, w3, w2, tm)
    y1 = take_rows(ys, pick(dest, e_lo))
    y2 = take_rows(ys, pick(dest, e_hi))
    return _moe_combine(x2, y1, y2, w12, mod, seq)


def kernel(x, c, positions, ada_w, ada_b, norm1_g, norm2_g, w_in, gate_b, conv_w, conv_b, lru_wa, lru_ba,
           lru_wx, lru_bx, lru_lambda, diff_qn, diff_kn, diff_lq1, diff_lk1, diff_lq2, diff_lk2, diff_subln,
           moba_qn, moba_kn, w_br_a, w_br_b, w_br_c, w_out, ffn_w1, ffn_w3, ffn_w2, router_w, router_b,
           moe_w1, moe_w3, moe_w2):
    B, S, D = x.shape
    L = ada_w.shape[0]
    T = B * S
    assert D == D_MODEL and w_in.shape[2] == PROJ_COL["g_br"] + N_BRANCH * D
    assert S % MOBA_BLOCK == 0 and S // MOBA_BLOCK <= 56 and S % ROW_TILE == 0
    x2 = x.reshape(T, D)
    mod_all = _ada_mod(c, ada_w, ada_b).reshape(L, B, 6, D)
    cos, sin = _rope_tables(positions)
    bf = lambda w: w.astype(BF16)
    tile2 = lambda v: jnp.tile(v, LANES // HEAD_DIM)

    for l in range(L):
        mod = mod_all[l]
        lam_init = 0.8 - 0.6 * math.exp(-0.3 * l)
        proj = _in_proj(x2, mod, norm1_g[l], w_in, l, S)
        gains = jnp.stack([tile2(diff_qn[l]), tile2(diff_kn[l]), tile2(moba_qn[l]), tile2(moba_kn[l])])
        qk = _qk_prep(proj, gains, cos, sin)
        y_a = _rg_lru_branch(proj, S, conv_w[l], conv_b[l], _pair_block_diag(lru_wa[l]), lru_ba[l],
                             _pair_block_diag(lru_wx[l]), lru_bx[l], lru_lambda[l])
        lam_params = jnp.stack([diff_lq1[l], diff_lk1[l], diff_lq2[l], diff_lk2[l]])
        y_b = _diff_attention(qk, proj, S, lam_params, diff_subln[l], lam_init)
        y_c = _moba_attention(qk, proj, S)
        x2 = _merge_out(x2, y_a, y_b, y_c, proj, gate_b[l], mod, bf(w_br_a[l]), bf(w_br_b[l]),
                        bf(w_br_c[l]), bf(w_out[l]), S)
        if l % 2 == 0:
            x2 = _dense_ffn(x2, mod, norm2_g[l], ffn_w1[l // 2], ffn_w3[l // 2], ffn_w2[l // 2], S)
        else:
            x2 = _moe_ffn(x2, mod, norm2_g[l], router_w[l // 2], router_b[l // 2], moe_w1[l // 2],
                          moe_w3[l // 2], moe_w2[l // 2], S)
    return x2.reshape(B, S, D)
```

```python
import functools
import math

import jax
import jax.numpy as jnp
from jax import lax
from jax.experimental import pallas as pl
from jax.experimental.pallas import tpu as pltpu

F32 = jnp.float32
BF16 = jnp.bfloat16
HIGHEST = lax.Precision.HIGHEST

HEAD_DIM = 64
ROPE_THETA = 10000.0
RNN_BLOCKS = 16
CONV_W = 4
LRU_C = 8.0
DIFF_HEADS = 4
MOBA_HEADS = 8
MOBA_BLOCK = 256
MOBA_TOPK = 3
N_BRANCH = 3
N_EXPERTS = 8
TOP_K = 2
EPS = 1e-6
NEG = -1e30

LANES = 128
VMEM_LIMIT = 56 * 1024 * 1024

ROW_TILE = 512

D_MODEL = 1024
ATTN_W = DIFF_HEADS * 2 * HEAD_DIM


def _starts(widths):
    out, col = {}, 0
    for name, width in widths:
        out[name] = col
        col += width
    return out


PROJ_COL = _starts((("x_rnn", D_MODEL), ("g_rnn", D_MODEL), ("dq", ATTN_W), ("dk", ATTN_W), ("dv", ATTN_W),
                    ("mq", ATTN_W), ("mk", ATTN_W), ("mv", ATTN_W), ("g_br", N_BRANCH * D_MODEL)))
QK_COL = _starts((("dq", ATTN_W), ("dk", ATTN_W), ("mq", ATTN_W), ("mk", ATTN_W)))


def _cparams(*sem):
    return pltpu.CompilerParams(dimension_semantics=sem, vmem_limit_bytes=VMEM_LIMIT)


def _modulated_norm(x, g, mod, base):
    ms = jnp.mean(x * x, axis=-1, keepdims=True)
    y = x * lax.rsqrt(ms + EPS) * g
    return y * (1.0 + mod[base + 1:base + 2]) + mod[base:base + 1]


def _dot_nt(a, b, **kw):
    return lax.dot_general(a, b, (((1,), (1,)), ((), ())), preferred_element_type=F32, **kw)


def _ada_kernel(c_ref, w_ref, b_ref, o_ref):
    o_ref[0] = jnp.dot(c_ref[...], w_ref[0], preferred_element_type=F32, precision=HIGHEST) + b_ref[0]


def _ada_mod(c, ada_w, ada_b):
    L, D, N = ada_w.shape
    B = c.shape[0]
    tn = N // 2
    return pl.pallas_call(
        _ada_kernel,
        grid=(L, N // tn),
        in_specs=[pl.BlockSpec((B, D), lambda l, j: (0, 0)),
                  pl.BlockSpec((1, D, tn), lambda l, j: (l, 0, j)),
                  pl.BlockSpec((1, 1, tn), lambda l, j: (l, 0, j))],
        out_specs=pl.BlockSpec((1, B, tn), lambda l, j: (l, 0, j)),
        out_shape=jax.ShapeDtypeStruct((L, B, N), F32),
        compiler_params=_cparams("parallel", "parallel"),
        name="ada_mod",
    )(c, ada_w, ada_b.reshape(L, 1, N))


IN_LOADS = 8
IN_COL_CHUNK = 1024


def _inproj_kernel(x_ref, mod_ref, g_ref, w_ref, o_ref, wb):
    s = pl.program_id(0)

    @pl.when(s < IN_LOADS)
    def _():
        rows = w_ref.shape[1]
        wb[pl.ds(pl.multiple_of(s * rows, rows), rows), :] = w_ref[0].astype(BF16)

    @pl.when(s >= IN_LOADS)
    def _():
        h = _modulated_norm(x_ref[...], g_ref[...], mod_ref[0], 0).astype(BF16)
        for c in range(wb.shape[1] // IN_COL_CHUNK):
            cols = slice(c * IN_COL_CHUNK, (c + 1) * IN_COL_CHUNK)
            o_ref[:, cols] = jnp.dot(h, wb[:, cols], preferred_element_type=F32).astype(BF16)


def _in_proj(x2, mod, g, w_in, layer, seq):
    T, D = x2.shape
    N = w_in.shape[2]
    tm = ROW_TILE
    per_b = seq // tm
    assert D % (16 * IN_LOADS) == 0 and N % IN_COL_CHUNK == 0
    tile = lambda s: jnp.maximum(s - IN_LOADS, 0)
    return pl.pallas_call(
        _inproj_kernel,
        grid=(IN_LOADS + T // tm,),
        in_specs=[pl.BlockSpec((tm, D), lambda s: (tile(s), 0)),
                  pl.BlockSpec((1, 6, D), lambda s: (tile(s) // per_b, 0, 0)),
                  pl.BlockSpec((1, D), lambda s: (0, 0)),
                  pl.BlockSpec((1, D // IN_LOADS, N), lambda s: (layer, jnp.minimum(s, IN_LOADS - 1), 0))],
        out_specs=pl.BlockSpec((tm, N), lambda s: (tile(s), 0)),
        out_shape=jax.ShapeDtypeStruct((T, N), BF16),
        scratch_shapes=[pltpu.VMEM((D, N), BF16)],
        compiler_params=_cparams("arbitrary"),
        name="in_proj",
    )(x2, mod, g.reshape(1, D), w_in)


def _rope_kernel(pos_ref, inv_ref, sign_ref, cos_ref, sin_ref):
    ang = pos_ref[...] * inv_ref[...]
    cos_ref[...] = jnp.cos(ang)
    sin_ref[...] = jnp.sin(ang) * sign_ref[...]


def _rope_tables(positions):
    T = positions.size
    pos = positions.reshape(T, 1).astype(F32)
    inv = 1.0 / (ROPE_THETA ** (jnp.arange(0, HEAD_DIM, 2, dtype=F32) / HEAD_DIM))
    half = HEAD_DIM // 2
    inv128 = jnp.tile(inv, LANES // half).reshape(1, LANES)
    sign = jnp.tile(jnp.concatenate([-jnp.ones((half,), F32), jnp.ones((half,), F32)]),
                    LANES // HEAD_DIM).reshape(1, LANES)
    tm = 1024
    return pl.pallas_call(
        _rope_kernel,
        grid=(T // tm,),
        in_specs=[pl.BlockSpec((tm, 1), lambda i: (i, 0)),
                  pl.BlockSpec((1, LANES), lambda i: (0, 0)),
                  pl.BlockSpec((1, LANES), lambda i: (0, 0))],
        out_specs=[pl.BlockSpec((tm, LANES), lambda i: (i, 0))] * 2,
        out_shape=[jax.ShapeDtypeStruct((T, LANES), F32)] * 2,
        compiler_params=_cparams("parallel"),
        name="rope_tables",
    )(pos, inv128, sign)


def _qkprep_kernel(dq_ref, dk_ref, mq_ref, mk_ref, gain_ref, cos_ref, sin_ref, seg_ref, o_ref):
    cos = cos_ref[...]
    sin = sin_ref[...]
    seg = seg_ref[...]
    lane = lax.broadcasted_iota(jnp.int32, cos.shape, 1)
    first_half = (lane % HEAD_DIM) < (HEAD_DIM // 2)
    width = dq_ref.shape[1]
    for gi, ref in enumerate((dq_ref, dk_ref, mq_ref, mk_ref)):
        gain = gain_ref[gi:gi + 1, :]
        for cb in range(width // LANES):
            x = ref[:, cb * LANES:(cb + 1) * LANES].astype(F32)
            sq = x * x
            sq_hi = sq.astype(BF16)
            sq_lo = (sq - sq_hi.astype(F32)).astype(BF16)
            ms = (jnp.dot(sq_hi, seg, preferred_element_type=F32)
                  + jnp.dot(sq_lo, seg, preferred_element_type=F32))
            y = x * lax.rsqrt(ms + EPS) * gain
            swapped = jnp.where(first_half, pltpu.roll(y, LANES - HEAD_DIM // 2, 1),
                                pltpu.roll(y, HEAD_DIM // 2, 1))
            r = y * cos + swapped * sin
            if gi % 2 == 0:
                r = r * (math.log2(math.e) / math.sqrt(HEAD_DIM))
            col = gi * width + cb * LANES
            o_ref[:, col:col + LANES] = r.astype(BF16)


def _qk_prep(proj, gains, cos, sin):
    T = proj.shape[0]
    width = ATTN_W
    seg = jnp.kron(jnp.eye(LANES // HEAD_DIM, dtype=F32),
                   jnp.full((HEAD_DIM, HEAD_DIM), 1.0 / HEAD_DIM, F32)).astype(BF16)
    tm = ROW_TILE
    col_blocks = [PROJ_COL[name] // width for name in QK_COL]
    in_specs = [pl.BlockSpec((tm, width), functools.partial(lambda i, c: (i, c), c=c)) for c in col_blocks]
    in_specs += [pl.BlockSpec((4, LANES), lambda i: (0, 0)),
                 pl.BlockSpec((tm, LANES), lambda i: (i, 0)),
                 pl.BlockSpec((tm, LANES), lambda i: (i, 0)),
                 pl.BlockSpec((LANES, LANES), lambda i: (0, 0))]
    return pl.pallas_call(
        _qkprep_kernel,
        grid=(T // tm,),
        in_specs=in_specs,
        out_specs=pl.BlockSpec((tm, 4 * width), lambda i: (i, 0)),
        out_shape=jax.ShapeDtypeStruct((T, 4 * width), BF16),
        compiler_params=_cparams("parallel"),
        name="qk_prep",
    )(proj, proj, proj, proj, gains, cos, sin, seg)


def _gelu_tanh(x):
    return 0.5 * x * (1.0 + jnp.tanh(math.sqrt(2.0 / math.pi) * (x + 0.044715 * x * x * x)))


SUBLANES = 8
LRU_LANE_BLOCKS = 4


def _rglru_kernel(x_ref, g_ref, cw_ref, cb_ref, wa_ref, ba_ref, wx_ref, bx_ref, lam_ref, o_ref, *scratch):
    S = x_ref.shape[0]
    steps = S // SUBLANES
    pitch = steps + SUBLANES
    nblk = x_ref.shape[1] // LANES
    per_blk = len(scratch) // nblk
    sub = lax.broadcasted_iota(jnp.int32, (SUBLANES, LANES), 0)

    for blk in range(nblk):
        slab, fold_x, a_buf, u_buf, p_buf = scratch[blk * per_blk:(blk + 1) * per_blk]
        lanes = slice(blk * LANES, (blk + 1) * LANES)

        for s in range(SUBLANES):
            slab[s * pitch:s * pitch + steps, :] = x_ref[s * steps:(s + 1) * steps, lanes].astype(F32)

        def fold(step, carry):
            fold_x[pl.ds(pl.multiple_of(step * SUBLANES, SUBLANES), SUBLANES), :] = (
                slab[pl.ds(step, SUBLANES, stride=pitch), :])
            return carry

        lax.fori_loop(0, steps, fold, 0, unroll=8)
        xf = fold_x[...]

        def head(j):
            tail = xf[(steps - j) * SUBLANES:, :]
            vregs = [jnp.where(sub == 0, 0.0, pltpu.roll(tail[v * SUBLANES:(v + 1) * SUBLANES, :], 1, 0))
                     for v in range(j)]
            return jnp.concatenate(vregs + [xf[:(steps - j) * SUBLANES, :]], axis=0)

        xc = cb_ref[:, lanes] + cw_ref[CONV_W - 1:CONV_W, lanes] * xf
        for j in range(1, CONV_W):
            xc = xc + cw_ref[CONV_W - 1 - j:CONV_W - j, lanes] * head(j)

        xcb = xc.astype(BF16)
        r = jax.nn.sigmoid(jnp.dot(xcb, wa_ref[blk], preferred_element_type=F32) + ba_ref[:, lanes])
        gi = jax.nn.sigmoid(jnp.dot(xcb, wx_ref[blk], preferred_element_type=F32) + bx_ref[:, lanes])
        neg_lam = -lam_ref[:, lanes]
        softplus = jnp.maximum(neg_lam, 0.0) + jnp.log1p(jnp.exp(-jnp.abs(neg_lam)))
        a = jnp.exp2(r * ((-LRU_C * math.log2(math.e)) * softplus))
        gap = 1.0 - a * a
        mult = jnp.where(gap > 0.0, gap * lax.rsqrt(gap), 0.0)
        a_buf[...] = a
        u_buf[...] = mult * gi * xc

    def scan(step, carry):
        rows = pl.ds(pl.multiple_of(step * SUBLANES, SUBLANES), SUBLANES)
        out = []
        for blk in range(nblk):
            _, _, a_buf, u_buf, p_buf = scratch[blk * per_blk:(blk + 1) * per_blk]
            h, p = carry[blk]
            a = a_buf[rows, :]
            h = a * h + u_buf[rows, :]
            p = a * p
            u_buf[rows, :] = h
            p_buf[rows, :] = p
            out.append((h, p))
        return tuple(out)

    ones = jnp.ones((SUBLANES, LANES), F32)
    last = lax.fori_loop(0, steps, scan, ((jnp.zeros_like(ones), ones),) * nblk, unroll=8)

    for blk in range(nblk):
        slab, fold_x, a_buf, u_buf, p_buf = scratch[blk * per_blk:(blk + 1) * per_blk]
        lanes = slice(blk * LANES, (blk + 1) * LANES)
        h_end, p_end = last[blk]
        h_in = [jnp.zeros((1, LANES), F32)]
        for s in range(SUBLANES - 1):
            h_in.append(h_end[s:s + 1, :] + p_end[s:s + 1, :] * h_in[s])
        h_in = jnp.concatenate(h_in, axis=0)
        u_buf[...] = u_buf[...] + p_buf[...] * jnp.concatenate([h_in] * steps, axis=0)

        def unfold(step, carry):
            slab[pl.ds(step, SUBLANES, stride=pitch), :] = (
                u_buf[pl.ds(pl.multiple_of(step * SUBLANES, SUBLANES), SUBLANES), :])
            return carry

        lax.fori_loop(0, steps, unfold, 0, unroll=8)
        for s in range(SUBLANES):
            rows = slice(s * steps, (s + 1) * steps)
            h = slab[s * pitch:s * pitch + steps, :]
            o_ref[rows, lanes] = (_gelu_tanh(g_ref[rows, lanes].astype(F32)) * h).astype(BF16)


def _rg_lru_branch(proj, seq, conv_w, conv_b, wa2, ba, wx2, bx, lam):
    T = proj.shape[0]
    C = conv_w.shape[1]
    B = T // seq
    nblk = LRU_LANE_BLOCKS
    W = nblk * LANES
    assert PROJ_COL["x_rnn"] == 0 and PROJ_COL["g_rnn"] % W == 0
    g_off = PROJ_COL["g_rnn"] // W
    steps = seq // SUBLANES
    assert seq % (SUBLANES * SUBLANES) == 0 and C % W == 0
    vec = lambda v: v.reshape(1, C)
    cols = lambda rows: pl.BlockSpec((rows, W), lambda b, n: (0, n))
    per_blk = [pltpu.VMEM((SUBLANES * (steps + SUBLANES), LANES), F32)] + [pltpu.VMEM((seq, LANES), F32)] * 4
    return pl.pallas_call(
        _rglru_kernel,
        grid=(B, C // W),
        in_specs=[pl.BlockSpec((seq, W), lambda b, n: (b, n)),
                  pl.BlockSpec((seq, W), lambda b, n: (b, g_off + n)),
                  cols(CONV_W), cols(1),
                  pl.BlockSpec((nblk, LANES, LANES), lambda b, n: (n, 0, 0)), cols(1),
                  pl.BlockSpec((nblk, LANES, LANES), lambda b, n: (n, 0, 0)), cols(1), cols(1)],
        out_specs=pl.BlockSpec((seq, W), lambda b, n: (b, n)),
        out_shape=jax.ShapeDtypeStruct((T, C), BF16),
        scratch_shapes=per_blk * nblk,
        compiler_params=_cparams("parallel", "parallel"),
        name="rg_lru",
    )(proj, proj, conv_w, vec(conv_b), wa2, vec(ba), wx2, vec(bx), vec(lam))


def _pair_block_diag(w):
    n, d, _ = w.shape
    z = jnp.zeros((n // 2, 2, d, 2, d), w.dtype)
    z = z.at[:, 0, :, 0, :].set(w[0::2]).at[:, 1, :, 1, :].set(w[1::2])
    return z.reshape(n // 2, 2 * d, 2 * d).astype(BF16)


ATTN_TILE = 256


def _lane_tile(x, n):
    return jnp.concatenate([x] * n, axis=1)


def _softmax_update(s, state, v_aug):
    row_max = jnp.max(s, axis=-1, keepdims=True)
    if state is None:
        m_new = jnp.broadcast_to(row_max, (s.shape[0], LANES))
        p = jnp.exp2(s - _lane_tile(m_new, s.shape[1] // LANES)).astype(BF16)
        return m_new, jnp.dot(p, v_aug, preferred_element_type=F32)
    m_old, acc = state
    m_new = jnp.maximum(m_old, row_max)
    p = jnp.exp2(s - _lane_tile(m_new, s.shape[1] // LANES)).astype(BF16)
    alpha = _lane_tile(jnp.exp2(m_old - m_new), acc.shape[1] // LANES)
    return m_new, alpha * acc + jnp.dot(p, v_aug, preferred_element_type=F32)


def _softmax_finish(state):
    _, acc = state
    return acc[:, :LANES] / acc[:, LANES:]


def _state_rows(state, start, stop):
    m, acc = state
    return m[start:stop], acc[start:stop]


def _causal_mask(t):
    return lax.broadcasted_iota(jnp.int32, (t, t), 1) <= lax.broadcasted_iota(jnp.int32, (t, t), 0)


def _mask_diagonal(s, causal):
    t = causal.shape[0]
    if s.shape[0] == t:
        return jnp.where(causal, s, NEG)
    return jnp.concatenate([jnp.where(causal, s[:t], NEG), s[t:]], axis=0)


def _with_ones(v):
    return jnp.concatenate([v, jnp.ones_like(v)], axis=1)


DIFF_HEADS_PER_STEP = 2


def _diffattn_kernel(q_ref, k_ref, v_ref, lam_ref, g_ref, o_ref, *, lam_init):
    for head in range(q_ref.shape[1] // LANES):
        lanes = pl.ds(head * LANES, LANES)
        _diffattn_head(q_ref.at[:, lanes], k_ref.at[:, lanes], v_ref.at[:, lanes], lam_ref, g_ref,
                       o_ref.at[:, lanes], lam_init=lam_init)


def _diffattn_head(q_ref, k_ref, v_ref, lam_ref, g_ref, o_ref, *, lam_init):
    t = ATTN_TILE
    S = q_ref.shape[0]
    q = q_ref[...]
    lane = lax.broadcasted_iota(jnp.int32, q.shape, 1)
    zero = jnp.zeros_like(q)
    q1 = jnp.where(lane < HEAD_DIM, q, zero)
    q2 = jnp.where(lane < HEAD_DIM, zero, q)
    causal = _causal_mask(t)
    lp = lam_ref[...]
    lam = (jnp.exp(jnp.sum(lp[0:1] * lp[1:2], axis=-1, keepdims=True))
           - jnp.exp(jnp.sum(lp[2:3] * lp[3:4], axis=-1, keepdims=True)) + lam_init)
    st1 = st2 = None
    for j in range(S // t):
        k = k_ref[j * t:(j + 1) * t, :]
        v_aug = _with_ones(v_ref[j * t:(j + 1) * t, :])
        st1 = _softmax_update(_mask_diagonal(_dot_nt(q1[j * t:, :], k), causal), st1, v_aug)
        st2 = _softmax_update(_mask_diagonal(_dot_nt(q2[j * t:, :], k), causal), st2, v_aug)
        o = _softmax_finish(_state_rows(st1, 0, t)) - lam * _softmax_finish(_state_rows(st2, 0, t))
        ms = jnp.mean(o * o, axis=-1, keepdims=True)
        o_ref[j * t:(j + 1) * t, :] = (o * lax.rsqrt(ms + EPS) * g_ref[...] * (1.0 - lam_init)).astype(BF16)
        if (j + 1) * t < S:
            st1 = _state_rows(st1, t, None)
            st2 = _state_rows(st2, t, None)


def _diff_attention(qk, proj, seq, lam_params, subln_g, lam_init):
    T = qk.shape[0]
    B = T // seq
    H = DIFF_HEADS
    P = DIFF_HEADS_PER_STEP
    W = P * LANES
    assert H % P == 0 and all(col % W == 0 for col in (QK_COL["dq"], QK_COL["dk"], PROJ_COL["dv"]))
    return pl.pallas_call(
        functools.partial(_diffattn_kernel, lam_init=lam_init),
        grid=(B, H // P),
        in_specs=[pl.BlockSpec((seq, W), lambda b, h: (b, QK_COL["dq"] // W + h)),
                  pl.BlockSpec((seq, W), lambda b, h: (b, QK_COL["dk"] // W + h)),
                  pl.BlockSpec((seq, W), lambda b, h: (b, PROJ_COL["dv"] // W + h)),
                  pl.BlockSpec((4, HEAD_DIM), lambda b, h: (0, 0)),
                  pl.BlockSpec((1, LANES), lambda b, h: (0, 0))],
        out_specs=pl.BlockSpec((seq, W), lambda b, h: (b, h)),
        out_shape=jax.ShapeDtypeStruct((T, H * LANES), BF16),
        compiler_params=_cparams("parallel", "parallel"),
        name="diff_attn",
    )(qk, qk, proj, lam_params, subln_g.reshape(1, LANES))


MOBA_PAIRS_PER_STEP = 2


def _moba_kernel(q_ref, k_ref, v_ref, o_ref, kmean):
    for pair in range(q_ref.shape[1] // LANES):
        lanes = pl.ds(pair * LANES, LANES)
        _moba_pair(q_ref.at[:, lanes], k_ref.at[:, lanes], v_ref.at[:, lanes], o_ref.at[:, lanes], kmean.at[pair])


def _moba_pair(q_ref, k_ref, v_ref, o_ref, kmean):
    t = ATTN_TILE
    S = q_ref.shape[0]
    nb = S // MOBA_BLOCK
    nb8 = -(-nb // 8) * 8
    lane_row = lax.broadcasted_iota(jnp.int32, (1, LANES), 1)

    kmean[...] = jnp.zeros_like(kmean)
    for blk in range(nb):
        mean = jnp.mean(k_ref[blk * MOBA_BLOCK:(blk + 1) * MOBA_BLOCK, :].astype(F32), axis=0, keepdims=True)
        kmean[blk:blk + 1, :] = jnp.where(lane_row < HEAD_DIM, mean, 0.0)
        kmean[nb8 + blk:nb8 + blk + 1, :] = jnp.where(lane_row < HEAD_DIM, 0.0, mean)

    q = q_ref[...]
    lane = lax.broadcasted_iota(jnp.int32, q.shape, 1)
    zero = jnp.zeros_like(q)
    q_a = jnp.where(lane < HEAD_DIM, q, zero)
    q_b = jnp.where(lane < HEAD_DIM, zero, q)

    km = kmean[...]
    km_hi = km.astype(BF16)
    km_mid = (km - km_hi.astype(F32)).astype(BF16)
    km_lo = (km - km_hi.astype(F32) - km_mid.astype(F32)).astype(BF16)
    gate = _dot_nt(km_hi, q) + _dot_nt(km_mid, q) + _dot_nt(km_lo, q)
    gate_blk = lax.broadcasted_iota(jnp.int32, (nb8, S), 0)
    own = lax.broadcasted_iota(jnp.int32, (nb8, S), 1) // MOBA_BLOCK
    past = gate_blk < own
    unused_rows = jnp.full((HEAD_DIM - nb8, S), NEG, F32)
    bias_rows = []
    for head in range(2):
        g = jnp.where(past, gate[head * nb8:(head + 1) * nb8, :], NEG)
        rank = jnp.zeros(g.shape, jnp.int32)
        for jb in range(nb - 1):
            other = g[jb:jb + 1, :]
            beats = (other > g) | ((other == g) & (jb < gate_blk))
            rank = rank + beats.astype(jnp.int32)
        allowed = (past & (rank < MOBA_TOPK)) | (gate_blk == own)
        bias_rows += [jnp.where(allowed, 0.0, NEG), unused_rows]
    bias = jnp.concatenate(bias_rows, axis=0).T.astype(BF16)
    qa_aug = jnp.concatenate([q_a, bias], axis=1)
    qb_aug = jnp.concatenate([q_b, bias], axis=1)

    causal = _causal_mask(t)
    key_lane = lax.broadcasted_iota(jnp.int32, (t, LANES), 1)
    key_row = lax.broadcasted_iota(jnp.int32, (t, LANES), 0)
    st_a = st_b = None
    for j in range(S // t):
        k = k_ref[j * t:(j + 1) * t, :]
        key_blk = (j * t + key_row) // MOBA_BLOCK
        ka_aug = jnp.concatenate([k, (key_lane == key_blk).astype(BF16)], axis=1)
        kb_aug = jnp.concatenate([k, (key_lane == key_blk + HEAD_DIM).astype(BF16)], axis=1)
        v_aug = _with_ones(v_ref[j * t:(j + 1) * t, :])
        st_a = _softmax_update(_mask_diagonal(_dot_nt(qa_aug[j * t:, :], ka_aug), causal), st_a, v_aug)
        st_b = _softmax_update(_mask_diagonal(_dot_nt(qb_aug[j * t:, :], kb_aug), causal), st_b, v_aug)
        o = jnp.where(key_lane < HEAD_DIM, _softmax_finish(_state_rows(st_a, 0, t)),
                      _softmax_finish(_state_rows(st_b, 0, t)))
        o_ref[j * t:(j + 1) * t, :] = o.astype(BF16)
        if (j + 1) * t < S:
            st_a = _state_rows(st_a, t, None)
            st_b = _state_rows(st_b, t, None)


def _moba_attention(qk, proj, seq):
    T = qk.shape[0]
    B = T // seq
    HP = MOBA_HEADS // 2
    P = MOBA_PAIRS_PER_STEP
    W = P * LANES
    assert HP % P == 0 and all(col % W == 0 for col in (QK_COL["mq"], QK_COL["mk"], PROJ_COL["mv"]))
    gate_rows = 2 * SUBLANES * pl.cdiv(seq // MOBA_BLOCK, SUBLANES)
    return pl.pallas_call(
        _moba_kernel,
        grid=(B, HP // P),
        in_specs=[pl.BlockSpec((seq, W), lambda b, h: (b, QK_COL["mq"] // W + h)),
                  pl.BlockSpec((seq, W), lambda b, h: (b, QK_COL["mk"] // W + h)),
                  pl.BlockSpec((seq, W), lambda b, h: (b, PROJ_COL["mv"] // W + h))],
        out_specs=pl.BlockSpec((seq, W), lambda b, h: (b, h)),
        out_shape=jax.ShapeDtypeStruct((T, HP * LANES), BF16),
        scratch_shapes=[pltpu.VMEM((P, gate_rows, LANES), F32)],
        compiler_params=_cparams("parallel", "parallel"),
        name="moba_attn",
    )(qk, qk, proj)


def _merge_kernel(x_ref, ya_ref, yb_ref, yc_ref, ga_ref, gbb_ref, gc_ref, gb_ref, mod_ref,
                  wa_ref, wb_ref, wc_ref, wo_ref, o_ref):
    merged = None
    branches = ((ya_ref, wa_ref, ga_ref), (yb_ref, wb_ref, gbb_ref), (yc_ref, wc_ref, gc_ref))
    for n, (y_ref, w_ref, gl_ref) in enumerate(branches):
        gate = jax.nn.sigmoid(gl_ref[...].astype(F32) + gb_ref[n:n + 1, :])
        term = gate * jnp.dot(y_ref[...], w_ref[...], preferred_element_type=F32)
        merged = term if merged is None else merged + term
    mix = jnp.dot(merged.astype(BF16), wo_ref[...], preferred_element_type=F32)
    o_ref[...] = x_ref[...] + mod_ref[0][2:3] * mix


def _merge_out(x2, ya, yb, yc, proj, gate_b, mod, wa, wb, wc, wo, seq):
    T, D = x2.shape
    tm = ROW_TILE
    per_b = seq // tm
    assert PROJ_COL["g_br"] % D == 0
    gbr_block = PROJ_COL["g_br"] // D
    wspec = lambda w: pl.BlockSpec(w.shape, lambda i: (0, 0))
    return pl.pallas_call(
        _merge_kernel,
        grid=(T // tm,),
        in_specs=[pl.BlockSpec((tm, D), lambda i: (i, 0)),
                  pl.BlockSpec((tm, ya.shape[1]), lambda i: (i, 0)),
                  pl.BlockSpec((tm, yb.shape[1]), lambda i: (i, 0)),
                  pl.BlockSpec((tm, yc.shape[1]), lambda i: (i, 0)),
                  pl.BlockSpec((tm, D), lambda i: (i, gbr_block)),
                  pl.BlockSpec((tm, D), lambda i: (i, gbr_block + 1)),
                  pl.BlockSpec((tm, D), lambda i: (i, gbr_block + 2)),
                  pl.BlockSpec((N_BRANCH, D), lambda i: (0, 0)),
                  pl.BlockSpec((1, 6, D), lambda i: (i // per_b, 0, 0)),
                  wspec(wa), wspec(wb), wspec(wc), wspec(wo)],
        out_specs=pl.BlockSpec((tm, D), lambda i: (i, 0)),
        out_shape=jax.ShapeDtypeStruct((T, D), F32),
        compiler_params=_cparams("parallel"),
        name="merge_out",
    )(x2, ya, yb, yc, proj, proj, proj, gate_b, mod, wa, wb, wc, wo)


FF_CHUNK = 256
FF_LOADS = 8


def _stash_weight_slab(k, w1_ref, w3_ref, w2_ref, wb1, wb3, wb2):
    for w_ref, wb in ((w1_ref, wb1), (w3_ref, wb3), (w2_ref, wb2)):
        rows, cols = w_ref.shape[-2:]
        start = pl.multiple_of(k * rows, rows)
        wb[pl.ds(start, rows), :] = w_ref[...].reshape(rows, cols).astype(BF16)


def _swiglu_resident(h, wb1, wb3, wb2):
    acc = None
    for f in range(wb2.shape[0] // FF_CHUNK):
        cols = slice(f * FF_CHUNK, (f + 1) * FF_CHUNK)
        g = jnp.dot(h, wb1[:, cols], preferred_element_type=F32)
        u = jnp.dot(h, wb3[:, cols], preferred_element_type=F32)
        act = (g * jax.nn.sigmoid(g) * u).astype(BF16)
        part = jnp.dot(act, wb2[cols, :], preferred_element_type=F32)
        acc = part if acc is None else acc + part
    return acc


def _ffn_weight_specs(D, F, lead, slab_index):
    assert D % (16 * FF_LOADS) == 0 and F % (16 * FF_LOADS) == 0 and F % FF_CHUNK == 0
    index = lambda *a: slab_index(*a) + (0,)
    return [pl.BlockSpec(lead + (D // FF_LOADS, F), index), pl.BlockSpec(lead + (D // FF_LOADS, F), index),
            pl.BlockSpec(lead + (F // FF_LOADS, D), index)]


def _ffn_kernel(x_ref, mod_ref, g_ref, w1_ref, w3_ref, w2_ref, o_ref, wb1, wb3, wb2):
    s = pl.program_id(0)

    @pl.when(s < FF_LOADS)
    def _():
        _stash_weight_slab(s, w1_ref, w3_ref, w2_ref, wb1, wb3, wb2)

    @pl.when(s >= FF_LOADS)
    def _():
        h = _modulated_norm(x_ref[...], g_ref[...], mod_ref[0], 3).astype(BF16)
        o_ref[...] = x_ref[...] + mod_ref[0][5:6] * _swiglu_resident(h, wb1, wb3, wb2)


def _dense_ffn(x2, mod, g, w1, w3, w2, seq):
    T, D = x2.shape
    F = w1.shape[1]
    tm = ROW_TILE
    per_b = seq // tm
    tile = lambda s: jnp.maximum(s - FF_LOADS, 0)
    return pl.pallas_call(
        _ffn_kernel,
        grid=(FF_LOADS + T // tm,),
        in_specs=[pl.BlockSpec((tm, D), lambda s: (tile(s), 0)),
                  pl.BlockSpec((1, 6, D), lambda s: (tile(s) // per_b, 0, 0)),
                  pl.BlockSpec((1, D), lambda s: (0, 0))]
        + _ffn_weight_specs(D, F, (), lambda s: (jnp.minimum(s, FF_LOADS - 1),)),
        out_specs=pl.BlockSpec((tm, D), lambda s: (tile(s), 0)),
        out_shape=jax.ShapeDtypeStruct((T, D), F32),
        scratch_shapes=[pltpu.VMEM((D, F), BF16), pltpu.VMEM((D, F), BF16), pltpu.VMEM((F, D), BF16)],
        compiler_params=_cparams("arbitrary"),
        name="dense_ffn",
    )(x2, mod, g.reshape(1, D), w1, w3, w2)


def _router_kernel(x_ref, mod_ref, g_ref, rw_ref, rb_ref, h_ref, comb_ref, sel_ref):
    h = _modulated_norm(x_ref[...], g_ref[...], mod_ref[0], 3)
    h_ref[...] = h
    logits = _dot_nt(rw_ref[...], h, precision=HIGHEST) + rb_ref[...]
    E = logits.shape[0]
    eid = lax.broadcasted_iota(jnp.int32, logits.shape, 0)
    v1 = jnp.max(logits, axis=0, keepdims=True)
    i1 = jnp.min(jnp.where(logits == v1, eid, E), axis=0, keepdims=True)
    rest = jnp.where(eid == i1, -jnp.inf, logits)
    v2 = jnp.max(rest, axis=0, keepdims=True)
    i2 = jnp.min(jnp.where(rest == v2, eid, E), axis=0, keepdims=True)
    e2 = jnp.exp(v2 - v1)
    w1 = 1.0 / (1.0 + e2)
    w2 = e2 / (1.0 + e2)
    comb_ref[...] = jnp.where(eid == i1, w1, 0.0) + jnp.where(eid == i2, w2, 0.0)
    sel_ref[...] = ((eid == i1) | (eid == i2)).astype(jnp.int32)


def _router(x2, mod, g, router_w, router_b, seq):
    T, D = x2.shape
    E = router_w.shape[1]
    tm = ROW_TILE
    per_b = seq // tm
    return pl.pallas_call(
        _router_kernel,
        grid=(T // tm,),
        in_specs=[pl.BlockSpec((tm, D), lambda i: (i, 0)),
                  pl.BlockSpec((1, 6, D), lambda i: (i // per_b, 0, 0)),
                  pl.BlockSpec((1, D), lambda i: (0, 0)),
                  pl.BlockSpec((E, D), lambda i: (0, 0)),
                  pl.BlockSpec((E, 1), lambda i: (0, 0))],
        out_specs=[pl.BlockSpec((tm, D), lambda i: (i, 0)),
                   pl.BlockSpec((E, tm), lambda i: (0, i)),
                   pl.BlockSpec((E, tm), lambda i: (0, i))],
        out_shape=[jax.ShapeDtypeStruct((T, D), F32),
                   jax.ShapeDtypeStruct((E, T), F32),
                   jax.ShapeDtypeStruct((E, T), jnp.int32)],
        compiler_params=_cparams("parallel"),
        name="moe_router",
    )(x2, mod, g.reshape(1, D), router_w.T, router_b.reshape(E, 1))


STEP_LOAD, STEP_TILE, STEP_BOTH, STEP_ZERO, STEP_NOOP = 0, 1, 2, 3, 4
MOE_SLOTS = 2


def _moe_kernel(kind_ref, slot_ref, slab_ref, we_ref, wk_ref, t_ref, xs_ref, w1_ref, w3_ref, w2_ref, o_ref,
                wb1, wb3, wb2):
    s = pl.program_id(0)
    kind = kind_ref[s]
    slot = slot_ref[s]

    @pl.when(kind == STEP_LOAD)
    def _():
        _stash_weight_slab(slab_ref[s], w1_ref, w3_ref, w2_ref, wb1.at[slot], wb3.at[slot], wb2.at[slot])

    @pl.when(kind == STEP_TILE)
    def _():
        o_ref[...] = _swiglu_resident(xs_ref[...].astype(BF16), wb1.at[slot], wb3.at[slot], wb2.at[slot])

    for use in range(MOE_SLOTS):
        fill = (use + 1) % MOE_SLOTS

        @pl.when((kind == STEP_BOTH) & (slot == use))
        def _():
            _stash_weight_slab(slab_ref[s], w1_ref, w3_ref, w2_ref, wb1.at[fill], wb3.at[fill], wb2.at[fill])
            o_ref[...] = _swiglu_resident(xs_ref[...].astype(BF16), wb1.at[use], wb3.at[use], wb2.at[use])

    @pl.when(kind == STEP_ZERO)
    def _():
        o_ref[...] = jnp.zeros_like(o_ref)


def _moe_schedule(first_tile, tiles, t0, nt):
    E = tiles.shape[0]
    L = FF_LOADS
    i32 = jnp.int32
    ids = jnp.arange(E, dtype=i32)
    lo = jnp.clip(first_tile - t0, 0, nt)
    cnt = jnp.clip(first_tile + tiles - t0, 0, nt) - lo
    active = cnt > 0
    n_act = jnp.sum(active.astype(i32))
    order = jnp.cumsum(active.astype(i32)) - 1
    pick = ((order[None, :] == ids[:, None]) & active[None, :]).astype(i32)
    act_e = jnp.sum(pick * ids[None, :], axis=1)
    act_cnt = jnp.sum(pick * cnt[None, :], axis=1)
    act_lo = jnp.sum(pick * lo[None, :], axis=1)
    has_next = ids + 1 < n_act
    blk_len = jnp.where(ids < n_act, jnp.maximum(act_cnt, jnp.where(has_next, L, 0)), 0)
    blk_start = L + jnp.cumsum(blk_len) - blk_len
    n_sched = jnp.where(n_act > 0, L + jnp.sum(blk_len), 0)
    n_used = jnp.sum(cnt)

    s = jnp.arange(L * E + nt, dtype=i32)
    k = jnp.clip(jnp.sum(((blk_start[None, :] <= s[:, None]) & (ids[None, :] < n_act)).astype(i32), axis=1) - 1,
                 0, E - 1)
    p = s - blk_start[k]
    in_block = (s >= L) & (s < n_sched)
    first_loads = (s < L) & (n_act > 0)
    tile_step = in_block & (p < act_cnt[k])
    load_step = first_loads | (in_block & has_next[k] & (p < L))
    load_k = jnp.where(first_loads, 0, k + 1)
    zero_step = (s >= n_sched) & (s - n_sched < nt - n_used)

    kind = jnp.where(tile_step & load_step, STEP_BOTH,
                     jnp.where(tile_step, STEP_TILE,
                               jnp.where(load_step, STEP_LOAD, jnp.where(zero_step, STEP_ZERO, STEP_NOOP))))
    slot = jnp.where(tile_step, k, load_k) % MOE_SLOTS
    slab = jnp.clip(jnp.where(first_loads, s, p), 0, L - 1)
    last_load = lax.cummax(jnp.where(load_step, s, 0), axis=0)
    we = act_e[jnp.clip(load_k, 0, E - 1)][last_load]
    wk = slab[last_load]
    own = jnp.where(tile_step, act_lo[k] + p, jnp.where(zero_step, n_used + s - n_sched, nt - 1))
    tile = jnp.clip(lax.cummin(own, axis=0, reverse=True), 0, nt - 1)
    return tuple(a.astype(i32) for a in (kind, slot, slab, we, wk, tile))


def _moe_grouped(xs, first_tile, tiles, w1, w3, w2, tm):
    N, D = xs.shape
    nt = N // tm
    E, _, F = w1.shape
    table = _moe_schedule(first_tile, tiles, 0, nt)
    grid_spec = pltpu.PrefetchScalarGridSpec(
        num_scalar_prefetch=len(table),
        grid=(FF_LOADS * E + nt,),
        in_specs=[pl.BlockSpec((tm, D), lambda s, *pre: (pre[5][s], 0))]
        + _ffn_weight_specs(D, F, (1,), lambda s, *pre: (pre[3][s], pre[4][s])),
        out_specs=pl.BlockSpec((tm, D), lambda s, *pre: (pre[5][s], 0)),
        scratch_shapes=[pltpu.VMEM((MOE_SLOTS, D, F), BF16), pltpu.VMEM((MOE_SLOTS, D, F), BF16),
                        pltpu.VMEM((MOE_SLOTS, F, D), BF16)],
    )
    return pl.pallas_call(
        _moe_kernel,
        grid_spec=grid_spec,
        out_shape=jax.ShapeDtypeStruct((N, D), F32),
        compiler_params=_cparams("arbitrary"),
        name="moe_experts",
    )(*table, xs, w1, w3, w2)


def _combine_kernel(x_ref, y1_ref, y2_ref, w_ref, mod_ref, o_ref):
    w = w_ref[...]
    y = w[:, 0:1] * y1_ref[...] + w[:, 1:2] * y2_ref[...]
    o_ref[...] = x_ref[...] + mod_ref[0][5:6] * y


def _moe_combine(x2, y1, y2, w12, mod, seq):
    T, D = x2.shape
    tm = ROW_TILE
    per_b = seq // tm
    return pl.pallas_call(
        _combine_kernel,
        grid=(T // tm,),
        in_specs=[pl.BlockSpec((tm, D), lambda i: (i, 0)),
                  pl.BlockSpec((tm, D), lambda i: (i, 0)),
                  pl.BlockSpec((tm, D), lambda i: (i, 0)),
                  pl.BlockSpec((tm, 2), lambda i: (i, 0)),
                  pl.BlockSpec((1, 6, D), lambda i: (i // per_b, 0, 0))],
        out_specs=pl.BlockSpec((tm, D), lambda i: (i, 0)),
        out_shape=jax.ShapeDtypeStruct((T, D), F32),
        compiler_params=_cparams("parallel"),
        name="moe_combine",
    )(x2, y1, y2, w12, mod)


def _moe_ffn(x2, mod, g, router_w, router_b, w1, w3, w2, seq):
    T, D = x2.shape
    E = router_w.shape[1]
    tm = ROW_TILE
    h, comb, sel = _router(x2, mod, g, router_w, router_b, seq)

    counts = jnp.sum(sel, axis=1)
    padded = ((counts + tm - 1) // tm) * tm
    group_end = jnp.cumsum(padded)
    group_start = group_end - padded
    rank = jnp.cumsum(sel, axis=1) - sel
    dest = group_start[:, None] + rank
    n_rows = TOP_K * T + E * tm
    n_tiles = n_rows // tm
    tile_start = jnp.arange(n_tiles, dtype=jnp.int32) * tm
    tile_expert = jnp.sum((group_end[None, :] <= tile_start[:, None]).astype(jnp.int32), axis=1)
    tile_expert = jnp.minimum(tile_expert, jnp.max(jnp.where(counts > 0, jnp.arange(E), 0))).astype(jnp.int32)

    eid = jnp.arange(E, dtype=jnp.int32)[:, None]
    e_lo = jnp.min(jnp.where(sel > 0, eid, E), axis=0)
    e_hi = jnp.max(jnp.where(sel > 0, eid, -1), axis=0)
    pick = lambda a, e: jnp.sum(jnp.where(eid == e[None, :], a, 0), axis=0)
    w12 = jnp.stack([pick(comb, e_lo), pick(comb, e_hi)], axis=1)

    tok = jnp.arange(T, dtype=jnp.int32)
    tok_sorted = jnp.sort(jnp.concatenate([e_lo * T + tok, e_hi * T + tok])) % T
    row = jnp.arange(n_rows, dtype=jnp.int32)
    row_expert = jnp.repeat(tile_expert, tm)
    first_sorted = (jnp.cumsum(counts) - counts)[row_expert]
    src = tok_sorted[jnp.clip(first_sorted + row - group_start[row_expert], 0, TOP_K * T - 1)]

    take_rows = lambda a, idx: a.at[idx].get(mode="promise_in_bounds")
    ys = _moe_grouped(take_rows(h, src), group_start // tm, padded // tm, w1, w3, w2, tm)
    y1 = take_rows(ys, pick(dest, e_lo))
    y2 = take_rows(ys, pick(dest, e_hi))
    return _moe_combine(x2, y1, y2, w12, mod, seq)


def kernel(x, c, positions, ada_w, ada_b, norm1_g, norm2_g, w_in, gate_b, conv_w, conv_b, lru_wa, lru_ba,
           lru_wx, lru_bx, lru_lambda, diff_qn, diff_kn, diff_lq1, diff_lk1, diff_lq2, diff_lk2, diff_subln,
           moba_qn, moba_kn, w_br_a, w_br_b, w_br_c, w_out, ffn_w1, ffn_w3, ffn_w2, router_w, router_b,
           moe_w1, moe_w3, moe_w2):
    B, S, D = x.shape
    L = ada_w.shape[0]
    T = B * S
    assert D == D_MODEL and w_in.shape[2] == PROJ_COL["g_br"] + N_BRANCH * D
    assert S % MOBA_BLOCK == 0 and S // MOBA_BLOCK <= 56 and S % ROW_TILE == 0
    x2 = x.reshape(T, D)
    mod_all = _ada_mod(c, ada_w, ada_b).reshape(L, B, 6, D)
    cos, sin = _rope_tables(positions)
    bf = lambda w: w.astype(BF16)
    tile2 = lambda v: jnp.tile(v, LANES // HEAD_DIM)

    for l in range(L):
        mod = mod_all[l]
        lam_init = 0.8 - 0.6 * math.exp(-0.3 * l)
        proj = _in_proj(x2, mod, norm1_g[l], w_in, l, S)
        gains = jnp.stack([tile2(diff_qn[l]), tile2(diff_kn[l]), tile2(moba_qn[l]), tile2(moba_kn[l])])
        qk = _qk_prep(proj, gains, cos, sin)
        y_a = _rg_lru_branch(proj, S, conv_w[l], conv_b[l], _pair_block_diag(lru_wa[l]), lru_ba[l],
                             _pair_block_diag(lru_wx[l]), lru_bx[l], lru_lambda[l])
        lam_params = jnp.stack([diff_lq1[l], diff_lk1[l], diff_lq2[l], diff_lk2[l]])
        y_b = _diff_attention(qk, proj, S, lam_params, diff_subln[l], lam_init)
        y_c = _moba_attention(qk, proj, S)
        x2 = _merge_out(x2, y_a, y_b, y_c, proj, gate_b[l], mod, bf(w_br_a[l]), bf(w_br_b[l]),
                        bf(w_br_c[l]), bf(w_out[l]), S)
        if l % 2 == 0:
            x2 = _dense_ffn(x2, mod, norm2_g[l], ffn_w1[l // 2], ffn_w3[l // 2], ffn_w2[l // 2], S)
        else:
            x2 = _moe_ffn(x2, mod, norm2_g[l], router_w[l // 2], router_b[l // 2], moe_w1[l // 2],
                          moe_w3[l // 2], moe_w2[l // 2], S)
    return x2.reshape(B, S, D)
```

```python
import functools
import math

import jax
import jax.numpy as jnp
from jax import lax
from jax.experimental import pallas as pl
from jax.experimental.pallas import tpu as pltpu

F32 = jnp.float32
BF16 = jnp.bfloat16
HIGHEST = lax.Precision.HIGHEST

HEAD_DIM = 64
ROPE_THETA = 10000.0
RNN_BLOCKS = 16
CONV_W = 4
LRU_C = 8.0
DIFF_HEADS = 4
MOBA_HEADS = 8
MOBA_BLOCK = 256
MOBA_TOPK = 3
N_BRANCH = 3
N_EXPERTS = 8
TOP_K = 2
EPS = 1e-6
NEG = -1e30

LANES = 128
VMEM_LIMIT = 56 * 1024 * 1024

ROW_TILE = 512

D_MODEL = 1024
ATTN_W = DIFF_HEADS * 2 * HEAD_DIM


def _starts(widths):
    out, col = {}, 0
    for name, width in widths:
        out[name] = col
        col += width
    return out


PROJ_COL = _starts((("x_rnn", D_MODEL), ("g_rnn", D_MODEL), ("dq", ATTN_W), ("dk", ATTN_W), ("dv", ATTN_W),
                    ("mq", ATTN_W), ("mk", ATTN_W), ("mv", ATTN_W), ("g_br", N_BRANCH * D_MODEL)))
QK_COL = _starts((("dq", ATTN_W), ("dk", ATTN_W), ("mq", ATTN_W), ("mk", ATTN_W)))


def _cparams(*sem):
    return pltpu.CompilerParams(dimension_semantics=sem, vmem_limit_bytes=VMEM_LIMIT)


def _modulated_norm(x, g, mod, base):
    ms = jnp.mean(x * x, axis=-1, keepdims=True)
    y = x * lax.rsqrt(ms + EPS) * g
    return y * (1.0 + mod[base + 1:base + 2]) + mod[base:base + 1]


def _dot_nt(a, b, **kw):
    return lax.dot_general(a, b, (((1,), (1,)), ((), ())), preferred_element_type=F32, **kw)


def _ada_kernel(c_ref, w_ref, b_ref, o_ref):
    o_ref[0] = jnp.dot(c_ref[...], w_ref[0], preferred_element_type=F32, precision=HIGHEST) + b_ref[0]


def _ada_mod(c, ada_w, ada_b):
    L, D, N = ada_w.shape
    B = c.shape[0]
    tn = N // 2
    return pl.pallas_call(
        _ada_kernel,
        grid=(L, N // tn),
        in_specs=[pl.BlockSpec((B, D), lambda l, j: (0, 0)),
                  pl.BlockSpec((1, D, tn), lambda l, j: (l, 0, j)),
                  pl.BlockSpec((1, 1, tn), lambda l, j: (l, 0, j))],
        out_specs=pl.BlockSpec((1, B, tn), lambda l, j: (l, 0, j)),
        out_shape=jax.ShapeDtypeStruct((L, B, N), F32),
        compiler_params=_cparams("parallel", "parallel"),
        name="ada_mod",
    )(c, ada_w, ada_b.reshape(L, 1, N))


IN_LOADS = 8
IN_COL_CHUNK = 1024


def _inproj_kernel(x_ref, mod_ref, g_ref, w_ref, o_ref, wb):
    s = pl.program_id(0)

    @pl.when(s < IN_LOADS)
    def _():
        rows = w_ref.shape[1]
        wb[pl.ds(pl.multiple_of(s * rows, rows), rows), :] = w_ref[0].astype(BF16)

    @pl.when(s >= IN_LOADS)
    def _():
        h = _modulated_norm(x_ref[...], g_ref[...], mod_ref[0], 0).astype(BF16)
        for c in range(wb.shape[1] // IN_COL_CHUNK):
            cols = slice(c * IN_COL_CHUNK, (c + 1) * IN_COL_CHUNK)
            o_ref[:, cols] = jnp.dot(h, wb[:, cols], preferred_element_type=F32).astype(BF16)


def _in_proj(x2, mod, g, w_in, layer, seq):
    T, D = x2.shape
    N = w_in.shape[2]
    tm = ROW_TILE
    per_b = seq // tm
    assert D % (16 * IN_LOADS) == 0 and N % IN_COL_CHUNK == 0
    tile = lambda s: jnp.maximum(s - IN_LOADS, 0)
    return pl.pallas_call(
        _inproj_kernel,
        grid=(IN_LOADS + T // tm,),
        in_specs=[pl.BlockSpec((tm, D), lambda s: (tile(s), 0)),
                  pl.BlockSpec((1, 6, D), lambda s: (tile(s) // per_b, 0, 0)),
                  pl.BlockSpec((1, D), lambda s: (0, 0)),
                  pl.BlockSpec((1, D // IN_LOADS, N), lambda s: (layer, jnp.minimum(s, IN_LOADS - 1), 0))],
        out_specs=pl.BlockSpec((tm, N), lambda s: (tile(s), 0)),
        out_shape=jax.ShapeDtypeStruct((T, N), BF16),
        scratch_shapes=[pltpu.VMEM((D, N), BF16)],
        compiler_params=_cparams("arbitrary"),
        name="in_proj",
    )(x2, mod, g.reshape(1, D), w_in)


def _rope_kernel(pos_ref, inv_ref, sign_ref, cos_ref, sin_ref):
    ang = pos_ref[...] * inv_ref[...]
    cos_ref[...] = jnp.cos(ang)
    sin_ref[...] = jnp.sin(ang) * sign_ref[...]


def _rope_tables(positions):
    T = positions.size
    pos = positions.reshape(T, 1).astype(F32)
    inv = 1.0 / (ROPE_THETA ** (jnp.arange(0, HEAD_DIM, 2, dtype=F32) / HEAD_DIM))
    half = HEAD_DIM // 2
    inv128 = jnp.tile(inv, LANES // half).reshape(1, LANES)
    sign = jnp.tile(jnp.concatenate([-jnp.ones((half,), F32), jnp.ones((half,), F32)]),
                    LANES // HEAD_DIM).reshape(1, LANES)
    tm = 1024
    return pl.pallas_call(
        _rope_kernel,
        grid=(T // tm,),
        in_specs=[pl.BlockSpec((tm, 1), lambda i: (i, 0)),
                  pl.BlockSpec((1, LANES), lambda i: (0, 0)),
                  pl.BlockSpec((1, LANES), lambda i: (0, 0))],
        out_specs=[pl.BlockSpec((tm, LANES), lambda i: (i, 0))] * 2,
        out_shape=[jax.ShapeDtypeStruct((T, LANES), F32)] * 2,
        compiler_params=_cparams("parallel"),
        name="rope_tables",
    )(pos, inv128, sign)


def _qkprep_kernel(dq_ref, dk_ref, mq_ref, mk_ref, gain_ref, cos_ref, sin_ref, seg_ref, o_ref):
    cos = cos_ref[...]
    sin = sin_ref[...]
    seg = seg_ref[...]
    lane = lax.broadcasted_iota(jnp.int32, cos.shape, 1)
    first_half = (lane % HEAD_DIM) < (HEAD_DIM // 2)
    width = dq_ref.shape[1]
    for gi, ref in enumerate((dq_ref, dk_ref, mq_ref, mk_ref)):
        gain = gain_ref[gi:gi + 1, :]
        for cb in range(width // LANES):
            x = ref[:, cb * LANES:(cb + 1) * LANES].astype(F32)
            sq = x * x
            sq_hi = sq.astype(BF16)
            sq_lo = (sq - sq_hi.astype(F32)).astype(BF16)
            ms = (jnp.dot(sq_hi, seg, preferred_element_type=F32)
                  + jnp.dot(sq_lo, seg, preferred_element_type=F32))
            y = x * lax.rsqrt(ms + EPS) * gain
            swapped = jnp.where(first_half, pltpu.roll(y, LANES - HEAD_DIM // 2, 1),
                                pltpu.roll(y, HEAD_DIM // 2, 1))
            r = y * cos + swapped * sin
            if gi % 2 == 0:
                r = r * (math.log2(math.e) / math.sqrt(HEAD_DIM))
            col = gi * width + cb * LANES
            o_ref[:, col:col + LANES] = r.astype(BF16)


def _qk_prep(proj, gains, cos, sin):
    T = proj.shape[0]
    width = ATTN_W
    seg = jnp.kron(jnp.eye(LANES // HEAD_DIM, dtype=F32),
                   jnp.full((HEAD_DIM, HEAD_DIM), 1.0 / HEAD_DIM, F32)).astype(BF16)
    tm = 2 * ROW_TILE
    col_blocks = [PROJ_COL[name] // width for name in QK_COL]
    in_specs = [pl.BlockSpec((tm, width), functools.partial(lambda i, c: (i, c), c=c)) for c in col_blocks]
    in_specs += [pl.BlockSpec((4, LANES), lambda i: (0, 0)),
                 pl.BlockSpec((tm, LANES), lambda i: (i, 0)),
                 pl.BlockSpec((tm, LANES), lambda i: (i, 0)),
                 pl.BlockSpec((LANES, LANES), lambda i: (0, 0))]
    return pl.pallas_call(
        _qkprep_kernel,
        grid=(T // tm,),
        in_specs=in_specs,
        out_specs=pl.BlockSpec((tm, 4 * width), lambda i: (i, 0)),
        out_shape=jax.ShapeDtypeStruct((T, 4 * width), BF16),
        compiler_params=_cparams("parallel"),
        name="qk_prep",
    )(proj, proj, proj, proj, gains, cos, sin, seg)


def _gelu_tanh(x):
    return 0.5 * x * (1.0 + jnp.tanh(math.sqrt(2.0 / math.pi) * (x + 0.044715 * x * x * x)))


SUBLANES = 8
LRU_LANE_BLOCKS = 4


def _rglru_kernel(x_ref, g_ref, cw_ref, cb_ref, wa_ref, ba_ref, wx_ref, bx_ref, lam_ref, o_ref, *scratch):
    S = x_ref.shape[0]
    steps = S // SUBLANES
    pitch = steps + SUBLANES
    nblk = x_ref.shape[1] // LANES
    per_blk = len(scratch) // nblk
    sub = lax.broadcasted_iota(jnp.int32, (SUBLANES, LANES), 0)

    for blk in range(nblk):
        slab, fold_x, a_buf, u_buf, p_buf = scratch[blk * per_blk:(blk + 1) * per_blk]
        lanes = slice(blk * LANES, (blk + 1) * LANES)

        for s in range(SUBLANES):
            slab[s * pitch:s * pitch + steps, :] = x_ref[s * steps:(s + 1) * steps, lanes].astype(F32)

        def fold(step, carry):
            fold_x[pl.ds(pl.multiple_of(step * SUBLANES, SUBLANES), SUBLANES), :] = (
                slab[pl.ds(step, SUBLANES, stride=pitch), :])
            return carry

        lax.fori_loop(0, steps, fold, 0, unroll=8)
        xf = fold_x[...]

        def head(j):
            tail = xf[(steps - j) * SUBLANES:, :]
            vregs = [jnp.where(sub == 0, 0.0, pltpu.roll(tail[v * SUBLANES:(v + 1) * SUBLANES, :], 1, 0))
                     for v in range(j)]
            return jnp.concatenate(vregs + [xf[:(steps - j) * SUBLANES, :]], axis=0)

        xc = cb_ref[:, lanes] + cw_ref[CONV_W - 1:CONV_W, lanes] * xf
        for j in range(1, CONV_W):
            xc = xc + cw_ref[CONV_W - 1 - j:CONV_W - j, lanes] * head(j)

        xcb = xc.astype(BF16)
        r = jax.nn.sigmoid(jnp.dot(xcb, wa_ref[blk], preferred_element_type=F32) + ba_ref[:, lanes])
        gi = jax.nn.sigmoid(jnp.dot(xcb, wx_ref[blk], preferred_element_type=F32) + bx_ref[:, lanes])
        neg_lam = -lam_ref[:, lanes]
        softplus = jnp.maximum(neg_lam, 0.0) + jnp.log1p(jnp.exp(-jnp.abs(neg_lam)))
        a = jnp.exp2(r * ((-LRU_C * math.log2(math.e)) * softplus))
        gap = 1.0 - a * a
        mult = jnp.where(gap > 0.0, gap * lax.rsqrt(gap), 0.0)
        a_buf[...] = a
        u_buf[...] = mult * gi * xc

    def scan(step, carry):
        rows = pl.ds(pl.multiple_of(step * SUBLANES, SUBLANES), SUBLANES)
        out = []
        for blk in range(nblk):
            _, _, a_buf, u_buf, p_buf = scratch[blk * per_blk:(blk + 1) * per_blk]
            h, p = carry[blk]
            a = a_buf[rows, :]
            h = a * h + u_buf[rows, :]
            p = a * p
            u_buf[rows, :] = h
            p_buf[rows, :] = p
            out.append((h, p))
        return tuple(out)

    ones = jnp.ones((SUBLANES, LANES), F32)
    last = lax.fori_loop(0, steps, scan, ((jnp.zeros_like(ones), ones),) * nblk, unroll=8)

    for blk in range(nblk):
        slab, fold_x, a_buf, u_buf, p_buf = scratch[blk * per_blk:(blk + 1) * per_blk]
        lanes = slice(blk * LANES, (blk + 1) * LANES)
        h_end, p_end = last[blk]
        h_in = [jnp.zeros((1, LANES), F32)]
        for s in range(SUBLANES - 1):
            h_in.append(h_end[s:s + 1, :] + p_end[s:s + 1, :] * h_in[s])
        h_in = jnp.concatenate(h_in, axis=0)
        u_buf[...] = u_buf[...] + p_buf[...] * jnp.concatenate([h_in] * steps, axis=0)

        def unfold(step, carry):
            slab[pl.ds(step, SUBLANES, stride=pitch), :] = (
                u_buf[pl.ds(pl.multiple_of(step * SUBLANES, SUBLANES), SUBLANES), :])
            return carry

        lax.fori_loop(0, steps, unfold, 0, unroll=8)
        for s in range(SUBLANES):
            rows = slice(s * steps, (s + 1) * steps)
            h = slab[s * pitch:s * pitch + steps, :]
            o_ref[rows, lanes] = (_gelu_tanh(g_ref[rows, lanes].astype(F32)) * h).astype(BF16)


def _rg_lru_branch(proj, seq, conv_w, conv_b, wa2, ba, wx2, bx, lam):
    T = proj.shape[0]
    C = conv_w.shape[1]
    B = T // seq
    nblk = LRU_LANE_BLOCKS
    W = nblk * LANES
    assert PROJ_COL["x_rnn"] == 0 and PROJ_COL["g_rnn"] % W == 0
    g_off = PROJ_COL["g_rnn"] // W
    steps = seq // SUBLANES
    assert seq % (SUBLANES * SUBLANES) == 0 and C % W == 0
    vec = lambda v: v.reshape(1, C)
    cols = lambda rows: pl.BlockSpec((rows, W), lambda b, n: (0, n))
    per_blk = [pltpu.VMEM((SUBLANES * (steps + SUBLANES), LANES), F32)] + [pltpu.VMEM((seq, LANES), F32)] * 4
    return pl.pallas_call(
        _rglru_kernel,
        grid=(B, C // W),
        in_specs=[pl.BlockSpec((seq, W), lambda b, n: (b, n)),
                  pl.BlockSpec((seq, W), lambda b, n: (b, g_off + n)),
                  cols(CONV_W), cols(1),
                  pl.BlockSpec((nblk, LANES, LANES), lambda b, n: (n, 0, 0)), cols(1),
                  pl.BlockSpec((nblk, LANES, LANES), lambda b, n: (n, 0, 0)), cols(1), cols(1)],
        out_specs=pl.BlockSpec((seq, W), lambda b, n: (b, n)),
        out_shape=jax.ShapeDtypeStruct((T, C), BF16),
        scratch_shapes=per_blk * nblk,
        compiler_params=_cparams("parallel", "parallel"),
        name="rg_lru",
    )(proj, proj, conv_w, vec(conv_b), wa2, vec(ba), wx2, vec(bx), vec(lam))


def _pair_block_diag(w):
    n, d, _ = w.shape
    z = jnp.zeros((n // 2, 2, d, 2, d), w.dtype)
    z = z.at[:, 0, :, 0, :].set(w[0::2]).at[:, 1, :, 1, :].set(w[1::2])
    return z.reshape(n // 2, 2 * d, 2 * d).astype(BF16)


ATTN_TILE = 256


def _lane_tile(x, n):
    return jnp.concatenate([x] * n, axis=1)


def _softmax_update(s, state, v_aug):
    row_max = jnp.max(s, axis=-1, keepdims=True)
    if state is None:
        m_new = jnp.broadcast_to(row_max, (s.shape[0], LANES))
        p = jnp.exp2(s - _lane_tile(m_new, s.shape[1] // LANES)).astype(BF16)
        return m_new, jnp.dot(p, v_aug, preferred_element_type=F32)
    m_old, acc = state
    m_new = jnp.maximum(m_old, row_max)
    p = jnp.exp2(s - _lane_tile(m_new, s.shape[1] // LANES)).astype(BF16)
    alpha = _lane_tile(jnp.exp2(m_old - m_new), acc.shape[1] // LANES)
    return m_new, alpha * acc + jnp.dot(p, v_aug, preferred_element_type=F32)


def _softmax_finish(state):
    _, acc = state
    return acc[:, :LANES] / acc[:, LANES:]


def _state_rows(state, start, stop):
    m, acc = state
    return m[start:stop], acc[start:stop]


def _causal_mask(t):
    return lax.broadcasted_iota(jnp.int32, (t, t), 1) <= lax.broadcasted_iota(jnp.int32, (t, t), 0)


def _mask_diagonal(s, causal):
    t = causal.shape[0]
    if s.shape[0] == t:
        return jnp.where(causal, s, NEG)
    return jnp.concatenate([jnp.where(causal, s[:t], NEG), s[t:]], axis=0)


def _with_ones(v):
    return jnp.concatenate([v, jnp.ones_like(v)], axis=1)


DIFF_HEADS_PER_STEP = 2


def _diffattn_kernel(q_ref, k_ref, v_ref, lam_ref, g_ref, o_ref, *, lam_init):
    for head in range(q_ref.shape[1] // LANES):
        lanes = pl.ds(head * LANES, LANES)
        _diffattn_head(q_ref.at[:, lanes], k_ref.at[:, lanes], v_ref.at[:, lanes], lam_ref, g_ref,
                       o_ref.at[:, lanes], lam_init=lam_init)


def _diffattn_head(q_ref, k_ref, v_ref, lam_ref, g_ref, o_ref, *, lam_init):
    t = ATTN_TILE
    S = q_ref.shape[0]
    q = q_ref[...]
    lane = lax.broadcasted_iota(jnp.int32, q.shape, 1)
    zero = jnp.zeros_like(q)
    q1 = jnp.where(lane < HEAD_DIM, q, zero)
    q2 = jnp.where(lane < HEAD_DIM, zero, q)
    causal = _causal_mask(t)
    lp = lam_ref[...]
    lam = (jnp.exp(jnp.sum(lp[0:1] * lp[1:2], axis=-1, keepdims=True))
           - jnp.exp(jnp.sum(lp[2:3] * lp[3:4], axis=-1, keepdims=True)) + lam_init)
    st1 = st2 = None
    for j in range(S // t):
        k = k_ref[j * t:(j + 1) * t, :]
        v_aug = _with_ones(v_ref[j * t:(j + 1) * t, :])
        st1 = _softmax_update(_mask_diagonal(_dot_nt(q1[j * t:, :], k), causal), st1, v_aug)
        st2 = _softmax_update(_mask_diagonal(_dot_nt(q2[j * t:, :], k), causal), st2, v_aug)
        o = _softmax_finish(_state_rows(st1, 0, t)) - lam * _softmax_finish(_state_rows(st2, 0, t))
        ms = jnp.mean(o * o, axis=-1, keepdims=True)
        o_ref[j * t:(j + 1) * t, :] = (o * lax.rsqrt(ms + EPS) * g_ref[...] * (1.0 - lam_init)).astype(BF16)
        if (j + 1) * t < S:
            st1 = _state_rows(st1, t, None)
            st2 = _state_rows(st2, t, None)


def _diff_attention(qk, proj, seq, lam_params, subln_g, lam_init):
    T = qk.shape[0]
    B = T // seq
    H = DIFF_HEADS
    P = DIFF_HEADS_PER_STEP
    W = P * LANES
    assert H % P == 0 and all(col % W == 0 for col in (QK_COL["dq"], QK_COL["dk"], PROJ_COL["dv"]))
    return pl.pallas_call(
        functools.partial(_diffattn_kernel, lam_init=lam_init),
        grid=(B, H // P),
        in_specs=[pl.BlockSpec((seq, W), lambda b, h: (b, QK_COL["dq"] // W + h)),
                  pl.BlockSpec((seq, W), lambda b, h: (b, QK_COL["dk"] // W + h)),
                  pl.BlockSpec((seq, W), lambda b, h: (b, PROJ_COL["dv"] // W + h)),
                  pl.BlockSpec((4, HEAD_DIM), lambda b, h: (0, 0)),
                  pl.BlockSpec((1, LANES), lambda b, h: (0, 0))],
        out_specs=pl.BlockSpec((seq, W), lambda b, h: (b, h)),
        out_shape=jax.ShapeDtypeStruct((T, H * LANES), BF16),
        compiler_params=_cparams("parallel", "parallel"),
        name="diff_attn",
    )(qk, qk, proj, lam_params, subln_g.reshape(1, LANES))


MOBA_PAIRS_PER_STEP = 2


def _moba_kernel(q_ref, k_ref, v_ref, o_ref, kmean):
    for pair in range(q_ref.shape[1] // LANES):
        lanes = pl.ds(pair * LANES, LANES)
        _moba_pair(q_ref.at[:, lanes], k_ref.at[:, lanes], v_ref.at[:, lanes], o_ref.at[:, lanes], kmean.at[pair])


def _moba_pair(q_ref, k_ref, v_ref, o_ref, kmean):
    t = ATTN_TILE
    S = q_ref.shape[0]
    nb = S // MOBA_BLOCK
    nb8 = -(-nb // 8) * 8
    lane_row = lax.broadcasted_iota(jnp.int32, (1, LANES), 1)

    kmean[...] = jnp.zeros_like(kmean)
    for blk in range(nb):
        mean = jnp.mean(k_ref[blk * MOBA_BLOCK:(blk + 1) * MOBA_BLOCK, :].astype(F32), axis=0, keepdims=True)
        kmean[blk:blk + 1, :] = jnp.where(lane_row < HEAD_DIM, mean, 0.0)
        kmean[nb8 + blk:nb8 + blk + 1, :] = jnp.where(lane_row < HEAD_DIM, 0.0, mean)

    q = q_ref[...]
    lane = lax.broadcasted_iota(jnp.int32, q.shape, 1)
    zero = jnp.zeros_like(q)
    q_a = jnp.where(lane < HEAD_DIM, q, zero)
    q_b = jnp.where(lane < HEAD_DIM, zero, q)

    km = kmean[...]
    km_hi = km.astype(BF16)
    km_mid = (km - km_hi.astype(F32)).astype(BF16)
    km_lo = (km - km_hi.astype(F32) - km_mid.astype(F32)).astype(BF16)
    gate = _dot_nt(km_hi, q) + _dot_nt(km_mid, q) + _dot_nt(km_lo, q)
    gate_blk = lax.broadcasted_iota(jnp.int32, (nb8, S), 0)
    own = lax.broadcasted_iota(jnp.int32, (nb8, S), 1) // MOBA_BLOCK
    past = gate_blk < own
    unused_rows = jnp.full((HEAD_DIM - nb8, S), NEG, F32)
    bias_rows = []
    for head in range(2):
        g = jnp.where(past, gate[head * nb8:(head + 1) * nb8, :], NEG)
        rank = jnp.zeros(g.shape, jnp.int32)
        for jb in range(nb - 1):
            other = g[jb:jb + 1, :]
            beats = (other > g) | ((other == g) & (jb < gate_blk))
            rank = rank + beats.astype(jnp.int32)
        allowed = (past & (rank < MOBA_TOPK)) | (gate_blk == own)
        bias_rows += [jnp.where(allowed, 0.0, NEG), unused_rows]
    bias = jnp.concatenate(bias_rows, axis=0).T.astype(BF16)
    qa_aug = jnp.concatenate([q_a, bias], axis=1)
    qb_aug = jnp.concatenate([q_b, bias], axis=1)

    causal = _causal_mask(t)
    key_lane = lax.broadcasted_iota(jnp.int32, (t, LANES), 1)
    key_row = lax.broadcasted_iota(jnp.int32, (t, LANES), 0)
    st_a = st_b = None
    for j in range(S // t):
        k = k_ref[j * t:(j + 1) * t, :]
        key_blk = (j * t + key_row) // MOBA_BLOCK
        ka_aug = jnp.concatenate([k, (key_lane == key_blk).astype(BF16)], axis=1)
        kb_aug = jnp.concatenate([k, (key_lane == key_blk + HEAD_DIM).astype(BF16)], axis=1)
        v_aug = _with_ones(v_ref[j * t:(j + 1) * t, :])
        st_a = _softmax_update(_mask_diagonal(_dot_nt(qa_aug[j * t:, :], ka_aug), causal), st_a, v_aug)
        st_b = _softmax_update(_mask_diagonal(_dot_nt(qb_aug[j * t:, :], kb_aug), causal), st_b, v_aug)
        o = jnp.where(key_lane < HEAD_DIM, _softmax_finish(_state_rows(st_a, 0, t)),
                      _softmax_finish(_state_rows(st_b, 0, t)))
        o_ref[j * t:(j + 1) * t, :] = o.astype(BF16)
        if (j + 1) * t < S:
            st_a = _state_rows(st_a, t, None)
            st_b = _state_rows(st_b, t, None)


def _moba_attention(qk, proj, seq):
    T = qk.shape[0]
    B = T // seq
    HP = MOBA_HEADS // 2
    P = MOBA_PAIRS_PER_STEP
    W = P * LANES
    assert HP % P == 0 and all(col % W == 0 for col in (QK_COL["mq"], QK_COL["mk"], PROJ_COL["mv"]))
    gate_rows = 2 * SUBLANES * pl.cdiv(seq // MOBA_BLOCK, SUBLANES)
    return pl.pallas_call(
        _moba_kernel,
        grid=(B, HP // P),
        in_specs=[pl.BlockSpec((seq, W), lambda b, h: (b, QK_COL["mq"] // W + h)),
                  pl.BlockSpec((seq, W), lambda b, h: (b, QK_COL["mk"] // W + h)),
                  pl.BlockSpec((seq, W), lambda b, h: (b, PROJ_COL["mv"] // W + h))],
        out_specs=pl.BlockSpec((seq, W), lambda b, h: (b, h)),
        out_shape=jax.ShapeDtypeStruct((T, HP * LANES), BF16),
        scratch_shapes=[pltpu.VMEM((P, gate_rows, LANES), F32)],
        compiler_params=_cparams("parallel", "parallel"),
        name="moba_attn",
    )(qk, qk, proj)


def _merge_kernel(x_ref, ya_ref, yb_ref, yc_ref, ga_ref, gbb_ref, gc_ref, gb_ref, mod_ref,
                  wa_ref, wb_ref, wc_ref, wo_ref, o_ref):
    merged = None
    branches = ((ya_ref, wa_ref, ga_ref), (yb_ref, wb_ref, gbb_ref), (yc_ref, wc_ref, gc_ref))
    for n, (y_ref, w_ref, gl_ref) in enumerate(branches):
        gate = jax.nn.sigmoid(gl_ref[...].astype(F32) + gb_ref[n:n + 1, :])
        term = gate * jnp.dot(y_ref[...], w_ref[...], preferred_element_type=F32)
        merged = term if merged is None else merged + term
    mix = jnp.dot(merged.astype(BF16), wo_ref[...], preferred_element_type=F32)
    o_ref[...] = x_ref[...] + mod_ref[0][2:3] * mix


def _merge_out(x2, ya, yb, yc, proj, gate_b, mod, wa, wb, wc, wo, seq):
    T, D = x2.shape
    tm = 2 * ROW_TILE
    per_b = seq // tm
    assert PROJ_COL["g_br"] % D == 0
    gbr_block = PROJ_COL["g_br"] // D
    wspec = lambda w: pl.BlockSpec(w.shape, lambda i: (0, 0))
    return pl.pallas_call(
        _merge_kernel,
        grid=(T // tm,),
        in_specs=[pl.BlockSpec((tm, D), lambda i: (i, 0)),
                  pl.BlockSpec((tm, ya.shape[1]), lambda i: (i, 0)),
                  pl.BlockSpec((tm, yb.shape[1]), lambda i: (i, 0)),
                  pl.BlockSpec((tm, yc.shape[1]), lambda i: (i, 0)),
                  pl.BlockSpec((tm, D), lambda i: (i, gbr_block)),
                  pl.BlockSpec((tm, D), lambda i: (i, gbr_block + 1)),
                  pl.BlockSpec((tm, D), lambda i: (i, gbr_block + 2)),
                  pl.BlockSpec((N_BRANCH, D), lambda i: (0, 0)),
                  pl.BlockSpec((1, 6, D), lambda i: (i // per_b, 0, 0)),
                  wspec(wa), wspec(wb), wspec(wc), wspec(wo)],
        out_specs=pl.BlockSpec((tm, D), lambda i: (i, 0)),
        out_shape=jax.ShapeDtypeStruct((T, D), F32),
        compiler_params=_cparams("parallel"),
        name="merge_out",
    )(x2, ya, yb, yc, proj, proj, proj, gate_b, mod, wa, wb, wc, wo)


FF_CHUNK = 256
FF_LOADS = 8


def _stash_weight_slab(k, w1_ref, w3_ref, w2_ref, wb1, wb3, wb2):
    for w_ref, wb in ((w1_ref, wb1), (w3_ref, wb3), (w2_ref, wb2)):
        rows, cols = w_ref.shape[-2:]
        start = pl.multiple_of(k * rows, rows)
        wb[pl.ds(start, rows), :] = w_ref[...].reshape(rows, cols).astype(BF16)


def _swiglu_resident(h, wb1, wb3, wb2):
    acc = None
    for f in range(wb2.shape[0] // FF_CHUNK):
        cols = slice(f * FF_CHUNK, (f + 1) * FF_CHUNK)
        g = jnp.dot(h, wb1[:, cols], preferred_element_type=F32)
        u = jnp.dot(h, wb3[:, cols], preferred_element_type=F32)
        act = (g * jax.nn.sigmoid(g) * u).astype(BF16)
        part = jnp.dot(act, wb2[cols, :], preferred_element_type=F32)
        acc = part if acc is None else acc + part
    return acc


def _ffn_weight_specs(D, F, lead, slab_index):
    assert D % (16 * FF_LOADS) == 0 and F % (16 * FF_LOADS) == 0 and F % FF_CHUNK == 0
    index = lambda *a: slab_index(*a) + (0,)
    return [pl.BlockSpec(lead + (D // FF_LOADS, F), index), pl.BlockSpec(lead + (D // FF_LOADS, F), index),
            pl.BlockSpec(lead + (F // FF_LOADS, D), index)]


def _ffn_kernel(x_ref, mod_ref, g_ref, w1_ref, w3_ref, w2_ref, o_ref, wb1, wb3, wb2):
    s = pl.program_id(0)

    @pl.when(s < FF_LOADS)
    def _():
        _stash_weight_slab(s, w1_ref, w3_ref, w2_ref, wb1, wb3, wb2)

    @pl.when(s >= FF_LOADS)
    def _():
        h = _modulated_norm(x_ref[...], g_ref[...], mod_ref[0], 3).astype(BF16)
        o_ref[...] = x_ref[...] + mod_ref[0][5:6] * _swiglu_resident(h, wb1, wb3, wb2)


def _dense_ffn(x2, mod, g, w1, w3, w2, seq):
    T, D = x2.shape
    F = w1.shape[1]
    tm = ROW_TILE
    per_b = seq // tm
    tile = lambda s: jnp.maximum(s - FF_LOADS, 0)
    return pl.pallas_call(
        _ffn_kernel,
        grid=(FF_LOADS + T // tm,),
        in_specs=[pl.BlockSpec((tm, D), lambda s: (tile(s), 0)),
                  pl.BlockSpec((1, 6, D), lambda s: (tile(s) // per_b, 0, 0)),
                  pl.BlockSpec((1, D), lambda s: (0, 0))]
        + _ffn_weight_specs(D, F, (), lambda s: (jnp.minimum(s, FF_LOADS - 1),)),
        out_specs=pl.BlockSpec((tm, D), lambda s: (tile(s), 0)),
        out_shape=jax.ShapeDtypeStruct((T, D), F32),
        scratch_shapes=[pltpu.VMEM((D, F), BF16), pltpu.VMEM((D, F), BF16), pltpu.VMEM((F, D), BF16)],
        compiler_params=_cparams("arbitrary"),
        name="dense_ffn",
    )(x2, mod, g.reshape(1, D), w1, w3, w2)


def _router_kernel(x_ref, mod_ref, g_ref, rw_ref, rb_ref, h_ref, comb_ref, sel_ref):
    h = _modulated_norm(x_ref[...], g_ref[...], mod_ref[0], 3)
    h_ref[...] = h
    logits = _dot_nt(rw_ref[...], h, precision=HIGHEST) + rb_ref[...]
    E = logits.shape[0]
    eid = lax.broadcasted_iota(jnp.int32, logits.shape, 0)
    v1 = jnp.max(logits, axis=0, keepdims=True)
    i1 = jnp.min(jnp.where(logits == v1, eid, E), axis=0, keepdims=True)
    rest = jnp.where(eid == i1, -jnp.inf, logits)
    v2 = jnp.max(rest, axis=0, keepdims=True)
    i2 = jnp.min(jnp.where(rest == v2, eid, E), axis=0, keepdims=True)
    e2 = jnp.exp(v2 - v1)
    w1 = 1.0 / (1.0 + e2)
    w2 = e2 / (1.0 + e2)
    comb_ref[...] = jnp.where(eid == i1, w1, 0.0) + jnp.where(eid == i2, w2, 0.0)
    sel_ref[...] = ((eid == i1) | (eid == i2)).astype(jnp.int32)


def _router(x2, mod, g, router_w, router_b, seq):
    T, D = x2.shape
    E = router_w.shape[1]
    tm = 2 * ROW_TILE
    per_b = seq // tm
    return pl.pallas_call(
        _router_kernel,
        grid=(T // tm,),
        in_specs=[pl.BlockSpec((tm, D), lambda i: (i, 0)),
                  pl.BlockSpec((1, 6, D), lambda i: (i // per_b, 0, 0)),
                  pl.BlockSpec((1, D), lambda i: (0, 0)),
                  pl.BlockSpec((E, D), lambda i: (0, 0)),
                  pl.BlockSpec((E, 1), lambda i: (0, 0))],
        out_specs=[pl.BlockSpec((tm, D), lambda i: (i, 0)),
                   pl.BlockSpec((E, tm), lambda i: (0, i)),
                   pl.BlockSpec((E, tm), lambda i: (0, i))],
        out_shape=[jax.ShapeDtypeStruct((T, D), F32),
                   jax.ShapeDtypeStruct((E, T), F32),
                   jax.ShapeDtypeStruct((E, T), jnp.int32)],
        compiler_params=_cparams("parallel"),
        name="moe_router",
    )(x2, mod, g.reshape(1, D), router_w.T, router_b.reshape(E, 1))


STEP_LOAD, STEP_TILE, STEP_BOTH, STEP_ZERO, STEP_NOOP = 0, 1, 2, 3, 4
MOE_SLOTS = 2


def _moe_kernel(kind_ref, slot_ref, slab_ref, we_ref, wk_ref, t_ref, xs_ref, w1_ref, w3_ref, w2_ref, o_ref,
                wb1, wb3, wb2):
    s = pl.program_id(0)
    kind = kind_ref[s]
    slot = slot_ref[s]

    @pl.when(kind == STEP_LOAD)
    def _():
        _stash_weight_slab(slab_ref[s], w1_ref, w3_ref, w2_ref, wb1.at[slot], wb3.at[slot], wb2.at[slot])

    @pl.when(kind == STEP_TILE)
    def _():
        o_ref[...] = _swiglu_resident(xs_ref[...].astype(BF16), wb1.at[slot], wb3.at[slot], wb2.at[slot])

    for use in range(MOE_SLOTS):
        fill = (use + 1) % MOE_SLOTS

        @pl.when((kind == STEP_BOTH) & (slot == use))
        def _():
            _stash_weight_slab(slab_ref[s], w1_ref, w3_ref, w2_ref, wb1.at[fill], wb3.at[fill], wb2.at[fill])
            o_ref[...] = _swiglu_resident(xs_ref[...].astype(BF16), wb1.at[use], wb3.at[use], wb2.at[use])

    @pl.when(kind == STEP_ZERO)
    def _():
        o_ref[...] = jnp.zeros_like(o_ref)


def _moe_schedule(first_tile, tiles, t0, nt):
    E = tiles.shape[0]
    L = FF_LOADS
    i32 = jnp.int32
    ids = jnp.arange(E, dtype=i32)
    lo = jnp.clip(first_tile - t0, 0, nt)
    cnt = jnp.clip(first_tile + tiles - t0, 0, nt) - lo
    active = cnt > 0
    n_act = jnp.sum(active.astype(i32))
    order = jnp.cumsum(active.astype(i32)) - 1
    pick = ((order[None, :] == ids[:, None]) & active[None, :]).astype(i32)
    act_e = jnp.sum(pick * ids[None, :], axis=1)
    act_cnt = jnp.sum(pick * cnt[None, :], axis=1)
    act_lo = jnp.sum(pick * lo[None, :], axis=1)
    has_next = ids + 1 < n_act
    blk_len = jnp.where(ids < n_act, jnp.maximum(act_cnt, jnp.where(has_next, L, 0)), 0)
    blk_start = L + jnp.cumsum(blk_len) - blk_len
    n_sched = jnp.where(n_act > 0, L + jnp.sum(blk_len), 0)
    n_used = jnp.sum(cnt)

    s = jnp.arange(L * E + nt, dtype=i32)
    k = jnp.clip(jnp.sum(((blk_start[None, :] <= s[:, None]) & (ids[None, :] < n_act)).astype(i32), axis=1) - 1,
                 0, E - 1)
    p = s - blk_start[k]
    in_block = (s >= L) & (s < n_sched)
    first_loads = (s < L) & (n_act > 0)
    tile_step = in_block & (p < act_cnt[k])
    load_step = first_loads | (in_block & has_next[k] & (p < L))
    load_k = jnp.where(first_loads, 0, k + 1)
    zero_step = (s >= n_sched) & (s - n_sched < nt - n_used)

    kind = jnp.where(tile_step & load_step, STEP_BOTH,
                     jnp.where(tile_step, STEP_TILE,
                               jnp.where(load_step, STEP_LOAD, jnp.where(zero_step, STEP_ZERO, STEP_NOOP))))
    slot = jnp.where(tile_step, k, load_k) % MOE_SLOTS
    slab = jnp.clip(jnp.where(first_loads, s, p), 0, L - 1)
    last_load = lax.cummax(jnp.where(load_step, s, 0), axis=0)
    we = act_e[jnp.clip(load_k, 0, E - 1)][last_load]
    wk = slab[last_load]
    own = jnp.where(tile_step, act_lo[k] + p, jnp.where(zero_step, n_used + s - n_sched, nt - 1))
    tile = jnp.clip(lax.cummin(own, axis=0, reverse=True), 0, nt - 1)
    return tuple(a.astype(i32) for a in (kind, slot, slab, we, wk, tile))


def _moe_grouped(xs, first_tile, tiles, w1, w3, w2, tm):
    N, D = xs.shape
    nt = N // tm
    E, _, F = w1.shape
    table = _moe_schedule(first_tile, tiles, 0, nt)
    grid_spec = pltpu.PrefetchScalarGridSpec(
        num_scalar_prefetch=len(table),
        grid=(FF_LOADS * E + nt,),
        in_specs=[pl.BlockSpec((tm, D), lambda s, *pre: (pre[5][s], 0))]
        + _ffn_weight_specs(D, F, (1,), lambda s, *pre: (pre[3][s], pre[4][s])),
        out_specs=pl.BlockSpec((tm, D), lambda s, *pre: (pre[5][s], 0)),
        scratch_shapes=[pltpu.VMEM((MOE_SLOTS, D, F), BF16), pltpu.VMEM((MOE_SLOTS, D, F), BF16),
                        pltpu.VMEM((MOE_SLOTS, F, D), BF16)],
    )
    return pl.pallas_call(
        _moe_kernel,
        grid_spec=grid_spec,
        out_shape=jax.ShapeDtypeStruct((N, D), F32),
        compiler_params=_cparams("arbitrary"),
        name="moe_experts",
    )(*table, xs, w1, w3, w2)


def _combine_kernel(x_ref, y1_ref, y2_ref, w_ref, mod_ref, o_ref):
    w = w_ref[...]
    y = w[:, 0:1] * y1_ref[...] + w[:, 1:2] * y2_ref[...]
    o_ref[...] = x_ref[...] + mod_ref[0][5:6] * y


def _moe_combine(x2, y1, y2, w12, mod, seq):
    T, D = x2.shape
    tm = 2 * ROW_TILE
    per_b = seq // tm
    return pl.pallas_call(
        _combine_kernel,
        grid=(T // tm,),
        in_specs=[pl.BlockSpec((tm, D), lambda i: (i, 0)),
                  pl.BlockSpec((tm, D), lambda i: (i, 0)),
                  pl.BlockSpec((tm, D), lambda i: (i, 0)),
                  pl.BlockSpec((tm, 2), lambda i: (i, 0)),
                  pl.BlockSpec((1, 6, D), lambda i: (i // per_b, 0, 0))],
        out_specs=pl.BlockSpec((tm, D), lambda i: (i, 0)),
        out_shape=jax.ShapeDtypeStruct((T, D), F32),
        compiler_params=_cparams("parallel"),
        name="moe_combine",
    )(x2, y1, y2, w12, mod)


def _moe_ffn(x2, mod, g, router_w, router_b, w1, w3, w2, seq):
    T, D = x2.shape
    E = router_w.shape[1]
    tm = ROW_TILE
    h, comb, sel = _router(x2, mod, g, router_w, router_b, seq)

    counts = jnp.sum(sel, axis=1)
    padded = ((counts + tm - 1) // tm) * tm
    group_end = jnp.cumsum(padded)
    group_start = group_end - padded
    rank = jnp.cumsum(sel, axis=1) - sel
    dest = group_start[:, None] + rank
    n_rows = TOP_K * T + E * tm
    n_tiles = n_rows // tm
    tile_start = jnp.arange(n_tiles, dtype=jnp.int32) * tm
    tile_expert = jnp.sum((group_end[None, :] <= tile_start[:, None]).astype(jnp.int32), axis=1)
    tile_expert = jnp.minimum(tile_expert, jnp.max(jnp.where(counts > 0, jnp.arange(E), 0))).astype(jnp.int32)

    eid = jnp.arange(E, dtype=jnp.int32)[:, None]
    e_lo = jnp.min(jnp.where(sel > 0, eid, E), axis=0)
    e_hi = jnp.max(jnp.where(sel > 0, eid, -1), axis=0)
    pick = lambda a, e: jnp.sum(jnp.where(eid == e[None, :], a, 0), axis=0)
    w12 = jnp.stack([pick(comb, e_lo), pick(comb, e_hi)], axis=1)

    tok = jnp.arange(T, dtype=jnp.int32)
    tok_sorted = jnp.sort(jnp.concatenate([e_lo * T + tok, e_hi * T + tok])) % T
    row = jnp.arange(n_rows, dtype=jnp.int32)
    row_expert = jnp.repeat(tile_expert, tm)
    first_sorted = (jnp.cumsum(counts) - counts)[row_expert]
    src = tok_sorted[jnp.clip(first_sorted + row - group_start[row_expert], 0, TOP_K * T - 1)]

    take_rows = lambda a, idx: a.at[idx].get(mode="promise_in_bounds")
    ys = _moe_grouped(take_rows(h, src), group_start // tm, padded // tm, w1, w3, w2, tm)
    y1 = take_rows(ys, pick(dest, e_lo))
    y2 = take_rows(ys, pick(dest, e_hi))
    return _moe_combine(x2, y1, y2, w12, mod, seq)


def kernel(x, c, positions, ada_w, ada_b, norm1_g, norm2_g, w_in, gate_b, conv_w, conv_b, lru_wa, lru_ba,
           lru_wx, lru_bx, lru_lambda, diff_qn, diff_kn, diff_lq1, diff_lk1, diff_lq2, diff_lk2, diff_subln,
           moba_qn, moba_kn, w_br_a, w_br_b, w_br_c, w_out, ffn_w1, ffn_w3, ffn_w2, router_w, router_b,
           moe_w1, moe_w3, moe_w2):
    B, S, D = x.shape
    L = ada_w.shape[0]
    T = B * S
    assert D == D_MODEL and w_in.shape[2] == PROJ_COL["g_br"] + N_BRANCH * D
    assert S % MOBA_BLOCK == 0 and S // MOBA_BLOCK <= 56 and S % ROW_TILE == 0
    x2 = x.reshape(T, D)
    mod_all = _ada_mod(c, ada_w, ada_b).reshape(L, B, 6, D)
    cos, sin = _rope_tables(positions)
    bf = lambda w: w.astype(BF16)
    tile2 = lambda v: jnp.tile(v, LANES // HEAD_DIM)

    for l in range(L):
        mod = mod_all[l]
        lam_init = 0.8 - 0.6 * math.exp(-0.3 * l)
        proj = _in_proj(x2, mod, norm1_g[l], w_in, l, S)
        gains = jnp.stack([tile2(diff_qn[l]), tile2(diff_kn[l]), tile2(moba_qn[l]), tile2(moba_kn[l])])
        qk = _qk_prep(proj, gains, cos, sin)
        y_a = _rg_lru_branch(proj, S, conv_w[l], conv_b[l], _pair_block_diag(lru_wa[l]), lru_ba[l],
                             _pair_block_diag(lru_wx[l]), lru_bx[l], lru_lambda[l])
        lam_params = jnp.stack([diff_lq1[l], diff_lk1[l], diff_lq2[l], diff_lk2[l]])
        y_b = _diff_attention(qk, proj, S, lam_params, diff_subln[l], lam_init)
        y_c = _moba_attention(qk, proj, S)
        x2 = _merge_out(x2, y_a, y_b, y_c, proj, gate_b[l], mod, bf(w_br_a[l]), bf(w_br_b[l]),
                        bf(w_br_c[l]), bf(w_out[l]), S)
        if l % 2 == 0:
            x2 = _dense_ffn(x2, mod, norm2_g[l], ffn_w1[l // 2], ffn_w3[l // 2], ffn_w2[l // 2], S)
        else:
            x2 = _moe_ffn(x2, mod, norm2_g[l], router_w[l // 2], router_b[l // 2], moe_w1[l // 2],
                          moe_w3[l // 2], moe_w2[l // 2], S)
    return x2.reshape(B, S, D)
```

```python
import functools
import math

import jax
import jax.numpy as jnp
from jax import lax
from jax.experimental import pallas as pl
from jax.experimental.pallas import tpu as pltpu

F32 = jnp.float32
BF16 = jnp.bfloat16
HIGHEST = lax.Precision.HIGHEST

HEAD_DIM = 64
ROPE_THETA = 10000.0
RNN_BLOCKS = 16
CONV_W = 4
LRU_C = 8.0
DIFF_HEADS = 4
MOBA_HEADS = 8
MOBA_BLOCK = 256
MOBA_TOPK = 3
N_BRANCH = 3
N_EXPERTS = 8
TOP_K = 2
EPS = 1e-6
NEG = -1e30

LANES = 128
VMEM_LIMIT = 56 * 1024 * 1024

ROW_TILE = 512

D_MODEL = 1024
ATTN_W = DIFF_HEADS * 2 * HEAD_DIM


def _starts(widths):
    out, col = {}, 0
    for name, width in widths:
        out[name] = col
        col += width
    return out


PROJ_COL = _starts((("x_rnn", D_MODEL), ("g_rnn", D_MODEL), ("dq", ATTN_W), ("dk", ATTN_W), ("dv", ATTN_W),
                    ("mq", ATTN_W), ("mk", ATTN_W), ("mv", ATTN_W), ("g_br", N_BRANCH * D_MODEL)))
QK_COL = _starts((("dq", ATTN_W), ("dk", ATTN_W), ("mq", ATTN_W), ("mk", ATTN_W)))


def _cparams(*sem):
    return pltpu.CompilerParams(dimension_semantics=sem, vmem_limit_bytes=VMEM_LIMIT)


def _modulated_norm(x, g, mod, base):
    ms = jnp.mean(x * x, axis=-1, keepdims=True)
    y = x * lax.rsqrt(ms + EPS) * g
    return y * (1.0 + mod[base + 1:base + 2]) + mod[base:base + 1]


def _dot_nt(a, b, **kw):
    return lax.dot_general(a, b, (((1,), (1,)), ((), ())), preferred_element_type=F32, **kw)


def _ada_kernel(c_ref, w_ref, b_ref, o_ref):
    o_ref[0] = jnp.dot(c_ref[...], w_ref[0], preferred_element_type=F32, precision=HIGHEST) + b_ref[0]


def _ada_mod(c, ada_w, ada_b):
    L, D, N = ada_w.shape
    B = c.shape[0]
    tn = N // 4
    return pl.pallas_call(
        _ada_kernel,
        grid=(L, N // tn),
        in_specs=[pl.BlockSpec((B, D), lambda l, j: (0, 0)),
                  pl.BlockSpec((1, D, tn), lambda l, j: (l, 0, j)),
                  pl.BlockSpec((1, 1, tn), lambda l, j: (l, 0, j))],
        out_specs=pl.BlockSpec((1, B, tn), lambda l, j: (l, 0, j)),
        out_shape=jax.ShapeDtypeStruct((L, B, N), F32),
        compiler_params=_cparams("parallel", "parallel"),
        name="ada_mod",
    )(c, ada_w, ada_b.reshape(L, 1, N))


IN_LOADS = 8
IN_COL_CHUNK = 1024


def _inproj_kernel(x_ref, mod_ref, g_ref, w_ref, o_ref, wb):
    s = pl.program_id(0)

    @pl.when(s < IN_LOADS)
    def _():
        rows = w_ref.shape[1]
        wb[pl.ds(pl.multiple_of(s * rows, rows), rows), :] = w_ref[0].astype(BF16)

    @pl.when(s >= IN_LOADS)
    def _():
        h = _modulated_norm(x_ref[...], g_ref[...], mod_ref[0], 0).astype(BF16)
        for c in range(wb.shape[1] // IN_COL_CHUNK):
            cols = slice(c * IN_COL_CHUNK, (c + 1) * IN_COL_CHUNK)
            o_ref[:, cols] = jnp.dot(h, wb[:, cols], preferred_element_type=F32).astype(BF16)


def _in_proj(x2, mod, g, w_in, layer, seq):
    T, D = x2.shape
    N = w_in.shape[2]
    tm = ROW_TILE
    per_b = seq // tm
    assert D % (16 * IN_LOADS) == 0 and N % IN_COL_CHUNK == 0
    tile = lambda s: jnp.maximum(s - IN_LOADS, 0)
    return pl.pallas_call(
        _inproj_kernel,
        grid=(IN_LOADS + T // tm,),
        in_specs=[pl.BlockSpec((tm, D), lambda s: (tile(s), 0)),
                  pl.BlockSpec((1, 6, D), lambda s: (tile(s) // per_b, 0, 0)),
                  pl.BlockSpec((1, D), lambda s: (0, 0)),
                  pl.BlockSpec((1, D // IN_LOADS, N), lambda s: (layer, jnp.minimum(s, IN_LOADS - 1), 0))],
        out_specs=pl.BlockSpec((tm, N), lambda s: (tile(s), 0)),
        out_shape=jax.ShapeDtypeStruct((T, N), BF16),
        scratch_shapes=[pltpu.VMEM((D, N), BF16)],
        compiler_params=_cparams("arbitrary"),
        name="in_proj",
    )(x2, mod, g.reshape(1, D), w_in)


def _rope_kernel(pos_ref, inv_ref, sign_ref, cos_ref, sin_ref):
    ang = pos_ref[...] * inv_ref[...]
    cos_ref[...] = jnp.cos(ang)
    sin_ref[...] = jnp.sin(ang) * sign_ref[...]


def _rope_tables(positions):
    T = positions.size
    pos = positions.reshape(T, 1).astype(F32)
    inv = 1.0 / (ROPE_THETA ** (jnp.arange(0, HEAD_DIM, 2, dtype=F32) / HEAD_DIM))
    half = HEAD_DIM // 2
    inv128 = jnp.tile(inv, LANES // half).reshape(1, LANES)
    sign = jnp.tile(jnp.concatenate([-jnp.ones((half,), F32), jnp.ones((half,), F32)]),
                    LANES // HEAD_DIM).reshape(1, LANES)
    tm = 1024
    return pl.pallas_call(
        _rope_kernel,
        grid=(T // tm,),
        in_specs=[pl.BlockSpec((tm, 1), lambda i: (i, 0)),
                  pl.BlockSpec((1, LANES), lambda i: (0, 0)),
                  pl.BlockSpec((1, LANES), lambda i: (0, 0))],
        out_specs=[pl.BlockSpec((tm, LANES), lambda i: (i, 0))] * 2,
        out_shape=[jax.ShapeDtypeStruct((T, LANES), F32)] * 2,
        compiler_params=_cparams("parallel"),
        name="rope_tables",
    )(pos, inv128, sign)


def _qkprep_kernel(dq_ref, dk_ref, mq_ref, mk_ref, gain_ref, cos_ref, sin_ref, seg_ref, o_ref):
    cos = cos_ref[...]
    sin = sin_ref[...]
    seg = seg_ref[...]
    lane = lax.broadcasted_iota(jnp.int32, cos.shape, 1)
    first_half = (lane % HEAD_DIM) < (HEAD_DIM // 2)
    width = dq_ref.shape[1]
    for gi, ref in enumerate((dq_ref, dk_ref, mq_ref, mk_ref)):
        gain = gain_ref[gi:gi + 1, :]
        for cb in range(width // LANES):
            x = ref[:, cb * LANES:(cb + 1) * LANES].astype(F32)
            sq = x * x
            sq_hi = sq.astype(BF16)
            sq_lo = (sq - sq_hi.astype(F32)).astype(BF16)
            ms = (jnp.dot(sq_hi, seg, preferred_element_type=F32)
                  + jnp.dot(sq_lo, seg, preferred_element_type=F32))
            y = x * lax.rsqrt(ms + EPS) * gain
            swapped = jnp.where(first_half, pltpu.roll(y, LANES - HEAD_DIM // 2, 1),
                                pltpu.roll(y, HEAD_DIM // 2, 1))
            r = y * cos + swapped * sin
            if gi % 2 == 0:
                r = r * (math.log2(math.e) / math.sqrt(HEAD_DIM))
            col = gi * width + cb * LANES
            o_ref[:, col:col + LANES] = r.astype(BF16)


def _qk_prep(proj, gains, cos, sin):
    T = proj.shape[0]
    width = ATTN_W
    seg = jnp.kron(jnp.eye(LANES // HEAD_DIM, dtype=F32),
                   jnp.full((HEAD_DIM, HEAD_DIM), 1.0 / HEAD_DIM, F32)).astype(BF16)
    tm = 2 * ROW_TILE
    col_blocks = [PROJ_COL[name] // width for name in QK_COL]
    in_specs = [pl.BlockSpec((tm, width), functools.partial(lambda i, c: (i, c), c=c)) for c in col_blocks]
    in_specs += [pl.BlockSpec((4, LANES), lambda i: (0, 0)),
                 pl.BlockSpec((tm, LANES), lambda i: (i, 0)),
                 pl.BlockSpec((tm, LANES), lambda i: (i, 0)),
                 pl.BlockSpec((LANES, LANES), lambda i: (0, 0))]
    return pl.pallas_call(
        _qkprep_kernel,
        grid=(T // tm,),
        in_specs=in_specs,
        out_specs=pl.BlockSpec((tm, 4 * width), lambda i: (i, 0)),
        out_shape=jax.ShapeDtypeStruct((T, 4 * width), BF16),
        compiler_params=_cparams("parallel"),
        name="qk_prep",
    )(proj, proj, proj, proj, gains, cos, sin, seg)


def _gelu_tanh(x):
    return 0.5 * x * (1.0 + jnp.tanh(math.sqrt(2.0 / math.pi) * (x + 0.044715 * x * x * x)))


SUBLANES = 8
LRU_LANE_BLOCKS = 4


def _rglru_kernel(x_ref, g_ref, cw_ref, cb_ref, wa_ref, ba_ref, wx_ref, bx_ref, lam_ref, o_ref, *scratch):
    S = x_ref.shape[0]
    steps = S // SUBLANES
    pitch = steps + SUBLANES
    nblk = x_ref.shape[1] // LANES
    per_blk = len(scratch) // nblk
    sub = lax.broadcasted_iota(jnp.int32, (SUBLANES, LANES), 0)

    for blk in range(nblk):
        slab, fold_x, a_buf, u_buf, p_buf = scratch[blk * per_blk:(blk + 1) * per_blk]
        lanes = slice(blk * LANES, (blk + 1) * LANES)

        for s in range(SUBLANES):
            slab[s * pitch:s * pitch + steps, :] = x_ref[s * steps:(s + 1) * steps, lanes].astype(F32)

        def fold(step, carry):
            fold_x[pl.ds(pl.multiple_of(step * SUBLANES, SUBLANES), SUBLANES), :] = (
                slab[pl.ds(step, SUBLANES, stride=pitch), :])
            return carry

        lax.fori_loop(0, steps, fold, 0, unroll=8)
        xf = fold_x[...]

        def head(j):
            tail = xf[(steps - j) * SUBLANES:, :]
            vregs = [jnp.where(sub == 0, 0.0, pltpu.roll(tail[v * SUBLANES:(v + 1) * SUBLANES, :], 1, 0))
                     for v in range(j)]
            return jnp.concatenate(vregs + [xf[:(steps - j) * SUBLANES, :]], axis=0)

        xc = cb_ref[:, lanes] + cw_ref[CONV_W - 1:CONV_W, lanes] * xf
        for j in range(1, CONV_W):
            xc = xc + cw_ref[CONV_W - 1 - j:CONV_W - j, lanes] * head(j)

        xcb = xc.astype(BF16)
        r = jax.nn.sigmoid(jnp.dot(xcb, wa_ref[blk], preferred_element_type=F32) + ba_ref[:, lanes])
        gi = jax.nn.sigmoid(jnp.dot(xcb, wx_ref[blk], preferred_element_type=F32) + bx_ref[:, lanes])
        neg_lam = -lam_ref[:, lanes]
        softplus = jnp.maximum(neg_lam, 0.0) + jnp.log1p(jnp.exp(-jnp.abs(neg_lam)))
        a = jnp.exp2(r * ((-LRU_C * math.log2(math.e)) * softplus))
        gap = 1.0 - a * a
        mult = jnp.where(gap > 0.0, gap * lax.rsqrt(gap), 0.0)
        a_buf[...] = a
        u_buf[...] = mult * gi * xc

    def scan(step, carry):
        rows = pl.ds(pl.multiple_of(step * SUBLANES, SUBLANES), SUBLANES)
        out = []
        for blk in range(nblk):
            _, _, a_buf, u_buf, p_buf = scratch[blk * per_blk:(blk + 1) * per_blk]
            h, p = carry[blk]
            a = a_buf[rows, :]
            h = a * h + u_buf[rows, :]
            p = a * p
            u_buf[rows, :] = h
            p_buf[rows, :] = p
            out.append((h, p))
        return tuple(out)

    ones = jnp.ones((SUBLANES, LANES), F32)
    last = lax.fori_loop(0, steps, scan, ((jnp.zeros_like(ones), ones),) * nblk, unroll=8)

    for blk in range(nblk):
        slab, fold_x, a_buf, u_buf, p_buf = scratch[blk * per_blk:(blk + 1) * per_blk]
        lanes = slice(blk * LANES, (blk + 1) * LANES)
        h_end, p_end = last[blk]
        h_in = [jnp.zeros((1, LANES), F32)]
        for s in range(SUBLANES - 1):
            h_in.append(h_end[s:s + 1, :] + p_end[s:s + 1, :] * h_in[s])
        h_in = jnp.concatenate(h_in, axis=0)
        u_buf[...] = u_buf[...] + p_buf[...] * jnp.concatenate([h_in] * steps, axis=0)

        def unfold(step, carry):
            slab[pl.ds(step, SUBLANES, stride=pitch), :] = (
                u_buf[pl.ds(pl.multiple_of(step * SUBLANES, SUBLANES), SUBLANES), :])
            return carry

        lax.fori_loop(0, steps, unfold, 0, unroll=8)
        for s in range(SUBLANES):
            rows = slice(s * steps, (s + 1) * steps)
            h = slab[s * pitch:s * pitch + steps, :]
            o_ref[rows, lanes] = (_gelu_tanh(g_ref[rows, lanes].astype(F32)) * h).astype(BF16)


def _rg_lru_branch(proj, seq, conv_w, conv_b, wa2, ba, wx2, bx, lam):
    T = proj.shape[0]
    C = conv_w.shape[1]
    B = T // seq
    nblk = LRU_LANE_BLOCKS
    W = nblk * LANES
    assert PROJ_COL["x_rnn"] == 0 and PROJ_COL["g_rnn"] % W == 0
    g_off = PROJ_COL["g_rnn"] // W
    steps = seq // SUBLANES
    assert seq % (SUBLANES * SUBLANES) == 0 and C % W == 0
    vec = lambda v: v.reshape(1, C)
    cols = lambda rows: pl.BlockSpec((rows, W), lambda b, n: (0, n))
    per_blk = [pltpu.VMEM((SUBLANES * (steps + SUBLANES), LANES), F32)] + [pltpu.VMEM((seq, LANES), F32)] * 4
    return pl.pallas_call(
        _rglru_kernel,
        grid=(B, C // W),
        in_specs=[pl.BlockSpec((seq, W), lambda b, n: (b, n)),
                  pl.BlockSpec((seq, W), lambda b, n: (b, g_off + n)),
                  cols(CONV_W), cols(1),
                  pl.BlockSpec((nblk, LANES, LANES), lambda b, n: (n, 0, 0)), cols(1),
                  pl.BlockSpec((nblk, LANES, LANES), lambda b, n: (n, 0, 0)), cols(1), cols(1)],
        out_specs=pl.BlockSpec((seq, W), lambda b, n: (b, n)),
        out_shape=jax.ShapeDtypeStruct((T, C), BF16),
        scratch_shapes=per_blk * nblk,
        compiler_params=_cparams("parallel", "parallel"),
        name="rg_lru",
    )(proj, proj, conv_w, vec(conv_b), wa2, vec(ba), wx2, vec(bx), vec(lam))


def _pair_block_diag(w):
    n, d, _ = w.shape
    z = jnp.zeros((n // 2, 2, d, 2, d), w.dtype)
    z = z.at[:, 0, :, 0, :].set(w[0::2]).at[:, 1, :, 1, :].set(w[1::2])
    return z.reshape(n // 2, 2 * d, 2 * d).astype(BF16)


ATTN_TILE = 256


def _lane_tile(x, n):
    return jnp.concatenate([x] * n, axis=1)


def _softmax_update(s, state, v_aug):
    row_max = jnp.max(s, axis=-1, keepdims=True)
    if state is None:
        m_new = jnp.broadcast_to(row_max, (s.shape[0], LANES))
        p = jnp.exp2(s - _lane_tile(m_new, s.shape[1] // LANES)).astype(BF16)
        return m_new, jnp.dot(p, v_aug, preferred_element_type=F32)
    m_old, acc = state
    m_new = jnp.maximum(m_old, row_max)
    p = jnp.exp2(s - _lane_tile(m_new, s.shape[1] // LANES)).astype(BF16)
    alpha = _lane_tile(jnp.exp2(m_old - m_new), acc.shape[1] // LANES)
    return m_new, alpha * acc + jnp.dot(p, v_aug, preferred_element_type=F32)


def _softmax_finish(state):
    _, acc = state
    return acc[:, :LANES] / acc[:, LANES:]


def _state_rows(state, start, stop):
    m, acc = state
    return m[start:stop], acc[start:stop]


def _causal_mask(t):
    return lax.broadcasted_iota(jnp.int32, (t, t), 1) <= lax.broadcasted_iota(jnp.int32, (t, t), 0)


def _mask_diagonal(s, causal):
    t = causal.shape[0]
    if s.shape[0] == t:
        return jnp.where(causal, s, NEG)
    return jnp.concatenate([jnp.where(causal, s[:t], NEG), s[t:]], axis=0)


def _with_ones(v):
    return jnp.concatenate([v, jnp.ones_like(v)], axis=1)


DIFF_HEADS_PER_STEP = 2


def _diffattn_kernel(q_ref, k_ref, v_ref, lam_ref, g_ref, o_ref, *, lam_init):
    for head in range(q_ref.shape[1] // LANES):
        lanes = pl.ds(head * LANES, LANES)
        _diffattn_head(q_ref.at[:, lanes], k_ref.at[:, lanes], v_ref.at[:, lanes], lam_ref, g_ref,
                       o_ref.at[:, lanes], lam_init=lam_init)


def _diffattn_head(q_ref, k_ref, v_ref, lam_ref, g_ref, o_ref, *, lam_init):
    t = ATTN_TILE
    S = q_ref.shape[0]
    q = q_ref[...]
    lane = lax.broadcasted_iota(jnp.int32, q.shape, 1)
    zero = jnp.zeros_like(q)
    q1 = jnp.where(lane < HEAD_DIM, q, zero)
    q2 = jnp.where(lane < HEAD_DIM, zero, q)
    causal = _causal_mask(t)
    lp = lam_ref[...]
    lam = (jnp.exp(jnp.sum(lp[0:1] * lp[1:2], axis=-1, keepdims=True))
           - jnp.exp(jnp.sum(lp[2:3] * lp[3:4], axis=-1, keepdims=True)) + lam_init)
    st1 = st2 = None
    for j in range(S // t):
        k = k_ref[j * t:(j + 1) * t, :]
        v_aug = _with_ones(v_ref[j * t:(j + 1) * t, :])
        st1 = _softmax_update(_mask_diagonal(_dot_nt(q1[j * t:, :], k), causal), st1, v_aug)
        st2 = _softmax_update(_mask_diagonal(_dot_nt(q2[j * t:, :], k), causal), st2, v_aug)
        o = _softmax_finish(_state_rows(st1, 0, t)) - lam * _softmax_finish(_state_rows(st2, 0, t))
        ms = jnp.mean(o * o, axis=-1, keepdims=True)
        o_ref[j * t:(j + 1) * t, :] = (o * lax.rsqrt(ms + EPS) * g_ref[...] * (1.0 - lam_init)).astype(BF16)
        if (j + 1) * t < S:
            st1 = _state_rows(st1, t, None)
            st2 = _state_rows(st2, t, None)


def _diff_attention(qk, proj, seq, lam_params, subln_g, lam_init):
    T = qk.shape[0]
    B = T // seq
    H = DIFF_HEADS
    P = DIFF_HEADS_PER_STEP
    W = P * LANES
    assert H % P == 0 and all(col % W == 0 for col in (QK_COL["dq"], QK_COL["dk"], PROJ_COL["dv"]))
    return pl.pallas_call(
        functools.partial(_diffattn_kernel, lam_init=lam_init),
        grid=(B, H // P),
        in_specs=[pl.BlockSpec((seq, W), lambda b, h: (b, QK_COL["dq"] // W + h)),
                  pl.BlockSpec((seq, W), lambda b, h: (b, QK_COL["dk"] // W + h)),
                  pl.BlockSpec((seq, W), lambda b, h: (b, PROJ_COL["dv"] // W + h)),
                  pl.BlockSpec((4, HEAD_DIM), lambda b, h: (0, 0)),
                  pl.BlockSpec((1, LANES), lambda b, h: (0, 0))],
        out_specs=pl.BlockSpec((seq, W), lambda b, h: (b, h)),
        out_shape=jax.ShapeDtypeStruct((T, H * LANES), BF16),
        compiler_params=_cparams("parallel", "parallel"),
        name="diff_attn",
    )(qk, qk, proj, lam_params, subln_g.reshape(1, LANES))


MOBA_PAIRS_PER_STEP = 2


def _moba_kernel(q_ref, k_ref, v_ref, o_ref, kmean):
    for pair in range(q_ref.shape[1] // LANES):
        lanes = pl.ds(pair * LANES, LANES)
        _moba_pair(q_ref.at[:, lanes], k_ref.at[:, lanes], v_ref.at[:, lanes], o_ref.at[:, lanes], kmean.at[pair])


def _moba_pair(q_ref, k_ref, v_ref, o_ref, kmean):
    t = ATTN_TILE
    S = q_ref.shape[0]
    nb = S // MOBA_BLOCK
    nb8 = -(-nb // 8) * 8
    lane_row = lax.broadcasted_iota(jnp.int32, (1, LANES), 1)

    kmean[...] = jnp.zeros_like(kmean)
    for blk in range(nb):
        mean = jnp.mean(k_ref[blk * MOBA_BLOCK:(blk + 1) * MOBA_BLOCK, :].astype(F32), axis=0, keepdims=True)
        kmean[blk:blk + 1, :] = jnp.where(lane_row < HEAD_DIM, mean, 0.0)
        kmean[nb8 + blk:nb8 + blk + 1, :] = jnp.where(lane_row < HEAD_DIM, 0.0, mean)

    q = q_ref[...]
    lane = lax.broadcasted_iota(jnp.int32, q.shape, 1)
    zero = jnp.zeros_like(q)
    q_a = jnp.where(lane < HEAD_DIM, q, zero)
    q_b = jnp.where(lane < HEAD_DIM, zero, q)

    km = kmean[...]
    km_hi = km.astype(BF16)
    km_mid = (km - km_hi.astype(F32)).astype(BF16)
    km_lo = (km - km_hi.astype(F32) - km_mid.astype(F32)).astype(BF16)
    gate = _dot_nt(km_hi, q) + _dot_nt(km_mid, q) + _dot_nt(km_lo, q)
    gate_blk = lax.broadcasted_iota(jnp.int32, (nb8, S), 0)
    own = lax.broadcasted_iota(jnp.int32, (nb8, S), 1) // MOBA_BLOCK
    past = gate_blk < own
    unused_rows = jnp.full((HEAD_DIM - nb8, S), NEG, F32)
    bias_rows = []
    for head in range(2):
        g = jnp.where(past, gate[head * nb8:(head + 1) * nb8, :], NEG)
        rank = jnp.zeros(g.shape, jnp.int32)
        for jb in range(nb - 1):
            other = g[jb:jb + 1, :]
            beats = (other > g) | ((other == g) & (jb < gate_blk))
            rank = rank + beats.astype(jnp.int32)
        allowed = (past & (rank < MOBA_TOPK)) | (gate_blk == own)
        bias_rows += [jnp.where(allowed, 0.0, NEG), unused_rows]
    bias = jnp.concatenate(bias_rows, axis=0).T.astype(BF16)
    qa_aug = jnp.concatenate([q_a, bias], axis=1)
    qb_aug = jnp.concatenate([q_b, bias], axis=1)

    causal = _causal_mask(t)
    key_lane = lax.broadcasted_iota(jnp.int32, (t, LANES), 1)
    key_row = lax.broadcasted_iota(jnp.int32, (t, LANES), 0)
    st_a = st_b = None
    for j in range(S // t):
        k = k_ref[j * t:(j + 1) * t, :]
        key_blk = (j * t + key_row) // MOBA_BLOCK
        ka_aug = jnp.concatenate([k, (key_lane == key_blk).astype(BF16)], axis=1)
        kb_aug = jnp.concatenate([k, (key_lane == key_blk + HEAD_DIM).astype(BF16)], axis=1)
        v_aug = _with_ones(v_ref[j * t:(j + 1) * t, :])
        st_a = _softmax_update(_mask_diagonal(_dot_nt(qa_aug[j * t:, :], ka_aug), causal), st_a, v_aug)
        st_b = _softmax_update(_mask_diagonal(_dot_nt(qb_aug[j * t:, :], kb_aug), causal), st_b, v_aug)
        o = jnp.where(key_lane < HEAD_DIM, _softmax_finish(_state_rows(st_a, 0, t)),
                      _softmax_finish(_state_rows(st_b, 0, t)))
        o_ref[j * t:(j + 1) * t, :] = o.astype(BF16)
        if (j + 1) * t < S:
            st_a = _state_rows(st_a, t, None)
            st_b = _state_rows(st_b, t, None)


def _moba_attention(qk, proj, seq):
    T = qk.shape[0]
    B = T // seq
    HP = MOBA_HEADS // 2
    P = MOBA_PAIRS_PER_STEP
    W = P * LANES
    assert HP % P == 0 and all(col % W == 0 for col in (QK_COL["mq"], QK_COL["mk"], PROJ_COL["mv"]))
    gate_rows = 2 * SUBLANES * pl.cdiv(seq // MOBA_BLOCK, SUBLANES)
    return pl.pallas_call(
        _moba_kernel,
        grid=(B, HP // P),
        in_specs=[pl.BlockSpec((seq, W), lambda b, h: (b, QK_COL["mq"] // W + h)),
                  pl.BlockSpec((seq, W), lambda b, h: (b, QK_COL["mk"] // W + h)),
                  pl.BlockSpec((seq, W), lambda b, h: (b, PROJ_COL["mv"] // W + h))],
        out_specs=pl.BlockSpec((seq, W), lambda b, h: (b, h)),
        out_shape=jax.ShapeDtypeStruct((T, HP * LANES), BF16),
        scratch_shapes=[pltpu.VMEM((P, gate_rows, LANES), F32)],
        compiler_params=_cparams("parallel", "parallel"),
        name="moba_attn",
    )(qk, qk, proj)


def _merge_kernel(x_ref, ya_ref, yb_ref, yc_ref, ga_ref, gbb_ref, gc_ref, gb_ref, mod_ref,
                  wa_ref, wb_ref, wc_ref, wo_ref, o_ref):
    merged = None
    branches = ((ya_ref, wa_ref, ga_ref), (yb_ref, wb_ref, gbb_ref), (yc_ref, wc_ref, gc_ref))
    for n, (y_ref, w_ref, gl_ref) in enumerate(branches):
        gate = jax.nn.sigmoid(gl_ref[...].astype(F32) + gb_ref[n:n + 1, :])
        term = gate * jnp.dot(y_ref[...], w_ref[...], preferred_element_type=F32)
        merged = term if merged is None else merged + term
    mix = jnp.dot(merged.astype(BF16), wo_ref[...], preferred_element_type=F32)
    o_ref[...] = x_ref[...] + mod_ref[0][2:3] * mix


def _merge_out(x2, ya, yb, yc, proj, gate_b, mod, wa, wb, wc, wo, seq):
    T, D = x2.shape
    tm = 2 * ROW_TILE
    per_b = seq // tm
    assert PROJ_COL["g_br"] % D == 0
    gbr_block = PROJ_COL["g_br"] // D
    wspec = lambda w: pl.BlockSpec(w.shape, lambda i: (0, 0))
    return pl.pallas_call(
        _merge_kernel,
        grid=(T // tm,),
        in_specs=[pl.BlockSpec((tm, D), lambda i: (i, 0)),
                  pl.BlockSpec((tm, ya.shape[1]), lambda i: (i, 0)),
                  pl.BlockSpec((tm, yb.shape[1]), lambda i: (i, 0)),
                  pl.BlockSpec((tm, yc.shape[1]), lambda i: (i, 0)),
                  pl.BlockSpec((tm, D), lambda i: (i, gbr_block)),
                  pl.BlockSpec((tm, D), lambda i: (i, gbr_block + 1)),
                  pl.BlockSpec((tm, D), lambda i: (i, gbr_block + 2)),
                  pl.BlockSpec((N_BRANCH, D), lambda i: (0, 0)),
                  pl.BlockSpec((1, 6, D), lambda i: (i // per_b, 0, 0)),
                  wspec(wa), wspec(wb), wspec(wc), wspec(wo)],
        out_specs=pl.BlockSpec((tm, D), lambda i: (i, 0)),
        out_shape=jax.ShapeDtypeStruct((T, D), F32),
        compiler_params=_cparams("parallel"),
        name="merge_out",
    )(x2, ya, yb, yc, proj, proj, proj, gate_b, mod, wa, wb, wc, wo)


FF_CHUNK = 256
FF_LOADS = 8


def _stash_weight_slab(k, w1_ref, w3_ref, w2_ref, wb1, wb3, wb2):
    for w_ref, wb in ((w1_ref, wb1), (w3_ref, wb3), (w2_ref, wb2)):
        rows, cols = w_ref.shape[-2:]
        start = pl.multiple_of(k * rows, rows)
        wb[pl.ds(start, rows), :] = w_ref[...].reshape(rows, cols).astype(BF16)


def _swiglu_resident(h, wb1, wb3, wb2):
    acc = None
    for f in range(wb2.shape[0] // FF_CHUNK):
        cols = slice(f * FF_CHUNK, (f + 1) * FF_CHUNK)
        g = jnp.dot(h, wb1[:, cols], preferred_element_type=F32)
        u = jnp.dot(h, wb3[:, cols], preferred_element_type=F32)
        act = (g * jax.nn.sigmoid(g) * u).astype(BF16)
        part = jnp.dot(act, wb2[cols, :], preferred_element_type=F32)
        acc = part if acc is None else acc + part
    return acc


def _ffn_weight_specs(D, F, lead, slab_index):
    assert D % (16 * FF_LOADS) == 0 and F % (16 * FF_LOADS) == 0 and F % FF_CHUNK == 0
    index = lambda *a: slab_index(*a) + (0,)
    return [pl.BlockSpec(lead + (D // FF_LOADS, F), index), pl.BlockSpec(lead + (D // FF_LOADS, F), index),
            pl.BlockSpec(lead + (F // FF_LOADS, D), index)]


def _ffn_kernel(x_ref, mod_ref, g_ref, w1_ref, w3_ref, w2_ref, o_ref, wb1, wb3, wb2):
    s = pl.program_id(0)

    @pl.when(s < FF_LOADS)
    def _():
        _stash_weight_slab(s, w1_ref, w3_ref, w2_ref, wb1, wb3, wb2)

    @pl.when(s >= FF_LOADS)
    def _():
        h = _modulated_norm(x_ref[...], g_ref[...], mod_ref[0], 3).astype(BF16)
        o_ref[...] = x_ref[...] + mod_ref[0][5:6] * _swiglu_resident(h, wb1, wb3, wb2)


def _dense_ffn(x2, mod, g, w1, w3, w2, seq):
    T, D = x2.shape
    F = w1.shape[1]
    tm = ROW_TILE
    per_b = seq // tm
    tile = lambda s: jnp.maximum(s - FF_LOADS, 0)
    return pl.pallas_call(
        _ffn_kernel,
        grid=(FF_LOADS + T // tm,),
        in_specs=[pl.BlockSpec((tm, D), lambda s: (tile(s), 0)),
                  pl.BlockSpec((1, 6, D), lambda s: (tile(s) // per_b, 0, 0)),
                  pl.BlockSpec((1, D), lambda s: (0, 0))]
        + _ffn_weight_specs(D, F, (), lambda s: (jnp.minimum(s, FF_LOADS - 1),)),
        out_specs=pl.BlockSpec((tm, D), lambda s: (tile(s), 0)),
        out_shape=jax.ShapeDtypeStruct((T, D), F32),
        scratch_shapes=[pltpu.VMEM((D, F), BF16), pltpu.VMEM((D, F), BF16), pltpu.VMEM((F, D), BF16)],
        compiler_params=_cparams("arbitrary"),
        name="dense_ffn",
    )(x2, mod, g.reshape(1, D), w1, w3, w2)


def _router_kernel(x_ref, mod_ref, g_ref, rw_ref, rb_ref, h_ref, comb_ref, sel_ref):
    h = _modulated_norm(x_ref[...], g_ref[...], mod_ref[0], 3)
    h_ref[...] = h
    logits = _dot_nt(rw_ref[...], h, precision=HIGHEST) + rb_ref[...]
    E = logits.shape[0]
    eid = lax.broadcasted_iota(jnp.int32, logits.shape, 0)
    v1 = jnp.max(logits, axis=0, keepdims=True)
    i1 = jnp.min(jnp.where(logits == v1, eid, E), axis=0, keepdims=True)
    rest = jnp.where(eid == i1, -jnp.inf, logits)
    v2 = jnp.max(rest, axis=0, keepdims=True)
    i2 = jnp.min(jnp.where(rest == v2, eid, E), axis=0, keepdims=True)
    e2 = jnp.exp(v2 - v1)
    w1 = 1.0 / (1.0 + e2)
    w2 = e2 / (1.0 + e2)
    comb_ref[...] = jnp.where(eid == i1, w1, 0.0) + jnp.where(eid == i2, w2, 0.0)
    sel_ref[...] = ((eid == i1) | (eid == i2)).astype(jnp.int32)


def _router(x2, mod, g, router_w, router_b, seq):
    T, D = x2.shape
    E = router_w.shape[1]
    tm = 2 * ROW_TILE
    per_b = seq // tm
    return pl.pallas_call(
        _router_kernel,
        grid=(T // tm,),
        in_specs=[pl.BlockSpec((tm, D), lambda i: (i, 0)),
                  pl.BlockSpec((1, 6, D), lambda i: (i // per_b, 0, 0)),
                  pl.BlockSpec((1, D), lambda i: (0, 0)),
                  pl.BlockSpec((E, D), lambda i: (0, 0)),
                  pl.BlockSpec((E, 1), lambda i: (0, 0))],
        out_specs=[pl.BlockSpec((tm, D), lambda i: (i, 0)),
                   pl.BlockSpec((E, tm), lambda i: (0, i)),
                   pl.BlockSpec((E, tm), lambda i: (0, i))],
        out_shape=[jax.ShapeDtypeStruct((T, D), F32),
                   jax.ShapeDtypeStruct((E, T), F32),
                   jax.ShapeDtypeStruct((E, T), jnp.int32)],
        compiler_params=_cparams("parallel"),
        name="moe_router",
    )(x2, mod, g.reshape(1, D), router_w.T, router_b.reshape(E, 1))


STEP_LOAD, STEP_TILE, STEP_BOTH, STEP_ZERO, STEP_NOOP = 0, 1, 2, 3, 4
MOE_SLOTS = 2


def _moe_kernel(kind_ref, slot_ref, slab_ref, we_ref, wk_ref, t_ref, xs_ref, w1_ref, w3_ref, w2_ref, o_ref,
                wb1, wb3, wb2):
    s = pl.program_id(0)
    kind = kind_ref[s]
    slot = slot_ref[s]

    @pl.when(kind == STEP_LOAD)
    def _():
        _stash_weight_slab(slab_ref[s], w1_ref, w3_ref, w2_ref, wb1.at[slot], wb3.at[slot], wb2.at[slot])

    @pl.when(kind == STEP_TILE)
    def _():
        o_ref[...] = _swiglu_resident(xs_ref[...].astype(BF16), wb1.at[slot], wb3.at[slot], wb2.at[slot])

    for use in range(MOE_SLOTS):
        fill = (use + 1) % MOE_SLOTS

        @pl.when((kind == STEP_BOTH) & (slot == use))
        def _():
            _stash_weight_slab(slab_ref[s], w1_ref, w3_ref, w2_ref, wb1.at[fill], wb3.at[fill], wb2.at[fill])
            o_ref[...] = _swiglu_resident(xs_ref[...].astype(BF16), wb1.at[use], wb3.at[use], wb2.at[use])

    @pl.when(kind == STEP_ZERO)
    def _():
        o_ref[...] = jnp.zeros_like(o_ref)


def _moe_schedule(first_tile, tiles, t0, nt):
    E = tiles.shape[0]
    L = FF_LOADS
    i32 = jnp.int32
    ids = jnp.arange(E, dtype=i32)
    lo = jnp.clip(first_tile - t0, 0, nt)
    cnt = jnp.clip(first_tile + tiles - t0, 0, nt) - lo
    active = cnt > 0
    n_act = jnp.sum(active.astype(i32))
    order = jnp.cumsum(active.astype(i32)) - 1
    pick = ((order[None, :] == ids[:, None]) & active[None, :]).astype(i32)
    act_e = jnp.sum(pick * ids[None, :], axis=1)
    act_cnt = jnp.sum(pick * cnt[None, :], axis=1)
    act_lo = jnp.sum(pick * lo[None, :], axis=1)
    has_next = ids + 1 < n_act
    blk_len = jnp.where(ids < n_act, jnp.maximum(act_cnt, jnp.where(has_next, L, 0)), 0)
    blk_start = L + jnp.cumsum(blk_len) - blk_len
    n_sched = jnp.where(n_act > 0, L + jnp.sum(blk_len), 0)
    n_used = jnp.sum(cnt)

    s = jnp.arange(L * E + nt, dtype=i32)
    k = jnp.clip(jnp.sum(((blk_start[None, :] <= s[:, None]) & (ids[None, :] < n_act)).astype(i32), axis=1) - 1,
                 0, E - 1)
    p = s - blk_start[k]
    in_block = (s >= L) & (s < n_sched)
    first_loads = (s < L) & (n_act > 0)
    tile_step = in_block & (p < act_cnt[k])
    load_step = first_loads | (in_block & has_next[k] & (p < L))
    load_k = jnp.where(first_loads, 0, k + 1)
    zero_step = (s >= n_sched) & (s - n_sched < nt - n_used)

    kind = jnp.where(tile_step & load_step, STEP_BOTH,
                     jnp.where(tile_step, STEP_TILE,
                               jnp.where(load_step, STEP_LOAD, jnp.where(zero_step, STEP_ZERO, STEP_NOOP))))
    slot = jnp.where(tile_step, k, load_k) % MOE_SLOTS
    slab = jnp.clip(jnp.where(first_loads, s, p), 0, L - 1)
    last_load = lax.cummax(jnp.where(load_step, s, 0), axis=0)
    we = act_e[jnp.clip(load_k, 0, E - 1)][last_load]
    wk = slab[last_load]
    own = jnp.where(tile_step, act_lo[k] + p, jnp.where(zero_step, n_used + s - n_sched, nt - 1))
    tile = jnp.clip(lax.cummin(own, axis=0, reverse=True), 0, nt - 1)
    return tuple(a.astype(i32) for a in (kind, slot, slab, we, wk, tile))


def _moe_grouped(xs, first_tile, tiles, w1, w3, w2, tm):
    N, D = xs.shape
    nt = N // tm
    E, _, F = w1.shape
    table = _moe_schedule(first_tile, tiles, 0, nt)
    grid_spec = pltpu.PrefetchScalarGridSpec(
        num_scalar_prefetch=len(table),
        grid=(FF_LOADS * E + nt,),
        in_specs=[pl.BlockSpec((tm, D), lambda s, *pre: (pre[5][s], 0))]
        + _ffn_weight_specs(D, F, (1,), lambda s, *pre: (pre[3][s], pre[4][s])),
        out_specs=pl.BlockSpec((tm, D), lambda s, *pre: (pre[5][s], 0)),
        scratch_shapes=[pltpu.VMEM((MOE_SLOTS, D, F), BF16), pltpu.VMEM((MOE_SLOTS, D, F), BF16),
                        pltpu.VMEM((MOE_SLOTS, F, D), BF16)],
    )
    return pl.pallas_call(
        _moe_kernel,
        grid_spec=grid_spec,
        out_shape=jax.ShapeDtypeStruct((N, D), F32),
        compiler_params=_cparams("arbitrary"),
        name="moe_experts",
    )(*table, xs, w1, w3, w2)


def _combine_kernel(x_ref, y1_ref, y2_ref, w_ref, mod_ref, o_ref):
    w = w_ref[...]
    y = w[:, 0:1] * y1_ref[...] + w[:, 1:2] * y2_ref[...]
    o_ref[...] = x_ref[...] + mod_ref[0][5:6] * y


def _moe_combine(x2, y1, y2, w12, mod, seq):
    T, D = x2.shape
    tm = 2 * ROW_TILE
    per_b = seq // tm
    return pl.pallas_call(
        _combine_kernel,
        grid=(T // tm,),
        in_specs=[pl.BlockSpec((tm, D), lambda i: (i, 0)),
                  pl.BlockSpec((tm, D), lambda i: (i, 0)),
                  pl.BlockSpec((tm, D), lambda i: (i, 0)),
                  pl.BlockSpec((tm, 2), lambda i: (i, 0)),
                  pl.BlockSpec((1, 6, D), lambda i: (i // per_b, 0, 0))],
        out_specs=pl.BlockSpec((tm, D), lambda i: (i, 0)),
        out_shape=jax.ShapeDtypeStruct((T, D), F32),
        compiler_params=_cparams("parallel"),
        name="moe_combine",
    )(x2, y1, y2, w12, mod)


def _moe_ffn(x2, mod, g, router_w, router_b, w1, w3, w2, seq):
    T, D = x2.shape
    E = router_w.shape[1]
    tm = ROW_TILE
    h, comb, sel = _router(x2, mod, g, router_w, router_b, seq)

    counts = jnp.sum(sel, axis=1)
    padded = ((counts + tm - 1) // tm) * tm
    group_end = jnp.cumsum(padded)
    group_start = group_end - padded
    rank = jnp.cumsum(sel, axis=1) - sel
    dest = group_start[:, None] + rank
    n_rows = TOP_K * T + E * tm
    n_tiles = n_rows // tm
    tile_start = jnp.arange(n_tiles, dtype=jnp.int32) * tm
    tile_expert = jnp.sum((group_end[None, :] <= tile_start[:, None]).astype(jnp.int32), axis=1)
    tile_expert = jnp.minimum(tile_expert, jnp.max(jnp.where(counts > 0, jnp.arange(E), 0))).astype(jnp.int32)

    eid = jnp.arange(E, dtype=jnp.int32)[:, None]
    e_lo = jnp.min(jnp.where(sel > 0, eid, E), axis=0)
    e_hi = jnp.max(jnp.where(sel > 0, eid, -1), axis=0)
    pick = lambda a, e: jnp.sum(jnp.where(eid == e[None, :], a, 0), axis=0)
    w12 = jnp.stack([pick(comb, e_lo), pick(comb, e_hi)], axis=1)

    tok = jnp.arange(T, dtype=jnp.int32)
    tok_sorted = jnp.sort(jnp.concatenate([e_lo * T + tok, e_hi * T + tok])) % T
    row = jnp.arange(n_rows, dtype=jnp.int32)
    row_expert = jnp.repeat(tile_expert, tm)
    first_sorted = (jnp.cumsum(counts) - counts)[row_expert]
    in_group = row - group_start[row_expert]
    src = tok_sorted[jnp.clip(first_sorted + in_group, 0, TOP_K * T - 1)]
    src = jnp.where(in_group < counts[row_expert], src, 0)

    take_rows = lambda a, idx: a.at[idx].get(mode="promise_in_bounds")
    ys = _moe_grouped(take_rows(h, src), group_start // tm, padded // tm, w1, w3, w2, tm)
    y1 = take_rows(ys, pick(dest, e_lo))
    y2 = take_rows(ys, pick(dest, e_hi))
    return _moe_combine(x2, y1, y2, w12, mod, seq)


def kernel(x, c, positions, ada_w, ada_b, norm1_g, norm2_g, w_in, gate_b, conv_w, conv_b, lru_wa, lru_ba,
           lru_wx, lru_bx, lru_lambda, diff_qn, diff_kn, diff_lq1, diff_lk1, diff_lq2, diff_lk2, diff_subln,
           moba_qn, moba_kn, w_br_a, w_br_b, w_br_c, w_out, ffn_w1, ffn_w3, ffn_w2, router_w, router_b,
           moe_w1, moe_w3, moe_w2):
    B, S, D = x.shape
    L = ada_w.shape[0]
    T = B * S
    assert D == D_MODEL and w_in.shape[2] == PROJ_COL["g_br"] + N_BRANCH * D
    assert S % MOBA_BLOCK == 0 and S // MOBA_BLOCK <= 56 and S % ROW_TILE == 0
    x2 = x.reshape(T, D)
    mod_all = _ada_mod(c, ada_w, ada_b).reshape(L, B, 6, D)
    cos, sin = _rope_tables(positions)
    bf = lambda w: w.astype(BF16)
    tile2 = lambda v: jnp.tile(v, LANES // HEAD_DIM)

    for l in range(L):
        mod = mod_all[l]
        lam_init = 0.8 - 0.6 * math.exp(-0.3 * l)
        proj = _in_proj(x2, mod, norm1_g[l], w_in, l, S)
        gains = jnp.stack([tile2(diff_qn[l]), tile2(diff_kn[l]), tile2(moba_qn[l]), tile2(moba_kn[l])])
        qk = _qk_prep(proj, gains, cos, sin)
        y_a = _rg_lru_branch(proj, S, conv_w[l], conv_b[l], _pair_block_diag(lru_wa[l]), lru_ba[l],
                             _pair_block_diag(lru_wx[l]), lru_bx[l], lru_lambda[l])
        lam_params = jnp.stack([diff_lq1[l], diff_lk1[l], diff_lq2[l], diff_lk2[l]])
        y_b = _diff_attention(qk, proj, S, lam_params, diff_subln[l], lam_init)
        y_c = _moba_attention(qk, proj, S)
        x2 = _merge_out(x2, y_a, y_b, y_c, proj, gate_b[l], mod, bf(w_br_a[l]), bf(w_br_b[l]),
                        bf(w_br_c[l]), bf(w_out[l]), S)
        if l % 2 == 0:
            x2 = _dense_ffn(x2, mod, norm2_g[l], ffn_w1[l // 2], ffn_w3[l // 2], ffn_w2[l // 2], S)
        else:
            x2 = _moe_ffn(x2, mod, norm2_g[l], router_w[l // 2], router_b[l // 2], moe_w1[l // 2],
                          moe_w3[l // 2], moe_w2[l // 2], S)
    return x2.reshape(B, S, D)
```

```python
import functools
import math

import jax
import jax.numpy as jnp
from jax import lax
from jax.experimental import pallas as pl
from jax.experimental.pallas import tpu as pltpu

F32 = jnp.float32
BF16 = jnp.bfloat16
HIGHEST = lax.Precision.HIGHEST

HEAD_DIM = 64
ROPE_THETA = 10000.0
RNN_BLOCKS = 16
CONV_W = 4
LRU_C = 8.0
DIFF_HEADS = 4
MOBA_HEADS = 8
MOBA_BLOCK = 256
MOBA_TOPK = 3
N_BRANCH = 3
N_EXPERTS = 8
TOP_K = 2
EPS = 1e-6
NEG = -1e30

LANES = 128
VMEM_LIMIT = 56 * 1024 * 1024

ROW_TILE = 512

D_MODEL = 1024
ATTN_W = DIFF_HEADS * 2 * HEAD_DIM


def _starts(widths):
    out, col = {}, 0
    for name, width in widths:
        out[name] = col
        col += width
    return out


PROJ_COL = _starts((("x_rnn", D_MODEL), ("g_rnn", D_MODEL), ("dq", ATTN_W), ("dk", ATTN_W), ("dv", ATTN_W),
                    ("mq", ATTN_W), ("mk", ATTN_W), ("mv", ATTN_W), ("g_br", N_BRANCH * D_MODEL)))
QK_COL = _starts((("dq", ATTN_W), ("dk", ATTN_W), ("mq", ATTN_W), ("mk", ATTN_W)))


def _cparams(*sem):
    return pltpu.CompilerParams(dimension_semantics=sem, vmem_limit_bytes=VMEM_LIMIT)


def _modulated_norm(x, g, mod, base):
    ms = jnp.mean(x * x, axis=-1, keepdims=True)
    y = x * lax.rsqrt(ms + EPS) * g
    return y * (1.0 + mod[base + 1:base + 2]) + mod[base:base + 1]


def _dot_nt(a, b, **kw):
    return lax.dot_general(a, b, (((1,), (1,)), ((), ())), preferred_element_type=F32, **kw)


def _ada_kernel(c_ref, w_ref, b_ref, o_ref):
    o_ref[0] = jnp.dot(c_ref[...], w_ref[0], preferred_element_type=F32, precision=HIGHEST) + b_ref[0]


def _ada_mod(c, ada_w, ada_b):
    L, D, N = ada_w.shape
    B = c.shape[0]
    tn = N // 2
    return pl.pallas_call(
        _ada_kernel,
        grid=(L, N // tn),
        in_specs=[pl.BlockSpec((B, D), lambda l, j: (0, 0)),
                  pl.BlockSpec((1, D, tn), lambda l, j: (l, 0, j)),
                  pl.BlockSpec((1, 1, tn), lambda l, j: (l, 0, j))],
        out_specs=pl.BlockSpec((1, B, tn), lambda l, j: (l, 0, j)),
        out_shape=jax.ShapeDtypeStruct((L, B, N), F32),
        compiler_params=_cparams("parallel", "parallel"),
        name="ada_mod",
    )(c, ada_w, ada_b.reshape(L, 1, N))


IN_LOADS = 8
IN_COL_CHUNK = 1024


def _inproj_kernel(x_ref, mod_ref, g_ref, w_ref, o_ref, wb):
    s = pl.program_id(0)

    @pl.when(s < IN_LOADS)
    def _():
        rows = w_ref.shape[1]
        wb[pl.ds(pl.multiple_of(s * rows, rows), rows), :] = w_ref[0].astype(BF16)

    @pl.when(s >= IN_LOADS)
    def _():
        h = _modulated_norm(x_ref[...], g_ref[...], mod_ref[0], 0).astype(BF16)
        for c in range(wb.shape[1] // IN_COL_CHUNK):
            cols = slice(c * IN_COL_CHUNK, (c + 1) * IN_COL_CHUNK)
            o_ref[:, cols] = jnp.dot(h, wb[:, cols], preferred_element_type=F32).astype(BF16)


def _in_proj(x2, mod, g, w_in, layer, seq):
    T, D = x2.shape
    N = w_in.shape[2]
    tm = ROW_TILE
    per_b = seq // tm
    assert D % (16 * IN_LOADS) == 0 and N % IN_COL_CHUNK == 0
    tile = lambda s: jnp.maximum(s - IN_LOADS, 0)
    return pl.pallas_call(
        _inproj_kernel,
        grid=(IN_LOADS + T // tm,),
        in_specs=[pl.BlockSpec((tm, D), lambda s: (tile(s), 0)),
                  pl.BlockSpec((1, 6, D), lambda s: (tile(s) // per_b, 0, 0)),
                  pl.BlockSpec((1, D), lambda s: (0, 0)),
                  pl.BlockSpec((1, D // IN_LOADS, N), lambda s: (layer, jnp.minimum(s, IN_LOADS - 1), 0))],
        out_specs=pl.BlockSpec((tm, N), lambda s: (tile(s), 0)),
        out_shape=jax.ShapeDtypeStruct((T, N), BF16),
        scratch_shapes=[pltpu.VMEM((D, N), BF16)],
        compiler_params=_cparams("arbitrary"),
        name="in_proj",
    )(x2, mod, g.reshape(1, D), w_in)


def _rope_kernel(pos_ref, inv_ref, sign_ref, cos_ref, sin_ref):
    ang = pos_ref[...] * inv_ref[...]
    cos_ref[...] = jnp.cos(ang)
    sin_ref[...] = jnp.sin(ang) * sign_ref[...]


def _rope_tables(positions):
    T = positions.size
    pos = positions.reshape(T, 1).astype(F32)
    inv = 1.0 / (ROPE_THETA ** (jnp.arange(0, HEAD_DIM, 2, dtype=F32) / HEAD_DIM))
    half = HEAD_DIM // 2
    inv128 = jnp.tile(inv, LANES // half).reshape(1, LANES)
    sign = jnp.tile(jnp.concatenate([-jnp.ones((half,), F32), jnp.ones((half,), F32)]),
                    LANES // HEAD_DIM).reshape(1, LANES)
    tm = 1024
    return pl.pallas_call(
        _rope_kernel,
        grid=(T // tm,),
        in_specs=[pl.BlockSpec((tm, 1), lambda i: (i, 0)),
                  pl.BlockSpec((1, LANES), lambda i: (0, 0)),
                  pl.BlockSpec((1, LANES), lambda i: (0, 0))],
        out_specs=[pl.BlockSpec((tm, LANES), lambda i: (i, 0))] * 2,
        out_shape=[jax.ShapeDtypeStruct((T, LANES), F32)] * 2,
        compiler_params=_cparams("parallel"),
        name="rope_tables",
    )(pos, inv128, sign)


def _qkprep_kernel(dq_ref, dk_ref, mq_ref, mk_ref, gain_ref, cos_ref, sin_ref, seg_ref, swap_ref, o_ref):
    cos = cos_ref[...]
    sin = sin_ref[...]
    seg = seg_ref[...]
    swap = swap_ref[...]
    width = dq_ref.shape[1]
    for gi, ref in enumerate((dq_ref, dk_ref, mq_ref, mk_ref)):
        gain = gain_ref[gi:gi + 1, :]
        for cb in range(width // LANES):
            x = ref[:, cb * LANES:(cb + 1) * LANES].astype(F32)
            sq = x * x
            sq_hi = sq.astype(BF16)
            sq_lo = (sq - sq_hi.astype(F32)).astype(BF16)
            ms = (jnp.dot(sq_hi, seg, preferred_element_type=F32)
                  + jnp.dot(sq_lo, seg, preferred_element_type=F32))
            y = x * lax.rsqrt(ms + EPS) * gain
            swapped = jnp.dot(y.astype(BF16), swap, preferred_element_type=F32)
            r = y * cos + swapped * sin
            if gi % 2 == 0:
                r = r * (math.log2(math.e) / math.sqrt(HEAD_DIM))
            col = gi * width + cb * LANES
            o_ref[:, col:col + LANES] = r.astype(BF16)


def _qk_prep(proj, gains, cos, sin):
    T = proj.shape[0]
    width = ATTN_W
    seg = jnp.kron(jnp.eye(LANES // HEAD_DIM, dtype=F32),
                   jnp.full((HEAD_DIM, HEAD_DIM), 1.0 / HEAD_DIM, F32)).astype(BF16)
    lanes = jnp.arange(LANES)
    partner = jnp.where(lanes % HEAD_DIM < HEAD_DIM // 2, lanes + HEAD_DIM // 2, lanes - HEAD_DIM // 2)
    swap = (lanes[:, None] == partner[None, :]).astype(BF16)
    tm = 2 * ROW_TILE
    col_blocks = [PROJ_COL[name] // width for name in QK_COL]
    in_specs = [pl.BlockSpec((tm, width), functools.partial(lambda i, c: (i, c), c=c)) for c in col_blocks]
    in_specs += [pl.BlockSpec((4, LANES), lambda i: (0, 0)),
                 pl.BlockSpec((tm, LANES), lambda i: (i, 0)),
                 pl.BlockSpec((tm, LANES), lambda i: (i, 0)),
                 pl.BlockSpec((LANES, LANES), lambda i: (0, 0)),
                 pl.BlockSpec((LANES, LANES), lambda i: (0, 0))]
    return pl.pallas_call(
        _qkprep_kernel,
        grid=(T // tm,),
        in_specs=in_specs,
        out_specs=pl.BlockSpec((tm, 4 * width), lambda i: (i, 0)),
        out_shape=jax.ShapeDtypeStruct((T, 4 * width), BF16),
        compiler_params=_cparams("parallel"),
        name="qk_prep",
    )(proj, proj, proj, proj, gains, cos, sin, seg, swap)


def _gelu_tanh(x):
    return 0.5 * x * (1.0 + jnp.tanh(math.sqrt(2.0 / math.pi) * (x + 0.044715 * x * x * x)))


SUBLANES = 8
LRU_LANE_BLOCKS = 4


def _rglru_kernel(x_ref, g_ref, cw_ref, cb_ref, wa_ref, ba_ref, wx_ref, bx_ref, lam_ref, o_ref, *scratch):
    S = x_ref.shape[0]
    steps = S // SUBLANES
    pitch = steps + SUBLANES
    nblk = x_ref.shape[1] // LANES
    per_blk = len(scratch) // nblk
    sub = lax.broadcasted_iota(jnp.int32, (SUBLANES, LANES), 0)

    for blk in range(nblk):
        slab, fold_x, a_buf, u_buf, p_buf = scratch[blk * per_blk:(blk + 1) * per_blk]
        lanes = slice(blk * LANES, (blk + 1) * LANES)

        for s in range(SUBLANES):
            slab[s * pitch:s * pitch + steps, :] = x_ref[s * steps:(s + 1) * steps, lanes].astype(F32)

        def fold(step, carry):
            fold_x[pl.ds(pl.multiple_of(step * SUBLANES, SUBLANES), SUBLANES), :] = (
                slab[pl.ds(step, SUBLANES, stride=pitch), :])
            return carry

        lax.fori_loop(0, steps, fold, 0, unroll=8)
        xf = fold_x[...]

        def head(j):
            tail = xf[(steps - j) * SUBLANES:, :]
            vregs = [jnp.where(sub == 0, 0.0, pltpu.roll(tail[v * SUBLANES:(v + 1) * SUBLANES, :], 1, 0))
                     for v in range(j)]
            return jnp.concatenate(vregs + [xf[:(steps - j) * SUBLANES, :]], axis=0)

        xc = cb_ref[:, lanes] + cw_ref[CONV_W - 1:CONV_W, lanes] * xf
        for j in range(1, CONV_W):
            xc = xc + cw_ref[CONV_W - 1 - j:CONV_W - j, lanes] * head(j)

        xcb = xc.astype(BF16)
        r = jax.nn.sigmoid(jnp.dot(xcb, wa_ref[blk], preferred_element_type=F32) + ba_ref[:, lanes])
        gi = jax.nn.sigmoid(jnp.dot(xcb, wx_ref[blk], preferred_element_type=F32) + bx_ref[:, lanes])
        neg_lam = -lam_ref[:, lanes]
        softplus = jnp.maximum(neg_lam, 0.0) + jnp.log1p(jnp.exp(-jnp.abs(neg_lam)))
        a = jnp.exp2(r * ((-LRU_C * math.log2(math.e)) * softplus))
        gap = 1.0 - a * a
        mult = jnp.where(gap > 0.0, gap * lax.rsqrt(gap), 0.0)
        a_buf[...] = a
        u_buf[...] = mult * gi * xc

    def scan(step, carry):
        rows = pl.ds(pl.multiple_of(step * SUBLANES, SUBLANES), SUBLANES)
        out = []
        for blk in range(nblk):
            _, _, a_buf, u_buf, p_buf = scratch[blk * per_blk:(blk + 1) * per_blk]
            h, p = carry[blk]
            a = a_buf[rows, :]
            h = a * h + u_buf[rows, :]
            p = a * p
            u_buf[rows, :] = h
            p_buf[rows, :] = p
            out.append((h, p))
        return tuple(out)

    ones = jnp.ones((SUBLANES, LANES), F32)
    last = lax.fori_loop(0, steps, scan, ((jnp.zeros_like(ones), ones),) * nblk, unroll=8)

    for blk in range(nblk):
        slab, fold_x, a_buf, u_buf, p_buf = scratch[blk * per_blk:(blk + 1) * per_blk]
        lanes = slice(blk * LANES, (blk + 1) * LANES)
        h_end, p_end = last[blk]
        h_in = [jnp.zeros((1, LANES), F32)]
        for s in range(SUBLANES - 1):
            h_in.append(h_end[s:s + 1, :] + p_end[s:s + 1, :] * h_in[s])
        h_in = jnp.concatenate(h_in, axis=0)
        u_buf[...] = u_buf[...] + p_buf[...] * jnp.concatenate([h_in] * steps, axis=0)

        def unfold(step, carry):
            slab[pl.ds(step, SUBLANES, stride=pitch), :] = (
                u_buf[pl.ds(pl.multiple_of(step * SUBLANES, SUBLANES), SUBLANES), :])
            return carry

        lax.fori_loop(0, steps, unfold, 0, unroll=8)
        for s in range(SUBLANES):
            rows = slice(s * steps, (s + 1) * steps)
            h = slab[s * pitch:s * pitch + steps, :]
            o_ref[rows, lanes] = (_gelu_tanh(g_ref[rows, lanes].astype(F32)) * h).astype(BF16)


def _rg_lru_branch(proj, seq, conv_w, conv_b, wa2, ba, wx2, bx, lam):
    T = proj.shape[0]
    C = conv_w.shape[1]
    B = T // seq
    nblk = LRU_LANE_BLOCKS
    W = nblk * LANES
    assert PROJ_COL["x_rnn"] == 0 and PROJ_COL["g_rnn"] % W == 0
    g_off = PROJ_COL["g_rnn"] // W
    steps = seq // SUBLANES
    assert seq % (SUBLANES * SUBLANES) == 0 and C % W == 0
    vec = lambda v: v.reshape(1, C)
    cols = lambda rows: pl.BlockSpec((rows, W), lambda b, n: (0, n))
    per_blk = [pltpu.VMEM((SUBLANES * (steps + SUBLANES), LANES), F32)] + [pltpu.VMEM((seq, LANES), F32)] * 4
    return pl.pallas_call(
        _rglru_kernel,
        grid=(B, C // W),
        in_specs=[pl.BlockSpec((seq, W), lambda b, n: (b, n)),
                  pl.BlockSpec((seq, W), lambda b, n: (b, g_off + n)),
                  cols(CONV_W), cols(1),
                  pl.BlockSpec((nblk, LANES, LANES), lambda b, n: (n, 0, 0)), cols(1),
                  pl.BlockSpec((nblk, LANES, LANES), lambda b, n: (n, 0, 0)), cols(1), cols(1)],
        out_specs=pl.BlockSpec((seq, W), lambda b, n: (b, n)),
        out_shape=jax.ShapeDtypeStruct((T, C), BF16),
        scratch_shapes=per_blk * nblk,
        compiler_params=_cparams("parallel", "parallel"),
        name="rg_lru",
    )(proj, proj, conv_w, vec(conv_b), wa2, vec(ba), wx2, vec(bx), vec(lam))


def _pair_block_diag(w):
    n, d, _ = w.shape
    z = jnp.zeros((n // 2, 2, d, 2, d), w.dtype)
    z = z.at[:, 0, :, 0, :].set(w[0::2]).at[:, 1, :, 1, :].set(w[1::2])
    return z.reshape(n // 2, 2 * d, 2 * d).astype(BF16)


ATTN_TILE = 256


def _lane_tile(x, n):
    return jnp.concatenate([x] * n, axis=1)


def _softmax_update(s, state, v_aug):
    row_max = jnp.max(s, axis=-1, keepdims=True)
    if state is None:
        m_new = jnp.broadcast_to(row_max, (s.shape[0], LANES))
        p = jnp.exp2(s - _lane_tile(m_new, s.shape[1] // LANES)).astype(BF16)
        return m_new, jnp.dot(p, v_aug, preferred_element_type=F32)
    m_old, acc = state
    m_new = jnp.maximum(m_old, row_max)
    p = jnp.exp2(s - _lane_tile(m_new, s.shape[1] // LANES)).astype(BF16)
    alpha = _lane_tile(jnp.exp2(m_old - m_new), acc.shape[1] // LANES)
    return m_new, alpha * acc + jnp.dot(p, v_aug, preferred_element_type=F32)


def _softmax_finish(state):
    _, acc = state
    return acc[:, :LANES] / acc[:, LANES:]


def _state_rows(state, start, stop):
    m, acc = state
    return m[start:stop], acc[start:stop]


def _causal_mask(t):
    return lax.broadcasted_iota(jnp.int32, (t, t), 1) <= lax.broadcasted_iota(jnp.int32, (t, t), 0)


def _mask_diagonal(s, causal):
    t = causal.shape[0]
    if s.shape[0] == t:
        return jnp.where(causal, s, NEG)
    return jnp.concatenate([jnp.where(causal, s[:t], NEG), s[t:]], axis=0)


def _with_ones(v):
    return jnp.concatenate([v, jnp.ones_like(v)], axis=1)


DIFF_HEADS_PER_STEP = 2


def _diffattn_kernel(q_ref, k_ref, v_ref, lam_ref, g_ref, o_ref, *, lam_init):
    for head in range(q_ref.shape[1] // LANES):
        lanes = pl.ds(head * LANES, LANES)
        _diffattn_head(q_ref.at[:, lanes], k_ref.at[:, lanes], v_ref.at[:, lanes], lam_ref, g_ref,
                       o_ref.at[:, lanes], lam_init=lam_init)


def _diffattn_head(q_ref, k_ref, v_ref, lam_ref, g_ref, o_ref, *, lam_init):
    t = ATTN_TILE
    S = q_ref.shape[0]
    q = q_ref[...]
    lane = lax.broadcasted_iota(jnp.int32, q.shape, 1)
    zero = jnp.zeros_like(q)
    q1 = jnp.where(lane < HEAD_DIM, q, zero)
    q2 = jnp.where(lane < HEAD_DIM, zero, q)
    causal = _causal_mask(t)
    lp = lam_ref[...]
    lam = (jnp.exp(jnp.sum(lp[0:1] * lp[1:2], axis=-1, keepdims=True))
           - jnp.exp(jnp.sum(lp[2:3] * lp[3:4], axis=-1, keepdims=True)) + lam_init)
    st1 = st2 = None
    for j in range(S // t):
        k = k_ref[j * t:(j + 1) * t, :]
        v_aug = _with_ones(v_ref[j * t:(j + 1) * t, :])
        st1 = _softmax_update(_mask_diagonal(_dot_nt(q1[j * t:, :], k), causal), st1, v_aug)
        st2 = _softmax_update(_mask_diagonal(_dot_nt(q2[j * t:, :], k), causal), st2, v_aug)
        o = _softmax_finish(_state_rows(st1, 0, t)) - lam * _softmax_finish(_state_rows(st2, 0, t))
        ms = jnp.mean(o * o, axis=-1, keepdims=True)
        o_ref[j * t:(j + 1) * t, :] = (o * lax.rsqrt(ms + EPS) * g_ref[...] * (1.0 - lam_init)).astype(BF16)
        if (j + 1) * t < S:
            st1 = _state_rows(st1, t, None)
            st2 = _state_rows(st2, t, None)


def _diff_attention(qk, proj, seq, lam_params, subln_g, lam_init):
    T = qk.shape[0]
    B = T // seq
    H = DIFF_HEADS
    P = DIFF_HEADS_PER_STEP
    W = P * LANES
    assert H % P == 0 and all(col % W == 0 for col in (QK_COL["dq"], QK_COL["dk"], PROJ_COL["dv"]))
    return pl.pallas_call(
        functools.partial(_diffattn_kernel, lam_init=lam_init),
        grid=(B, H // P),
        in_specs=[pl.BlockSpec((seq, W), lambda b, h: (b, QK_COL["dq"] // W + h)),
                  pl.BlockSpec((seq, W), lambda b, h: (b, QK_COL["dk"] // W + h)),
                  pl.BlockSpec((seq, W), lambda b, h: (b, PROJ_COL["dv"] // W + h)),
                  pl.BlockSpec((4, HEAD_DIM), lambda b, h: (0, 0)),
                  pl.BlockSpec((1, LANES), lambda b, h: (0, 0))],
        out_specs=pl.BlockSpec((seq, W), lambda b, h: (b, h)),
        out_shape=jax.ShapeDtypeStruct((T, H * LANES), BF16),
        compiler_params=_cparams("parallel", "parallel"),
        name="diff_attn",
    )(qk, qk, proj, lam_params, subln_g.reshape(1, LANES))


MOBA_PAIRS_PER_STEP = 2


def _moba_kernel(q_ref, k_ref, v_ref, o_ref, kmean):
    for pair in range(q_ref.shape[1] // LANES):
        lanes = pl.ds(pair * LANES, LANES)
        _moba_pair(q_ref.at[:, lanes], k_ref.at[:, lanes], v_ref.at[:, lanes], o_ref.at[:, lanes], kmean.at[pair])


def _moba_pair(q_ref, k_ref, v_ref, o_ref, kmean):
    t = ATTN_TILE
    S = q_ref.shape[0]
    nb = S // MOBA_BLOCK
    nb8 = -(-nb // 8) * 8
    lane_row = lax.broadcasted_iota(jnp.int32, (1, LANES), 1)

    kmean[...] = jnp.zeros_like(kmean)
    for blk in range(nb):
        mean = jnp.mean(k_ref[blk * MOBA_BLOCK:(blk + 1) * MOBA_BLOCK, :].astype(F32), axis=0, keepdims=True)
        kmean[blk:blk + 1, :] = jnp.where(lane_row < HEAD_DIM, mean, 0.0)
        kmean[nb8 + blk:nb8 + blk + 1, :] = jnp.where(lane_row < HEAD_DIM, 0.0, mean)

    q = q_ref[...]
    lane = lax.broadcasted_iota(jnp.int32, q.shape, 1)
    zero = jnp.zeros_like(q)
    q_a = jnp.where(lane < HEAD_DIM, q, zero)
    q_b = jnp.where(lane < HEAD_DIM, zero, q)

    km = kmean[...]
    km_hi = km.astype(BF16)
    km_mid = (km - km_hi.astype(F32)).astype(BF16)
    km_lo = (km - km_hi.astype(F32) - km_mid.astype(F32)).astype(BF16)
    gate = _dot_nt(km_hi, q) + _dot_nt(km_mid, q) + _dot_nt(km_lo, q)
    gate_blk = lax.broadcasted_iota(jnp.int32, (nb8, S), 0)
    own = lax.broadcasted_iota(jnp.int32, (nb8, S), 1) // MOBA_BLOCK
    past = gate_blk < own
    unused_rows = jnp.full((HEAD_DIM - nb8, S), NEG, F32)
    bias_rows = []
    for head in range(2):
        g = jnp.where(past, gate[head * nb8:(head + 1) * nb8, :], NEG)
        rank = jnp.zeros(g.shape, jnp.int32)
        for jb in range(nb - 1):
            other = g[jb:jb + 1, :]
            beats = (other > g) | ((other == g) & (jb < gate_blk))
            rank = rank + beats.astype(jnp.int32)
        allowed = (past & (rank < MOBA_TOPK)) | (gate_blk == own)
        bias_rows += [jnp.where(allowed, 0.0, NEG), unused_rows]
    bias = jnp.concatenate(bias_rows, axis=0).T.astype(BF16)
    qa_aug = jnp.concatenate([q_a, bias], axis=1)
    qb_aug = jnp.concatenate([q_b, bias], axis=1)

    causal = _causal_mask(t)
    key_lane = lax.broadcasted_iota(jnp.int32, (t, LANES), 1)
    key_row = lax.broadcasted_iota(jnp.int32, (t, LANES), 0)
    st_a = st_b = None
    for j in range(S // t):
        k = k_ref[j * t:(j + 1) * t, :]
        key_blk = (j * t + key_row) // MOBA_BLOCK
        ka_aug = jnp.concatenate([k, (key_lane == key_blk).astype(BF16)], axis=1)
        kb_aug = jnp.concatenate([k, (key_lane == key_blk + HEAD_DIM).astype(BF16)], axis=1)
        v_aug = _with_ones(v_ref[j * t:(j + 1) * t, :])
        st_a = _softmax_update(_mask_diagonal(_dot_nt(qa_aug[j * t:, :], ka_aug), causal), st_a, v_aug)
        st_b = _softmax_update(_mask_diagonal(_dot_nt(qb_aug[j * t:, :], kb_aug), causal), st_b, v_aug)
        o = jnp.where(key_lane < HEAD_DIM, _softmax_finish(_state_rows(st_a, 0, t)),
                      _softmax_finish(_state_rows(st_b, 0, t)))
        o_ref[j * t:(j + 1) * t, :] = o.astype(BF16)
        if (j + 1) * t < S:
            st_a = _state_rows(st_a, t, None)
            st_b = _state_rows(st_b, t, None)


def _moba_attention(qk, proj, seq):
    T = qk.shape[0]
    B = T // seq
    HP = MOBA_HEADS // 2
    P = MOBA_PAIRS_PER_STEP
    W = P * LANES
    assert HP % P == 0 and all(col % W == 0 for col in (QK_COL["mq"], QK_COL["mk"], PROJ_COL["mv"]))
    gate_rows = 2 * SUBLANES * pl.cdiv(seq // MOBA_BLOCK, SUBLANES)
    return pl.pallas_call(
        _moba_kernel,
        grid=(B, HP // P),
        in_specs=[pl.BlockSpec((seq, W), lambda b, h: (b, QK_COL["mq"] // W + h)),
                  pl.BlockSpec((seq, W), lambda b, h: (b, QK_COL["mk"] // W + h)),
                  pl.BlockSpec((seq, W), lambda b, h: (b, PROJ_COL["mv"] // W + h))],
        out_specs=pl.BlockSpec((seq, W), lambda b, h: (b, h)),
        out_shape=jax.ShapeDtypeStruct((T, HP * LANES), BF16),
        scratch_shapes=[pltpu.VMEM((P, gate_rows, LANES), F32)],
        compiler_params=_cparams("parallel", "parallel"),
        name="moba_attn",
    )(qk, qk, proj)


def _merge_kernel(x_ref, ya_ref, yb_ref, yc_ref, ga_ref, gbb_ref, gc_ref, gb_ref, mod_ref,
                  wa_ref, wb_ref, wc_ref, wo_ref, o_ref):
    merged = None
    branches = ((ya_ref, wa_ref, ga_ref), (yb_ref, wb_ref, gbb_ref), (yc_ref, wc_ref, gc_ref))
    for n, (y_ref, w_ref, gl_ref) in enumerate(branches):
        gate = jax.nn.sigmoid(gl_ref[...].astype(F32) + gb_ref[n:n + 1, :])
        term = gate * jnp.dot(y_ref[...], w_ref[...], preferred_element_type=F32)
        merged = term if merged is None else merged + term
    mix = jnp.dot(merged.astype(BF16), wo_ref[...], preferred_element_type=F32)
    o_ref[...] = x_ref[...] + mod_ref[0][2:3] * mix


def _merge_out(x2, ya, yb, yc, proj, gate_b, mod, wa, wb, wc, wo, seq):
    T, D = x2.shape
    tm = 2 * ROW_TILE
    per_b = seq // tm
    assert PROJ_COL["g_br"] % D == 0
    gbr_block = PROJ_COL["g_br"] // D
    wspec = lambda w: pl.BlockSpec(w.shape, lambda i: (0, 0))
    return pl.pallas_call(
        _merge_kernel,
        grid=(T // tm,),
        in_specs=[pl.BlockSpec((tm, D), lambda i: (i, 0)),
                  pl.BlockSpec((tm, ya.shape[1]), lambda i: (i, 0)),
                  pl.BlockSpec((tm, yb.shape[1]), lambda i: (i, 0)),
                  pl.BlockSpec((tm, yc.shape[1]), lambda i: (i, 0)),
                  pl.BlockSpec((tm, D), lambda i: (i, gbr_block)),
                  pl.BlockSpec((tm, D), lambda i: (i, gbr_block + 1)),
                  pl.BlockSpec((tm, D), lambda i: (i, gbr_block + 2)),
                  pl.BlockSpec((N_BRANCH, D), lambda i: (0, 0)),
                  pl.BlockSpec((1, 6, D), lambda i: (i // per_b, 0, 0)),
                  wspec(wa), wspec(wb), wspec(wc), wspec(wo)],
        out_specs=pl.BlockSpec((tm, D), lambda i: (i, 0)),
        out_shape=jax.ShapeDtypeStruct((T, D), F32),
        compiler_params=_cparams("parallel"),
        name="merge_out",
    )(x2, ya, yb, yc, proj, proj, proj, gate_b, mod, wa, wb, wc, wo)


FF_CHUNK = 256
FF_LOADS = 8


def _stash_weight_slab(k, w1_ref, w3_ref, w2_ref, wb1, wb3, wb2):
    for w_ref, wb in ((w1_ref, wb1), (w3_ref, wb3), (w2_ref, wb2)):
        rows, cols = w_ref.shape[-2:]
        start = pl.multiple_of(k * rows, rows)
        wb[pl.ds(start, rows), :] = w_ref[...].reshape(rows, cols).astype(BF16)


def _swiglu_resident(h, wb1, wb3, wb2):
    acc = None
    for f in range(wb2.shape[0] // FF_CHUNK):
        cols = slice(f * FF_CHUNK, (f + 1) * FF_CHUNK)
        g = jnp.dot(h, wb1[:, cols], preferred_element_type=F32)
        u = jnp.dot(h, wb3[:, cols], preferred_element_type=F32)
        act = (g * jax.nn.sigmoid(g) * u).astype(BF16)
        part = jnp.dot(act, wb2[cols, :], preferred_element_type=F32)
        acc = part if acc is None else acc + part
    return acc


def _ffn_weight_specs(D, F, lead, slab_index):
    assert D % (16 * FF_LOADS) == 0 and F % (16 * FF_LOADS) == 0 and F % FF_CHUNK == 0
    index = lambda *a: slab_index(*a) + (0,)
    return [pl.BlockSpec(lead + (D // FF_LOADS, F), index), pl.BlockSpec(lead + (D // FF_LOADS, F), index),
            pl.BlockSpec(lead + (F // FF_LOADS, D), index)]


def _ffn_kernel(x_ref, mod_ref, g_ref, w1_ref, w3_ref, w2_ref, o_ref, wb1, wb3, wb2):
    s = pl.program_id(0)

    @pl.when(s < FF_LOADS)
    def _():
        _stash_weight_slab(s, w1_ref, w3_ref, w2_ref, wb1, wb3, wb2)

    @pl.when(s >= FF_LOADS)
    def _():
        h = _modulated_norm(x_ref[...], g_ref[...], mod_ref[0], 3).astype(BF16)
        o_ref[...] = x_ref[...] + mod_ref[0][5:6] * _swiglu_resident(h, wb1, wb3, wb2)


def _dense_ffn(x2, mod, g, w1, w3, w2, seq):
    T, D = x2.shape
    F = w1.shape[1]
    tm = ROW_TILE
    per_b = seq // tm
    tile = lambda s: jnp.maximum(s - FF_LOADS, 0)
    return pl.pallas_call(
        _ffn_kernel,
        grid=(FF_LOADS + T // tm,),
        in_specs=[pl.BlockSpec((tm, D), lambda s: (tile(s), 0)),
                  pl.BlockSpec((1, 6, D), lambda s: (tile(s) // per_b, 0, 0)),
                  pl.BlockSpec((1, D), lambda s: (0, 0))]
        + _ffn_weight_specs(D, F, (), lambda s: (jnp.minimum(s, FF_LOADS - 1),)),
        out_specs=pl.BlockSpec((tm, D), lambda s: (tile(s), 0)),
        out_shape=jax.ShapeDtypeStruct((T, D), F32),
        scratch_shapes=[pltpu.VMEM((D, F), BF16), pltpu.VMEM((D, F), BF16), pltpu.VMEM((F, D), BF16)],
        compiler_params=_cparams("arbitrary"),
        name="dense_ffn",
    )(x2, mod, g.reshape(1, D), w1, w3, w2)


def _router_kernel(x_ref, mod_ref, g_ref, rw_ref, rb_ref, h_ref, comb_ref, sel_ref):
    h = _modulated_norm(x_ref[...], g_ref[...], mod_ref[0], 3)
    h_ref[...] = h
    logits = _dot_nt(rw_ref[...], h, precision=HIGHEST) + rb_ref[...]
    E = logits.shape[0]
    eid = lax.broadcasted_iota(jnp.int32, logits.shape, 0)
    v1 = jnp.max(logits, axis=0, keepdims=True)
    i1 = jnp.min(jnp.where(logits == v1, eid, E), axis=0, keepdims=True)
    rest = jnp.where(eid == i1, -jnp.inf, logits)
    v2 = jnp.max(rest, axis=0, keepdims=True)
    i2 = jnp.min(jnp.where(rest == v2, eid, E), axis=0, keepdims=True)
    e2 = jnp.exp(v2 - v1)
    w1 = 1.0 / (1.0 + e2)
    w2 = e2 / (1.0 + e2)
    comb_ref[...] = jnp.where(eid == i1, w1, 0.0) + jnp.where(eid == i2, w2, 0.0)
    sel_ref[...] = ((eid == i1) | (eid == i2)).astype(jnp.int32)


def _router(x2, mod, g, router_w, router_b, seq):
    T, D = x2.shape
    E = router_w.shape[1]
    tm = 2 * ROW_TILE
    per_b = seq // tm
    return pl.pallas_call(
        _router_kernel,
        grid=(T // tm,),
        in_specs=[pl.BlockSpec((tm, D), lambda i: (i, 0)),
                  pl.BlockSpec((1, 6, D), lambda i: (i // per_b, 0, 0)),
                  pl.BlockSpec((1, D), lambda i: (0, 0)),
                  pl.BlockSpec((E, D), lambda i: (0, 0)),
                  pl.BlockSpec((E, 1), lambda i: (0, 0))],
        out_specs=[pl.BlockSpec((tm, D), lambda i: (i, 0)),
                   pl.BlockSpec((E, tm), lambda i: (0, i)),
                   pl.BlockSpec((E, tm), lambda i: (0, i))],
        out_shape=[jax.ShapeDtypeStruct((T, D), F32),
                   jax.ShapeDtypeStruct((E, T), F32),
                   jax.ShapeDtypeStruct((E, T), jnp.int32)],
        compiler_params=_cparams("parallel"),
        name="moe_router",
    )(x2, mod, g.reshape(1, D), router_w.T, router_b.reshape(E, 1))


STEP_LOAD, STEP_TILE, STEP_BOTH, STEP_ZERO, STEP_NOOP = 0, 1, 2, 3, 4
MOE_SLOTS = 2


def _moe_kernel(kind_ref, slot_ref, slab_ref, we_ref, wk_ref, t_ref, xs_ref, w1_ref, w3_ref, w2_ref, o_ref,
                wb1, wb3, wb2):
    s = pl.program_id(0)
    kind = kind_ref[s]
    slot = slot_ref[s]

    @pl.when(kind == STEP_LOAD)
    def _():
        _stash_weight_slab(slab_ref[s], w1_ref, w3_ref, w2_ref, wb1.at[slot], wb3.at[slot], wb2.at[slot])

    @pl.when(kind == STEP_TILE)
    def _():
        o_ref[...] = _swiglu_resident(xs_ref[...].astype(BF16), wb1.at[slot], wb3.at[slot], wb2.at[slot])

    for use in range(MOE_SLOTS):
        fill = (use + 1) % MOE_SLOTS

        @pl.when((kind == STEP_BOTH) & (slot == use))
        def _():
            _stash_weight_slab(slab_ref[s], w1_ref, w3_ref, w2_ref, wb1.at[fill], wb3.at[fill], wb2.at[fill])
            o_ref[...] = _swiglu_resident(xs_ref[...].astype(BF16), wb1.at[use], wb3.at[use], wb2.at[use])

    @pl.when(kind == STEP_ZERO)
    def _():
        o_ref[...] = jnp.zeros_like(o_ref)


def _moe_schedule(first_tile, tiles, t0, nt):
    E = tiles.shape[0]
    L = FF_LOADS
    i32 = jnp.int32
    ids = jnp.arange(E, dtype=i32)
    lo = jnp.clip(first_tile - t0, 0, nt)
    cnt = jnp.clip(first_tile + tiles - t0, 0, nt) - lo
    active = cnt > 0
    n_act = jnp.sum(active.astype(i32))
    order = jnp.cumsum(active.astype(i32)) - 1
    pick = ((order[None, :] == ids[:, None]) & active[None, :]).astype(i32)
    act_e = jnp.sum(pick * ids[None, :], axis=1)
    act_cnt = jnp.sum(pick * cnt[None, :], axis=1)
    act_lo = jnp.sum(pick * lo[None, :], axis=1)
    has_next = ids + 1 < n_act
    blk_len = jnp.where(ids < n_act, jnp.maximum(act_cnt, jnp.where(has_next, L, 0)), 0)
    blk_start = L + jnp.cumsum(blk_len) - blk_len
    n_sched = jnp.where(n_act > 0, L + jnp.sum(blk_len), 0)
    n_used = jnp.sum(cnt)

    s = jnp.arange(L * E + nt, dtype=i32)
    k = jnp.clip(jnp.sum(((blk_start[None, :] <= s[:, None]) & (ids[None, :] < n_act)).astype(i32), axis=1) - 1,
                 0, E - 1)
    p = s - blk_start[k]
    in_block = (s >= L) & (s < n_sched)
    first_loads = (s < L) & (n_act > 0)
    tile_step = in_block & (p < act_cnt[k])
    load_step = first_loads | (in_block & has_next[k] & (p < L))
    load_k = jnp.where(first_loads, 0, k + 1)
    zero_step = (s >= n_sched) & (s - n_sched < nt - n_used)

    kind = jnp.where(tile_step & load_step, STEP_BOTH,
                     jnp.where(tile_step, STEP_TILE,
                               jnp.where(load_step, STEP_LOAD, jnp.where(zero_step, STEP_ZERO, STEP_NOOP))))
    slot = jnp.where(tile_step, k, load_k) % MOE_SLOTS
    slab = jnp.clip(jnp.where(first_loads, s, p), 0, L - 1)
    last_load = lax.cummax(jnp.where(load_step, s, 0), axis=0)
    we = act_e[jnp.clip(load_k, 0, E - 1)][last_load]
    wk = slab[last_load]
    own = jnp.where(tile_step, act_lo[k] + p, jnp.where(zero_step, n_used + s - n_sched, nt - 1))
    tile = jnp.clip(lax.cummin(own, axis=0, reverse=True), 0, nt - 1)
    return tuple(a.astype(i32) for a in (kind, slot, slab, we, wk, tile))


def _moe_grouped(xs, first_tile, tiles, w1, w3, w2, tm):
    N, D = xs.shape
    nt = N // tm
    E, _, F = w1.shape
    table = _moe_schedule(first_tile, tiles, 0, nt)
    grid_spec = pltpu.PrefetchScalarGridSpec(
        num_scalar_prefetch=len(table),
        grid=(FF_LOADS * E + nt,),
        in_specs=[pl.BlockSpec((tm, D), lambda s, *pre: (pre[5][s], 0))]
        + _ffn_weight_specs(D, F, (1,), lambda s, *pre: (pre[3][s], pre[4][s])),
        out_specs=pl.BlockSpec((tm, D), lambda s, *pre: (pre[5][s], 0)),
        scratch_shapes=[pltpu.VMEM((MOE_SLOTS, D, F), BF16), pltpu.VMEM((MOE_SLOTS, D, F), BF16),
                        pltpu.VMEM((MOE_SLOTS, F, D), BF16)],
    )
    return pl.pallas_call(
        _moe_kernel,
        grid_spec=grid_spec,
        out_shape=jax.ShapeDtypeStruct((N, D), F32),
        compiler_params=_cparams("arbitrary"),
        name="moe_experts",
    )(*table, xs, w1, w3, w2)


def _combine_kernel(x_ref, y1_ref, y2_ref, w_ref, mod_ref, o_ref):
    w = w_ref[...]
    y = w[:, 0:1] * y1_ref[...] + w[:, 1:2] * y2_ref[...]
    o_ref[...] = x_ref[...] + mod_ref[0][5:6] * y


def _moe_combine(x2, y1, y2, w12, mod, seq):
    T, D = x2.shape
    tm = 2 * ROW_TILE
    per_b = seq // tm
    return pl.pallas_call(
        _combine_kernel,
        grid=(T // tm,),
        in_specs=[pl.BlockSpec((tm, D), lambda i: (i, 0)),
                  pl.BlockSpec((tm, D), lambda i: (i, 0)),
                  pl.BlockSpec((tm, D), lambda i: (i, 0)),
                  pl.BlockSpec((tm, 2), lambda i: (i, 0)),
                  pl.BlockSpec((1, 6, D), lambda i: (i // per_b, 0, 0))],
        out_specs=pl.BlockSpec((tm, D), lambda i: (i, 0)),
        out_shape=jax.ShapeDtypeStruct((T, D), F32),
        compiler_params=_cparams("parallel"),
        name="moe_combine",
    )(x2, y1, y2, w12, mod)


def _moe_ffn(x2, mod, g, router_w, router_b, w1, w3, w2, seq):
    T, D = x2.shape
    E = router_w.shape[1]
    tm = ROW_TILE
    h, comb, sel = _router(x2, mod, g, router_w, router_b, seq)

    counts = jnp.sum(sel, axis=1)
    padded = ((counts + tm - 1) // tm) * tm
    group_end = jnp.cumsum(padded)
    group_start = group_end - padded
    rank = jnp.cumsum(sel, axis=1) - sel
    dest = group_start[:, None] + rank
    n_rows = TOP_K * T + E * tm
    n_tiles = n_rows // tm
    tile_start = jnp.arange(n_tiles, dtype=jnp.int32) * tm
    tile_expert = jnp.sum((group_end[None, :] <= tile_start[:, None]).astype(jnp.int32), axis=1)
    tile_expert = jnp.minimum(tile_expert, jnp.max(jnp.where(counts > 0, jnp.arange(E), 0))).astype(jnp.int32)

    eid = jnp.arange(E, dtype=jnp.int32)[:, None]
    e_lo = jnp.min(jnp.where(sel > 0, eid, E), axis=0)
    e_hi = jnp.max(jnp.where(sel > 0, eid, -1), axis=0)
    pick = lambda a, e: jnp.sum(jnp.where(eid == e[None, :], a, 0), axis=0)
    w12 = jnp.stack([pick(comb, e_lo), pick(comb, e_hi)], axis=1)

    tok = jnp.arange(T, dtype=jnp.int32)
    tok_sorted = jnp.sort(jnp.concatenate([e_lo * T + tok, e_hi * T + tok])) % T
    row = jnp.arange(n_rows, dtype=jnp.int32)
    row_expert = jnp.repeat(tile_expert, tm)
    first_sorted = (jnp.cumsum(counts) - counts)[row_expert]
    src = tok_sorted[jnp.clip(first_sorted + row - group_start[row_expert], 0, TOP_K * T - 1)]

    take_rows = lambda a, idx: a.at[idx].get(mode="promise_in_bounds")
    ys = _moe_grouped(take_rows(h, src), group_start // tm, padded // tm, w1, w3, w2, tm)
    y1 = take_rows(ys, pick(dest, e_lo))
    y2 = take_rows(ys, pick(dest, e_hi))
    return _moe_combine(x2, y1, y2, w12, mod, seq)


def kernel(x, c, positions, ada_w, ada_b, norm1_g, norm2_g, w_in, gate_b, conv_w, conv_b, lru_wa, lru_ba,
           lru_wx, lru_bx, lru_lambda, diff_qn, diff_kn, diff_lq1, diff_lk1, diff_lq2, diff_lk2, diff_subln,
           moba_qn, moba_kn, w_br_a, w_br_b, w_br_c, w_out, ffn_w1, ffn_w3, ffn_w2, router_w, router_b,
           moe_w1, moe_w3, moe_w2):
    B, S, D = x.shape
    L = ada_w.shape[0]
    T = B * S
    assert D == D_MODEL and w_in.shape[2] == PROJ_COL["g_br"] + N_BRANCH * D
    assert S % MOBA_BLOCK == 0 and S // MOBA_BLOCK <= 56 and S % ROW_TILE == 0
    x2 = x.reshape(T, D)
    mod_all = _ada_mod(c, ada_w, ada_b).reshape(L, B, 6, D)
    cos, sin = _rope_tables(positions)
    bf = lambda w: w.astype(BF16)
    tile2 = lambda v: jnp.tile(v, LANES // HEAD_DIM)

    for l in range(L):
        mod = mod_all[l]
        lam_init = 0.8 - 0.6 * math.exp(-0.3 * l)
        proj = _in_proj(x2, mod, norm1_g[l], w_in, l, S)
        gains = jnp.stack([tile2(diff_qn[l]), tile2(diff_kn[l]), tile2(moba_qn[l]), tile2(moba_kn[l])])
        qk = _qk_prep(proj, gains, cos, sin)
        y_a = _rg_lru_branch(proj, S, conv_w[l], conv_b[l], _pair_block_diag(lru_wa[l]), lru_ba[l],
                             _pair_block_diag(lru_wx[l]), lru_bx[l], lru_lambda[l])
        lam_params = jnp.stack([diff_lq1[l], diff_lk1[l], diff_lq2[l], diff_lk2[l]])
        y_b = _diff_attention(qk, proj, S, lam_params, diff_subln[l], lam_init)
        y_c = _moba_attention(qk, proj, S)
        x2 = _merge_out(x2, y_a, y_b, y_c, proj, gate_b[l], mod, bf(w_br_a[l]), bf(w_br_b[l]),
                        bf(w_br_c[l]), bf(w_out[l]), S)
        if l % 2 == 0:
            x2 = _dense_ffn(x2, mod, norm2_g[l], ffn_w1[l // 2], ffn_w3[l // 2], ffn_w2[l // 2], S)
        else:
            x2 = _moe_ffn(x2, mod, norm2_g[l], router_w[l // 2], router_b[l // 2], moe_w1[l // 2],
                          moe_w3[l // 2], moe_w2[l // 2], S)
    return x2.reshape(B, S, D)
```
